```python
import math
import jax, jax.numpy as jnp
from jax import lax
import numpy as np

D_MODEL = 1024
BATCH = 2
SEQ = 8192
DEPTH = 1
DEC_BATCH = 8
DEC_SEQ = 64
PAST_LEN = 4096

CHUNK = 64
Q_BLOCK = 128
CONV_WIDTH = 512
CONV_K = 3
N_HEADS = 8
QK_NOPE = 64
QK_ROPE = 32
V_HEAD = 64
Q_LORA = 256
KV_LORA = 128
ATTN_WIDTH = N_HEADS * V_HEAD
MIX_WIDTH = CONV_WIDTH + ATTN_WIDTH
IN_COLS = 3 * CONV_WIDTH + Q_LORA + KV_LORA + QK_ROPE
ROPE_THETA = 10000.0
ATTN_SCALE = 1.0 / math.sqrt(QK_NOPE + QK_ROPE)
N_EXPERTS = 32
TOP_K = 4
D_FF = 1024
SWIGLU_LIMIT = 7.0
SWIGLU_ALPHA = 1.702
MOE_BLOCK = 128
N_MOD = 6
DEEPNORM_ALPHA = (2.0 * DEPTH) ** 0.25
DEEPNORM_BETA = (8.0 * DEPTH) ** -0.25
LN_EPS = 1e-5
RMS_EPS = 1e-6

kernel_name = 'hybrid_conv_mla_moe_streaming_encoder_step'


def _layernorm(x):
    xf = x.astype(jnp.float32)
    mu = xf.mean(-1, keepdims=True)
    var = jnp.square(xf - mu).mean(-1, keepdims=True)
    return ((xf - mu) * lax.rsqrt(var + LN_EPS)).astype(x.dtype)


def _rmsnorm(x, g):
    xf = x.astype(jnp.float32)
    y = xf * lax.rsqrt(jnp.mean(jnp.square(xf), -1, keepdims=True) + RMS_EPS)
    return y.astype(x.dtype) * g


def _rope(x, pos):
    half = QK_ROPE // 2
    inv = ROPE_THETA ** (-jnp.arange(half, dtype=jnp.float32) / half)
    ang = pos.astype(jnp.float32)[:, None] * inv[None, :]
    cos = jnp.cos(ang)[None, :, None, :]
    sin = jnp.sin(ang)[None, :, None, :]
    x1 = x[..., :half].astype(jnp.float32)
    x2 = x[..., half:].astype(jnp.float32)
    return jnp.concatenate([x1 * cos - x2 * sin, x2 * cos + x1 * sin], -1).astype(x.dtype)


def _mla_attend(q_lat, q_rope, ckv, krope, q_pos, k_pos):
    s = jnp.einsum('bqhc,bkc->bhqk', q_lat, ckv, preferred_element_type=jnp.float32)
    s = s + jnp.einsum('bqhr,bkr->bhqk', q_rope, krope, preferred_element_type=jnp.float32)
    s = s * ATTN_SCALE
    allowed = (k_pos[None, :] // CHUNK) <= (q_pos[:, None] // CHUNK)
    s = jnp.where(allowed[None, None], s, -jnp.inf)
    p = jax.nn.softmax(s, axis=-1).astype(ckv.dtype)
    return jnp.einsum('bhqk,bkc->bqhc', p, ckv)


def _mixer(h, conv_prev, ckv_past, krope_past, pos0, sweep,
           w_in, conv_w, g_qa, w_qb, g_kva, w_kvb, g_out_conv, g_out_attn, w_out):
    B, T, _ = h.shape
    proj = h @ w_in
    o1 = CONV_WIDTH
    o2 = o1 + CONV_WIDTH
    o3 = o2 + CONV_WIDTH
    o4 = o3 + Q_LORA
    o5 = o4 + KV_LORA
    xb, xc, xv, q_a, kv_a, k_r = jnp.split(proj, [o1, o2, o3, o4, o5], axis=-1)
    u = xc * xv
    upad = jnp.concatenate([conv_prev, u], axis=1)
    conv = sum(conv_w[j] * upad[:, j:j + T] for j in range(CONV_K))
    a_out = xb * conv
    conv_new = upad[:, -(CONV_K - 1):]
    pos = pos0 + jnp.arange(T, dtype=jnp.int32)
    q = (_rmsnorm(q_a, g_qa) @ w_qb).reshape(B, T, N_HEADS, QK_NOPE + QK_ROPE)
    q_nope = q[..., :QK_NOPE]
    q_rope = _rope(q[..., QK_NOPE:], pos)
    ckv_new = _rmsnorm(kv_a, g_kva)
    krope_new = _rope(k_r[:, :, None, :], pos)[:, :, 0]
    w_uk = w_kvb[:, :, :QK_NOPE]
    w_uv = w_kvb[:, :, QK_NOPE:]
    q_lat = jnp.einsum('bthd,chd->bthc', q_nope, w_uk)
    ckv_all = jnp.concatenate([ckv_past, ckv_new], axis=1)
    krope_all = jnp.concatenate([krope_past, krope_new], axis=1)
    k_pos = jnp.arange(ckv_all.shape[1], dtype=jnp.int32)
    if sweep:
        nb = T // Q_BLOCK
        qb = q_lat.reshape(B, nb, Q_BLOCK, N_HEADS, KV_LORA).transpose(1, 0, 2, 3, 4)
        rb = q_rope.reshape(B, nb, Q_BLOCK, N_HEADS, QK_ROPE).transpose(1, 0, 2, 3, 4)
        pb = pos.reshape(nb, Q_BLOCK)
        ob = lax.map(lambda a: _mla_attend(a[0], a[1], ckv_all, krope_all, a[2], k_pos), (qb, rb, pb))
        o_lat = ob.transpose(1, 0, 2, 3, 4).reshape(B, T, N_HEADS, KV_LORA)
    else:
        o_lat = _mla_attend(q_lat, q_rope, ckv_all, krope_all, pos, k_pos)
    b_out = jnp.einsum('bthc,chd->bthd', o_lat, w_uv).reshape(B, T, ATTN_WIDTH)
    mix = jnp.concatenate([_rmsnorm(a_out, g_out_conv), _rmsnorm(b_out, g_out_attn)], axis=-1)
    return mix @ w_out, ckv_new, krope_new, conv_new


def _moe(h, router_w, router_b, w_gu, b_gu, w_down, b_down):
    N, D = h.shape
    n_rows = N * TOP_K
    n_pad = ((n_rows + N_EXPERTS * (MOE_BLOCK - 1)) + MOE_BLOCK - 1) // MOE_BLOCK * MOE_BLOCK
    nblk = n_pad // MOE_BLOCK
    logits = (h @ router_w + router_b).astype(jnp.float32)
    top_val, top_idx = lax.top_k(logits, TOP_K)
    gates = jax.nn.softmax(top_val, axis=-1).astype(h.dtype)
    flat_e = top_idx.reshape(-1)
    flat_gate = gates.reshape(-1)
    counts = jnp.bincount(flat_e, length=N_EXPERTS)
    padded = (counts + MOE_BLOCK - 1) // MOE_BLOCK * MOE_BLOCK
    start_sorted = jnp.cumsum(counts) - counts
    pad_end = jnp.cumsum(padded)
    start_pad = pad_end - padded
    order = jnp.argsort(flat_e, stable=True)
    sorted_e = flat_e[order]
    rank = jnp.arange(n_rows, dtype=jnp.int32) - start_sorted[sorted_e]
    dest = start_pad[sorted_e] + rank
    tok_sorted = order // TOP_K
    gate_sorted = flat_gate[order]
    pad_tok = jnp.full((n_pad,), N, jnp.int32).at[dest].set(tok_sorted.astype(jnp.int32))
    h_ext = jnp.concatenate([h, jnp.zeros((1, D), h.dtype)], axis=0)
    x_pad = h_ext[pad_tok].reshape(nblk, MOE_BLOCK, D)
    blk_e = jnp.clip(jnp.searchsorted(pad_end, jnp.arange(nblk, dtype=jnp.int32) * MOE_BLOCK, side='right'),
                     0, N_EXPERTS - 1)

    def expert_block(a):
        xb, e = a
        gu = xb @ w_gu[e] + b_gu[e]
        g = jnp.minimum(gu[:, :D_FF], SWIGLU_LIMIT)
        lin = jnp.clip(gu[:, D_FF:], -SWIGLU_LIMIT, SWIGLU_LIMIT)
        act = g * jax.nn.sigmoid(SWIGLU_ALPHA * g) * (lin + 1.0)
        return act @ w_down[e] + b_down[e]

    out_pad = lax.map(expert_block, (x_pad, blk_e)).reshape(n_pad, D)
    return jax.ops.segment_sum(out_pad[dest] * gate_sorted[:, None], tok_sorted, num_segments=N)


def _layer(x, c, conv_prev, ckv_past, krope_past, pos0, sweep,
           w_ada, b_ada, w_in, conv_w, g_qa, w_qb, g_kva, w_kvb, g_out_conv, g_out_attn, w_out,
           ln1_g, ln1_b, router_w, router_b, w_gu, b_gu, w_down, b_down, ln2_g, ln2_b):
    mod = jax.nn.silu(c) @ w_ada + b_ada
    sh1, sc1, gt1, sh2, sc2, gt2 = jnp.split(mod[:, None, :], N_MOD, axis=-1)
    h = _layernorm(x) * (1.0 + sc1) + sh1
    m, ckv_new, krope_new, conv_new = _mixer(h, conv_prev, ckv_past, krope_past, pos0, sweep,
                                             w_in, conv_w, g_qa, w_qb, g_kva, w_kvb,
                                             g_out_conv, g_out_attn, w_out)
    x = _layernorm(DEEPNORM_ALPHA * x + gt1 * m) * ln1_g + ln1_b
    h = _layernorm(x) * (1.0 + sc2) + sh2
    B, T, D = h.shape
    f = _moe(h.reshape(B * T, D), router_w, router_b, w_gu, b_gu, w_down, b_down).reshape(B, T, D)
    x = _layernorm(DEEPNORM_ALPHA * x + gt2 * f) * ln2_g + ln2_b
    return x, ckv_new, krope_new, conv_new


def setup_inputs(seed: int = 0) -> dict:
    key = jax.random.key(seed)
    ks = jax.random.split(key, 40)
    f32 = jnp.float32
    L = DEPTH

    def nrm(k, shape, scale):
        return jax.random.normal(k, shape, f32) * scale

    w_kvb = jnp.concatenate([
        nrm(ks[13], (L, KV_LORA, N_HEADS, QK_NOPE), KV_LORA ** -0.5),
        nrm(ks[14], (L, KV_LORA, N_HEADS, V_HEAD), KV_LORA ** -0.5 * DEEPNORM_BETA)], axis=-1)
    return {
        'x_prompt': nrm(ks[0], (BATCH, SEQ, D_MODEL), 1.0),
        'x_sample': nrm(ks[1], (DEC_BATCH, DEC_SEQ, D_MODEL), 1.0),
        'c_prompt': nrm(ks[2], (BATCH, D_MODEL), 1.0),
        'c_sample': nrm(ks[3], (DEC_BATCH, D_MODEL), 1.0),
        'cache_ckv': nrm(ks[4], (L, DEC_BATCH, PAST_LEN, KV_LORA), 1.0),
        'cache_krope': nrm(ks[5], (L, DEC_BATCH, PAST_LEN, QK_ROPE), 1.0),
        'state_conv': nrm(ks[6], (L, DEC_BATCH, CONV_K - 1, CONV_WIDTH), 1.0),
        'w_ada': nrm(ks[7], (L, D_MODEL, N_MOD * D_MODEL), D_MODEL ** -0.5),
        'b_ada': nrm(ks[8], (L, N_MOD * D_MODEL), 0.02),
        'w_in': nrm(ks[9], (L, D_MODEL, IN_COLS), D_MODEL ** -0.5),
        'conv_w': nrm(ks[10], (L, CONV_K, CONV_WIDTH), CONV_K ** -0.5),
        'g_qa': 1.0 + nrm(ks[11], (L, Q_LORA), 0.02),
        'w_qb': nrm(ks[12], (L, Q_LORA, N_HEADS * (QK_NOPE + QK_ROPE)), Q_LORA ** -0.5),
        'g_kva': 1.0 + nrm(ks[15], (L, KV_LORA), 0.02),
        'w_kvb': w_kvb,
        'g_out_conv': 1.0 + nrm(ks[16], (L, CONV_WIDTH), 0.02),
        'g_out_attn': 1.0 + nrm(ks[17], (L, ATTN_WIDTH), 0.02),
        'w_out': nrm(ks[18], (L, MIX_WIDTH, D_MODEL), MIX_WIDTH ** -0.5 * DEEPNORM_BETA),
        'ln1_g': 1.0 + nrm(ks[19], (L, D_MODEL), 0.02),
        'ln1_b': nrm(ks[20], (L, D_MODEL), 0.02),
        'router_w': nrm(ks[21], (L, D_MODEL, N_EXPERTS), D_MODEL ** -0.5),
        'router_b': nrm(ks[22], (L, N_EXPERTS), 0.01),
        'w_gu': nrm(ks[23], (L, N_EXPERTS, D_MODEL, 2 * D_FF), D_MODEL ** -0.5 * DEEPNORM_BETA),
        'b_gu': nrm(ks[24], (L, N_EXPERTS, 2 * D_FF), 0.01),
        'w_down': nrm(ks[25], (L, N_EXPERTS, D_FF, D_MODEL), D_FF ** -0.5 * DEEPNORM_BETA),
        'b_down': nrm(ks[26], (L, N_EXPERTS, D_MODEL), 0.01),
        'ln2_g': 1.0 + nrm(ks[27], (L, D_MODEL), 0.02),
        'ln2_b': nrm(ks[28], (L, D_MODEL), 0.02),
    }


def reference(x_prompt, x_sample, c_prompt, c_sample, cache_ckv, cache_krope, state_conv,
              w_ada, b_ada, w_in, conv_w, g_qa, w_qb, g_kva, w_kvb, g_out_conv, g_out_attn, w_out,
              ln1_g, ln1_b, router_w, router_b, w_gu, b_gu, w_down, b_down, ln2_g, ln2_b):
    xp = x_prompt
    xs = x_sample
    Bp = x_prompt.shape[0]
    dt = x_prompt.dtype
    ckv_p_l, kr_p_l, cv_p_l, ckv_s_l, kr_s_l, cv_s_l = [], [], [], [], [], []
    for l in range(DEPTH):
        lw = (w_ada[l], b_ada[l], w_in[l], conv_w[l], g_qa[l], w_qb[l], g_kva[l], w_kvb[l],
              g_out_conv[l], g_out_attn[l], w_out[l], ln1_g[l], ln1_b[l], router_w[l], router_b[l],
              w_gu[l], b_gu[l], w_down[l], b_down[l], ln2_g[l], ln2_b[l])
        xp, ckv_p, kr_p, cv_p = _layer(
            xp, c_prompt,
            jnp.zeros((Bp, CONV_K - 1, CONV_WIDTH), dt),
            jnp.zeros((Bp, 0, KV_LORA), dt),
            jnp.zeros((Bp, 0, QK_ROPE), dt),
            0, True, *lw)
        xs, ckv_s, kr_s, cv_s = _layer(
            xs, c_sample, state_conv[l], cache_ckv[l], cache_krope[l],
            PAST_LEN, False, *lw)
        ckv_p_l.append(ckv_p)
        kr_p_l.append(kr_p)
        cv_p_l.append(cv_p)
        ckv_s_l.append(ckv_s)
        kr_s_l.append(kr_s)
        cv_s_l.append(cv_s)
    return (xp, xs, jnp.stack(ckv_p_l), jnp.stack(kr_p_l), jnp.stack(cv_p_l),
            jnp.stack(ckv_s_l), jnp.stack(kr_s_l), jnp.stack(cv_s_l))
```

```python
import functools
import math

import jax
import jax.numpy as jnp
from jax import lax
from jax.experimental import pallas as pl
from jax.experimental.pallas import tpu as pltpu

F32 = jnp.float32
BF16 = jnp.bfloat16

D_MODEL = 1024
CHUNK = 64
CONV_WIDTH = 512
CONV_K = 3
N_HEADS = 8
QK_NOPE = 64
QK_ROPE = 32
V_HEAD = 64
Q_LORA = 256
KV_LORA = 128
ATTN_WIDTH = N_HEADS * V_HEAD
ROPE_THETA = 10000.0
ATTN_SCALE = 1.0 / math.sqrt(QK_NOPE + QK_ROPE)
N_EXPERTS = 32
TOP_K = 4
D_FF = 1024
SWIGLU_LIMIT = 7.0
SWIGLU_ALPHA = 1.702
N_MOD = 6
LN_EPS = 1e-5
RMS_EPS = 1e-6

LANES = 128
NEG = -1e30
VMEM_LIMIT = 56 * 1024 * 1024

_O_XB, _O_XC, _O_XV = 0, CONV_WIDTH, 2 * CONV_WIDTH
_O_QA = 3 * CONV_WIDTH
_O_KVA = _O_QA + Q_LORA
_O_KR = _O_KVA + KV_LORA
_O_KRS = _O_KR + LANES
IN_COLS_EXT = _O_KRS + LANES


def _cparams(sem):
    return pltpu.CompilerParams(dimension_semantics=sem, vmem_limit_bytes=VMEM_LIMIT)


def _layernorm(x):
    mu = jnp.mean(x, axis=-1, keepdims=True)
    xc = x - mu
    var = jnp.mean(xc * xc, axis=-1, keepdims=True)
    return xc * lax.rsqrt(var + LN_EPS)


def _rms(x):
    return x * lax.rsqrt(jnp.mean(x * x, axis=-1, keepdims=True) + RMS_EPS)


def _sigmoid(x):
    return 1.0 / (1.0 + jnp.exp(-x))


def _ada_kernel(c_ref, w_ref, b_ref, o_ref):
    c = c_ref[...]
    s = (c * _sigmoid(c)).astype(BF16)
    o_ref[...] = jnp.dot(s, w_ref[...].astype(BF16), preferred_element_type=F32) + b_ref[...]


def _ada(c_all, w_ada, b_ada):
    rows = c_all.shape[0]
    ncol = w_ada.shape[1]
    tn = 1024
    return pl.pallas_call(
        _ada_kernel,
        grid=(ncol // tn,),
        in_specs=[pl.BlockSpec((rows, D_MODEL), lambda j: (0, 0)),
                  pl.BlockSpec((D_MODEL, tn), lambda j: (0, j)),
                  pl.BlockSpec((1, tn), lambda j: (0, j))],
        out_specs=pl.BlockSpec((rows, tn), lambda j: (0, j)),
        out_shape=jax.ShapeDtypeStruct((rows, ncol), F32),
        compiler_params=_cparams(("arbitrary",)),
        name="ada",
    )(c_all, w_ada, b_ada.reshape(1, ncol))


def _inproj_kernel(x_ref, sc_ref, sh_ref, win_ref, cw_ref, cprev_ref, gqa_ref, gkva_ref, goc_ref,
                   wqn_ref, wqr_ref, wqrs_ref, wuk_ref, cos_ref, sin_ref,
                   an_ref, qlat_ref, qrope_ref, kcat_ref, ckv_ref, krope_ref, cnew_ref,
                   ubuf, *, tm):
    j = pl.program_id(1)
    x = x_ref[0]
    h = _layernorm(x) * (1.0 + sc_ref[0]) + sh_ref[0]
    proj = jnp.dot(h.astype(BF16), win_ref[...], preferred_element_type=F32)
    xb = proj[:, _O_XB:_O_XB + CONV_WIDTH]
    xc = proj[:, _O_XC:_O_XC + CONV_WIDTH]
    xv = proj[:, _O_XV:_O_XV + CONV_WIDTH]
    q_a = proj[:, _O_QA:_O_QA + Q_LORA]
    kv_a = proj[:, _O_KVA:_O_KVA + KV_LORA]
    kr4 = proj[:, _O_KR:_O_KR + LANES]
    kr4s = proj[:, _O_KRS:_O_KRS + LANES]

    u = xc * xv

    @pl.when(j == 0)
    def _():
        ubuf[6:8, :] = cprev_ref[0]

    ubuf[8:8 + tm, :] = u
    conv = (cw_ref[0:1, :] * ubuf[6:6 + tm, :] + cw_ref[1:2, :] * ubuf[7:7 + tm, :]
            + cw_ref[2:3, :] * u)
    ubuf[0:8, :] = ubuf[tm:tm + 8, :]
    cnew_ref[0] = u[tm - (CONV_K - 1):tm, :]
    an_ref[0] = (_rms(xb * conv) * goc_ref[...]).astype(BF16)

    cos = cos_ref[...]
    sin = sin_ref[...]

    ckv = _rms(kv_a) * gkva_ref[...]
    kro4 = kr4 * cos + kr4s * sin
    ckv_ref[0] = ckv
    krope_ref[0] = kro4[:, :QK_ROPE]
    kcat_ref[0] = jnp.concatenate([ckv, kro4], axis=1).astype(BF16)

    qn = (_rms(q_a) * gqa_ref[...]).astype(BF16)
    q_nope = jnp.dot(qn, wqn_ref[...], preferred_element_type=F32)
    xr = jnp.dot(qn, wqr_ref[...], preferred_element_type=F32)
    xrs = jnp.dot(qn, wqrs_ref[...], preferred_element_type=F32)
    for g in range(2):
        sl = slice(g * LANES, (g + 1) * LANES)
        qrope_ref[0, :, sl] = ((xr[:, sl] * cos + xrs[:, sl] * sin) * ATTN_SCALE).astype(BF16)
    for p in range(N_HEADS // 2):
        qp = q_nope[:, p * LANES:(p + 1) * LANES].astype(BF16)
        ql = jnp.dot(qp, wuk_ref[p], preferred_element_type=F32)
        qlat_ref[0, :, p * 2 * KV_LORA:(p + 1) * 2 * KV_LORA] = (ql * ATTN_SCALE).astype(BF16)


def _inproj(x, sc1, sh1, conv_prev, cos_t, sin_t, w, tm):
    B, T, _ = x.shape
    nt = T // tm
    full = lambda shp: pl.BlockSpec(shp, lambda b, j: (0,) * len(shp))
    out_shapes = (
        jax.ShapeDtypeStruct((B, T, CONV_WIDTH), BF16),
        jax.ShapeDtypeStruct((B, T, N_HEADS * KV_LORA), BF16),
        jax.ShapeDtypeStruct((B, T, 2 * LANES), BF16),
        jax.ShapeDtypeStruct((B, T, 2 * LANES), BF16),
        jax.ShapeDtypeStruct((B, T, KV_LORA), F32),
        jax.ShapeDtypeStruct((B, T, QK_ROPE), F32),
        jax.ShapeDtypeStruct((B, CONV_K - 1, CONV_WIDTH), F32),
    )
    tile = lambda c: pl.BlockSpec((1, tm, c), lambda b, j: (b, j, 0))
    return pl.pallas_call(
        functools.partial(_inproj_kernel, tm=tm),
        grid=(B, nt),
        in_specs=[
            tile(D_MODEL),
            pl.BlockSpec((1, 1, D_MODEL), lambda b, j: (b, 0, 0)),
            pl.BlockSpec((1, 1, D_MODEL), lambda b, j: (b, 0, 0)),
            full((D_MODEL, IN_COLS_EXT)),
            full((CONV_K, CONV_WIDTH)),
            pl.BlockSpec((1, CONV_K - 1, CONV_WIDTH), lambda b, j: (b, 0, 0)),
            full((1, Q_LORA)), full((1, KV_LORA)), full((1, CONV_WIDTH)),
            full((Q_LORA, N_HEADS * QK_NOPE)), full((Q_LORA, 2 * LANES)), full((Q_LORA, 2 * LANES)),
            full((N_HEADS // 2, LANES, 2 * KV_LORA)),
            pl.BlockSpec((tm, LANES), lambda b, j: (j, 0)),
            pl.BlockSpec((tm, LANES), lambda b, j: (j, 0)),
        ],
        out_specs=(tile(CONV_WIDTH), tile(N_HEADS * KV_LORA), tile(2 * LANES), tile(2 * LANES),
                   tile(KV_LORA), tile(QK_ROPE),
                   pl.BlockSpec((1, CONV_K - 1, CONV_WIDTH), lambda b, j: (b, 0, 0))),
        out_shape=out_shapes,
        scratch_shapes=[pltpu.VMEM((tm + 8, CONV_WIDTH), F32)],
        compiler_params=_cparams(("arbitrary", "arbitrary")),
        name="inproj",
    )(x, sc1, sh1, w["w_in"], w["conv_w"], conv_prev, w["g_qa"], w["g_kva"], w["g_oc"],
      w["wq_nope"], w["wq_rope"], w["wq_rope_sw"], w["w_uk_pair"], cos_t, sin_t)


def _stack_queries(qlat_ref, qrope_ref, qs, tq):
    lane = lax.broadcasted_iota(jnp.int32, (tq, LANES), 1)
    for h in range(N_HEADS):
        g, i = divmod(h, 4)
        rope = qrope_ref[0, :, g * LANES:(g + 1) * LANES]
        keep = (lane >= i * QK_ROPE) & (lane < (i + 1) * QK_ROPE)
        qs[h * tq:(h + 1) * tq, 0:KV_LORA] = qlat_ref[0, :, h * KV_LORA:(h + 1) * KV_LORA]
        qs[h * tq:(h + 1) * tq, KV_LORA:KV_LORA + LANES] = jnp.where(keep, rope, jnp.zeros_like(rope))


def _softmax_step(q, k, v, m_s, l_s, acc_s, mask=None):
    tk = k.shape[0]
    s = lax.dot_general(q, k, (((1,), (1,)), ((), ())), preferred_element_type=F32)
    if mask is not None:
        s = jnp.where(mask, s, NEG)
    m_prev = m_s[...]
    m_new = jnp.maximum(m_prev, jnp.max(s, axis=1, keepdims=True))
    alpha = jnp.exp(m_prev - m_new)
    if tk % LANES == 0:
        p = jnp.exp(s - jnp.tile(m_new, (1, tk // LANES)))
    else:
        p = jnp.exp(s - m_new[:, :tk])
    l_s[...] = alpha * l_s[...] + jnp.sum(p, axis=1, keepdims=True)
    acc_s[...] = alpha * acc_s[...] + jnp.dot(p.astype(BF16), v, preferred_element_type=F32)
    m_s[...] = m_new


def _attn_epilogue(acc_s, l_s, wuv_ref, g_ref, o_ref, tq):
    o = acc_s[...] / l_s[...]
    parts = []
    for p in range(N_HEADS // 2):
        op = jnp.concatenate([o[(2 * p) * tq:(2 * p + 1) * tq], o[(2 * p + 1) * tq:(2 * p + 2) * tq]], axis=1)
        parts.append(jnp.dot(op.astype(BF16), wuv_ref[p], preferred_element_type=F32))
    b = jnp.concatenate(parts, axis=1)
    o_ref[0] = (_rms(b) * g_ref[...]).astype(BF16)


def _attn_prompt_kernel(qlat_ref, qrope_ref, k_ref, wuv_ref, g_ref, o_ref, qs, m_s, l_s, acc_s, *, tq, tk):
    i = pl.program_id(1)
    M = N_HEADS * tq
    _stack_queries(qlat_ref, qrope_ref, qs, tq)
    m_s[...] = jnp.full(m_s.shape, NEG, F32)
    l_s[...] = jnp.zeros(l_s.shape, F32)
    acc_s[...] = jnp.zeros(acc_s.shape, F32)
    q0 = i * tq
    n_full = (q0 + CHUNK) // tk

    def body(t, carry):
        start = pl.multiple_of(t * tk, tk)
        k = k_ref[0, pl.ds(start, tk), :]
        _softmax_step(qs[...], k, k[:, :KV_LORA], m_s, l_s, acc_s)
        return carry

    lax.fori_loop(0, n_full, body, 0)

    start = pl.multiple_of(n_full * tk, tk)
    k = k_ref[0, pl.ds(start, tk), :]
    col = start + lax.broadcasted_iota(jnp.int32, (1, tk), 1)
    row_t = jnp.bitwise_and(lax.broadcasted_iota(jnp.int32, (M, 1), 0), tq - 1)
    limit = q0 + (jnp.right_shift(row_t, CHUNK.bit_length() - 1) + 1) * CHUNK
    _softmax_step(qs[...], k, k[:, :KV_LORA], m_s, l_s, acc_s, mask=col < limit)
    _attn_epilogue(acc_s, l_s, wuv_ref, g_ref, o_ref, tq)


def _attn_prompt(qlat, qrope, kcat, w, tq, tk):
    B, T, _ = qlat.shape
    M = N_HEADS * tq
    return pl.pallas_call(
        functools.partial(_attn_prompt_kernel, tq=tq, tk=tk),
        grid=(B, T // tq),
        in_specs=[
            pl.BlockSpec((1, tq, N_HEADS * KV_LORA), lambda b, i: (b, i, 0)),
            pl.BlockSpec((1, tq, 2 * LANES), lambda b, i: (b, i, 0)),
            pl.BlockSpec((1, T, 2 * LANES), lambda b, i: (b, 0, 0)),
            pl.BlockSpec((N_HEADS // 2, 2 * KV_LORA, LANES), lambda b, i: (0, 0, 0)),
            pl.BlockSpec((1, ATTN_WIDTH), lambda b, i: (0, 0)),
        ],
        out_specs=pl.BlockSpec((1, tq, ATTN_WIDTH), lambda b, i: (b, i, 0)),
        out_shape=jax.ShapeDtypeStruct((B, T, ATTN_WIDTH), BF16),
        scratch_shapes=[pltpu.VMEM((M, 2 * LANES), BF16), pltpu.VMEM((M, LANES), F32),
                        pltpu.VMEM((M, LANES), F32), pltpu.VMEM((M, KV_LORA), F32)],
        compiler_params=_cparams(("arbitrary", "arbitrary")),
        name="attn_prompt",
    )(qlat, qrope, kcat, w["w_uv_pair"], w["g_oa"])


def _attn_sample_kernel(qlat_ref, qrope_ref, knew_ref, pckv_ref, pkr_ref, wuv_ref, g_ref, o_ref,
                        qs, m_s, l_s, acc_s, *, tq, tk, n_past):
    _stack_queries(qlat_ref, qrope_ref, qs, tq)
    m_s[...] = jnp.full(m_s.shape, NEG, F32)
    l_s[...] = jnp.zeros(l_s.shape, F32)
    acc_s[...] = jnp.zeros(acc_s.shape, F32)

    def body(t, carry):
        start = pl.multiple_of(t * tk, tk)
        ck = pckv_ref[0, pl.ds(start, tk), :]
        kr = pkr_ref[0, pl.ds(start, tk), :]
        k = jnp.concatenate([ck, kr, kr, kr, kr], axis=1).astype(BF16)
        _softmax_step(qs[...], k, k[:, :KV_LORA], m_s, l_s, acc_s)
        return carry

    lax.fori_loop(0, n_past // tk, body, 0)
    k = knew_ref[0]
    _softmax_step(qs[...], k, k[:, :KV_LORA], m_s, l_s, acc_s)
    _attn_epilogue(acc_s, l_s, wuv_ref, g_ref, o_ref, tq)


def _attn_sample(qlat, qrope, kcat, past_ckv, past_krope, w, tk):
    B, T, _ = qlat.shape
    n_past = past_ckv.shape[1]
    M = N_HEADS * T
    per_b = lambda r, c: pl.BlockSpec((1, r, c), lambda b: (b, 0, 0))
    return pl.pallas_call(
        functools.partial(_attn_sample_kernel, tq=T, tk=tk, n_past=n_past),
        grid=(B,),
        in_specs=[per_b(T, N_HEADS * KV_LORA), per_b(T, 2 * LANES), per_b(T, 2 * LANES),
                  per_b(n_past, KV_LORA), per_b(n_past, QK_ROPE),
                  pl.BlockSpec((N_HEADS // 2, 2 * KV_LORA, LANES), lambda b: (0, 0, 0)),
                  pl.BlockSpec((1, ATTN_WIDTH), lambda b: (0, 0))],
        out_specs=per_b(T, ATTN_WIDTH),
        out_shape=jax.ShapeDtypeStruct((B, T, ATTN_WIDTH), BF16),
        scratch_shapes=[pltpu.VMEM((M, 2 * LANES), BF16), pltpu.VMEM((M, LANES), F32),
                        pltpu.VMEM((M, LANES), F32), pltpu.VMEM((M, KV_LORA), F32)],
        compiler_params=_cparams(("arbitrary",)),
        name="attn_sample",
    )(qlat, qrope, kcat, past_ckv, past_krope, w["w_uv_pair"], w["g_oa"])


def _outproj_kernel(x_ref, an_ref, bn_ref, gt1_ref, sc2_ref, sh2_ref, wo_ref, l1g_ref, l1b_ref,
                    rw_ref, rb_ref, x1_ref, h2_ref, meta_ref, gates_ref, cnt_ref, run_s,
                    *, tm, alpha):
    first = (pl.program_id(0) == 0) & (pl.program_id(1) == 0)

    @pl.when(first)
    def _():
        run_s[...] = jnp.zeros(run_s.shape, F32)

    m = (jnp.dot(an_ref[0], wo_ref[0:CONV_WIDTH, :], preferred_element_type=F32)
         + jnp.dot(bn_ref[0], wo_ref[CONV_WIDTH:, :], preferred_element_type=F32))
    x1 = _layernorm(alpha * x_ref[0] + gt1_ref[0] * m) * l1g_ref[...] + l1b_ref[...]
    x1_ref[0] = x1
    h2 = _layernorm(x1) * (1.0 + sc2_ref[0]) + sh2_ref[0]
    h2_ref[0] = h2

    logits = jnp.dot(h2.astype(BF16), rw_ref[...], preferred_element_type=F32) + rb_ref[...]
    lane = lax.broadcasted_iota(jnp.int32, (tm, LANES), 1)
    lane_f = lane.astype(F32)
    lg = logits
    vals, sels = [], []
    chosen = jnp.zeros((tm, LANES), F32)
    for _ in range(TOP_K):
        mx = jnp.max(lg, axis=1, keepdims=True)
        idx = jnp.min(jnp.where(lg == mx, lane_f, float(LANES)), axis=1, keepdims=True)
        sel = lane_f == idx
        vals.append(mx)
        sels.append((sel, idx))
        chosen = jnp.where(sel, 1.0, chosen)
        lg = jnp.where(sel, NEG, lg)

    es = [jnp.exp(v - vals[0]) for v in vals]
    denom = es[0] + es[1] + es[2] + es[3]

    r_i = lax.broadcasted_iota(jnp.int32, (tm, tm), 0)
    c_i = lax.broadcasted_iota(jnp.int32, (tm, tm), 1)
    tri = jnp.where(c_i < r_i, 1.0, 0.0).astype(BF16)
    before = jnp.dot(tri, chosen.astype(BF16), preferred_element_type=F32) + run_s[0:1, :]
    run_s[0:1, :] = run_s[0:1, :] + jnp.sum(chosen, axis=0, keepdims=True)
    cnt_ref[...] = jnp.broadcast_to(run_s[0:1, :], cnt_ref.shape)

    meta = jnp.zeros((tm, LANES), jnp.int32)
    gates = jnp.zeros((tm, LANES), F32)
    for k in range(TOP_K):
        sel, idx = sels[k]
        rank = jnp.sum(jnp.where(sel, before, 0.0), axis=1, keepdims=True).astype(jnp.int32)
        meta = jnp.where(lane == k, idx.astype(jnp.int32), meta)
        meta = jnp.where(lane == TOP_K + k, rank, meta)
        gates = jnp.where(lane == k, es[k] / denom, gates)
    meta_ref[0] = meta
    gates_ref[0] = gates


def _outproj(x, a_n, b_n, gt1, sc2, sh2, w, tm, alpha):
    B, T, _ = x.shape
    tile = lambda c: pl.BlockSpec((1, tm, c), lambda b, j: (b, j, 0))
    modv = pl.BlockSpec((1, 1, D_MODEL), lambda b, j: (b, 0, 0))
    full = lambda shp: pl.BlockSpec(shp, lambda b, j: (0,) * len(shp))
    return pl.pallas_call(
        functools.partial(_outproj_kernel, tm=tm, alpha=alpha),
        grid=(B, T // tm),
        in_specs=[tile(D_MODEL), tile(CONV_WIDTH), tile(ATTN_WIDTH), modv, modv, modv,
                  full((CONV_WIDTH + ATTN_WIDTH, D_MODEL)), full((1, D_MODEL)), full((1, D_MODEL)),
                  full((D_MODEL, LANES)), full((1, LANES))],
        out_specs=(tile(D_MODEL), tile(D_MODEL), tile(LANES), tile(LANES),
                   pl.BlockSpec((8, LANES), lambda b, j: (0, 0))),
        out_shape=(jax.ShapeDtypeStruct((B, T, D_MODEL), F32), jax.ShapeDtypeStruct((B, T, D_MODEL), F32),
                   jax.ShapeDtypeStruct((B, T, LANES), jnp.int32), jax.ShapeDtypeStruct((B, T, LANES), F32),
                   jax.ShapeDtypeStruct((8, LANES), F32)),
        scratch_shapes=[pltpu.VMEM((8, LANES), F32)],
        compiler_params=_cparams(("arbitrary", "arbitrary")),
        name="outproj",
    )(x, a_n, b_n, gt1, sc2, sh2, w["w_out"], w["ln1_g"], w["ln1_b"], w["router_w"], w["router_b"])


def _scatter_kernel(dest_ref, zrow_ref, h_ref, xs_ref, zbuf, sem, *, tm, bm):
    i = pl.program_id(0)

    @pl.when(i == 0)
    def _():
        zbuf[...] = jnp.zeros(zbuf.shape, F32)
        nblk = xs_ref.shape[0] // bm
        used = zrow_ref[N_EXPERTS]

        def zcopy(r):
            return pltpu.make_async_copy(zbuf, xs_ref.at[pl.ds(pl.multiple_of(r, bm), bm)], sem.at[0])

        def zstart(e, c):
            @pl.when(zrow_ref[e] >= 0)
            def _():
                zcopy(zrow_ref[e]).start()
            return c

        def zwait(e, c):
            @pl.when(zrow_ref[e] >= 0)
            def _():
                zcopy(0).wait()
            return c

        lax.fori_loop(0, N_EXPERTS, zstart, 0)
        lax.fori_loop(used, nblk, lambda b, c: (zcopy(b * bm).start(), c)[1], 0)
        lax.fori_loop(0, N_EXPERTS, zwait, 0)
        lax.fori_loop(used, nblk, lambda b, c: (zcopy(0).wait(), c)[1], 0)

    base = i * (tm * TOP_K)

    def issue(r, c):
        for k in range(TOP_K):
            d = dest_ref[base + r * TOP_K + k]
            pltpu.make_async_copy(h_ref.at[pl.ds(r, 1)], xs_ref.at[pl.ds(d, 1)], sem.at[1]).start()
        return c

    lax.fori_loop(0, tm, issue, 0)
    for k in range(TOP_K):
        pltpu.make_async_copy(h_ref, xs_ref.at[pl.ds(0, tm)], sem.at[1]).wait()


def _scatter(dest_flat, zrow, h2, n_pad, tm, bm):
    N = h2.shape[0]
    return pl.pallas_call(
        functools.partial(_scatter_kernel, tm=tm, bm=bm),
        grid_spec=pltpu.PrefetchScalarGridSpec(
            num_scalar_prefetch=2,
            grid=(N // tm,),
            in_specs=[pl.BlockSpec((tm, D_MODEL), lambda i, d, z: (i, 0))],
            out_specs=pl.BlockSpec(memory_space=pl.ANY),
            scratch_shapes=[pltpu.VMEM((bm, D_MODEL), F32), pltpu.SemaphoreType.DMA((2,))],
        ),
        out_shape=jax.ShapeDtypeStruct((n_pad, D_MODEL), F32),
        compiler_params=_cparams(("arbitrary",)),
        name="moe_scatter",
    )(dest_flat, zrow, h2)


def _expert_kernel(blk_e_ref, used_ref, xs_ref, wgu_ref, bgu_ref, wd_ref, bd_ref, y_ref, wgu_bf, wd_bf):
    i = pl.program_id(0)
    prev = blk_e_ref[jnp.maximum(i - 1, 0)]
    fresh = (i == 0) | (blk_e_ref[i] != prev)

    @pl.when(fresh)
    def _():
        wgu_bf[...] = wgu_ref[0].astype(BF16)
        wd_bf[...] = wd_ref[0].astype(BF16)

    @pl.when(i < used_ref[0])
    def _():
        x = xs_ref[...].astype(BF16)
        gu = jnp.dot(x, wgu_bf[...], preferred_element_type=F32) + bgu_ref[0]
        g = jnp.minimum(gu[:, :D_FF], SWIGLU_LIMIT)
        lin = jnp.clip(gu[:, D_FF:], -SWIGLU_LIMIT, SWIGLU_LIMIT)
        act = g * _sigmoid(SWIGLU_ALPHA * g) * (lin + 1.0)
        y_ref[...] = jnp.dot(act.astype(BF16), wd_bf[...], preferred_element_type=F32) + bd_ref[0]

    @pl.when(i >= used_ref[0])
    def _():
        y_ref[...] = jnp.zeros(y_ref.shape, F32)


def _expert(blk_e, used, xs, w, bm):
    n_pad = xs.shape[0]
    nblk = n_pad // bm
    return pl.pallas_call(
        _expert_kernel,
        grid_spec=pltpu.PrefetchScalarGridSpec(
            num_scalar_prefetch=2,
            grid=(nblk,),
            in_specs=[
                pl.BlockSpec((bm, D_MODEL), lambda i, be, u: (jnp.minimum(i, u[0] - 1), 0)),
                pl.BlockSpec((1, D_MODEL, 2 * D_FF), lambda i, be, u: (be[i], 0, 0)),
                pl.BlockSpec((1, 1, 2 * D_FF), lambda i, be, u: (be[i], 0, 0)),
                pl.BlockSpec((1, D_FF, D_MODEL), lambda i, be, u: (be[i], 0, 0)),
                pl.BlockSpec((1, 1, D_MODEL), lambda i, be, u: (be[i], 0, 0)),
            ],
            out_specs=pl.BlockSpec((bm, D_MODEL), lambda i, be, u: (i, 0)),
            scratch_shapes=[pltpu.VMEM((D_MODEL, 2 * D_FF), BF16), pltpu.VMEM((D_FF, D_MODEL), BF16)],
        ),
        out_shape=jax.ShapeDtypeStruct((n_pad, D_MODEL), F32),
        compiler_params=_cparams(("arbitrary",)),
        name="moe_expert",
    )(blk_e, used, xs, w["w_gu"], w["b_gu"], w["w_down"], w["b_down"])


def _combine_kernel(dest_ref, y_ref, x1_ref, gates_ref, gt2_ref, l2g_ref, l2b_ref, o_ref, gbuf, sem,
                    *, tm, alpha):
    i = pl.program_id(0)
    base = i * (tm * TOP_K)

    def issue(r, c):
        for k in range(TOP_K):
            d = dest_ref[base + r * TOP_K + k]
            pltpu.make_async_copy(y_ref.at[pl.ds(d, 1)], gbuf.at[k, pl.ds(r, 1)], sem.at[0]).start()
        return c

    lax.fori_loop(0, tm, issue, 0)
    for k in range(TOP_K):
        pltpu.make_async_copy(y_ref.at[pl.ds(0, tm)], gbuf.at[k], sem.at[0]).wait()

    gates = gates_ref[...]
    f = gates[:, 0:1] * gbuf[0]
    for k in range(1, TOP_K):
        f = f + gates[:, k:k + 1] * gbuf[k]
    o_ref[...] = _layernorm(alpha * x1_ref[...] + gt2_ref[0] * f) * l2g_ref[...] + l2b_ref[...]


def _combine(dest_flat, y_pad, x1, gates, gt2, w, tm, tokens_per_batch, alpha):
    N = x1.shape[0]
    per_b = tokens_per_batch // tm
    return pl.pallas_call(
        functools.partial(_combine_kernel, tm=tm, alpha=alpha),
        grid_spec=pltpu.PrefetchScalarGridSpec(
            num_scalar_prefetch=1,
            grid=(N // tm,),
            in_specs=[
                pl.BlockSpec(memory_space=pl.ANY),
                pl.BlockSpec((tm, D_MODEL), lambda i, d: (i, 0)),
                pl.BlockSpec((tm, LANES), lambda i, d: (i, 0)),
                pl.BlockSpec((1, 1, D_MODEL), lambda i, d: (i // per_b, 0, 0)),
                pl.BlockSpec((1, D_MODEL), lambda i, d: (0, 0)),
                pl.BlockSpec((1, D_MODEL), lambda i, d: (0, 0)),
            ],
            out_specs=pl.BlockSpec((tm, D_MODEL), lambda i, d: (i, 0)),
            scratch_shapes=[pltpu.VMEM((TOP_K, tm, D_MODEL), F32), pltpu.SemaphoreType.DMA((1,))],
        ),
        out_shape=jax.ShapeDtypeStruct((N, D_MODEL), F32),
        compiler_params=_cparams(("arbitrary",)),
        name="moe_combine",
    )(dest_flat, y_pad, x1, gates, gt2, w["ln2_g"], w["ln2_b"])


def _moe(h2, x1, meta, gates, counts, gt2, w, tokens_per_batch, bm, tm_s, tm_c, alpha):
    N = h2.shape[0]
    n_rows = N * TOP_K
    n_pad = (n_rows + N_EXPERTS * (bm - 1)) // bm * bm
    nblk = n_pad // bm
    idx = meta[:, :TOP_K]
    rank = meta[:, TOP_K:2 * TOP_K]
    padded = (counts + bm - 1) // bm * bm
    pad_end = jnp.cumsum(padded)
    start_pad = pad_end - padded
    dest_flat = (start_pad[idx] + rank).reshape(-1).astype(jnp.int32)
    used = (pad_end[-1] // bm).astype(jnp.int32)
    blk = jnp.minimum(jnp.arange(nblk, dtype=jnp.int32), used - 1)
    blk_e = jnp.clip(jnp.searchsorted(pad_end, blk * bm, side="right"), 0, N_EXPERTS - 1).astype(jnp.int32)
    zrow = jnp.concatenate([jnp.where(padded > 0, pad_end - bm, -1), used.reshape(1)]).astype(jnp.int32)
    xs = _scatter(dest_flat, zrow, h2, n_pad, tm_s, bm)
    y_pad = _expert(blk_e, used.reshape(1), xs, w, bm)
    return _combine(dest_flat, y_pad, x1, gates, gt2, w, tm_c, tokens_per_batch, alpha)


def _rope_tables(pos):
    half = QK_ROPE // 2
    inv = ROPE_THETA ** (-jnp.arange(half, dtype=F32) / half)
    ang = pos.astype(F32)[:, None] * inv[None, :]
    cos, sin = jnp.cos(ang), jnp.sin(ang)
    cos32 = jnp.concatenate([cos, cos], axis=1)
    sin32 = jnp.concatenate([-sin, sin], axis=1)
    return jnp.tile(cos32, (1, LANES // QK_ROPE)), jnp.tile(sin32, (1, LANES // QK_ROPE))


def _swap_halves(w32):
    shp = w32.shape
    w = w32.reshape(shp[:-1] + (shp[-1] // QK_ROPE, 2, QK_ROPE // 2))
    return w[..., ::-1, :].reshape(shp)


def _prep_weights(l, w_in, conv_w, g_qa, w_qb, g_kva, w_kvb, g_out_conv, g_out_attn, w_out,
                  ln1_g, ln1_b, router_w, router_b, w_gu, b_gu, w_down, b_down, ln2_g, ln2_b):
    w = {}
    wi = w_in[l]
    k_r = wi[:, _O_KR:_O_KR + QK_ROPE]
    rep = LANES // QK_ROPE
    w["w_in"] = jnp.concatenate([wi[:, :_O_KR], jnp.tile(k_r, (1, rep)), jnp.tile(_swap_halves(k_r), (1, rep))],
                                axis=1).astype(BF16)
    w["conv_w"] = conv_w[l]
    w["g_qa"] = g_qa[l].reshape(1, Q_LORA)
    w["g_kva"] = g_kva[l].reshape(1, KV_LORA)
    w["g_oc"] = g_out_conv[l].reshape(1, CONV_WIDTH)
    w["g_oa"] = g_out_attn[l].reshape(1, ATTN_WIDTH)
    wq = w_qb[l].reshape(Q_LORA, N_HEADS, QK_NOPE + QK_ROPE)
    w["wq_nope"] = wq[:, :, :QK_NOPE].reshape(Q_LORA, N_HEADS * QK_NOPE).astype(BF16)
    wq_rope = wq[:, :, QK_NOPE:].reshape(Q_LORA, N_HEADS * QK_ROPE)
    w["wq_rope"] = wq_rope.astype(BF16)
    w["wq_rope_sw"] = _swap_halves(wq_rope).astype(BF16)
    w_uk = jnp.transpose(w_kvb[l][:, :, :QK_NOPE], (1, 2, 0))
    w_uv = jnp.transpose(w_kvb[l][:, :, QK_NOPE:], (1, 0, 2))
    zk = jnp.zeros((QK_NOPE, KV_LORA), F32)
    zv = jnp.zeros((KV_LORA, V_HEAD), F32)
    w["w_uk_pair"] = jnp.stack([
        jnp.concatenate([jnp.concatenate([w_uk[2 * p], zk], axis=1),
                         jnp.concatenate([zk, w_uk[2 * p + 1]], axis=1)], axis=0)
        for p in range(N_HEADS // 2)]).astype(BF16)
    w["w_uv_pair"] = jnp.stack([
        jnp.concatenate([jnp.concatenate([w_uv[2 * p], zv], axis=1),
                         jnp.concatenate([zv, w_uv[2 * p + 1]], axis=1)], axis=0)
        for p in range(N_HEADS // 2)]).astype(BF16)
    w["w_out"] = w_out[l].astype(BF16)
    w["ln1_g"] = ln1_g[l].reshape(1, D_MODEL)
    w["ln1_b"] = ln1_b[l].reshape(1, D_MODEL)
    w["ln2_g"] = ln2_g[l].reshape(1, D_MODEL)
    w["ln2_b"] = ln2_b[l].reshape(1, D_MODEL)
    w["router_w"] = jnp.pad(router_w[l], ((0, 0), (0, LANES - N_EXPERTS))).astype(BF16)
    w["router_b"] = jnp.concatenate([router_b[l], jnp.full((LANES - N_EXPERTS,), NEG, F32)]).reshape(1, LANES)
    w["w_gu"] = w_gu[l]
    w["b_gu"] = b_gu[l].reshape(N_EXPERTS, 1, 2 * D_FF)
    w["w_down"] = w_down[l]
    w["b_down"] = b_down[l].reshape(N_EXPERTS, 1, D_MODEL)
    return w


def _layer(x, mod, conv_prev, past, pos0, w, alpha, *, tm_in, tm_out, bm, tm_s, tm_c, tq=128, tk=512):
    B, T, _ = x.shape
    sh1, sc1, gt1, sh2, sc2, gt2 = [mod[:, None, i * D_MODEL:(i + 1) * D_MODEL] for i in range(N_MOD)]
    cos_t, sin_t = _rope_tables(pos0 + jnp.arange(T, dtype=jnp.int32))
    a_n, qlat, qrope, kcat, ckv, krope, conv_new = _inproj(x, sc1, sh1, conv_prev, cos_t, sin_t, w, tm_in)
    if past is None:
        b_n = _attn_prompt(qlat, qrope, kcat, w, tq, tk)
    else:
        b_n = _attn_sample(qlat, qrope, kcat, past[0], past[1], w, tk)
    x1, h2, meta, gates, cnt = _outproj(x, a_n, b_n, gt1, sc2, sh2, w, tm_out, alpha)
    N = B * T
    counts = cnt[0, :N_EXPERTS].astype(jnp.int32)
    y = _moe(h2.reshape(N, D_MODEL), x1.reshape(N, D_MODEL), meta.reshape(N, LANES), gates.reshape(N, LANES),
             counts, gt2, w, T, bm, tm_s, tm_c, alpha)
    return y.reshape(B, T, D_MODEL), ckv, krope, conv_new


def kernel(x_prompt, x_sample, c_prompt, c_sample, cache_ckv, cache_krope, state_conv, w_ada, b_ada, w_in, conv_w, g_qa, w_qb, g_kva, w_kvb, g_out_conv, g_out_attn, w_out, ln1_g, ln1_b, router_w, router_b, w_gu, b_gu, w_down, b_down, ln2_g, ln2_b):
    depth = w_ada.shape[0]
    Bp, Tp, _ = x_prompt.shape
    Bs, Ts, _ = x_sample.shape
    past_len = cache_ckv.shape[2]
    assert Ts == CHUNK and past_len % CHUNK == 0 and Tp % 512 == 0
    alpha = (2.0 * depth) ** 0.25
    xp, xs = x_prompt, x_sample
    outs = [[] for _ in range(6)]
    c_all = jnp.concatenate([c_prompt, c_sample, jnp.zeros((16 - Bp - Bs, D_MODEL), F32)], axis=0)
    for l in range(depth):
        w = _prep_weights(l, w_in, conv_w, g_qa, w_qb, g_kva, w_kvb, g_out_conv, g_out_attn, w_out,
                          ln1_g, ln1_b, router_w, router_b, w_gu, b_gu, w_down, b_down, ln2_g, ln2_b)
        mod = _ada(c_all, w_ada[l], b_ada[l])
        xp, ckv_p, kr_p, cv_p = _layer(xp, mod[:Bp], jnp.zeros((Bp, CONV_K - 1, CONV_WIDTH), F32), None, 0, w, alpha,
                                       tm_in=512, tm_out=256, bm=256, tm_s=256, tm_c=128)
        xs, ckv_s, kr_s, cv_s = _layer(xs, mod[Bp:Bp + Bs], state_conv[l], (cache_ckv[l], cache_krope[l]),
                                       past_len, w, alpha, tm_in=Ts, tm_out=Ts, bm=128, tm_s=128, tm_c=Ts)
        for o, v in zip(outs, (ckv_p, kr_p, cv_p, ckv_s, kr_s, cv_s)):
            o.append(v)
    return (xp, xs) + tuple(jnp.stack(o) for o in outs)
```

```python
import functools
import math

import jax
import jax.numpy as jnp
from jax import lax
from jax.experimental import pallas as pl
from jax.experimental.pallas import tpu as pltpu

F32 = jnp.float32
BF16 = jnp.bfloat16

D_MODEL = 1024
CHUNK = 64
CONV_WIDTH = 512
CONV_K = 3
N_HEADS = 8
QK_NOPE = 64
QK_ROPE = 32
V_HEAD = 64
Q_LORA = 256
KV_LORA = 128
ATTN_WIDTH = N_HEADS * V_HEAD
ROPE_THETA = 10000.0
ATTN_SCALE = 1.0 / math.sqrt(QK_NOPE + QK_ROPE)
Q_SCALE = ATTN_SCALE * math.log2(math.e)
N_EXPERTS = 32
TOP_K = 4
D_FF = 1024
SWIGLU_LIMIT = 7.0
SWIGLU_ALPHA = 1.702
N_MOD = 6
LN_EPS = 1e-5
RMS_EPS = 1e-6

LANES = 128
ATTN_GROUPS = 1
NEG = -1e30
VMEM_LIMIT = 56 * 1024 * 1024

_O_XB, _O_XC, _O_XV = 0, CONV_WIDTH, 2 * CONV_WIDTH
_O_QA = 3 * CONV_WIDTH
_O_KVA = _O_QA + Q_LORA
_O_KR = _O_KVA + KV_LORA
_O_KRS = _O_KR + LANES
IN_COLS_EXT = _O_KRS + LANES


def _cparams(sem):
    return pltpu.CompilerParams(dimension_semantics=sem, vmem_limit_bytes=VMEM_LIMIT)


def _layernorm(x):
    mu = jnp.mean(x, axis=-1, keepdims=True)
    xc = x - mu
    var = jnp.mean(xc * xc, axis=-1, keepdims=True)
    return xc * lax.rsqrt(var + LN_EPS)


def _rms(x):
    return x * lax.rsqrt(jnp.mean(x * x, axis=-1, keepdims=True) + RMS_EPS)


def _sigmoid(x):
    return 1.0 / (1.0 + jnp.exp(-x))


def _ada_kernel(c_ref, w_ref, b_ref, o_ref):
    c = c_ref[...]
    s = (c * _sigmoid(c)).astype(BF16)
    o_ref[...] = jnp.dot(s, w_ref[...].astype(BF16), preferred_element_type=F32) + b_ref[...]


def _ada(c_all, w_ada, b_ada):
    rows = c_all.shape[0]
    ncol = w_ada.shape[1]
    tn = 1024
    return pl.pallas_call(
        _ada_kernel,
        grid=(ncol // tn,),
        in_specs=[pl.BlockSpec((rows, D_MODEL), lambda j: (0, 0)),
                  pl.BlockSpec((D_MODEL, tn), lambda j: (0, j)),
                  pl.BlockSpec((1, tn), lambda j: (0, j))],
        out_specs=pl.BlockSpec((rows, tn), lambda j: (0, j)),
        out_shape=jax.ShapeDtypeStruct((rows, ncol), F32),
        compiler_params=_cparams(("arbitrary",)),
        name="ada",
    )(c_all, w_ada, b_ada.reshape(1, ncol))


def _inproj_kernel(x_ref, sc_ref, sh_ref, win_ref, cw_ref, cprev_ref, gqa_ref, gkva_ref, goc_ref,
                   wqn_ref, wqr_ref, wqrs_ref, wuk_ref, cos_ref, sin_ref,
                   an_ref, qlat_ref, qrope_ref, kcat_ref, ckv_ref, krope_ref, cnew_ref,
                   ubuf, *, tm):
    j = pl.program_id(1)
    x = x_ref[0]
    h = _layernorm(x) * (1.0 + sc_ref[0]) + sh_ref[0]
    proj = jnp.dot(h.astype(BF16), win_ref[...], preferred_element_type=F32)
    xb = proj[:, _O_XB:_O_XB + CONV_WIDTH]
    xc = proj[:, _O_XC:_O_XC + CONV_WIDTH]
    xv = proj[:, _O_XV:_O_XV + CONV_WIDTH]
    q_a = proj[:, _O_QA:_O_QA + Q_LORA]
    kv_a = proj[:, _O_KVA:_O_KVA + KV_LORA]
    kr4 = proj[:, _O_KR:_O_KR + LANES]
    kr4s = proj[:, _O_KRS:_O_KRS + LANES]

    u = xc * xv

    @pl.when(j == 0)
    def _():
        ubuf[6:8, :] = cprev_ref[0]

    ubuf[8:8 + tm, :] = u
    conv = (cw_ref[0:1, :] * ubuf[6:6 + tm, :] + cw_ref[1:2, :] * ubuf[7:7 + tm, :]
            + cw_ref[2:3, :] * u)
    ubuf[0:8, :] = ubuf[tm:tm + 8, :]
    cnew_ref[0] = u[tm - (CONV_K - 1):tm, :]
    an_ref[0] = (_rms(xb * conv) * goc_ref[...]).astype(BF16)

    cos = cos_ref[...]
    sin = sin_ref[...]

    ckv = _rms(kv_a) * gkva_ref[...]
    kro4 = kr4 * cos + kr4s * sin
    ckv_ref[0] = ckv
    krope_ref[0] = kro4[:, :QK_ROPE]
    kcat_ref[0] = jnp.concatenate([ckv, kro4], axis=1).astype(BF16)

    qn = (_rms(q_a) * gqa_ref[...]).astype(BF16)
    q_nope = jnp.dot(qn, wqn_ref[...], preferred_element_type=F32)
    xr = jnp.dot(qn, wqr_ref[...], preferred_element_type=F32)
    xrs = jnp.dot(qn, wqrs_ref[...], preferred_element_type=F32)
    for g in range(2):
        sl = slice(g * LANES, (g + 1) * LANES)
        qrope_ref[0, :, sl] = ((xr[:, sl] * cos + xrs[:, sl] * sin) * Q_SCALE).astype(BF16)
    for p in range(N_HEADS // 2):
        qp = q_nope[:, p * LANES:(p + 1) * LANES].astype(BF16)
        ql = jnp.dot(qp, wuk_ref[p], preferred_element_type=F32)
        qlat_ref[0, :, p * 2 * KV_LORA:(p + 1) * 2 * KV_LORA] = (ql * Q_SCALE).astype(BF16)


def _inproj(x, sc1, sh1, conv_prev, cos_t, sin_t, w, tm):
    B, T, _ = x.shape
    nt = T // tm
    full = lambda shp: pl.BlockSpec(shp, lambda b, j: (0,) * len(shp))
    out_shapes = (
        jax.ShapeDtypeStruct((B, T, CONV_WIDTH), BF16),
        jax.ShapeDtypeStruct((B, T, N_HEADS * KV_LORA), BF16),
        jax.ShapeDtypeStruct((B, T, 2 * LANES), BF16),
        jax.ShapeDtypeStruct((B, T, 2 * LANES), BF16),
        jax.ShapeDtypeStruct((B, T, KV_LORA), F32),
        jax.ShapeDtypeStruct((B, T, QK_ROPE), F32),
        jax.ShapeDtypeStruct((B, CONV_K - 1, CONV_WIDTH), F32),
    )
    tile = lambda c: pl.BlockSpec((1, tm, c), lambda b, j: (b, j, 0))
    return pl.pallas_call(
        functools.partial(_inproj_kernel, tm=tm),
        grid=(B, nt),
        in_specs=[
            tile(D_MODEL),
            pl.BlockSpec((1, 1, D_MODEL), lambda b, j: (b, 0, 0)),
            pl.BlockSpec((1, 1, D_MODEL), lambda b, j: (b, 0, 0)),
            full((D_MODEL, IN_COLS_EXT)),
            full((CONV_K, CONV_WIDTH)),
            pl.BlockSpec((1, CONV_K - 1, CONV_WIDTH), lambda b, j: (b, 0, 0)),
            full((1, Q_LORA)), full((1, KV_LORA)), full((1, CONV_WIDTH)),
            full((Q_LORA, N_HEADS * QK_NOPE)), full((Q_LORA, 2 * LANES)), full((Q_LORA, 2 * LANES)),
            full((N_HEADS // 2, LANES, 2 * KV_LORA)),
            pl.BlockSpec((tm, LANES), lambda b, j: (j, 0)),
            pl.BlockSpec((tm, LANES), lambda b, j: (j, 0)),
        ],
        out_specs=(tile(CONV_WIDTH), tile(N_HEADS * KV_LORA), tile(2 * LANES), tile(2 * LANES),
                   tile(KV_LORA), tile(QK_ROPE),
                   pl.BlockSpec((1, CONV_K - 1, CONV_WIDTH), lambda b, j: (b, 0, 0))),
        out_shape=out_shapes,
        scratch_shapes=[pltpu.VMEM((tm + 8, CONV_WIDTH), F32)],
        compiler_params=_cparams(("arbitrary", "arbitrary")),
        name="inproj",
    )(x, sc1, sh1, w["w_in"], w["conv_w"], conv_prev, w["g_qa"], w["g_kva"], w["g_oc"],
      w["wq_nope"], w["wq_rope"], w["wq_rope_sw"], w["w_uk_pair"], cos_t, sin_t)


def _stack_queries(qlat_ref, qrope_ref, qs, tq):
    lane = lax.broadcasted_iota(jnp.int32, (tq, LANES), 1)
    for h in range(N_HEADS):
        g, i = divmod(h, 4)
        rope = qrope_ref[0, :, g * LANES:(g + 1) * LANES]
        keep = (lane >= i * QK_ROPE) & (lane < (i + 1) * QK_ROPE)
        qs[h * tq:(h + 1) * tq, 0:KV_LORA] = qlat_ref[0, :, h * KV_LORA:(h + 1) * KV_LORA]
        qs[h * tq:(h + 1) * tq, KV_LORA:KV_LORA + LANES] = jnp.where(keep, rope, jnp.zeros_like(rope))


def _softmax_step(qs, k, v, m_s, l_s, acc_s, mask=None, groups=1):
    tk = k.shape[0]
    rows = qs.shape[0] // groups
    for g in range(groups):
        r = slice(g * rows, (g + 1) * rows)
        s = lax.dot_general(qs[r, :], k, (((1,), (1,)), ((), ())), preferred_element_type=F32)
        if mask is not None:
            col, limit = mask
            s = jnp.where(col < limit[r], s, NEG)
        m_prev = m_s[r, :]
        m_new = jnp.maximum(m_prev, jnp.max(s, axis=1, keepdims=True))
        alpha = jnp.exp2(m_prev - m_new)
        if tk % LANES == 0:
            p = jnp.exp2(s - jnp.tile(m_new, (1, tk // LANES)))
        else:
            p = jnp.exp2(s - m_new[:, :tk])
        l_s[r, :] = alpha * l_s[r, :] + jnp.sum(p, axis=1, keepdims=True)
        acc_s[r, :] = alpha * acc_s[r, :] + jnp.dot(p.astype(BF16), v, preferred_element_type=F32)
        m_s[r, :] = m_new


def _attn_epilogue(acc_s, l_s, wuv_ref, g_ref, o_ref, tq):
    o = acc_s[...] / l_s[...]
    parts = []
    for p in range(N_HEADS // 2):
        op = jnp.concatenate([o[(2 * p) * tq:(2 * p + 1) * tq], o[(2 * p + 1) * tq:(2 * p + 2) * tq]], axis=1)
        parts.append(jnp.dot(op.astype(BF16), wuv_ref[p], preferred_element_type=F32))
    b = jnp.concatenate(parts, axis=1)
    o_ref[0] = (_rms(b) * g_ref[...]).astype(BF16)


def _attn_prompt_kernel(qlat_ref, qrope_ref, k_ref, wuv_ref, g_ref, o_ref, qs, m_s, l_s, acc_s, *, tq, tk,
                        groups):
    i = pl.program_id(1)
    M = N_HEADS * tq
    _stack_queries(qlat_ref, qrope_ref, qs, tq)
    m_s[...] = jnp.full(m_s.shape, NEG, F32)
    l_s[...] = jnp.zeros(l_s.shape, F32)
    acc_s[...] = jnp.zeros(acc_s.shape, F32)
    q0 = i * tq
    n_full = (q0 + CHUNK) // tk

    def body(t, carry):
        start = pl.multiple_of(t * tk, tk)
        k = k_ref[0, pl.ds(start, tk), :]
        _softmax_step(qs, k, k[:, :KV_LORA], m_s, l_s, acc_s, groups=groups)
        return carry

    lax.fori_loop(0, n_full, body, 0)

    start = pl.multiple_of(n_full * tk, tk)
    k = k_ref[0, pl.ds(start, tk), :]
    col = start + lax.broadcasted_iota(jnp.int32, (1, tk), 1)
    row_t = jnp.bitwise_and(lax.broadcasted_iota(jnp.int32, (M, 1), 0), tq - 1)
    limit = q0 + (jnp.right_shift(row_t, CHUNK.bit_length() - 1) + 1) * CHUNK
    _softmax_step(qs, k, k[:, :KV_LORA], m_s, l_s, acc_s, mask=(col, limit), groups=groups)
    _attn_epilogue(acc_s, l_s, wuv_ref, g_ref, o_ref, tq)


def _attn_prompt(qlat, qrope, kcat, w, tq, tk):
    B, T, _ = qlat.shape
    M = N_HEADS * tq
    return pl.pallas_call(
        functools.partial(_attn_prompt_kernel, tq=tq, tk=tk, groups=ATTN_GROUPS),
        grid=(B, T // tq),
        in_specs=[
            pl.BlockSpec((1, tq, N_HEADS * KV_LORA), lambda b, i: (b, i, 0)),
            pl.BlockSpec((1, tq, 2 * LANES), lambda b, i: (b, i, 0)),
            pl.BlockSpec((1, T, 2 * LANES), lambda b, i: (b, 0, 0)),
            pl.BlockSpec((N_HEADS // 2, 2 * KV_LORA, LANES), lambda b, i: (0, 0, 0)),
            pl.BlockSpec((1, ATTN_WIDTH), lambda b, i: (0, 0)),
        ],
        out_specs=pl.BlockSpec((1, tq, ATTN_WIDTH), lambda b, i: (b, i, 0)),
        out_shape=jax.ShapeDtypeStruct((B, T, ATTN_WIDTH), BF16),
        scratch_shapes=[pltpu.VMEM((M, 2 * LANES), BF16), pltpu.VMEM((M, LANES), F32),
                        pltpu.VMEM((M, LANES), F32), pltpu.VMEM((M, KV_LORA), F32)],
        compiler_params=_cparams(("arbitrary", "arbitrary")),
        name="attn_prompt",
    )(qlat, qrope, kcat, w["w_uv_pair"], w["g_oa"])


def _attn_sample_kernel(qlat_ref, qrope_ref, knew_ref, pckv_ref, pkr_ref, wuv_ref, g_ref, o_ref,
                        qs, m_s, l_s, acc_s, *, tq, tk, n_past):
    _stack_queries(qlat_ref, qrope_ref, qs, tq)
    m_s[...] = jnp.full(m_s.shape, NEG, F32)
    l_s[...] = jnp.zeros(l_s.shape, F32)
    acc_s[...] = jnp.zeros(acc_s.shape, F32)

    def body(t, carry):
        start = pl.multiple_of(t * tk, tk)
        ck = pckv_ref[0, pl.ds(start, tk), :]
        kr = pkr_ref[0, pl.ds(start, tk), :]
        k = jnp.concatenate([ck, kr, kr, kr, kr], axis=1).astype(BF16)
        _softmax_step(qs, k, k[:, :KV_LORA], m_s, l_s, acc_s, groups=2)
        return carry

    lax.fori_loop(0, n_past // tk, body, 0)
    k = knew_ref[0]
    _softmax_step(qs, k, k[:, :KV_LORA], m_s, l_s, acc_s, groups=2)
    _attn_epilogue(acc_s, l_s, wuv_ref, g_ref, o_ref, tq)


def _attn_sample(qlat, qrope, kcat, past_ckv, past_krope, w, tk):
    B, T, _ = qlat.shape
    n_past = past_ckv.shape[1]
    M = N_HEADS * T
    per_b = lambda r, c: pl.BlockSpec((1, r, c), lambda b: (b, 0, 0))
    return pl.pallas_call(
        functools.partial(_attn_sample_kernel, tq=T, tk=tk, n_past=n_past),
        grid=(B,),
        in_specs=[per_b(T, N_HEADS * KV_LORA), per_b(T, 2 * LANES), per_b(T, 2 * LANES),
                  per_b(n_past, KV_LORA), per_b(n_past, QK_ROPE),
                  pl.BlockSpec((N_HEADS // 2, 2 * KV_LORA, LANES), lambda b: (0, 0, 0)),
                  pl.BlockSpec((1, ATTN_WIDTH), lambda b: (0, 0))],
        out_specs=per_b(T, ATTN_WIDTH),
        out_shape=jax.ShapeDtypeStruct((B, T, ATTN_WIDTH), BF16),
        scratch_shapes=[pltpu.VMEM((M, 2 * LANES), BF16), pltpu.VMEM((M, LANES), F32),
                        pltpu.VMEM((M, LANES), F32), pltpu.VMEM((M, KV_LORA), F32)],
        compiler_params=_cparams(("arbitrary",)),
        name="attn_sample",
    )(qlat, qrope, kcat, past_ckv, past_krope, w["w_uv_pair"], w["g_oa"])


def _outproj_kernel(x_ref, an_ref, bn_ref, gt1_ref, sc2_ref, sh2_ref, wo_ref, l1g_ref, l1b_ref,
                    rw_ref, rb_ref, x1_ref, h2_ref, meta_ref, gates_ref, cnt_ref, run_s,
                    *, tm, alpha):
    first = (pl.program_id(0) == 0) & (pl.program_id(1) == 0)

    @pl.when(first)
    def _():
        run_s[...] = jnp.zeros(run_s.shape, F32)

    m = (jnp.dot(an_ref[0], wo_ref[0:CONV_WIDTH, :], preferred_element_type=F32)
         + jnp.dot(bn_ref[0], wo_ref[CONV_WIDTH:, :], preferred_element_type=F32))
    x1 = _layernorm(alpha * x_ref[0] + gt1_ref[0] * m) * l1g_ref[...] + l1b_ref[...]
    x1_ref[0] = x1
    h2 = _layernorm(x1) * (1.0 + sc2_ref[0]) + sh2_ref[0]
    h2_ref[0] = h2

    logits = jnp.dot(h2.astype(BF16), rw_ref[...], preferred_element_type=F32) + rb_ref[...]
    lane = lax.broadcasted_iota(jnp.int32, (tm, LANES), 1)
    lane_f = lane.astype(F32)
    lg = logits
    vals, sels = [], []
    chosen = jnp.zeros((tm, LANES), F32)
    for _ in range(TOP_K):
        mx = jnp.max(lg, axis=1, keepdims=True)
        idx = jnp.min(jnp.where(lg == mx, lane_f, float(LANES)), axis=1, keepdims=True)
        sel = lane_f == idx
        vals.append(mx)
        sels.append((sel, idx))
        chosen = jnp.where(sel, 1.0, chosen)
        lg = jnp.where(sel, NEG, lg)

    es = [jnp.exp(v - vals[0]) for v in vals]
    denom = es[0] + es[1] + es[2] + es[3]

    r_i = lax.broadcasted_iota(jnp.int32, (tm, tm), 0)
    c_i = lax.broadcasted_iota(jnp.int32, (tm, tm), 1)
    tri = jnp.where(c_i < r_i, 1.0, 0.0).astype(BF16)
    before = jnp.dot(tri, chosen.astype(BF16), preferred_element_type=F32) + run_s[0:1, :]
    run_s[0:1, :] = run_s[0:1, :] + jnp.sum(chosen, axis=0, keepdims=True)
    cnt_ref[...] = jnp.broadcast_to(run_s[0:1, :], cnt_ref.shape)

    meta = jnp.zeros((tm, LANES), jnp.int32)
    gates = jnp.zeros((tm, LANES), F32)
    for k in range(TOP_K):
        sel, idx = sels[k]
        rank = jnp.sum(jnp.where(sel, before, 0.0), axis=1, keepdims=True).astype(jnp.int32)
        meta = jnp.where(lane == k, idx.astype(jnp.int32), meta)
        meta = jnp.where(lane == TOP_K + k, rank, meta)
        gates = jnp.where(lane == k, es[k] / denom, gates)
    meta_ref[0] = meta
    gates_ref[0] = gates


def _outproj(x, a_n, b_n, gt1, sc2, sh2, w, tm, alpha):
    B, T, _ = x.shape
    tile = lambda c: pl.BlockSpec((1, tm, c), lambda b, j: (b, j, 0))
    modv = pl.BlockSpec((1, 1, D_MODEL), lambda b, j: (b, 0, 0))
    full = lambda shp: pl.BlockSpec(shp, lambda b, j: (0,) * len(shp))
    return pl.pallas_call(
        functools.partial(_outproj_kernel, tm=tm, alpha=alpha),
        grid=(B, T // tm),
        in_specs=[tile(D_MODEL), tile(CONV_WIDTH), tile(ATTN_WIDTH), modv, modv, modv,
                  full((CONV_WIDTH + ATTN_WIDTH, D_MODEL)), full((1, D_MODEL)), full((1, D_MODEL)),
                  full((D_MODEL, LANES)), full((1, LANES))],
        out_specs=(tile(D_MODEL), tile(D_MODEL), tile(LANES), tile(LANES),
                   pl.BlockSpec((8, LANES), lambda b, j: (0, 0))),
        out_shape=(jax.ShapeDtypeStruct((B, T, D_MODEL), F32), jax.ShapeDtypeStruct((B, T, D_MODEL), F32),
                   jax.ShapeDtypeStruct((B, T, LANES), jnp.int32), jax.ShapeDtypeStruct((B, T, LANES), F32),
                   jax.ShapeDtypeStruct((8, LANES), F32)),
        scratch_shapes=[pltpu.VMEM((8, LANES), F32)],
        compiler_params=_cparams(("arbitrary", "arbitrary")),
        name="outproj",
    )(x, a_n, b_n, gt1, sc2, sh2, w["w_out"], w["ln1_g"], w["ln1_b"], w["router_w"], w["router_b"])


def _scatter_kernel(dest_ref, zrow_ref, h_ref, xs_ref, zbuf, sem, *, tm, bm):
    i = pl.program_id(0)

    @pl.when(i == 0)
    def _():
        zbuf[...] = jnp.zeros(zbuf.shape, F32)
        nblk = xs_ref.shape[0] // bm
        used = zrow_ref[N_EXPERTS]

        def zcopy(r):
            return pltpu.make_async_copy(zbuf, xs_ref.at[pl.ds(pl.multiple_of(r, bm), bm)], sem.at[0])

        def zstart(e, c):
            @pl.when(zrow_ref[e] >= 0)
            def _():
                zcopy(zrow_ref[e]).start()
            return c

        def zwait(e, c):
            @pl.when(zrow_ref[e] >= 0)
            def _():
                zcopy(0).wait()
            return c

        lax.fori_loop(0, N_EXPERTS, zstart, 0)
        lax.fori_loop(used, nblk, lambda b, c: (zcopy(b * bm).start(), c)[1], 0)
        lax.fori_loop(0, N_EXPERTS, zwait, 0)
        lax.fori_loop(used, nblk, lambda b, c: (zcopy(0).wait(), c)[1], 0)

    base = i * (tm * TOP_K)

    def issue(r, c):
        for k in range(TOP_K):
            d = dest_ref[base + r * TOP_K + k]
            pltpu.make_async_copy(h_ref.at[pl.ds(r, 1)], xs_ref.at[pl.ds(d, 1)], sem.at[1]).start()
        return c

    lax.fori_loop(0, tm, issue, 0)
    for k in range(TOP_K):
        pltpu.make_async_copy(h_ref, xs_ref.at[pl.ds(0, tm)], sem.at[1]).wait()


def _scatter(dest_flat, zrow, h2, n_pad, tm, bm):
    N = h2.shape[0]
    return pl.pallas_call(
        functools.partial(_scatter_kernel, tm=tm, bm=bm),
        grid_spec=pltpu.PrefetchScalarGridSpec(
            num_scalar_prefetch=2,
            grid=(N // tm,),
            in_specs=[pl.BlockSpec((tm, D_MODEL), lambda i, d, z: (i, 0))],
            out_specs=pl.BlockSpec(memory_space=pl.ANY),
            scratch_shapes=[pltpu.VMEM((bm, D_MODEL), F32), pltpu.SemaphoreType.DMA((2,))],
        ),
        out_shape=jax.ShapeDtypeStruct((n_pad, D_MODEL), F32),
        compiler_params=_cparams(("arbitrary",)),
        name="moe_scatter",
    )(dest_flat, zrow, h2)


def _expert_kernel(blk_e_ref, used_ref, xs_ref, wgu_ref, bgu_ref, wd_ref, bd_ref, y_ref, wgu_bf, wd_bf):
    i = pl.program_id(0)
    prev = blk_e_ref[jnp.maximum(i - 1, 0)]
    fresh = (i == 0) | (blk_e_ref[i] != prev)

    @pl.when(fresh)
    def _():
        wgu_bf[...] = wgu_ref[0].astype(BF16)
        wd_bf[...] = wd_ref[0].astype(BF16)

    @pl.when(i < used_ref[0])
    def _():
        x = xs_ref[...].astype(BF16)
        gu = jnp.dot(x, wgu_bf[...], preferred_element_type=F32) + bgu_ref[0]
        g = jnp.minimum(gu[:, :D_FF], SWIGLU_LIMIT)
        lin = jnp.clip(gu[:, D_FF:], -SWIGLU_LIMIT, SWIGLU_LIMIT)
        act = g * _sigmoid(SWIGLU_ALPHA * g) * (lin + 1.0)
        y_ref[...] = jnp.dot(act.astype(BF16), wd_bf[...], preferred_element_type=F32) + bd_ref[0]

    @pl.when(i >= used_ref[0])
    def _():
        y_ref[...] = jnp.zeros(y_ref.shape, F32)


def _expert(blk_e, used, xs, w, bm):
    n_pad = xs.shape[0]
    nblk = n_pad // bm
    return pl.pallas_call(
        _expert_kernel,
        grid_spec=pltpu.PrefetchScalarGridSpec(
            num_scalar_prefetch=2,
            grid=(nblk,),
            in_specs=[
                pl.BlockSpec((bm, D_MODEL), lambda i, be, u: (jnp.minimum(i, u[0] - 1), 0)),
                pl.BlockSpec((1, D_MODEL, 2 * D_FF), lambda i, be, u: (be[i], 0, 0)),
                pl.BlockSpec((1, 1, 2 * D_FF), lambda i, be, u: (be[i], 0, 0)),
                pl.BlockSpec((1, D_FF, D_MODEL), lambda i, be, u: (be[i], 0, 0)),
                pl.BlockSpec((1, 1, D_MODEL), lambda i, be, u: (be[i], 0, 0)),
            ],
            out_specs=pl.BlockSpec((bm, D_MODEL), lambda i, be, u: (i, 0)),
            scratch_shapes=[pltpu.VMEM((D_MODEL, 2 * D_FF), BF16), pltpu.VMEM((D_FF, D_MODEL), BF16)],
        ),
        out_shape=jax.ShapeDtypeStruct((n_pad, D_MODEL), F32),
        compiler_params=_cparams(("arbitrary",)),
        name="moe_expert",
    )(blk_e, used, xs, w["w_gu"], w["b_gu"], w["w_down"], w["b_down"])


def _combine_kernel(dest_ref, y_ref, x1_ref, gates_ref, gt2_ref, l2g_ref, l2b_ref, o_ref, gbuf, sem,
                    *, tm, alpha):
    i = pl.program_id(0)
    base = i * (tm * TOP_K)

    def issue(r, c):
        for k in range(TOP_K):
            d = dest_ref[base + r * TOP_K + k]
            pltpu.make_async_copy(y_ref.at[pl.ds(d, 1)], gbuf.at[k, pl.ds(r, 1)], sem.at[0]).start()
        return c

    lax.fori_loop(0, tm, issue, 0)
    for k in range(TOP_K):
        pltpu.make_async_copy(y_ref.at[pl.ds(0, tm)], gbuf.at[k], sem.at[0]).wait()

    gates = gates_ref[...]
    f = gates[:, 0:1] * gbuf[0]
    for k in range(1, TOP_K):
        f = f + gates[:, k:k + 1] * gbuf[k]
    o_ref[...] = _layernorm(alpha * x1_ref[...] + gt2_ref[0] * f) * l2g_ref[...] + l2b_ref[...]


def _combine(dest_flat, y_pad, x1, gates, gt2, w, tm, tokens_per_batch, alpha):
    N = x1.shape[0]
    per_b = tokens_per_batch // tm
    return pl.pallas_call(
        functools.partial(_combine_kernel, tm=tm, alpha=alpha),
        grid_spec=pltpu.PrefetchScalarGridSpec(
            num_scalar_prefetch=1,
            grid=(N // tm,),
            in_specs=[
                pl.BlockSpec(memory_space=pl.ANY),
                pl.BlockSpec((tm, D_MODEL), lambda i, d: (i, 0)),
                pl.BlockSpec((tm, LANES), lambda i, d: (i, 0)),
                pl.BlockSpec((1, 1, D_MODEL), lambda i, d: (i // per_b, 0, 0)),
                pl.BlockSpec((1, D_MODEL), lambda i, d: (0, 0)),
                pl.BlockSpec((1, D_MODEL), lambda i, d: (0, 0)),
            ],
            out_specs=pl.BlockSpec((tm, D_MODEL), lambda i, d: (i, 0)),
            scratch_shapes=[pltpu.VMEM((TOP_K, tm, D_MODEL), F32), pltpu.SemaphoreType.DMA((1,))],
        ),
        out_shape=jax.ShapeDtypeStruct((N, D_MODEL), F32),
        compiler_params=_cparams(("arbitrary",)),
        name="moe_combine",
    )(dest_flat, y_pad, x1, gates, gt2, w["ln2_g"], w["ln2_b"])


def _moe(h2, x1, meta, gates, counts, gt2, w, tokens_per_batch, bm, tm_s, tm_c, alpha):
    N = h2.shape[0]
    n_rows = N * TOP_K
    n_pad = (n_rows + N_EXPERTS * (bm - 1)) // bm * bm
    nblk = n_pad // bm
    idx = meta[:, :TOP_K]
    rank = meta[:, TOP_K:2 * TOP_K]
    padded = (counts + bm - 1) // bm * bm
    pad_end = jnp.cumsum(padded)
    start_pad = pad_end - padded
    experts = jnp.arange(N_EXPERTS, dtype=jnp.int32)
    start_of = jnp.sum(jnp.where(idx[..., None] == experts, start_pad, 0), axis=-1)
    dest_flat = (start_of + rank).reshape(-1).astype(jnp.int32)
    used = (pad_end[-1] // bm).astype(jnp.int32)
    blk = jnp.minimum(jnp.arange(nblk, dtype=jnp.int32), used - 1)
    blk_e = jnp.minimum(jnp.sum(pad_end[None, :] <= (blk * bm)[:, None], axis=1), N_EXPERTS - 1).astype(jnp.int32)
    zrow = jnp.concatenate([jnp.where(padded > 0, pad_end - bm, -1), used.reshape(1)]).astype(jnp.int32)
    xs = _scatter(dest_flat, zrow, h2, n_pad, tm_s, bm)
    y_pad = _expert(blk_e, used.reshape(1), xs, w, bm)
    return _combine(dest_flat, y_pad, x1, gates, gt2, w, tm_c, tokens_per_batch, alpha)


def _rope_tables(pos):
    half = QK_ROPE // 2
    inv = ROPE_THETA ** (-jnp.arange(half, dtype=F32) / half)
    ang = pos.astype(F32)[:, None] * inv[None, :]
    cos, sin = jnp.cos(ang), jnp.sin(ang)
    cos32 = jnp.concatenate([cos, cos], axis=1)
    sin32 = jnp.concatenate([-sin, sin], axis=1)
    return jnp.tile(cos32, (1, LANES // QK_ROPE)), jnp.tile(sin32, (1, LANES // QK_ROPE))


def _swap_halves(w32):
    shp = w32.shape
    w = w32.reshape(shp[:-1] + (shp[-1] // QK_ROPE, 2, QK_ROPE // 2))
    return w[..., ::-1, :].reshape(shp)


def _prep_weights(l, w_in, conv_w, g_qa, w_qb, g_kva, w_kvb, g_out_conv, g_out_attn, w_out,
                  ln1_g, ln1_b, router_w, router_b, w_gu, b_gu, w_down, b_down, ln2_g, ln2_b):
    w = {}
    wi = w_in[l]
    k_r = wi[:, _O_KR:_O_KR + QK_ROPE]
    rep = LANES // QK_ROPE
    w["w_in"] = jnp.concatenate([wi[:, :_O_KR], jnp.tile(k_r, (1, rep)), jnp.tile(_swap_halves(k_r), (1, rep))],
                                axis=1).astype(BF16)
    w["conv_w"] = conv_w[l]
    w["g_qa"] = g_qa[l].reshape(1, Q_LORA)
    w["g_kva"] = g_kva[l].reshape(1, KV_LORA)
    w["g_oc"] = g_out_conv[l].reshape(1, CONV_WIDTH)
    w["g_oa"] = g_out_attn[l].reshape(1, ATTN_WIDTH)
    wq = w_qb[l].reshape(Q_LORA, N_HEADS, QK_NOPE + QK_ROPE)
    w["wq_nope"] = wq[:, :, :QK_NOPE].reshape(Q_LORA, N_HEADS * QK_NOPE).astype(BF16)
    wq_rope = wq[:, :, QK_NOPE:].reshape(Q_LORA, N_HEADS * QK_ROPE)
    w["wq_rope"] = wq_rope.astype(BF16)
    w["wq_rope_sw"] = _swap_halves(wq_rope).astype(BF16)
    w_uk = jnp.transpose(w_kvb[l][:, :, :QK_NOPE], (1, 2, 0))
    w_uv = jnp.transpose(w_kvb[l][:, :, QK_NOPE:], (1, 0, 2))
    zk = jnp.zeros((QK_NOPE, KV_LORA), F32)
    zv = jnp.zeros((KV_LORA, V_HEAD), F32)
    w["w_uk_pair"] = jnp.stack([
        jnp.concatenate([jnp.concatenate([w_uk[2 * p], zk], axis=1),
                         jnp.concatenate([zk, w_uk[2 * p + 1]], axis=1)], axis=0)
        for p in range(N_HEADS // 2)]).astype(BF16)
    w["w_uv_pair"] = jnp.stack([
        jnp.concatenate([jnp.concatenate([w_uv[2 * p], zv], axis=1),
                         jnp.concatenate([zv, w_uv[2 * p + 1]], axis=1)], axis=0)
        for p in range(N_HEADS // 2)]).astype(BF16)
    w["w_out"] = w_out[l].astype(BF16)
    w["ln1_g"] = ln1_g[l].reshape(1, D_MODEL)
    w["ln1_b"] = ln1_b[l].reshape(1, D_MODEL)
    w["ln2_g"] = ln2_g[l].reshape(1, D_MODEL)
    w["ln2_b"] = ln2_b[l].reshape(1, D_MODEL)
    w["router_w"] = jnp.pad(router_w[l], ((0, 0), (0, LANES - N_EXPERTS))).astype(BF16)
    w["router_b"] = jnp.concatenate([router_b[l], jnp.full((LANES - N_EXPERTS,), NEG, F32)]).reshape(1, LANES)
    w["w_gu"] = w_gu[l]
    w["b_gu"] = b_gu[l].reshape(N_EXPERTS, 1, 2 * D_FF)
    w["w_down"] = w_down[l]
    w["b_down"] = b_down[l].reshape(N_EXPERTS, 1, D_MODEL)
    return w


def _layer(x, mod, conv_prev, past, pos0, w, alpha, *, tm_in, tm_out, bm, tm_s, tm_c, tq=128, tk=512):
    B, T, _ = x.shape
    sh1, sc1, gt1, sh2, sc2, gt2 = [mod[:, None, i * D_MODEL:(i + 1) * D_MODEL] for i in range(N_MOD)]
    cos_t, sin_t = _rope_tables(pos0 + jnp.arange(T, dtype=jnp.int32))
    a_n, qlat, qrope, kcat, ckv, krope, conv_new = _inproj(x, sc1, sh1, conv_prev, cos_t, sin_t, w, tm_in)
    if past is None:
        b_n = _attn_prompt(qlat, qrope, kcat, w, tq, tk)
    else:
        b_n = _attn_sample(qlat, qrope, kcat, past[0], past[1], w, tk)
    x1, h2, meta, gates, cnt = _outproj(x, a_n, b_n, gt1, sc2, sh2, w, tm_out, alpha)
    N = B * T
    counts = cnt[0, :N_EXPERTS].astype(jnp.int32)
    y = _moe(h2.reshape(N, D_MODEL), x1.reshape(N, D_MODEL), meta.reshape(N, LANES), gates.reshape(N, LANES),
             counts, gt2, w, T, bm, tm_s, tm_c, alpha)
    return y.reshape(B, T, D_MODEL), ckv, krope, conv_new


def kernel(x_prompt, x_sample, c_prompt, c_sample, cache_ckv, cache_krope, state_conv, w_ada, b_ada, w_in, conv_w, g_qa, w_qb, g_kva, w_kvb, g_out_conv, g_out_attn, w_out, ln1_g, ln1_b, router_w, router_b, w_gu, b_gu, w_down, b_down, ln2_g, ln2_b):
    depth = w_ada.shape[0]
    Bp, Tp, _ = x_prompt.shape
    Bs, Ts, _ = x_sample.shape
    past_len = cache_ckv.shape[2]
    assert Ts == CHUNK and past_len % CHUNK == 0 and Tp % 512 == 0
    alpha = (2.0 * depth) ** 0.25
    xp, xs = x_prompt, x_sample
    outs = [[] for _ in range(6)]
    c_all = jnp.concatenate([c_prompt, c_sample, jnp.zeros((16 - Bp - Bs, D_MODEL), F32)], axis=0)
    for l in range(depth):
        w = _prep_weights(l, w_in, conv_w, g_qa, w_qb, g_kva, w_kvb, g_out_conv, g_out_attn, w_out,
                          ln1_g, ln1_b, router_w, router_b, w_gu, b_gu, w_down, b_down, ln2_g, ln2_b)
        mod = _ada(c_all, w_ada[l], b_ada[l])
        xp, ckv_p, kr_p, cv_p = _layer(xp, mod[:Bp], jnp.zeros((Bp, CONV_K - 1, CONV_WIDTH), F32), None, 0, w, alpha,
                                       tm_in=512, tm_out=256, bm=256, tm_s=256, tm_c=128)
        xs, ckv_s, kr_s, cv_s = _layer(xs, mod[Bp:Bp + Bs], state_conv[l], (cache_ckv[l], cache_krope[l]),
                                       past_len, w, alpha, tm_in=Ts, tm_out=Ts, bm=128, tm_s=128, tm_c=Ts)
        for o, v in zip(outs, (ckv_p, kr_p, cv_p, ckv_s, kr_s, cv_s)):
            o.append(v)
    return (xp, xs) + tuple(jnp.stack(o) for o in outs)
```

```python
import functools
import math

import jax
import jax.numpy as jnp
from jax import lax
from jax.experimental import pallas as pl
from jax.experimental.pallas import tpu as pltpu

F32 = jnp.float32
BF16 = jnp.bfloat16

D_MODEL = 1024
CHUNK = 64
CONV_WIDTH = 512
CONV_K = 3
N_HEADS = 8
QK_NOPE = 64
QK_ROPE = 32
V_HEAD = 64
Q_LORA = 256
KV_LORA = 128
ATTN_WIDTH = N_HEADS * V_HEAD
ROPE_THETA = 10000.0
ATTN_SCALE = 1.0 / math.sqrt(QK_NOPE + QK_ROPE)
Q_SCALE = ATTN_SCALE * math.log2(math.e)
N_EXPERTS = 32
TOP_K = 4
D_FF = 1024
SWIGLU_LIMIT = 7.0
SWIGLU_ALPHA = 1.702
N_MOD = 6
LN_EPS = 1e-5
RMS_EPS = 1e-6

LANES = 128
ATTN_GROUPS = 1
NEG = -1e30
VMEM_LIMIT = 56 * 1024 * 1024

_O_XB, _O_XC, _O_XV = 0, CONV_WIDTH, 2 * CONV_WIDTH
_O_QA = 3 * CONV_WIDTH
_O_KVA = _O_QA + Q_LORA
_O_KR = _O_KVA + KV_LORA
_O_KRS = _O_KR + LANES
IN_COLS_EXT = _O_KRS + LANES


def _cparams(sem):
    return pltpu.CompilerParams(dimension_semantics=sem, vmem_limit_bytes=VMEM_LIMIT)


def _layernorm(x):
    mu = jnp.mean(x, axis=-1, keepdims=True)
    xc = x - mu
    var = jnp.mean(xc * xc, axis=-1, keepdims=True)
    return xc * lax.rsqrt(var + LN_EPS)


def _rms(x):
    return x * lax.rsqrt(jnp.mean(x * x, axis=-1, keepdims=True) + RMS_EPS)


def _sigmoid(x):
    return 1.0 / (1.0 + jnp.exp(-x))


def _ada_kernel(c_ref, w_ref, b_ref, o_ref):
    c = c_ref[...]
    s = (c * _sigmoid(c)).astype(BF16)
    o_ref[...] = jnp.dot(s, w_ref[...].astype(BF16), preferred_element_type=F32) + b_ref[...]


def _ada(c_all, w_ada, b_ada):
    rows = c_all.shape[0]
    ncol = w_ada.shape[1]
    tn = 1024
    return pl.pallas_call(
        _ada_kernel,
        grid=(ncol // tn,),
        in_specs=[pl.BlockSpec((rows, D_MODEL), lambda j: (0, 0)),
                  pl.BlockSpec((D_MODEL, tn), lambda j: (0, j)),
                  pl.BlockSpec((1, tn), lambda j: (0, j))],
        out_specs=pl.BlockSpec((rows, tn), lambda j: (0, j)),
        out_shape=jax.ShapeDtypeStruct((rows, ncol), F32),
        compiler_params=_cparams(("arbitrary",)),
        name="ada",
    )(c_all, w_ada, b_ada.reshape(1, ncol))


def _inproj_kernel(x_ref, sc_ref, sh_ref, win_ref, cw_ref, cprev_ref, gqa_ref, gkva_ref, goc_ref,
                   wqn_ref, wqr_ref, wqrs_ref, wuk_ref, cos_ref, sin_ref,
                   an_ref, qlat_ref, qrope_ref, kcat_ref, ckv_ref, krope_ref, cnew_ref,
                   ubuf, *, tm):
    j = pl.program_id(1)
    x = x_ref[0]
    h = _layernorm(x) * (1.0 + sc_ref[0]) + sh_ref[0]
    proj = jnp.dot(h.astype(BF16), win_ref[...], preferred_element_type=F32)
    xb = proj[:, _O_XB:_O_XB + CONV_WIDTH]
    xc = proj[:, _O_XC:_O_XC + CONV_WIDTH]
    xv = proj[:, _O_XV:_O_XV + CONV_WIDTH]
    q_a = proj[:, _O_QA:_O_QA + Q_LORA]
    kv_a = proj[:, _O_KVA:_O_KVA + KV_LORA]
    kr4 = proj[:, _O_KR:_O_KR + LANES]
    kr4s = proj[:, _O_KRS:_O_KRS + LANES]

    u = xc * xv

    @pl.when(j == 0)
    def _():
        ubuf[6:8, :] = cprev_ref[0]

    ubuf[8:8 + tm, :] = u
    conv = (cw_ref[0:1, :] * ubuf[6:6 + tm, :] + cw_ref[1:2, :] * ubuf[7:7 + tm, :]
            + cw_ref[2:3, :] * u)
    ubuf[0:8, :] = ubuf[tm:tm + 8, :]
    cnew_ref[0] = u[tm - (CONV_K - 1):tm, :]
    an_ref[0] = (_rms(xb * conv) * goc_ref[...]).astype(BF16)

    cos = cos_ref[...]
    sin = sin_ref[...]

    ckv = _rms(kv_a) * gkva_ref[...]
    kro4 = kr4 * cos + kr4s * sin
    ckv_ref[0] = ckv
    krope_ref[0] = kro4[:, :QK_ROPE]
    kcat_ref[0] = jnp.concatenate([ckv, kro4], axis=1).astype(BF16)

    qn = (_rms(q_a) * gqa_ref[...]).astype(BF16)
    q_nope = jnp.dot(qn, wqn_ref[...], preferred_element_type=F32)
    xr = jnp.dot(qn, wqr_ref[...], preferred_element_type=F32)
    xrs = jnp.dot(qn, wqrs_ref[...], preferred_element_type=F32)
    for g in range(2):
        sl = slice(g * LANES, (g + 1) * LANES)
        qrope_ref[0, :, sl] = ((xr[:, sl] * cos + xrs[:, sl] * sin) * Q_SCALE).astype(BF16)
    for p in range(N_HEADS // 2):
        qp = q_nope[:, p * LANES:(p + 1) * LANES].astype(BF16)
        ql = jnp.dot(qp, wuk_ref[p], preferred_element_type=F32)
        qlat_ref[0, :, p * 2 * KV_LORA:(p + 1) * 2 * KV_LORA] = (ql * Q_SCALE).astype(BF16)


def _inproj(x, sc1, sh1, conv_prev, cos_t, sin_t, w, tm):
    B, T, _ = x.shape
    nt = T // tm
    full = lambda shp: pl.BlockSpec(shp, lambda b, j: (0,) * len(shp))
    out_shapes = (
        jax.ShapeDtypeStruct((B, T, CONV_WIDTH), BF16),
        jax.ShapeDtypeStruct((B, T, N_HEADS * KV_LORA), BF16),
        jax.ShapeDtypeStruct((B, T, 2 * LANES), BF16),
        jax.ShapeDtypeStruct((B, T, 2 * LANES), BF16),
        jax.ShapeDtypeStruct((B, T, KV_LORA), F32),
        jax.ShapeDtypeStruct((B, T, QK_ROPE), F32),
        jax.ShapeDtypeStruct((B, CONV_K - 1, CONV_WIDTH), F32),
    )
    tile = lambda c: pl.BlockSpec((1, tm, c), lambda b, j: (b, j, 0))
    return pl.pallas_call(
        functools.partial(_inproj_kernel, tm=tm),
        grid=(B, nt),
        in_specs=[
            tile(D_MODEL),
            pl.BlockSpec((1, 1, D_MODEL), lambda b, j: (b, 0, 0)),
            pl.BlockSpec((1, 1, D_MODEL), lambda b, j: (b, 0, 0)),
            full((D_MODEL, IN_COLS_EXT)),
            full((CONV_K, CONV_WIDTH)),
            pl.BlockSpec((1, CONV_K - 1, CONV_WIDTH), lambda b, j: (b, 0, 0)),
            full((1, Q_LORA)), full((1, KV_LORA)), full((1, CONV_WIDTH)),
            full((Q_LORA, N_HEADS * QK_NOPE)), full((Q_LORA, 2 * LANES)), full((Q_LORA, 2 * LANES)),
            full((N_HEADS // 2, LANES, 2 * KV_LORA)),
            pl.BlockSpec((tm, LANES), lambda b, j: (j, 0)),
            pl.BlockSpec((tm, LANES), lambda b, j: (j, 0)),
        ],
        out_specs=(tile(CONV_WIDTH), tile(N_HEADS * KV_LORA), tile(2 * LANES), tile(2 * LANES),
                   tile(KV_LORA), tile(QK_ROPE),
                   pl.BlockSpec((1, CONV_K - 1, CONV_WIDTH), lambda b, j: (b, 0, 0))),
        out_shape=out_shapes,
        scratch_shapes=[pltpu.VMEM((tm + 8, CONV_WIDTH), F32)],
        compiler_params=_cparams(("arbitrary", "arbitrary")),
        name="inproj",
    )(x, sc1, sh1, w["w_in"], w["conv_w"], conv_prev, w["g_qa"], w["g_kva"], w["g_oc"],
      w["wq_nope"], w["wq_rope"], w["wq_rope_sw"], w["w_uk_pair"], cos_t, sin_t)


def _stack_queries(qlat_ref, qrope_ref, qs, tq):
    lane = lax.broadcasted_iota(jnp.int32, (tq, LANES), 1)
    for h in range(N_HEADS):
        g, i = divmod(h, 4)
        rope = qrope_ref[0, :, g * LANES:(g + 1) * LANES]
        keep = (lane >= i * QK_ROPE) & (lane < (i + 1) * QK_ROPE)
        qs[h * tq:(h + 1) * tq, 0:KV_LORA] = qlat_ref[0, :, h * KV_LORA:(h + 1) * KV_LORA]
        qs[h * tq:(h + 1) * tq, KV_LORA:KV_LORA + LANES] = jnp.where(keep, rope, jnp.zeros_like(rope))


def _softmax_step(qs, k, v, m_s, l_s, acc_s, mask=None, groups=1):
    tk = k.shape[0]
    rows = qs.shape[0] // groups
    for g in range(groups):
        r = slice(g * rows, (g + 1) * rows)
        s = lax.dot_general(qs[r, :], k, (((1,), (1,)), ((), ())), preferred_element_type=F32)
        if mask is not None:
            col, limit = mask
            s = jnp.where(col < limit[r], s, NEG)
        m_prev = m_s[r, :]
        m_new = jnp.maximum(m_prev, jnp.max(s, axis=1, keepdims=True))
        alpha = jnp.exp2(m_prev - m_new)
        if tk % LANES == 0:
            p = jnp.exp2(s - jnp.tile(m_new, (1, tk // LANES)))
        else:
            p = jnp.exp2(s - m_new[:, :tk])
        l_s[r, :] = alpha * l_s[r, :] + jnp.sum(p, axis=1, keepdims=True)
        acc_s[r, :] = alpha * acc_s[r, :] + jnp.dot(p.astype(BF16), v, preferred_element_type=F32)
        m_s[r, :] = m_new


def _attn_epilogue(acc_s, l_s, wuv_ref, g_ref, o_ref, tq):
    o = acc_s[...] / l_s[...]
    parts = []
    for p in range(N_HEADS // 2):
        op = jnp.concatenate([o[(2 * p) * tq:(2 * p + 1) * tq], o[(2 * p + 1) * tq:(2 * p + 2) * tq]], axis=1)
        parts.append(jnp.dot(op.astype(BF16), wuv_ref[p], preferred_element_type=F32))
    b = jnp.concatenate(parts, axis=1)
    o_ref[0] = (_rms(b) * g_ref[...]).astype(BF16)


def _attn_prompt_kernel(qlat_ref, qrope_ref, k_ref, wuv_ref, g_ref, o_ref, qs, m_s, l_s, acc_s, *, tq, tk,
                        groups):
    i = pl.program_id(1)
    M = N_HEADS * tq
    _stack_queries(qlat_ref, qrope_ref, qs, tq)
    m_s[...] = jnp.full(m_s.shape, NEG, F32)
    l_s[...] = jnp.zeros(l_s.shape, F32)
    acc_s[...] = jnp.zeros(acc_s.shape, F32)
    q0 = i * tq
    n_full = (q0 + CHUNK) // tk

    def body(t, carry):
        start = pl.multiple_of(t * tk, tk)
        k = k_ref[0, pl.ds(start, tk), :]
        _softmax_step(qs, k, k[:, :KV_LORA], m_s, l_s, acc_s, groups=groups)
        return carry

    lax.fori_loop(0, n_full, body, 0)

    start = pl.multiple_of(n_full * tk, tk)
    k = k_ref[0, pl.ds(start, tk), :]
    col = start + lax.broadcasted_iota(jnp.int32, (1, tk), 1)
    row_t = jnp.bitwise_and(lax.broadcasted_iota(jnp.int32, (M, 1), 0), tq - 1)
    limit = q0 + (jnp.right_shift(row_t, CHUNK.bit_length() - 1) + 1) * CHUNK
    _softmax_step(qs, k, k[:, :KV_LORA], m_s, l_s, acc_s, mask=(col, limit), groups=groups)
    _attn_epilogue(acc_s, l_s, wuv_ref, g_ref, o_ref, tq)


def _attn_prompt(qlat, qrope, kcat, w, tq, tk):
    B, T, _ = qlat.shape
    M = N_HEADS * tq
    return pl.pallas_call(
        functools.partial(_attn_prompt_kernel, tq=tq, tk=tk, groups=ATTN_GROUPS),
        grid=(B, T // tq),
        in_specs=[
            pl.BlockSpec((1, tq, N_HEADS * KV_LORA), lambda b, i: (b, i, 0)),
            pl.BlockSpec((1, tq, 2 * LANES), lambda b, i: (b, i, 0)),
            pl.BlockSpec((1, T, 2 * LANES), lambda b, i: (b, 0, 0)),
            pl.BlockSpec((N_HEADS // 2, 2 * KV_LORA, LANES), lambda b, i: (0, 0, 0)),
            pl.BlockSpec((1, ATTN_WIDTH), lambda b, i: (0, 0)),
        ],
        out_specs=pl.BlockSpec((1, tq, ATTN_WIDTH), lambda b, i: (b, i, 0)),
        out_shape=jax.ShapeDtypeStruct((B, T, ATTN_WIDTH), BF16),
        scratch_shapes=[pltpu.VMEM((M, 2 * LANES), BF16), pltpu.VMEM((M, LANES), F32),
                        pltpu.VMEM((M, LANES), F32), pltpu.VMEM((M, KV_LORA), F32)],
        compiler_params=_cparams(("arbitrary", "arbitrary")),
        name="attn_prompt",
    )(qlat, qrope, kcat, w["w_uv_pair"], w["g_oa"])


def _attn_sample_kernel(qlat_ref, qrope_ref, knew_ref, pckv_ref, pkr_ref, wuv_ref, g_ref, o_ref,
                        qs, m_s, l_s, acc_s, *, tq, tk, n_past):
    _stack_queries(qlat_ref, qrope_ref, qs, tq)
    m_s[...] = jnp.full(m_s.shape, NEG, F32)
    l_s[...] = jnp.zeros(l_s.shape, F32)
    acc_s[...] = jnp.zeros(acc_s.shape, F32)

    def body(t, carry):
        start = pl.multiple_of(t * tk, tk)
        ck = pckv_ref[0, pl.ds(start, tk), :]
        kr = pkr_ref[0, pl.ds(start, tk), :]
        k = jnp.concatenate([ck, kr, kr, kr, kr], axis=1).astype(BF16)
        _softmax_step(qs, k, k[:, :KV_LORA], m_s, l_s, acc_s)
        return carry

    lax.fori_loop(0, n_past // tk, body, 0)
    k = knew_ref[0]
    _softmax_step(qs, k, k[:, :KV_LORA], m_s, l_s, acc_s)
    _attn_epilogue(acc_s, l_s, wuv_ref, g_ref, o_ref, tq)


def _attn_sample(qlat, qrope, kcat, past_ckv, past_krope, w, tk):
    B, T, _ = qlat.shape
    n_past = past_ckv.shape[1]
    M = N_HEADS * T
    per_b = lambda r, c: pl.BlockSpec((1, r, c), lambda b: (b, 0, 0))
    return pl.pallas_call(
        functools.partial(_attn_sample_kernel, tq=T, tk=tk, n_past=n_past),
        grid=(B,),
        in_specs=[per_b(T, N_HEADS * KV_LORA), per_b(T, 2 * LANES), per_b(T, 2 * LANES),
                  per_b(n_past, KV_LORA), per_b(n_past, QK_ROPE),
                  pl.BlockSpec((N_HEADS // 2, 2 * KV_LORA, LANES), lambda b: (0, 0, 0)),
                  pl.BlockSpec((1, ATTN_WIDTH), lambda b: (0, 0))],
        out_specs=per_b(T, ATTN_WIDTH),
        out_shape=jax.ShapeDtypeStruct((B, T, ATTN_WIDTH), BF16),
        scratch_shapes=[pltpu.VMEM((M, 2 * LANES), BF16), pltpu.VMEM((M, LANES), F32),
                        pltpu.VMEM((M, LANES), F32), pltpu.VMEM((M, KV_LORA), F32)],
        compiler_params=_cparams(("arbitrary",)),
        name="attn_sample",
    )(qlat, qrope, kcat, past_ckv, past_krope, w["w_uv_pair"], w["g_oa"])


def _outproj_kernel(x_ref, an_ref, bn_ref, gt1_ref, sc2_ref, sh2_ref, wo_ref, l1g_ref, l1b_ref,
                    rw_ref, rb_ref, x1_ref, h2_ref, meta_ref, gates_ref, cnt_ref, run_s,
                    *, tm, alpha):
    first = (pl.program_id(0) == 0) & (pl.program_id(1) == 0)

    @pl.when(first)
    def _():
        run_s[...] = jnp.zeros(run_s.shape, F32)

    m = (jnp.dot(an_ref[0], wo_ref[0:CONV_WIDTH, :], preferred_element_type=F32)
         + jnp.dot(bn_ref[0], wo_ref[CONV_WIDTH:, :], preferred_element_type=F32))
    x1 = _layernorm(alpha * x_ref[0] + gt1_ref[0] * m) * l1g_ref[...] + l1b_ref[...]
    x1_ref[0] = x1
    h2 = _layernorm(x1) * (1.0 + sc2_ref[0]) + sh2_ref[0]
    h2_ref[0] = h2

    logits = jnp.dot(h2.astype(BF16), rw_ref[...], preferred_element_type=F32) + rb_ref[...]
    lane = lax.broadcasted_iota(jnp.int32, (tm, LANES), 1)
    lane_f = lane.astype(F32)
    lg = logits
    vals, sels = [], []
    chosen = jnp.zeros((tm, LANES), F32)
    for _ in range(TOP_K):
        mx = jnp.max(lg, axis=1, keepdims=True)
        idx = jnp.min(jnp.where(lg == mx, lane_f, float(LANES)), axis=1, keepdims=True)
        sel = lane_f == idx
        vals.append(mx)
        sels.append(idx)
        chosen = jnp.where(sel, 1.0, chosen)
        lg = jnp.where(sel, NEG, lg)

    es = [jnp.exp(v - vals[0]) for v in vals]
    denom = es[0] + es[1] + es[2] + es[3]

    run_s[0:1, :] = run_s[0:1, :] + jnp.sum(chosen, axis=0, keepdims=True)
    cnt_ref[...] = jnp.broadcast_to(run_s[0:1, :], cnt_ref.shape)

    meta = jnp.zeros((tm, LANES), jnp.int32)
    gates = jnp.zeros((tm, LANES), F32)
    for k in range(TOP_K):
        meta = jnp.where(lane == k, sels[k].astype(jnp.int32), meta)
        gates = jnp.where(lane == k, es[k] / denom, gates)
    meta_ref[0] = meta
    gates_ref[0] = gates


def _outproj(x, a_n, b_n, gt1, sc2, sh2, w, tm, alpha):
    B, T, _ = x.shape
    tile = lambda c: pl.BlockSpec((1, tm, c), lambda b, j: (b, j, 0))
    modv = pl.BlockSpec((1, 1, D_MODEL), lambda b, j: (b, 0, 0))
    full = lambda shp: pl.BlockSpec(shp, lambda b, j: (0,) * len(shp))
    return pl.pallas_call(
        functools.partial(_outproj_kernel, tm=tm, alpha=alpha),
        grid=(B, T // tm),
        in_specs=[tile(D_MODEL), tile(CONV_WIDTH), tile(ATTN_WIDTH), modv, modv, modv,
                  full((CONV_WIDTH + ATTN_WIDTH, D_MODEL)), full((1, D_MODEL)), full((1, D_MODEL)),
                  full((D_MODEL, LANES)), full((1, LANES))],
        out_specs=(tile(D_MODEL), tile(D_MODEL), tile(LANES), tile(LANES),
                   pl.BlockSpec((8, LANES), lambda b, j: (0, 0))),
        out_shape=(jax.ShapeDtypeStruct((B, T, D_MODEL), F32), jax.ShapeDtypeStruct((B, T, D_MODEL), F32),
                   jax.ShapeDtypeStruct((B, T, LANES), jnp.int32), jax.ShapeDtypeStruct((B, T, LANES), F32),
                   jax.ShapeDtypeStruct((8, LANES), F32)),
        scratch_shapes=[pltpu.VMEM((8, LANES), F32)],
        compiler_params=_cparams(("arbitrary", "arbitrary")),
        name="outproj",
    )(x, a_n, b_n, gt1, sc2, sh2, w["w_out"], w["ln1_g"], w["ln1_b"], w["router_w"], w["router_b"])


def _expert_kernel(order_ref, blk_e_ref, blk_j0_ref, blk_nv_ref, nxt_e_ref, used_ref,
                   h_ref, wgu_hbm, bgu_ref, wd_hbm, bd_ref, y_ref,
                   x0, x1, y0, y1, wgu_st, wd_st, wgu_bf, wd_bf, gsem, ssem, wsem, *, bm, n_tok):
    s = pl.program_id(0)
    used = used_ref[0]
    dump0 = TOP_K * n_tok

    def weight_copies(e):
        return (pltpu.make_async_copy(wgu_hbm.at[e], wgu_st, wsem.at[0]),
                pltpu.make_async_copy(wd_hbm.at[e], wd_st, wsem.at[1]))

    def gather_start(j0, xbuf, sem, r):
        tok = jnp.right_shift(order_ref[j0 + r], 2)
        pltpu.make_async_copy(h_ref.at[pl.ds(tok, 1)], xbuf.at[pl.ds(r, 1)], sem).start()

    def scatter_start(j0, nv, ybuf, sem, r, odd):
        o = order_ref[j0 + r]
        dst = jnp.where(r < nv, jnp.bitwise_and(o, TOP_K - 1) * n_tok + jnp.right_shift(o, 2),
                        dump0 + odd * bm + r)
        pltpu.make_async_copy(ybuf.at[pl.ds(r, 1)], y_ref.at[pl.ds(dst, 1)], sem).start()

    def gather_wait(xbuf, sem):
        pltpu.make_async_copy(h_ref.at[pl.ds(0, bm)], xbuf, sem).wait()

    def scatter_wait(ybuf, sem):
        pltpu.make_async_copy(ybuf, y_ref.at[pl.ds(0, bm)], sem).wait()

    def switch_weights(b):
        prev = blk_e_ref[jnp.maximum(b - 1, 0)]
        e = blk_e_ref[b]

        @pl.when((b == 0) | (e != prev))
        def _():
            for c in weight_copies(e):
                c.wait()
            wgu_bf[...] = wgu_st[...].astype(BF16)
            wd_bf[...] = wd_st[...].astype(BF16)

            @pl.when(nxt_e_ref[e] >= 0)
            def _():
                for c in weight_copies(nxt_e_ref[e]):
                    c.start()

    def block(b, x_in, y_out, g_next, x_next, gsem_next, s_prev, y_prev, ssem_prev, prev_odd):
        gj0 = blk_j0_ref[g_next + 1]
        sj0 = blk_j0_ref[s_prev + 1]
        snv = blk_nv_ref[s_prev + 1]
        for r in range(bm):
            gather_start(gj0, x_next, gsem_next, r)
            scatter_start(sj0, snv, y_prev, ssem_prev, r, prev_odd)
        x = x_in[...].astype(BF16)
        e = blk_e_ref[b]
        gu = jnp.dot(x, wgu_bf[...], preferred_element_type=F32) + bgu_ref[e]
        g = jnp.minimum(gu[:, :D_FF], SWIGLU_LIMIT)
        lin = jnp.clip(gu[:, D_FF:], -SWIGLU_LIMIT, SWIGLU_LIMIT)
        act = g * _sigmoid(SWIGLU_ALPHA * g) * (lin + 1.0)
        y_out[...] = jnp.dot(act.astype(BF16), wd_bf[...], preferred_element_type=F32) + bd_ref[e]

    b0 = 2 * s
    b1 = b0 + 1

    @pl.when(b0 < used)
    def _():
        @pl.when(s == 0)
        def _():
            for c in weight_copies(blk_e_ref[0]):
                c.start()
            y1[...] = jnp.zeros(y1.shape, F32)
            zero_dump = pltpu.make_async_copy(y1, y_ref.at[pl.ds(dump0, bm)], ssem.at[0])
            zero_dump.start()
            zero_dump.wait()

            def first(r, c):
                gather_start(blk_j0_ref[1], x0, gsem.at[0], r)
                return c

            lax.fori_loop(0, bm, first, 0)

        @pl.when(s > 0)
        def _():
            scatter_wait(y0, ssem.at[0])

        switch_weights(b0)
        gather_wait(x0, gsem.at[0])
        block(b0, x0, y0, b1, x1, gsem.at[1], b0 - 1, y1, ssem.at[1], 1)

        scatter_wait(y1, ssem.at[1])
        switch_weights(b1)
        gather_wait(x1, gsem.at[1])
        block(b1, x1, y1, b0 + 2, x0, gsem.at[0], b0, y0, ssem.at[0], 0)

        @pl.when(b0 + 2 >= used)
        def _():
            def last(r, c):
                scatter_start(blk_j0_ref[b1 + 1], blk_nv_ref[b1 + 1], y1, ssem.at[1], r, 1)
                return c

            lax.fori_loop(0, bm, last, 0)
            scatter_wait(y0, ssem.at[0])
            scatter_wait(y1, ssem.at[1])
            gather_wait(x0, gsem.at[0])


def _expert(order, blk_e, blk_j0, blk_nv, nxt_e, used, h2, w, bm):
    n_tok = h2.shape[0]
    nblk = blk_e.shape[0]
    f32buf = lambda shp: pltpu.VMEM(shp, F32)
    return pl.pallas_call(
        functools.partial(_expert_kernel, bm=bm, n_tok=n_tok),
        grid_spec=pltpu.PrefetchScalarGridSpec(
            num_scalar_prefetch=6,
            grid=(nblk // 2,),
            in_specs=[
                pl.BlockSpec(memory_space=pl.ANY),
                pl.BlockSpec(memory_space=pl.ANY),
                pl.BlockSpec((N_EXPERTS, 1, 2 * D_FF), lambda s, *_: (0, 0, 0)),
                pl.BlockSpec(memory_space=pl.ANY),
                pl.BlockSpec((N_EXPERTS, 1, D_MODEL), lambda s, *_: (0, 0, 0)),
            ],
            out_specs=pl.BlockSpec(memory_space=pl.ANY),
            scratch_shapes=[f32buf((bm, D_MODEL)), f32buf((bm, D_MODEL)), f32buf((bm, D_MODEL)),
                            f32buf((bm, D_MODEL)), f32buf((D_MODEL, 2 * D_FF)), f32buf((D_FF, D_MODEL)),
                            pltpu.VMEM((D_MODEL, 2 * D_FF), BF16), pltpu.VMEM((D_FF, D_MODEL), BF16),
                            pltpu.SemaphoreType.DMA((2,)), pltpu.SemaphoreType.DMA((2,)),
                            pltpu.SemaphoreType.DMA((2,))],
        ),
        out_shape=jax.ShapeDtypeStruct((TOP_K * n_tok + 2 * bm, D_MODEL), F32),
        compiler_params=_cparams(("arbitrary",)),
        name="moe_expert",
    )(order, blk_e, blk_j0, blk_nv, nxt_e, used, h2, w["w_gu"], w["b_gu"], w["w_down"], w["b_down"])


def _combine_kernel(y0_ref, y1_ref, y2_ref, y3_ref, x1_ref, gates_ref, gt2_ref, l2g_ref, l2b_ref, o_ref, *, alpha):
    gates = gates_ref[...]
    f = gates[:, 0:1] * y0_ref[...]
    for k, y_ref in enumerate((y1_ref, y2_ref, y3_ref), start=1):
        f = f + gates[:, k:k + 1] * y_ref[...]
    o_ref[...] = _layernorm(alpha * x1_ref[...] + gt2_ref[0] * f) * l2g_ref[...] + l2b_ref[...]


def _combine(y_rows, x1, gates, gt2, w, tm, tokens_per_batch, alpha):
    N = x1.shape[0]
    per_b = tokens_per_batch // tm
    nt = N // tm
    slot = lambda k: pl.BlockSpec((tm, D_MODEL), lambda i: (k * nt + i, 0))
    return pl.pallas_call(
        functools.partial(_combine_kernel, alpha=alpha),
        grid=(nt,),
        in_specs=[slot(0), slot(1), slot(2), slot(3),
                  pl.BlockSpec((tm, D_MODEL), lambda i: (i, 0)),
                  pl.BlockSpec((tm, LANES), lambda i: (i, 0)),
                  pl.BlockSpec((1, 1, D_MODEL), lambda i: (i // per_b, 0, 0)),
                  pl.BlockSpec((1, D_MODEL), lambda i: (0, 0)),
                  pl.BlockSpec((1, D_MODEL), lambda i: (0, 0))],
        out_specs=pl.BlockSpec((tm, D_MODEL), lambda i: (i, 0)),
        out_shape=jax.ShapeDtypeStruct((N, D_MODEL), F32),
        compiler_params=_cparams(("arbitrary",)),
        name="moe_combine",
    )(y_rows, y_rows, y_rows, y_rows, x1, gates, gt2, w["ln2_g"], w["ln2_b"])


def _moe(h2, x1, idx, gates, counts, gt2, w, tokens_per_batch, bm, tm_c, alpha):
    N = h2.shape[0]
    n_rows = N * TOP_K
    nblk = (n_rows + N_EXPERTS * (bm - 1)) // bm
    nblk += nblk % 2
    experts = jnp.arange(N_EXPERTS, dtype=jnp.int32)
    order = jnp.concatenate([jnp.argsort(idx.reshape(-1)).astype(jnp.int32), jnp.zeros((bm,), jnp.int32)])
    nb_e = (counts + bm - 1) // bm
    blk_end = jnp.cumsum(nb_e)
    first_blk = blk_end - nb_e
    start_sorted = jnp.cumsum(counts) - counts
    used = blk_end[-1].astype(jnp.int32)
    b = jnp.arange(-1, nblk + 1, dtype=jnp.int32)
    bc = jnp.clip(b, 0, used - 1)
    e = jnp.minimum(jnp.sum(blk_end[None, :] <= bc[:, None], axis=1), N_EXPERTS - 1).astype(jnp.int32)
    pick = lambda table: jnp.sum(jnp.where(e[:, None] == experts, table, 0), axis=1)
    local = bc - pick(first_blk)
    blk_j0 = (pick(start_sorted) + local * bm).astype(jnp.int32)
    blk_nv = jnp.where((b >= 0) & (b < used), jnp.minimum(bm, pick(counts) - local * bm), 0).astype(jnp.int32)
    blk_e = e[1:nblk + 1]
    later = (experts[None, :] > experts[:, None]) & (counts[None, :] > 0)
    nxt = jnp.min(jnp.where(later, experts[None, :], N_EXPERTS), axis=1)
    nxt_e = jnp.where(nxt < N_EXPERTS, nxt, -1).astype(jnp.int32)
    y_rows = _expert(order, blk_e, blk_j0, blk_nv, nxt_e, used.reshape(1), h2, w, bm)
    return _combine(y_rows, x1, gates, gt2, w, tm_c, tokens_per_batch, alpha)


def _rope_tables(pos):
    half = QK_ROPE // 2
    inv = ROPE_THETA ** (-jnp.arange(half, dtype=F32) / half)
    ang = pos.astype(F32)[:, None] * inv[None, :]
    cos, sin = jnp.cos(ang), jnp.sin(ang)
    cos32 = jnp.concatenate([cos, cos], axis=1)
    sin32 = jnp.concatenate([-sin, sin], axis=1)
    return jnp.tile(cos32, (1, LANES // QK_ROPE)), jnp.tile(sin32, (1, LANES // QK_ROPE))


def _swap_halves(w32):
    shp = w32.shape
    w = w32.reshape(shp[:-1] + (shp[-1] // QK_ROPE, 2, QK_ROPE // 2))
    return w[..., ::-1, :].reshape(shp)


def _prep_weights(l, w_in, conv_w, g_qa, w_qb, g_kva, w_kvb, g_out_conv, g_out_attn, w_out,
                  ln1_g, ln1_b, router_w, router_b, w_gu, b_gu, w_down, b_down, ln2_g, ln2_b):
    w = {}
    wi = w_in[l]
    k_r = wi[:, _O_KR:_O_KR + QK_ROPE]
    rep = LANES // QK_ROPE
    w["w_in"] = jnp.concatenate([wi[:, :_O_KR], jnp.tile(k_r, (1, rep)), jnp.tile(_swap_halves(k_r), (1, rep))],
                                axis=1).astype(BF16)
    w["conv_w"] = conv_w[l]
    w["g_qa"] = g_qa[l].reshape(1, Q_LORA)
    w["g_kva"] = g_kva[l].reshape(1, KV_LORA)
    w["g_oc"] = g_out_conv[l].reshape(1, CONV_WIDTH)
    w["g_oa"] = g_out_attn[l].reshape(1, ATTN_WIDTH)
    wq = w_qb[l].reshape(Q_LORA, N_HEADS, QK_NOPE + QK_ROPE)
    w["wq_nope"] = wq[:, :, :QK_NOPE].reshape(Q_LORA, N_HEADS * QK_NOPE).astype(BF16)
    wq_rope = wq[:, :, QK_NOPE:].reshape(Q_LORA, N_HEADS * QK_ROPE)
    w["wq_rope"] = wq_rope.astype(BF16)
    w["wq_rope_sw"] = _swap_halves(wq_rope).astype(BF16)
    w_uk = jnp.transpose(w_kvb[l][:, :, :QK_NOPE], (1, 2, 0))
    w_uv = jnp.transpose(w_kvb[l][:, :, QK_NOPE:], (1, 0, 2))
    zk = jnp.zeros((QK_NOPE, KV_LORA), F32)
    zv = jnp.zeros((KV_LORA, V_HEAD), F32)
    w["w_uk_pair"] = jnp.stack([
        jnp.concatenate([jnp.concatenate([w_uk[2 * p], zk], axis=1),
                         jnp.concatenate([zk, w_uk[2 * p + 1]], axis=1)], axis=0)
        for p in range(N_HEADS // 2)]).astype(BF16)
    w["w_uv_pair"] = jnp.stack([
        jnp.concatenate([jnp.concatenate([w_uv[2 * p], zv], axis=1),
                         jnp.concatenate([zv, w_uv[2 * p + 1]], axis=1)], axis=0)
        for p in range(N_HEADS // 2)]).astype(BF16)
    w["w_out"] = w_out[l].astype(BF16)
    w["ln1_g"] = ln1_g[l].reshape(1, D_MODEL)
    w["ln1_b"] = ln1_b[l].reshape(1, D_MODEL)
    w["ln2_g"] = ln2_g[l].reshape(1, D_MODEL)
    w["ln2_b"] = ln2_b[l].reshape(1, D_MODEL)
    w["router_w"] = jnp.pad(router_w[l], ((0, 0), (0, LANES - N_EXPERTS))).astype(BF16)
    w["router_b"] = jnp.concatenate([router_b[l], jnp.full((LANES - N_EXPERTS,), NEG, F32)]).reshape(1, LANES)
    w["w_gu"] = w_gu[l]
    w["b_gu"] = b_gu[l].reshape(N_EXPERTS, 1, 2 * D_FF)
    w["w_down"] = w_down[l]
    w["b_down"] = b_down[l].reshape(N_EXPERTS, 1, D_MODEL)
    return w


def _layer(x, mod, conv_prev, past, pos0, w, alpha, *, tm_in, tm_out, bm, tm_c, tq=128, tk=512):
    B, T, _ = x.shape
    sh1, sc1, gt1, sh2, sc2, gt2 = [mod[:, None, i * D_MODEL:(i + 1) * D_MODEL] for i in range(N_MOD)]
    cos_t, sin_t = _rope_tables(pos0 + jnp.arange(T, dtype=jnp.int32))
    a_n, qlat, qrope, kcat, ckv, krope, conv_new = _inproj(x, sc1, sh1, conv_prev, cos_t, sin_t, w, tm_in)
    if past is None:
        b_n = _attn_prompt(qlat, qrope, kcat, w, tq, tk)
    else:
        b_n = _attn_sample(qlat, qrope, kcat, past[0], past[1], w, tk)
    x1, h2, meta, gates, cnt = _outproj(x, a_n, b_n, gt1, sc2, sh2, w, tm_out, alpha)
    N = B * T
    counts = cnt[0, :N_EXPERTS].astype(jnp.int32)
    y = _moe(h2.reshape(N, D_MODEL), x1.reshape(N, D_MODEL), meta.reshape(N, LANES)[:, :TOP_K],
             gates.reshape(N, LANES), counts, gt2, w, T, bm, tm_c, alpha)
    return y.reshape(B, T, D_MODEL), ckv, krope, conv_new


def kernel(x_prompt, x_sample, c_prompt, c_sample, cache_ckv, cache_krope, state_conv, w_ada, b_ada, w_in, conv_w, g_qa, w_qb, g_kva, w_kvb, g_out_conv, g_out_attn, w_out, ln1_g, ln1_b, router_w, router_b, w_gu, b_gu, w_down, b_down, ln2_g, ln2_b):
    depth = w_ada.shape[0]
    Bp, Tp, _ = x_prompt.shape
    Bs, Ts, _ = x_sample.shape
    past_len = cache_ckv.shape[2]
    assert Ts == CHUNK and past_len % CHUNK == 0 and Tp % 512 == 0
    alpha = (2.0 * depth) ** 0.25
    xp, xs = x_prompt, x_sample
    outs = [[] for _ in range(6)]
    c_all = jnp.concatenate([c_prompt, c_sample, jnp.zeros((16 - Bp - Bs, D_MODEL), F32)], axis=0)
    for l in range(depth):
        w = _prep_weights(l, w_in, conv_w, g_qa, w_qb, g_kva, w_kvb, g_out_conv, g_out_attn, w_out,
                          ln1_g, ln1_b, router_w, router_b, w_gu, b_gu, w_down, b_down, ln2_g, ln2_b)
        mod = _ada(c_all, w_ada[l], b_ada[l])
        xp, ckv_p, kr_p, cv_p = _layer(xp, mod[:Bp], jnp.zeros((Bp, CONV_K - 1, CONV_WIDTH), F32), None, 0, w, alpha,
                                       tm_in=512, tm_out=256, bm=256, tm_c=256)
        xs, ckv_s, kr_s, cv_s = _layer(xs, mod[Bp:Bp + Bs], state_conv[l], (cache_ckv[l], cache_krope[l]),
                                       past_len, w, alpha, tm_in=Ts, tm_out=Ts, bm=128, tm_c=Ts)
        for o, v in zip(outs, (ckv_p, kr_p, cv_p, ckv_s, kr_s, cv_s)):
            o.append(v)
    return (xp, xs) + tuple(jnp.stack(o) for o in outs)
```

```python
import functools
import math

import jax
import jax.numpy as jnp
from jax import lax
from jax.experimental import pallas as pl
from jax.experimental.pallas import tpu as pltpu

F32 = jnp.float32
BF16 = jnp.bfloat16

D_MODEL = 1024
CHUNK = 64
CONV_WIDTH = 512
CONV_K = 3
N_HEADS = 8
QK_NOPE = 64
QK_ROPE = 32
V_HEAD = 64
Q_LORA = 256
KV_LORA = 128
ATTN_WIDTH = N_HEADS * V_HEAD
ROPE_THETA = 10000.0
ATTN_SCALE = 1.0 / math.sqrt(QK_NOPE + QK_ROPE)
Q_SCALE = ATTN_SCALE * math.log2(math.e)
N_EXPERTS = 32
TOP_K = 4
D_FF = 1024
SWIGLU_LIMIT = 7.0
SWIGLU_ALPHA = 1.702
N_MOD = 6
LN_EPS = 1e-5
RMS_EPS = 1e-6

LANES = 128
ATTN_GROUPS = 1
FFN_PARTS = 6
NEG = -1e30
VMEM_LIMIT = 56 * 1024 * 1024

_O_XB, _O_XC, _O_XV = 0, CONV_WIDTH, 2 * CONV_WIDTH
_O_QA = 3 * CONV_WIDTH
_O_KVA = _O_QA + Q_LORA
_O_KR = _O_KVA + KV_LORA
_O_KRS = _O_KR + LANES
IN_COLS_EXT = _O_KRS + LANES


def _cparams(sem):
    return pltpu.CompilerParams(dimension_semantics=sem, vmem_limit_bytes=VMEM_LIMIT)


def _layernorm(x):
    mu = jnp.mean(x, axis=-1, keepdims=True)
    xc = x - mu
    var = jnp.mean(xc * xc, axis=-1, keepdims=True)
    return xc * lax.rsqrt(var + LN_EPS)


def _rms(x):
    return x * lax.rsqrt(jnp.mean(x * x, axis=-1, keepdims=True) + RMS_EPS)


def _sigmoid(x):
    return 1.0 / (1.0 + jnp.exp(-x))


ROW_TILE = D_MODEL // LANES


def _rows_from_tiles(ref, n):
    return jnp.concatenate([ref[pl.ds(c, n, stride=ROW_TILE), :] for c in range(ROW_TILE)], axis=1)


def _rows_to_tiles(ref, x):
    n = x.shape[0]
    for c in range(ROW_TILE):
        ref[pl.ds(c, n, stride=ROW_TILE), :] = x[:, c * LANES:(c + 1) * LANES]


def _ada_kernel(c_ref, w_ref, b_ref, o_ref):
    c = c_ref[...]
    s = (c * _sigmoid(c)).astype(BF16)
    o_ref[...] = jnp.dot(s, w_ref[...].astype(BF16), preferred_element_type=F32) + b_ref[...]


def _ada(c_all, w_ada, b_ada):
    rows = c_all.shape[0]
    ncol = w_ada.shape[1]
    tn = 1024
    return pl.pallas_call(
        _ada_kernel,
        grid=(ncol // tn,),
        in_specs=[pl.BlockSpec((rows, D_MODEL), lambda j: (0, 0)),
                  pl.BlockSpec((D_MODEL, tn), lambda j: (0, j)),
                  pl.BlockSpec((1, tn), lambda j: (0, j))],
        out_specs=pl.BlockSpec((rows, tn), lambda j: (0, j)),
        out_shape=jax.ShapeDtypeStruct((rows, ncol), F32),
        compiler_params=_cparams(("arbitrary",)),
        name="ada",
    )(c_all, w_ada, b_ada.reshape(1, ncol))


def _inproj_kernel(x_ref, sc_ref, sh_ref, win_ref, cw_ref, cprev_ref, gqa_ref, gkva_ref, goc_ref,
                   wqn_ref, wqr_ref, wqrs_ref, wuk_ref, cos_ref, sin_ref,
                   an_ref, qlat_ref, qrope_ref, kcat_ref, ckv_ref, krope_ref, cnew_ref,
                   ubuf, *, tm):
    j = pl.program_id(1)
    x = x_ref[0]
    h = _layernorm(x) * (1.0 + sc_ref[0]) + sh_ref[0]
    proj = jnp.dot(h.astype(BF16), win_ref[...], preferred_element_type=F32)
    xb = proj[:, _O_XB:_O_XB + CONV_WIDTH]
    xc = proj[:, _O_XC:_O_XC + CONV_WIDTH]
    xv = proj[:, _O_XV:_O_XV + CONV_WIDTH]
    q_a = proj[:, _O_QA:_O_QA + Q_LORA]
    kv_a = proj[:, _O_KVA:_O_KVA + KV_LORA]
    kr4 = proj[:, _O_KR:_O_KR + LANES]
    kr4s = proj[:, _O_KRS:_O_KRS + LANES]

    u = xc * xv

    @pl.when(j == 0)
    def _():
        ubuf[6:8, :] = cprev_ref[0]

    ubuf[8:8 + tm, :] = u
    conv = (cw_ref[0:1, :] * ubuf[6:6 + tm, :] + cw_ref[1:2, :] * ubuf[7:7 + tm, :]
            + cw_ref[2:3, :] * u)
    ubuf[0:8, :] = ubuf[tm:tm + 8, :]
    cnew_ref[0] = u[tm - (CONV_K - 1):tm, :]
    an_ref[0] = (_rms(xb * conv) * goc_ref[...]).astype(BF16)

    cos = cos_ref[...]
    sin = sin_ref[...]

    ckv = _rms(kv_a) * gkva_ref[...]
    kro4 = kr4 * cos + kr4s * sin
    ckv_ref[0] = ckv
    krope_ref[0] = kro4[:, :QK_ROPE]
    kcat_ref[0] = jnp.concatenate([ckv, kro4], axis=1).astype(BF16)

    qn = (_rms(q_a) * gqa_ref[...]).astype(BF16)
    q_nope = jnp.dot(qn, wqn_ref[...], preferred_element_type=F32)
    xr = jnp.dot(qn, wqr_ref[...], preferred_element_type=F32)
    xrs = jnp.dot(qn, wqrs_ref[...], preferred_element_type=F32)
    for g in range(2):
        sl = slice(g * LANES, (g + 1) * LANES)
        qrope_ref[0, :, sl] = ((xr[:, sl] * cos + xrs[:, sl] * sin) * Q_SCALE).astype(BF16)
    for p in range(N_HEADS // 2):
        qp = q_nope[:, p * LANES:(p + 1) * LANES].astype(BF16)
        ql = jnp.dot(qp, wuk_ref[p], preferred_element_type=F32)
        qlat_ref[0, :, p * 2 * KV_LORA:(p + 1) * 2 * KV_LORA] = (ql * Q_SCALE).astype(BF16)


def _inproj(x, sc1, sh1, conv_prev, cos_t, sin_t, w, tm):
    B, T, _ = x.shape
    nt = T // tm
    full = lambda shp: pl.BlockSpec(shp, lambda b, j: (0,) * len(shp))
    out_shapes = (
        jax.ShapeDtypeStruct((B, T, CONV_WIDTH), BF16),
        jax.ShapeDtypeStruct((B, T, N_HEADS * KV_LORA), BF16),
        jax.ShapeDtypeStruct((B, T, 2 * LANES), BF16),
        jax.ShapeDtypeStruct((B, T, 2 * LANES), BF16),
        jax.ShapeDtypeStruct((B, T, KV_LORA), F32),
        jax.ShapeDtypeStruct((B, T, QK_ROPE), F32),
        jax.ShapeDtypeStruct((B, CONV_K - 1, CONV_WIDTH), F32),
    )
    tile = lambda c: pl.BlockSpec((1, tm, c), lambda b, j: (b, j, 0))
    return pl.pallas_call(
        functools.partial(_inproj_kernel, tm=tm),
        grid=(B, nt),
        in_specs=[
            tile(D_MODEL),
            pl.BlockSpec((1, 1, D_MODEL), lambda b, j: (b, 0, 0)),
            pl.BlockSpec((1, 1, D_MODEL), lambda b, j: (b, 0, 0)),
            full((D_MODEL, IN_COLS_EXT)),
            full((CONV_K, CONV_WIDTH)),
            pl.BlockSpec((1, CONV_K - 1, CONV_WIDTH), lambda b, j: (b, 0, 0)),
            full((1, Q_LORA)), full((1, KV_LORA)), full((1, CONV_WIDTH)),
            full((Q_LORA, N_HEADS * QK_NOPE)), full((Q_LORA, 2 * LANES)), full((Q_LORA, 2 * LANES)),
            full((N_HEADS // 2, LANES, 2 * KV_LORA)),
            pl.BlockSpec((tm, LANES), lambda b, j: (j, 0)),
            pl.BlockSpec((tm, LANES), lambda b, j: (j, 0)),
        ],
        out_specs=(tile(CONV_WIDTH), tile(N_HEADS * KV_LORA), tile(2 * LANES), tile(2 * LANES),
                   tile(KV_LORA), tile(QK_ROPE),
                   pl.BlockSpec((1, CONV_K - 1, CONV_WIDTH), lambda b, j: (b, 0, 0))),
        out_shape=out_shapes,
        scratch_shapes=[pltpu.VMEM((tm + 8, CONV_WIDTH), F32)],
        compiler_params=_cparams(("arbitrary", "arbitrary")),
        name="inproj",
    )(x, sc1, sh1, w["w_in"], w["conv_w"], conv_prev, w["g_qa"], w["g_kva"], w["g_oc"],
      w["wq_nope"], w["wq_rope"], w["wq_rope_sw"], w["w_uk_pair"], cos_t, sin_t)


def _stack_queries(qlat_ref, qrope_ref, qs, tq):
    lane = lax.broadcasted_iota(jnp.int32, (tq, LANES), 1)
    for h in range(N_HEADS):
        g, i = divmod(h, 4)
        rope = qrope_ref[0, :, g * LANES:(g + 1) * LANES]
        keep = (lane >= i * QK_ROPE) & (lane < (i + 1) * QK_ROPE)
        qs[h * tq:(h + 1) * tq, 0:KV_LORA] = qlat_ref[0, :, h * KV_LORA:(h + 1) * KV_LORA]
        qs[h * tq:(h + 1) * tq, KV_LORA:KV_LORA + LANES] = jnp.where(keep, rope, jnp.zeros_like(rope))


def _softmax_step(qs, k, v, m_s, l_s, acc_s, mask=None, groups=1):
    tk = k.shape[0]
    rows = qs.shape[0] // groups
    for g in range(groups):
        r = slice(g * rows, (g + 1) * rows)
        s = lax.dot_general(qs[r, :], k, (((1,), (1,)), ((), ())), preferred_element_type=F32)
        if mask is not None:
            col, limit = mask
            s = jnp.where(col < limit[r], s, NEG)
        m_prev = m_s[r, :]
        m_new = jnp.maximum(m_prev, jnp.max(s, axis=1, keepdims=True))
        alpha = jnp.exp2(m_prev - m_new)
        if tk % LANES == 0:
            p = jnp.exp2(s - jnp.tile(m_new, (1, tk // LANES)))
        else:
            p = jnp.exp2(s - m_new[:, :tk])
        l_s[r, :] = alpha * l_s[r, :] + jnp.sum(p, axis=1, keepdims=True)
        acc_s[r, :] = alpha * acc_s[r, :] + jnp.dot(p.astype(BF16), v, preferred_element_type=F32)
        m_s[r, :] = m_new


def _attn_epilogue(acc_s, l_s, wuv_ref, g_ref, o_ref, tq):
    o = acc_s[...] / l_s[...]
    parts = []
    for p in range(N_HEADS // 2):
        op = jnp.concatenate([o[(2 * p) * tq:(2 * p + 1) * tq], o[(2 * p + 1) * tq:(2 * p + 2) * tq]], axis=1)
        parts.append(jnp.dot(op.astype(BF16), wuv_ref[p], preferred_element_type=F32))
    b = jnp.concatenate(parts, axis=1)
    o_ref[0] = (_rms(b) * g_ref[...]).astype(BF16)


def _attn_prompt_kernel(qlat_ref, qrope_ref, k_ref, wuv_ref, g_ref, o_ref, qs, m_s, l_s, acc_s, *, tq, tk,
                        groups):
    i = pl.program_id(1)
    M = N_HEADS * tq
    _stack_queries(qlat_ref, qrope_ref, qs, tq)
    m_s[...] = jnp.full(m_s.shape, NEG, F32)
    l_s[...] = jnp.zeros(l_s.shape, F32)
    acc_s[...] = jnp.zeros(acc_s.shape, F32)
    q0 = i * tq
    n_full = (q0 + CHUNK) // tk

    def body(t, carry):
        start = pl.multiple_of(t * tk, tk)
        k = k_ref[0, pl.ds(start, tk), :]
        _softmax_step(qs, k, k[:, :KV_LORA], m_s, l_s, acc_s, groups=groups)
        return carry

    lax.fori_loop(0, n_full, body, 0)

    start = pl.multiple_of(n_full * tk, tk)
    k = k_ref[0, pl.ds(start, tk), :]
    col = start + lax.broadcasted_iota(jnp.int32, (1, tk), 1)
    row_t = jnp.bitwise_and(lax.broadcasted_iota(jnp.int32, (M, 1), 0), tq - 1)
    limit = q0 + (jnp.right_shift(row_t, CHUNK.bit_length() - 1) + 1) * CHUNK
    _softmax_step(qs, k, k[:, :KV_LORA], m_s, l_s, acc_s, mask=(col, limit), groups=groups)
    _attn_epilogue(acc_s, l_s, wuv_ref, g_ref, o_ref, tq)


def _attn_prompt(qlat, qrope, kcat, w, tq, tk):
    B, T, _ = qlat.shape
    M = N_HEADS * tq
    return pl.pallas_call(
        functools.partial(_attn_prompt_kernel, tq=tq, tk=tk, groups=ATTN_GROUPS),
        grid=(B, T // tq),
        in_specs=[
            pl.BlockSpec((1, tq, N_HEADS * KV_LORA), lambda b, i: (b, i, 0)),
            pl.BlockSpec((1, tq, 2 * LANES), lambda b, i: (b, i, 0)),
            pl.BlockSpec((1, T, 2 * LANES), lambda b, i: (b, 0, 0)),
            pl.BlockSpec((N_HEADS // 2, 2 * KV_LORA, LANES), lambda b, i: (0, 0, 0)),
            pl.BlockSpec((1, ATTN_WIDTH), lambda b, i: (0, 0)),
        ],
        out_specs=pl.BlockSpec((1, tq, ATTN_WIDTH), lambda b, i: (b, i, 0)),
        out_shape=jax.ShapeDtypeStruct((B, T, ATTN_WIDTH), BF16),
        scratch_shapes=[pltpu.VMEM((M, 2 * LANES), BF16), pltpu.VMEM((M, LANES), F32),
                        pltpu.VMEM((M, LANES), F32), pltpu.VMEM((M, KV_LORA), F32)],
        compiler_params=_cparams(("arbitrary", "arbitrary")),
        name="attn_prompt",
    )(qlat, qrope, kcat, w["w_uv_pair"], w["g_oa"])


def _attn_sample_kernel(qlat_ref, qrope_ref, knew_ref, pckv_ref, pkr_ref, wuv_ref, g_ref, o_ref,
                        qs, m_s, l_s, acc_s, *, tq, tk, n_past):
    _stack_queries(qlat_ref, qrope_ref, qs, tq)
    m_s[...] = jnp.full(m_s.shape, NEG, F32)
    l_s[...] = jnp.zeros(l_s.shape, F32)
    acc_s[...] = jnp.zeros(acc_s.shape, F32)

    def body(t, carry):
        start = pl.multiple_of(t * tk, tk)
        ck = pckv_ref[0, pl.ds(start, tk), :]
        kr = pkr_ref[0, pl.ds(start, tk), :]
        k = jnp.concatenate([ck, kr, kr, kr, kr], axis=1).astype(BF16)
        _softmax_step(qs, k, k[:, :KV_LORA], m_s, l_s, acc_s)
        return carry

    lax.fori_loop(0, n_past // tk, body, 0)
    k = knew_ref[0]
    _softmax_step(qs, k, k[:, :KV_LORA], m_s, l_s, acc_s)
    _attn_epilogue(acc_s, l_s, wuv_ref, g_ref, o_ref, tq)


def _attn_sample(qlat, qrope, kcat, past_ckv, past_krope, w, tk):
    B, T, _ = qlat.shape
    n_past = past_ckv.shape[1]
    M = N_HEADS * T
    per_b = lambda r, c: pl.BlockSpec((1, r, c), lambda b: (b, 0, 0))
    return pl.pallas_call(
        functools.partial(_attn_sample_kernel, tq=T, tk=tk, n_past=n_past),
        grid=(B,),
        in_specs=[per_b(T, N_HEADS * KV_LORA), per_b(T, 2 * LANES), per_b(T, 2 * LANES),
                  per_b(n_past, KV_LORA), per_b(n_past, QK_ROPE),
                  pl.BlockSpec((N_HEADS // 2, 2 * KV_LORA, LANES), lambda b: (0, 0, 0)),
                  pl.BlockSpec((1, ATTN_WIDTH), lambda b: (0, 0))],
        out_specs=per_b(T, ATTN_WIDTH),
        out_shape=jax.ShapeDtypeStruct((B, T, ATTN_WIDTH), BF16),
        scratch_shapes=[pltpu.VMEM((M, 2 * LANES), BF16), pltpu.VMEM((M, LANES), F32),
                        pltpu.VMEM((M, LANES), F32), pltpu.VMEM((M, KV_LORA), F32)],
        compiler_params=_cparams(("arbitrary",)),
        name="attn_sample",
    )(qlat, qrope, kcat, past_ckv, past_krope, w["w_uv_pair"], w["g_oa"])


def _outproj_kernel(x_ref, an_ref, bn_ref, gt1_ref, sc2_ref, sh2_ref, wo_ref, l1g_ref, l1b_ref,
                    rw_ref, rb_ref, x1_ref, h2_ref, meta_ref, gates_ref, cnt_ref, run_s,
                    *, tm, alpha):
    first = (pl.program_id(0) == 0) & (pl.program_id(1) == 0)

    @pl.when(first)
    def _():
        run_s[...] = jnp.zeros(run_s.shape, F32)

    m = (jnp.dot(an_ref[0], wo_ref[0:CONV_WIDTH, :], preferred_element_type=F32)
         + jnp.dot(bn_ref[0], wo_ref[CONV_WIDTH:, :], preferred_element_type=F32))
    x1 = _layernorm(alpha * x_ref[0] + gt1_ref[0] * m) * l1g_ref[...] + l1b_ref[...]
    x1_ref[0] = x1
    h2 = _layernorm(x1) * (1.0 + sc2_ref[0]) + sh2_ref[0]
    _rows_to_tiles(h2_ref, h2)

    logits = jnp.dot(h2.astype(BF16), rw_ref[...], preferred_element_type=F32) + rb_ref[...]
    lane = lax.broadcasted_iota(jnp.int32, (tm, LANES), 1)
    lane_f = lane.astype(F32)
    lg = logits
    vals, sels = [], []
    chosen = jnp.zeros((tm, LANES), F32)
    for _ in range(TOP_K):
        mx = jnp.max(lg, axis=1, keepdims=True)
        idx = jnp.min(jnp.where(lg == mx, lane_f, float(LANES)), axis=1, keepdims=True)
        sel = lane_f == idx
        vals.append(mx)
        sels.append(idx)
        chosen = jnp.where(sel, 1.0, chosen)
        lg = jnp.where(sel, NEG, lg)

    es = [jnp.exp(v - vals[0]) for v in vals]
    denom = es[0] + es[1] + es[2] + es[3]

    run_s[0:1, :] = run_s[0:1, :] + jnp.sum(chosen, axis=0, keepdims=True)
    cnt_ref[...] = jnp.broadcast_to(run_s[0:1, :], cnt_ref.shape)

    meta = jnp.zeros((tm, LANES), jnp.int32)
    gates = jnp.zeros((tm, LANES), F32)
    for k in range(TOP_K):
        meta = jnp.where(lane == k, sels[k].astype(jnp.int32), meta)
        gates = jnp.where(lane == k, es[k] / denom, gates)
    meta_ref[0] = meta
    gates_ref[0] = gates


def _outproj(x, a_n, b_n, gt1, sc2, sh2, w, tm, alpha):
    B, T, _ = x.shape
    tile = lambda c: pl.BlockSpec((1, tm, c), lambda b, j: (b, j, 0))
    modv = pl.BlockSpec((1, 1, D_MODEL), lambda b, j: (b, 0, 0))
    full = lambda shp: pl.BlockSpec(shp, lambda b, j: (0,) * len(shp))
    return pl.pallas_call(
        functools.partial(_outproj_kernel, tm=tm, alpha=alpha),
        grid=(B, T // tm),
        in_specs=[tile(D_MODEL), tile(CONV_WIDTH), tile(ATTN_WIDTH), modv, modv, modv,
                  full((CONV_WIDTH + ATTN_WIDTH, D_MODEL)), full((1, D_MODEL)), full((1, D_MODEL)),
                  full((D_MODEL, LANES)), full((1, LANES))],
        out_specs=(tile(D_MODEL), pl.BlockSpec((tm * ROW_TILE, LANES), lambda b, j: (b * (T // tm) + j, 0)),
                   tile(LANES), tile(LANES), pl.BlockSpec((8, LANES), lambda b, j: (0, 0))),
        out_shape=(jax.ShapeDtypeStruct((B, T, D_MODEL), F32),
                   jax.ShapeDtypeStruct((B * T * ROW_TILE, LANES), F32),
                   jax.ShapeDtypeStruct((B, T, LANES), jnp.int32), jax.ShapeDtypeStruct((B, T, LANES), F32),
                   jax.ShapeDtypeStruct((8, LANES), F32)),
        scratch_shapes=[pltpu.VMEM((8, LANES), F32)],
        compiler_params=_cparams(("arbitrary", "arbitrary")),
        name="outproj",
    )(x, a_n, b_n, gt1, sc2, sh2, w["w_out"], w["ln1_g"], w["ln1_b"], w["router_w"], w["router_b"])


def _expert_kernel(order_ref, blk_e_ref, blk_j0_ref, blk_nv_ref, nxt_e_ref, used_ref,
                   h_ref, wgu_hbm, bgu_ref, wd_hbm, bd_ref, y_ref,
                   x0, x1, y0, y1, xb, act_s, wgu_st, wd_st, wgu_bf, wd_bf, gsem, ssem, wsem, *, bm, n_tok):
    s = pl.program_id(0)
    used = used_ref[0]
    dump0 = TOP_K * n_tok

    def weight_copies(e):
        return (pltpu.make_async_copy(wgu_hbm.at[e], wgu_st, wsem.at[0]),
                pltpu.make_async_copy(wd_hbm.at[e], wd_st, wsem.at[1]))

    def gather_start(j0, xbuf, sem, r):
        tok = jnp.right_shift(order_ref[j0 + r], 2)
        pltpu.make_async_copy(h_ref.at[tok], xbuf.at[pl.ds(r * ROW_TILE, ROW_TILE)], sem).start()

    def scatter_start(j0, nv, ybuf, sem, r, odd):
        o = order_ref[j0 + r]
        real = jnp.bitwise_and(o, TOP_K - 1) * n_tok + jnp.right_shift(o, 2)
        dump = dump0 + odd * bm + r
        valid = jnp.right_shift(r - nv, 31)
        dst = dump + jnp.bitwise_and(valid, real - dump)
        pltpu.make_async_copy(ybuf.at[pl.ds(r * ROW_TILE, ROW_TILE)], y_ref.at[dst], sem).start()

    def block_wait(buf, sem):
        pltpu.make_async_copy(buf, buf, sem).wait()

    def switch_weights(b):
        prev = blk_e_ref[jnp.maximum(b - 1, 0)]
        e = blk_e_ref[b]

        @pl.when((b == 0) | (e != prev))
        def _():
            for c in weight_copies(e):
                c.wait()
            wgu_bf[...] = wgu_st[...].astype(BF16)
            wd_bf[...] = wd_st[...].astype(BF16)

            @pl.when(nxt_e_ref[e] >= 0)
            def _():
                for c in weight_copies(nxt_e_ref[e]):
                    c.start()

    def block(b, x_in, y_out, g_next, x_next, gsem_next, s_prev, y_prev, ssem_prev, prev_odd):
        gj0 = blk_j0_ref[g_next + 1]
        sj0 = blk_j0_ref[s_prev + 1]
        snv = blk_nv_ref[s_prev + 1]
        def issue(part, after=None):
            zero = 0
            if after is not None:
                bits = pltpu.bitcast(jnp.abs(after[0:8, 0:LANES]), jnp.int32)
                zero = jnp.minimum(bits[0, 0], 0)
            for r in range(part * bm // FFN_PARTS, (part + 1) * bm // FFN_PARTS):
                gather_start(gj0 + zero, x_next, gsem_next, r)
                scatter_start(sj0 + zero, snv, y_prev, ssem_prev, r, prev_odd)

        xb[...] = _rows_from_tiles(x_in, bm).astype(BF16)
        e = blk_e_ref[b]
        cw = D_FF // 4
        prev = None
        for c in range(4):
            issue(c, prev)
            gs, ls = slice(c * cw, (c + 1) * cw), slice(D_FF + c * cw, D_FF + (c + 1) * cw)
            g = jnp.dot(xb[...], wgu_bf[:, gs], preferred_element_type=F32) + bgu_ref[e][:, gs]
            lin = jnp.dot(xb[...], wgu_bf[:, ls], preferred_element_type=F32) + bgu_ref[e][:, ls]
            prev = g
            g = jnp.minimum(g, SWIGLU_LIMIT)
            lin = jnp.clip(lin, -SWIGLU_LIMIT, SWIGLU_LIMIT)
            act_s[:, gs] = (g * _sigmoid(SWIGLU_ALPHA * g) * (lin + 1.0)).astype(BF16)
        hw = D_MODEL // 2
        for h in range(2):
            issue(4 + h, prev)
            y = jnp.dot(act_s[...], wd_bf[:, h * hw:(h + 1) * hw], preferred_element_type=F32)
            y = y + bd_ref[e][:, h * hw:(h + 1) * hw]
            prev = y
            for c in range(hw // LANES):
                y_out[pl.ds(h * (hw // LANES) + c, bm, stride=ROW_TILE), :] = y[:, c * LANES:(c + 1) * LANES]

    b0 = 2 * s
    b1 = b0 + 1

    @pl.when(b0 < used)
    def _():
        @pl.when(s == 0)
        def _():
            for c in weight_copies(blk_e_ref[0]):
                c.start()
            y1[...] = jnp.zeros(y1.shape, F32)

            def first(r, c):
                scatter_start(0, 0, y1, ssem.at[0], r, 0)
                gather_start(blk_j0_ref[1], x0, gsem.at[0], r)
                return c

            lax.fori_loop(0, bm, first, 0)
            block_wait(y1, ssem.at[0])

        @pl.when(s > 0)
        def _():
            block_wait(y0, ssem.at[0])

        switch_weights(b0)
        block_wait(x0, gsem.at[0])
        block(b0, x0, y0, b1, x1, gsem.at[1], b0 - 1, y1, ssem.at[1], 1)

        block_wait(y1, ssem.at[1])
        switch_weights(b1)
        block_wait(x1, gsem.at[1])
        block(b1, x1, y1, b0 + 2, x0, gsem.at[0], b0, y0, ssem.at[0], 0)

        @pl.when(b0 + 2 >= used)
        def _():
            def last(r, c):
                scatter_start(blk_j0_ref[b1 + 1], blk_nv_ref[b1 + 1], y1, ssem.at[1], r, 1)
                return c

            lax.fori_loop(0, bm, last, 0)
            block_wait(y0, ssem.at[0])
            block_wait(y1, ssem.at[1])
            block_wait(x0, gsem.at[0])


def _expert(order, blk_e, blk_j0, blk_nv, nxt_e, used, h2, w, bm):
    n_tok = h2.shape[0]
    nblk = blk_e.shape[0]
    f32buf = lambda shp: pltpu.VMEM(shp, F32)
    return pl.pallas_call(
        functools.partial(_expert_kernel, bm=bm, n_tok=n_tok),
        grid_spec=pltpu.PrefetchScalarGridSpec(
            num_scalar_prefetch=6,
            grid=(nblk // 2,),
            in_specs=[
                pl.BlockSpec(memory_space=pl.ANY),
                pl.BlockSpec(memory_space=pl.ANY),
                pl.BlockSpec((N_EXPERTS, 1, 2 * D_FF), lambda s, *_: (0, 0, 0)),
                pl.BlockSpec(memory_space=pl.ANY),
                pl.BlockSpec((N_EXPERTS, 1, D_MODEL), lambda s, *_: (0, 0, 0)),
            ],
            out_specs=pl.BlockSpec(memory_space=pl.ANY),
            scratch_shapes=[f32buf((bm * ROW_TILE, LANES)), f32buf((bm * ROW_TILE, LANES)),
                            f32buf((bm * ROW_TILE, LANES)), f32buf((bm * ROW_TILE, LANES)),
                            pltpu.VMEM((bm, D_MODEL), BF16), pltpu.VMEM((bm, D_FF), BF16),
                            f32buf((D_MODEL, 2 * D_FF)), f32buf((D_FF, D_MODEL)),
                            pltpu.VMEM((D_MODEL, 2 * D_FF), BF16), pltpu.VMEM((D_FF, D_MODEL), BF16),
                            pltpu.SemaphoreType.DMA((2,)), pltpu.SemaphoreType.DMA((2,)),
                            pltpu.SemaphoreType.DMA((2,))],
        ),
        out_shape=jax.ShapeDtypeStruct((TOP_K * n_tok + 2 * bm, ROW_TILE, LANES), F32),
        compiler_params=_cparams(("arbitrary",)),
        name="moe_expert",
    )(order, blk_e, blk_j0, blk_nv, nxt_e, used, h2, w["w_gu"], w["b_gu"], w["w_down"], w["b_down"])


def _combine_kernel(y0_ref, y1_ref, y2_ref, y3_ref, x1_ref, gates_ref, gt2_ref, l2g_ref, l2b_ref, o_ref, *, alpha):
    gates = gates_ref[...]
    tm = gates.shape[0]
    f = gates[:, 0:1] * _rows_from_tiles(y0_ref, tm)
    for k, y_ref in enumerate((y1_ref, y2_ref, y3_ref), start=1):
        f = f + gates[:, k:k + 1] * _rows_from_tiles(y_ref, tm)
    o_ref[...] = _layernorm(alpha * x1_ref[...] + gt2_ref[0] * f) * l2g_ref[...] + l2b_ref[...]


def _combine(y_rows, x1, gates, gt2, w, tm, tokens_per_batch, alpha):
    N = x1.shape[0]
    per_b = tokens_per_batch // tm
    nt = N // tm
    slot = lambda k: pl.BlockSpec((tm * ROW_TILE, LANES), lambda i: (k * nt + i, 0))
    return pl.pallas_call(
        functools.partial(_combine_kernel, alpha=alpha),
        grid=(nt,),
        in_specs=[slot(0), slot(1), slot(2), slot(3),
                  pl.BlockSpec((tm, D_MODEL), lambda i: (i, 0)),
                  pl.BlockSpec((tm, LANES), lambda i: (i, 0)),
                  pl.BlockSpec((1, 1, D_MODEL), lambda i: (i // per_b, 0, 0)),
                  pl.BlockSpec((1, D_MODEL), lambda i: (0, 0)),
                  pl.BlockSpec((1, D_MODEL), lambda i: (0, 0))],
        out_specs=pl.BlockSpec((tm, D_MODEL), lambda i: (i, 0)),
        out_shape=jax.ShapeDtypeStruct((N, D_MODEL), F32),
        compiler_params=_cparams(("arbitrary",)),
        name="moe_combine",
    )(y_rows, y_rows, y_rows, y_rows, x1, gates, gt2, w["ln2_g"], w["ln2_b"])


def _moe(h2, x1, idx, gates, counts, gt2, w, tokens_per_batch, bm, tm_c, alpha):
    N = h2.shape[0]
    n_rows = N * TOP_K
    nblk = (n_rows + N_EXPERTS * (bm - 1)) // bm
    nblk += nblk % 2
    experts = jnp.arange(N_EXPERTS, dtype=jnp.int32)
    order = jnp.concatenate([jnp.argsort(idx.reshape(-1)).astype(jnp.int32), jnp.zeros((bm,), jnp.int32)])
    nb_e = (counts + bm - 1) // bm
    blk_end = jnp.cumsum(nb_e)
    first_blk = blk_end - nb_e
    start_sorted = jnp.cumsum(counts) - counts
    used = blk_end[-1].astype(jnp.int32)
    b = jnp.arange(-1, nblk + 1, dtype=jnp.int32)
    bc = jnp.clip(b, 0, used - 1)
    e = jnp.minimum(jnp.sum(blk_end[None, :] <= bc[:, None], axis=1), N_EXPERTS - 1).astype(jnp.int32)
    pick = lambda table: jnp.sum(jnp.where(e[:, None] == experts, table, 0), axis=1)
    local = bc - pick(first_blk)
    blk_j0 = (pick(start_sorted) + local * bm).astype(jnp.int32)
    blk_nv = jnp.where((b >= 0) & (b < used), jnp.minimum(bm, pick(counts) - local * bm), 0).astype(jnp.int32)
    blk_e = e[1:nblk + 1]
    later = (experts[None, :] > experts[:, None]) & (counts[None, :] > 0)
    nxt = jnp.min(jnp.where(later, experts[None, :], N_EXPERTS), axis=1)
    nxt_e = jnp.where(nxt < N_EXPERTS, nxt, -1).astype(jnp.int32)
    y_rows = _expert(order, blk_e, blk_j0, blk_nv, nxt_e, used.reshape(1), h2, w, bm)
    return _combine(y_rows.reshape(-1, LANES), x1, gates, gt2, w, tm_c, tokens_per_batch, alpha)


def _rope_tables(pos):
    half = QK_ROPE // 2
    inv = ROPE_THETA ** (-jnp.arange(half, dtype=F32) / half)
    ang = pos.astype(F32)[:, None] * inv[None, :]
    cos, sin = jnp.cos(ang), jnp.sin(ang)
    cos32 = jnp.concatenate([cos, cos], axis=1)
    sin32 = jnp.concatenate([-sin, sin], axis=1)
    return jnp.tile(cos32, (1, LANES // QK_ROPE)), jnp.tile(sin32, (1, LANES // QK_ROPE))


def _swap_halves(w32):
    shp = w32.shape
    w = w32.reshape(shp[:-1] + (shp[-1] // QK_ROPE, 2, QK_ROPE // 2))
    return w[..., ::-1, :].reshape(shp)


def _prep_weights(l, w_in, conv_w, g_qa, w_qb, g_kva, w_kvb, g_out_conv, g_out_attn, w_out,
                  ln1_g, ln1_b, router_w, router_b, w_gu, b_gu, w_down, b_down, ln2_g, ln2_b):
    w = {}
    wi = w_in[l]
    k_r = wi[:, _O_KR:_O_KR + QK_ROPE]
    rep = LANES // QK_ROPE
    w["w_in"] = jnp.concatenate([wi[:, :_O_KR], jnp.tile(k_r, (1, rep)), jnp.tile(_swap_halves(k_r), (1, rep))],
                                axis=1).astype(BF16)
    w["conv_w"] = conv_w[l]
    w["g_qa"] = g_qa[l].reshape(1, Q_LORA)
    w["g_kva"] = g_kva[l].reshape(1, KV_LORA)
    w["g_oc"] = g_out_conv[l].reshape(1, CONV_WIDTH)
    w["g_oa"] = g_out_attn[l].reshape(1, ATTN_WIDTH)
    wq = w_qb[l].reshape(Q_LORA, N_HEADS, QK_NOPE + QK_ROPE)
    w["wq_nope"] = wq[:, :, :QK_NOPE].reshape(Q_LORA, N_HEADS * QK_NOPE).astype(BF16)
    wq_rope = wq[:, :, QK_NOPE:].reshape(Q_LORA, N_HEADS * QK_ROPE)
    w["wq_rope"] = wq_rope.astype(BF16)
    w["wq_rope_sw"] = _swap_halves(wq_rope).astype(BF16)
    w_uk = jnp.transpose(w_kvb[l][:, :, :QK_NOPE], (1, 2, 0))
    w_uv = jnp.transpose(w_kvb[l][:, :, QK_NOPE:], (1, 0, 2))
    zk = jnp.zeros((QK_NOPE, KV_LORA), F32)
    zv = jnp.zeros((KV_LORA, V_HEAD), F32)
    w["w_uk_pair"] = jnp.stack([
        jnp.concatenate([jnp.concatenate([w_uk[2 * p], zk], axis=1),
                         jnp.concatenate([zk, w_uk[2 * p + 1]], axis=1)], axis=0)
        for p in range(N_HEADS // 2)]).astype(BF16)
    w["w_uv_pair"] = jnp.stack([
        jnp.concatenate([jnp.concatenate([w_uv[2 * p], zv], axis=1),
                         jnp.concatenate([zv, w_uv[2 * p + 1]], axis=1)], axis=0)
        for p in range(N_HEADS // 2)]).astype(BF16)
    w["w_out"] = w_out[l].astype(BF16)
    w["ln1_g"] = ln1_g[l].reshape(1, D_MODEL)
    w["ln1_b"] = ln1_b[l].reshape(1, D_MODEL)
    w["ln2_g"] = ln2_g[l].reshape(1, D_MODEL)
    w["ln2_b"] = ln2_b[l].reshape(1, D_MODEL)
    w["router_w"] = jnp.pad(router_w[l], ((0, 0), (0, LANES - N_EXPERTS))).astype(BF16)
    w["router_b"] = jnp.concatenate([router_b[l], jnp.full((LANES - N_EXPERTS,), NEG, F32)]).reshape(1, LANES)
    w["w_gu"] = w_gu[l]
    w["b_gu"] = b_gu[l].reshape(N_EXPERTS, 1, 2 * D_FF)
    w["w_down"] = w_down[l]
    w["b_down"] = b_down[l].reshape(N_EXPERTS, 1, D_MODEL)
    return w


def _layer(x, mod, conv_prev, past, pos0, w, alpha, *, tm_in, tm_out, bm, tm_c, tq=128, tk=512):
    B, T, _ = x.shape
    sh1, sc1, gt1, sh2, sc2, gt2 = [mod[:, None, i * D_MODEL:(i + 1) * D_MODEL] for i in range(N_MOD)]
    cos_t, sin_t = _rope_tables(pos0 + jnp.arange(T, dtype=jnp.int32))
    a_n, qlat, qrope, kcat, ckv, krope, conv_new = _inproj(x, sc1, sh1, conv_prev, cos_t, sin_t, w, tm_in)
    if past is None:
        b_n = _attn_prompt(qlat, qrope, kcat, w, tq, tk)
    else:
        b_n = _attn_sample(qlat, qrope, kcat, past[0], past[1], w, tk)
    x1, h2, meta, gates, cnt = _outproj(x, a_n, b_n, gt1, sc2, sh2, w, tm_out, alpha)
    N = B * T
    counts = cnt[0, :N_EXPERTS].astype(jnp.int32)
    y = _moe(h2.reshape(N, ROW_TILE, LANES), x1.reshape(N, D_MODEL), meta.reshape(N, LANES)[:, :TOP_K],
             gates.reshape(N, LANES), counts, gt2, w, T, bm, tm_c, alpha)
    return y.reshape(B, T, D_MODEL), ckv, krope, conv_new


def kernel(x_prompt, x_sample, c_prompt, c_sample, cache_ckv, cache_krope, state_conv, w_ada, b_ada, w_in, conv_w, g_qa, w_qb, g_kva, w_kvb, g_out_conv, g_out_attn, w_out, ln1_g, ln1_b, router_w, router_b, w_gu, b_gu, w_down, b_down, ln2_g, ln2_b):
    depth = w_ada.shape[0]
    Bp, Tp, _ = x_prompt.shape
    Bs, Ts, _ = x_sample.shape
    past_len = cache_ckv.shape[2]
    assert Ts == CHUNK and past_len % CHUNK == 0 and Tp % 512 == 0
    alpha = (2.0 * depth) ** 0.25
    xp, xs = x_prompt, x_sample
    outs = [[] for _ in range(6)]
    c_all = jnp.concatenate([c_prompt, c_sample, jnp.zeros((16 - Bp - Bs, D_MODEL), F32)], axis=0)
    for l in range(depth):
        w = _prep_weights(l, w_in, conv_w, g_qa, w_qb, g_kva, w_kvb, g_out_conv, g_out_attn, w_out,
                          ln1_g, ln1_b, router_w, router_b, w_gu, b_gu, w_down, b_down, ln2_g, ln2_b)
        mod = _ada(c_all, w_ada[l], b_ada[l])
        xp, ckv_p, kr_p, cv_p = _layer(xp, mod[:Bp], jnp.zeros((Bp, CONV_K - 1, CONV_WIDTH), F32), None, 0, w, alpha,
                                       tm_in=512, tm_out=256, bm=256, tm_c=256)
        xs, ckv_s, kr_s, cv_s = _layer(xs, mod[Bp:Bp + Bs], state_conv[l], (cache_ckv[l], cache_krope[l]),
                                       past_len, w, alpha, tm_in=Ts, tm_out=Ts, bm=128, tm_c=Ts)
        for o, v in zip(outs, (ckv_p, kr_p, cv_p, ckv_s, kr_s, cv_s)):
            o.append(v)
    return (xp, xs) + tuple(jnp.stack(o) for o in outs)
```

```python
import functools
import math

import jax
import jax.numpy as jnp
from jax import lax
from jax.experimental import pallas as pl
from jax.experimental.pallas import tpu as pltpu

F32 = jnp.float32
BF16 = jnp.bfloat16

D_MODEL = 1024
CHUNK = 64
CONV_WIDTH = 512
CONV_K = 3
N_HEADS = 8
QK_NOPE = 64
QK_ROPE = 32
V_HEAD = 64
Q_LORA = 256
KV_LORA = 128
ATTN_WIDTH = N_HEADS * V_HEAD
ROPE_THETA = 10000.0
ATTN_SCALE = 1.0 / math.sqrt(QK_NOPE + QK_ROPE)
Q_SCALE = ATTN_SCALE * math.log2(math.e)
N_EXPERTS = 32
TOP_K = 4
D_FF = 1024
SWIGLU_LIMIT = 7.0
SWIGLU_ALPHA = 1.702
N_MOD = 6
LN_EPS = 1e-5
RMS_EPS = 1e-6

LANES = 128
ATTN_GROUPS = 1
FFN_PARTS = 6
NEG = -1e30
VMEM_LIMIT = 56 * 1024 * 1024

_O_XB, _O_XC, _O_XV = 0, CONV_WIDTH, 2 * CONV_WIDTH
_O_QA = 3 * CONV_WIDTH
_O_KVA = _O_QA + Q_LORA
_O_KR = _O_KVA + KV_LORA
_O_KRS = _O_KR + LANES
IN_COLS_EXT = _O_KRS + LANES


def _cparams(sem):
    return pltpu.CompilerParams(dimension_semantics=sem, vmem_limit_bytes=VMEM_LIMIT)


def _layernorm(x):
    mu = jnp.mean(x, axis=-1, keepdims=True)
    xc = x - mu
    var = jnp.mean(xc * xc, axis=-1, keepdims=True)
    return xc * lax.rsqrt(var + LN_EPS)


def _rms(x):
    return x * lax.rsqrt(jnp.mean(x * x, axis=-1, keepdims=True) + RMS_EPS)


def _sigmoid(x):
    return 1.0 / (1.0 + jnp.exp(-x))


ROW_TILE = D_MODEL // LANES


def _rows_from_tiles(ref, n):
    return jnp.concatenate([ref[pl.ds(c, n, stride=ROW_TILE), :] for c in range(ROW_TILE)], axis=1)


def _rows_to_tiles(ref, x):
    n = x.shape[0]
    for c in range(ROW_TILE):
        ref[pl.ds(c, n, stride=ROW_TILE), :] = x[:, c * LANES:(c + 1) * LANES]


def _ada_kernel(c_ref, w_ref, b_ref, o_ref):
    c = c_ref[...]
    s = (c * _sigmoid(c)).astype(BF16)
    o_ref[...] = jnp.dot(s, w_ref[...].astype(BF16), preferred_element_type=F32) + b_ref[...]


def _ada(c_all, w_ada, b_ada):
    rows = c_all.shape[0]
    ncol = w_ada.shape[1]
    tn = 1024
    return pl.pallas_call(
        _ada_kernel,
        grid=(ncol // tn,),
        in_specs=[pl.BlockSpec((rows, D_MODEL), lambda j: (0, 0)),
                  pl.BlockSpec((D_MODEL, tn), lambda j: (0, j)),
                  pl.BlockSpec((1, tn), lambda j: (0, j))],
        out_specs=pl.BlockSpec((rows, tn), lambda j: (0, j)),
        out_shape=jax.ShapeDtypeStruct((rows, ncol), F32),
        compiler_params=_cparams(("arbitrary",)),
        name="ada",
    )(c_all, w_ada, b_ada.reshape(1, ncol))


def _inproj_kernel(x_ref, sc_ref, sh_ref, win_ref, cw_ref, cprev_ref, gqa_ref, gkva_ref, goc_ref,
                   wqn_ref, wqr_ref, wqrs_ref, wuk_ref, cos_ref, sin_ref,
                   an_ref, qlat_ref, qrope_ref, kcat_ref, ckv_ref, krope_ref, cnew_ref,
                   ubuf, *, tm):
    j = pl.program_id(1)
    x = x_ref[0]
    h = _layernorm(x) * (1.0 + sc_ref[0]) + sh_ref[0]
    proj = jnp.dot(h.astype(BF16), win_ref[...], preferred_element_type=F32)
    xb = proj[:, _O_XB:_O_XB + CONV_WIDTH]
    xc = proj[:, _O_XC:_O_XC + CONV_WIDTH]
    xv = proj[:, _O_XV:_O_XV + CONV_WIDTH]
    q_a = proj[:, _O_QA:_O_QA + Q_LORA]
    kv_a = proj[:, _O_KVA:_O_KVA + KV_LORA]
    kr4 = proj[:, _O_KR:_O_KR + LANES]
    kr4s = proj[:, _O_KRS:_O_KRS + LANES]

    u = xc * xv

    @pl.when(j == 0)
    def _():
        ubuf[6:8, :] = cprev_ref[0]

    ubuf[8:8 + tm, :] = u
    conv = (cw_ref[0:1, :] * ubuf[6:6 + tm, :] + cw_ref[1:2, :] * ubuf[7:7 + tm, :]
            + cw_ref[2:3, :] * u)
    ubuf[0:8, :] = ubuf[tm:tm + 8, :]
    cnew_ref[0] = u[tm - (CONV_K - 1):tm, :]
    an_ref[0] = (_rms(xb * conv) * goc_ref[...]).astype(BF16)

    cos = cos_ref[...]
    sin = sin_ref[...]

    ckv = _rms(kv_a) * gkva_ref[...]
    kro4 = kr4 * cos + kr4s * sin
    ckv_ref[0] = ckv
    krope_ref[0] = kro4[:, :QK_ROPE]
    kcat_ref[0] = jnp.concatenate([ckv, kro4], axis=1).astype(BF16)

    qn = (_rms(q_a) * gqa_ref[...]).astype(BF16)
    q_nope = jnp.dot(qn, wqn_ref[...], preferred_element_type=F32)
    xr = jnp.dot(qn, wqr_ref[...], preferred_element_type=F32)
    xrs = jnp.dot(qn, wqrs_ref[...], preferred_element_type=F32)
    for g in range(2):
        sl = slice(g * LANES, (g + 1) * LANES)
        qrope_ref[0, :, sl] = ((xr[:, sl] * cos + xrs[:, sl] * sin) * Q_SCALE).astype(BF16)
    for p in range(N_HEADS // 2):
        qp = q_nope[:, p * LANES:(p + 1) * LANES].astype(BF16)
        ql = jnp.dot(qp, wuk_ref[p], preferred_element_type=F32)
        qlat_ref[0, :, p * 2 * KV_LORA:(p + 1) * 2 * KV_LORA] = (ql * Q_SCALE).astype(BF16)


def _inproj(x, sc1, sh1, conv_prev, cos_t, sin_t, w, tm):
    B, T, _ = x.shape
    nt = T // tm
    full = lambda shp: pl.BlockSpec(shp, lambda b, j: (0,) * len(shp))
    out_shapes = (
        jax.ShapeDtypeStruct((B, T, CONV_WIDTH), BF16),
        jax.ShapeDtypeStruct((B, T, N_HEADS * KV_LORA), BF16),
        jax.ShapeDtypeStruct((B, T, 2 * LANES), BF16),
        jax.ShapeDtypeStruct((B, T, 2 * LANES), BF16),
        jax.ShapeDtypeStruct((B, T, KV_LORA), F32),
        jax.ShapeDtypeStruct((B, T, QK_ROPE), F32),
        jax.ShapeDtypeStruct((B, CONV_K - 1, CONV_WIDTH), F32),
    )
    tile = lambda c: pl.BlockSpec((1, tm, c), lambda b, j: (b, j, 0))
    return pl.pallas_call(
        functools.partial(_inproj_kernel, tm=tm),
        grid=(B, nt),
        in_specs=[
            tile(D_MODEL),
            pl.BlockSpec((1, 1, D_MODEL), lambda b, j: (b, 0, 0)),
            pl.BlockSpec((1, 1, D_MODEL), lambda b, j: (b, 0, 0)),
            full((D_MODEL, IN_COLS_EXT)),
            full((CONV_K, CONV_WIDTH)),
            pl.BlockSpec((1, CONV_K - 1, CONV_WIDTH), lambda b, j: (b, 0, 0)),
            full((1, Q_LORA)), full((1, KV_LORA)), full((1, CONV_WIDTH)),
            full((Q_LORA, N_HEADS * QK_NOPE)), full((Q_LORA, 2 * LANES)), full((Q_LORA, 2 * LANES)),
            full((N_HEADS // 2, LANES, 2 * KV_LORA)),
            pl.BlockSpec((tm, LANES), lambda b, j: (j, 0)),
            pl.BlockSpec((tm, LANES), lambda b, j: (j, 0)),
        ],
        out_specs=(tile(CONV_WIDTH), tile(N_HEADS * KV_LORA), tile(2 * LANES), tile(2 * LANES),
                   tile(KV_LORA), tile(QK_ROPE),
                   pl.BlockSpec((1, CONV_K - 1, CONV_WIDTH), lambda b, j: (b, 0, 0))),
        out_shape=out_shapes,
        scratch_shapes=[pltpu.VMEM((tm + 8, CONV_WIDTH), F32)],
        compiler_params=_cparams(("arbitrary", "arbitrary")),
        name="inproj",
    )(x, sc1, sh1, w["w_in"], w["conv_w"], conv_prev, w["g_qa"], w["g_kva"], w["g_oc"],
      w["wq_nope"], w["wq_rope"], w["wq_rope_sw"], w["w_uk_pair"], cos_t, sin_t)


def _stack_queries(qlat_ref, qrope_ref, qs, tq):
    lane = lax.broadcasted_iota(jnp.int32, (tq, LANES), 1)
    for h in range(N_HEADS):
        g, i = divmod(h, 4)
        rope = qrope_ref[0, :, g * LANES:(g + 1) * LANES]
        keep = (lane >= i * QK_ROPE) & (lane < (i + 1) * QK_ROPE)
        qs[h * tq:(h + 1) * tq, 0:KV_LORA] = qlat_ref[0, :, h * KV_LORA:(h + 1) * KV_LORA]
        qs[h * tq:(h + 1) * tq, KV_LORA:KV_LORA + LANES] = jnp.where(keep, rope, jnp.zeros_like(rope))


def _softmax_step(qs, k, v, m_s, l_s, acc_s, mask=None, groups=1):
    tk = k.shape[0]
    rows = qs.shape[0] // groups
    for g in range(groups):
        r = slice(g * rows, (g + 1) * rows)
        s = lax.dot_general(qs[r, :], k, (((1,), (1,)), ((), ())), preferred_element_type=F32)
        if mask is not None:
            col, limit = mask
            s = jnp.where(col < limit[r], s, NEG)
        m_prev = m_s[r, :]
        m_new = jnp.maximum(m_prev, jnp.max(s, axis=1, keepdims=True))
        alpha = jnp.exp2(m_prev - m_new)
        if tk % LANES == 0:
            p = jnp.exp2(s - jnp.tile(m_new, (1, tk // LANES)))
        else:
            p = jnp.exp2(s - m_new[:, :tk])
        l_s[r, :] = alpha * l_s[r, :] + jnp.sum(p, axis=1, keepdims=True)
        acc_s[r, :] = alpha * acc_s[r, :] + jnp.dot(p.astype(BF16), v, preferred_element_type=F32)
        m_s[r, :] = m_new


def _attn_epilogue(acc_s, l_s, wuv_ref, g_ref, o_ref, tq):
    o = acc_s[...] / l_s[...]
    parts = []
    for p in range(N_HEADS // 2):
        op = jnp.concatenate([o[(2 * p) * tq:(2 * p + 1) * tq], o[(2 * p + 1) * tq:(2 * p + 2) * tq]], axis=1)
        parts.append(jnp.dot(op.astype(BF16), wuv_ref[p], preferred_element_type=F32))
    b = jnp.concatenate(parts, axis=1)
    o_ref[0] = (_rms(b) * g_ref[...]).astype(BF16)


def _attn_prompt_kernel(qlat_ref, qrope_ref, k_ref, wuv_ref, g_ref, o_ref, qs, m_s, l_s, acc_s, *, tq, tk,
                        groups):
    i = pl.program_id(1)
    M = N_HEADS * tq
    _stack_queries(qlat_ref, qrope_ref, qs, tq)
    m_s[...] = jnp.full(m_s.shape, NEG, F32)
    l_s[...] = jnp.zeros(l_s.shape, F32)
    acc_s[...] = jnp.zeros(acc_s.shape, F32)
    q0 = i * tq
    n_full = (q0 + CHUNK) // tk

    def body(t, carry):
        start = pl.multiple_of(t * tk, tk)
        k = k_ref[0, pl.ds(start, tk), :]
        _softmax_step(qs, k, k[:, :KV_LORA], m_s, l_s, acc_s, groups=groups)
        return carry

    lax.fori_loop(0, n_full, body, 0)

    start = pl.multiple_of(n_full * tk, tk)
    k = k_ref[0, pl.ds(start, tk), :]
    col = start + lax.broadcasted_iota(jnp.int32, (1, tk), 1)
    row_t = jnp.bitwise_and(lax.broadcasted_iota(jnp.int32, (M, 1), 0), tq - 1)
    limit = q0 + (jnp.right_shift(row_t, CHUNK.bit_length() - 1) + 1) * CHUNK
    _softmax_step(qs, k, k[:, :KV_LORA], m_s, l_s, acc_s, mask=(col, limit), groups=groups)
    _attn_epilogue(acc_s, l_s, wuv_ref, g_ref, o_ref, tq)


def _attn_prompt(qlat, qrope, kcat, w, tq, tk):
    B, T, _ = qlat.shape
    M = N_HEADS * tq
    return pl.pallas_call(
        functools.partial(_attn_prompt_kernel, tq=tq, tk=tk, groups=ATTN_GROUPS),
        grid=(B, T // tq),
        in_specs=[
            pl.BlockSpec((1, tq, N_HEADS * KV_LORA), lambda b, i: (b, i, 0)),
            pl.BlockSpec((1, tq, 2 * LANES), lambda b, i: (b, i, 0)),
            pl.BlockSpec((1, T, 2 * LANES), lambda b, i: (b, 0, 0)),
            pl.BlockSpec((N_HEADS // 2, 2 * KV_LORA, LANES), lambda b, i: (0, 0, 0)),
            pl.BlockSpec((1, ATTN_WIDTH), lambda b, i: (0, 0)),
        ],
        out_specs=pl.BlockSpec((1, tq, ATTN_WIDTH), lambda b, i: (b, i, 0)),
        out_shape=jax.ShapeDtypeStruct((B, T, ATTN_WIDTH), BF16),
        scratch_shapes=[pltpu.VMEM((M, 2 * LANES), BF16), pltpu.VMEM((M, LANES), F32),
                        pltpu.VMEM((M, LANES), F32), pltpu.VMEM((M, KV_LORA), F32)],
        compiler_params=_cparams(("arbitrary", "arbitrary")),
        name="attn_prompt",
    )(qlat, qrope, kcat, w["w_uv_pair"], w["g_oa"])


def _attn_sample_kernel(qlat_ref, qrope_ref, knew_ref, pckv_ref, pkr_ref, wuv_ref, g_ref, o_ref,
                        qs, m_s, l_s, acc_s, *, tq, tk, n_past):
    _stack_queries(qlat_ref, qrope_ref, qs, tq)
    m_s[...] = jnp.full(m_s.shape, NEG, F32)
    l_s[...] = jnp.zeros(l_s.shape, F32)
    acc_s[...] = jnp.zeros(acc_s.shape, F32)

    def body(t, carry):
        start = pl.multiple_of(t * tk, tk)
        ck = pckv_ref[0, pl.ds(start, tk), :]
        kr = pkr_ref[0, pl.ds(start, tk), :]
        k = jnp.concatenate([ck, kr, kr, kr, kr], axis=1).astype(BF16)
        _softmax_step(qs, k, k[:, :KV_LORA], m_s, l_s, acc_s)
        return carry

    lax.fori_loop(0, n_past // tk, body, 0)
    k = knew_ref[0]
    _softmax_step(qs, k, k[:, :KV_LORA], m_s, l_s, acc_s)
    _attn_epilogue(acc_s, l_s, wuv_ref, g_ref, o_ref, tq)


def _attn_sample(qlat, qrope, kcat, past_ckv, past_krope, w, tk):
    B, T, _ = qlat.shape
    n_past = past_ckv.shape[1]
    M = N_HEADS * T
    per_b = lambda r, c: pl.BlockSpec((1, r, c), lambda b: (b, 0, 0))
    return pl.pallas_call(
        functools.partial(_attn_sample_kernel, tq=T, tk=tk, n_past=n_past),
        grid=(B,),
        in_specs=[per_b(T, N_HEADS * KV_LORA), per_b(T, 2 * LANES), per_b(T, 2 * LANES),
                  per_b(n_past, KV_LORA), per_b(n_past, QK_ROPE),
                  pl.BlockSpec((N_HEADS // 2, 2 * KV_LORA, LANES), lambda b: (0, 0, 0)),
                  pl.BlockSpec((1, ATTN_WIDTH), lambda b: (0, 0))],
        out_specs=per_b(T, ATTN_WIDTH),
        out_shape=jax.ShapeDtypeStruct((B, T, ATTN_WIDTH), BF16),
        scratch_shapes=[pltpu.VMEM((M, 2 * LANES), BF16), pltpu.VMEM((M, LANES), F32),
                        pltpu.VMEM((M, LANES), F32), pltpu.VMEM((M, KV_LORA), F32)],
        compiler_params=_cparams(("arbitrary",)),
        name="attn_sample",
    )(qlat, qrope, kcat, past_ckv, past_krope, w["w_uv_pair"], w["g_oa"])


def _outproj_kernel(x_ref, an_ref, bn_ref, gt1_ref, sc2_ref, sh2_ref, wo_ref, l1g_ref, l1b_ref,
                    rw_ref, rb_ref, x1_ref, h2_ref, meta_ref, gates_ref, cnt_ref, run_s,
                    *, tm, alpha):
    first = (pl.program_id(0) == 0) & (pl.program_id(1) == 0)

    @pl.when(first)
    def _():
        run_s[...] = jnp.zeros(run_s.shape, F32)

    m = (jnp.dot(an_ref[0], wo_ref[0:CONV_WIDTH, :], preferred_element_type=F32)
         + jnp.dot(bn_ref[0], wo_ref[CONV_WIDTH:, :], preferred_element_type=F32))
    x1 = _layernorm(alpha * x_ref[0] + gt1_ref[0] * m) * l1g_ref[...] + l1b_ref[...]
    x1_ref[0] = x1
    h2 = _layernorm(x1) * (1.0 + sc2_ref[0]) + sh2_ref[0]
    _rows_to_tiles(h2_ref, h2)

    logits = jnp.dot(h2.astype(BF16), rw_ref[...], preferred_element_type=F32) + rb_ref[...]
    lane = lax.broadcasted_iota(jnp.int32, (tm, LANES), 1)
    lane_f = lane.astype(F32)
    lg = logits
    vals, sels = [], []
    chosen = jnp.zeros((tm, LANES), F32)
    for _ in range(TOP_K):
        mx = jnp.max(lg, axis=1, keepdims=True)
        idx = jnp.min(jnp.where(lg == mx, lane_f, float(LANES)), axis=1, keepdims=True)
        sel = lane_f == idx
        vals.append(mx)
        sels.append(idx)
        chosen = jnp.where(sel, 1.0, chosen)
        lg = jnp.where(sel, NEG, lg)

    es = [jnp.exp(v - vals[0]) for v in vals]
    denom = es[0] + es[1] + es[2] + es[3]

    run_s[0:1, :] = run_s[0:1, :] + jnp.sum(chosen, axis=0, keepdims=True)
    cnt_ref[...] = jnp.broadcast_to(run_s[0:1, :], cnt_ref.shape)

    meta = jnp.zeros((tm, LANES), jnp.int32)
    gates = jnp.zeros((tm, LANES), F32)
    for k in range(TOP_K):
        meta = jnp.where(lane == k, sels[k].astype(jnp.int32), meta)
        gates = jnp.where(lane == k, es[k] / denom, gates)
    meta_ref[0] = meta
    gates_ref[0] = gates


def _outproj(x, a_n, b_n, gt1, sc2, sh2, w, tm, alpha):
    B, T, _ = x.shape
    tile = lambda c: pl.BlockSpec((1, tm, c), lambda b, j: (b, j, 0))
    modv = pl.BlockSpec((1, 1, D_MODEL), lambda b, j: (b, 0, 0))
    full = lambda shp: pl.BlockSpec(shp, lambda b, j: (0,) * len(shp))
    return pl.pallas_call(
        functools.partial(_outproj_kernel, tm=tm, alpha=alpha),
        grid=(B, T // tm),
        in_specs=[tile(D_MODEL), tile(CONV_WIDTH), tile(ATTN_WIDTH), modv, modv, modv,
                  full((CONV_WIDTH + ATTN_WIDTH, D_MODEL)), full((1, D_MODEL)), full((1, D_MODEL)),
                  full((D_MODEL, LANES)), full((1, LANES))],
        out_specs=(tile(D_MODEL), pl.BlockSpec((tm * ROW_TILE, LANES), lambda b, j: (b * (T // tm) + j, 0)),
                   tile(LANES), tile(LANES), pl.BlockSpec((8, LANES), lambda b, j: (0, 0))),
        out_shape=(jax.ShapeDtypeStruct((B, T, D_MODEL), F32),
                   jax.ShapeDtypeStruct((B * T * ROW_TILE, LANES), F32),
                   jax.ShapeDtypeStruct((B, T, LANES), jnp.int32), jax.ShapeDtypeStruct((B, T, LANES), F32),
                   jax.ShapeDtypeStruct((8, LANES), F32)),
        scratch_shapes=[pltpu.VMEM((8, LANES), F32)],
        compiler_params=_cparams(("arbitrary", "arbitrary")),
        name="outproj",
    )(x, a_n, b_n, gt1, sc2, sh2, w["w_out"], w["ln1_g"], w["ln1_b"], w["router_w"], w["router_b"])


def _expert_kernel(order_ref, blk_e_ref, blk_j0_ref, blk_nv_ref, nxt_e_ref, used_ref,
                   h_ref, wgu_hbm, bgu_ref, wd_hbm, bd_ref, y_ref,
                   x0, x1, y0, y1, xb, act_s, wgu_st, wd_st, wgu_bf, wd_bf, gsem, ssem, wsem, *, bm, n_tok):
    s = pl.program_id(0)
    used = used_ref[0]
    dump0 = TOP_K * n_tok

    def weight_copies(e):
        return (pltpu.make_async_copy(wgu_hbm.at[e], wgu_st, wsem.at[0]),
                pltpu.make_async_copy(wd_hbm.at[e], wd_st, wsem.at[1]))

    def gather_start(j0, xbuf, sem, r, prio=0):
        tok = jnp.right_shift(order_ref[j0 + r], 2)
        pltpu.make_async_copy(h_ref.at[tok], xbuf.at[pl.ds(r * ROW_TILE, ROW_TILE)], sem).start(priority=prio)

    def scatter_start(j0, nv, ybuf, sem, r, odd, prio=0):
        o = order_ref[j0 + r]
        real = jnp.bitwise_and(o, TOP_K - 1) * n_tok + jnp.right_shift(o, 2)
        dump = dump0 + odd * bm + r
        valid = jnp.right_shift(r - nv, 31)
        dst = dump + jnp.bitwise_and(valid, real - dump)
        pltpu.make_async_copy(ybuf.at[pl.ds(r * ROW_TILE, ROW_TILE)], y_ref.at[dst], sem).start(priority=prio)

    def block_wait(buf, sem):
        pltpu.make_async_copy(buf, buf, sem).wait()

    def switch_weights(b):
        prev = blk_e_ref[jnp.maximum(b - 1, 0)]
        e = blk_e_ref[b]

        @pl.when((b == 0) | (e != prev))
        def _():
            for c in weight_copies(e):
                c.wait()
            wgu_bf[...] = wgu_st[...].astype(BF16)
            wd_bf[...] = wd_st[...].astype(BF16)

            @pl.when(nxt_e_ref[e] >= 0)
            def _():
                for c in weight_copies(nxt_e_ref[e]):
                    c.start()

    def block(b, x_in, y_out, g_next, x_next, gsem_next, s_prev, y_prev, ssem_prev, prev_odd):
        gj0 = blk_j0_ref[g_next + 1]
        sj0 = blk_j0_ref[s_prev + 1]
        snv = blk_nv_ref[s_prev + 1]
        def issue(part, after=None):
            zero = 0
            if after is not None:
                bits = pltpu.bitcast(jnp.abs(after[0:8, 0:LANES]), jnp.int32)
                zero = jnp.minimum(bits[0, 0], 0)
            for r in range(part * bm // FFN_PARTS, (part + 1) * bm // FFN_PARTS):
                gather_start(gj0 + zero, x_next, gsem_next, r, prio=r % 2)
                scatter_start(sj0 + zero, snv, y_prev, ssem_prev, r, prev_odd, prio=(r + 1) % 2)

        xb[...] = _rows_from_tiles(x_in, bm).astype(BF16)
        e = blk_e_ref[b]
        cw = D_FF // 4
        prev = None
        for c in range(4):
            issue(c, prev)
            gs, ls = slice(c * cw, (c + 1) * cw), slice(D_FF + c * cw, D_FF + (c + 1) * cw)
            g = jnp.dot(xb[...], wgu_bf[:, gs], preferred_element_type=F32) + bgu_ref[e][:, gs]
            lin = jnp.dot(xb[...], wgu_bf[:, ls], preferred_element_type=F32) + bgu_ref[e][:, ls]
            prev = g
            g = jnp.minimum(g, SWIGLU_LIMIT)
            lin = jnp.clip(lin, -SWIGLU_LIMIT, SWIGLU_LIMIT)
            act_s[:, gs] = (g * _sigmoid(SWIGLU_ALPHA * g) * (lin + 1.0)).astype(BF16)
        hw = D_MODEL // 2
        for h in range(2):
            issue(4 + h, prev)
            y = jnp.dot(act_s[...], wd_bf[:, h * hw:(h + 1) * hw], preferred_element_type=F32)
            y = y + bd_ref[e][:, h * hw:(h + 1) * hw]
            prev = y
            for c in range(hw // LANES):
                y_out[pl.ds(h * (hw // LANES) + c, bm, stride=ROW_TILE), :] = y[:, c * LANES:(c + 1) * LANES]

    b0 = 2 * s
    b1 = b0 + 1

    @pl.when(b0 < used)
    def _():
        @pl.when(s == 0)
        def _():
            for c in weight_copies(blk_e_ref[0]):
                c.start()
            y1[...] = jnp.zeros(y1.shape, F32)

            def first(r, c):
                scatter_start(0, 0, y1, ssem.at[0], r, 0)
                gather_start(blk_j0_ref[1], x0, gsem.at[0], r)
                return c

            lax.fori_loop(0, bm, first, 0)
            block_wait(y1, ssem.at[0])

        @pl.when(s > 0)
        def _():
            block_wait(y0, ssem.at[0])

        switch_weights(b0)
        block_wait(x0, gsem.at[0])
        block(b0, x0, y0, b1, x1, gsem.at[1], b0 - 1, y1, ssem.at[1], 1)

        block_wait(y1, ssem.at[1])
        switch_weights(b1)
        block_wait(x1, gsem.at[1])
        block(b1, x1, y1, b0 + 2, x0, gsem.at[0], b0, y0, ssem.at[0], 0)

        @pl.when(b0 + 2 >= used)
        def _():
            def last(r, c):
                scatter_start(blk_j0_ref[b1 + 1], blk_nv_ref[b1 + 1], y1, ssem.at[1], r, 1)
                return c

            lax.fori_loop(0, bm, last, 0)
            block_wait(y0, ssem.at[0])
            block_wait(y1, ssem.at[1])
            block_wait(x0, gsem.at[0])


def _expert(order, blk_e, blk_j0, blk_nv, nxt_e, used, h2, w, bm):
    n_tok = h2.shape[0]
    nblk = blk_e.shape[0]
    f32buf = lambda shp: pltpu.VMEM(shp, F32)
    return pl.pallas_call(
        functools.partial(_expert_kernel, bm=bm, n_tok=n_tok),
        grid_spec=pltpu.PrefetchScalarGridSpec(
            num_scalar_prefetch=6,
            grid=(nblk // 2,),
            in_specs=[
                pl.BlockSpec(memory_space=pl.ANY),
                pl.BlockSpec(memory_space=pl.ANY),
                pl.BlockSpec((N_EXPERTS, 1, 2 * D_FF), lambda s, *_: (0, 0, 0)),
                pl.BlockSpec(memory_space=pl.ANY),
                pl.BlockSpec((N_EXPERTS, 1, D_MODEL), lambda s, *_: (0, 0, 0)),
            ],
            out_specs=pl.BlockSpec(memory_space=pl.ANY),
            scratch_shapes=[f32buf((bm * ROW_TILE, LANES)), f32buf((bm * ROW_TILE, LANES)),
                            f32buf((bm * ROW_TILE, LANES)), f32buf((bm * ROW_TILE, LANES)),
                            pltpu.VMEM((bm, D_MODEL), BF16), pltpu.VMEM((bm, D_FF), BF16),
                            f32buf((D_MODEL, 2 * D_FF)), f32buf((D_FF, D_MODEL)),
                            pltpu.VMEM((D_MODEL, 2 * D_FF), BF16), pltpu.VMEM((D_FF, D_MODEL), BF16),
                            pltpu.SemaphoreType.DMA((2,)), pltpu.SemaphoreType.DMA((2,)),
                            pltpu.SemaphoreType.DMA((2,))],
        ),
        out_shape=jax.ShapeDtypeStruct((TOP_K * n_tok + 2 * bm, ROW_TILE, LANES), F32),
        compiler_params=_cparams(("arbitrary",)),
        name="moe_expert",
    )(order, blk_e, blk_j0, blk_nv, nxt_e, used, h2, w["w_gu"], w["b_gu"], w["w_down"], w["b_down"])


def _combine_kernel(y0_ref, y1_ref, y2_ref, y3_ref, x1_ref, gates_ref, gt2_ref, l2g_ref, l2b_ref, o_ref, *, alpha):
    gates = gates_ref[...]
    tm = gates.shape[0]
    f = gates[:, 0:1] * _rows_from_tiles(y0_ref, tm)
    for k, y_ref in enumerate((y1_ref, y2_ref, y3_ref), start=1):
        f = f + gates[:, k:k + 1] * _rows_from_tiles(y_ref, tm)
    o_ref[...] = _layernorm(alpha * x1_ref[...] + gt2_ref[0] * f) * l2g_ref[...] + l2b_ref[...]


def _combine(y_rows, x1, gates, gt2, w, tm, tokens_per_batch, alpha):
    N = x1.shape[0]
    per_b = tokens_per_batch // tm
    nt = N // tm
    slot = lambda k: pl.BlockSpec((tm * ROW_TILE, LANES), lambda i: (k * nt + i, 0))
    return pl.pallas_call(
        functools.partial(_combine_kernel, alpha=alpha),
        grid=(nt,),
        in_specs=[slot(0), slot(1), slot(2), slot(3),
                  pl.BlockSpec((tm, D_MODEL), lambda i: (i, 0)),
                  pl.BlockSpec((tm, LANES), lambda i: (i, 0)),
                  pl.BlockSpec((1, 1, D_MODEL), lambda i: (i // per_b, 0, 0)),
                  pl.BlockSpec((1, D_MODEL), lambda i: (0, 0)),
                  pl.BlockSpec((1, D_MODEL), lambda i: (0, 0))],
        out_specs=pl.BlockSpec((tm, D_MODEL), lambda i: (i, 0)),
        out_shape=jax.ShapeDtypeStruct((N, D_MODEL), F32),
        compiler_params=_cparams(("arbitrary",)),
        name="moe_combine",
    )(y_rows, y_rows, y_rows, y_rows, x1, gates, gt2, w["ln2_g"], w["ln2_b"])


def _moe(h2, x1, idx, gates, counts, gt2, w, tokens_per_batch, bm, tm_c, alpha):
    N = h2.shape[0]
    n_rows = N * TOP_K
    nblk = (n_rows + N_EXPERTS * (bm - 1)) // bm
    nblk += nblk % 2
    experts = jnp.arange(N_EXPERTS, dtype=jnp.int32)
    order = jnp.concatenate([jnp.argsort(idx.reshape(-1)).astype(jnp.int32), jnp.zeros((bm,), jnp.int32)])
    nb_e = (counts + bm - 1) // bm
    blk_end = jnp.cumsum(nb_e)
    first_blk = blk_end - nb_e
    start_sorted = jnp.cumsum(counts) - counts
    used = blk_end[-1].astype(jnp.int32)
    b = jnp.arange(-1, nblk + 1, dtype=jnp.int32)
    bc = jnp.clip(b, 0, used - 1)
    e = jnp.minimum(jnp.sum(blk_end[None, :] <= bc[:, None], axis=1), N_EXPERTS - 1).astype(jnp.int32)
    pick = lambda table: jnp.sum(jnp.where(e[:, None] == experts, table, 0), axis=1)
    local = bc - pick(first_blk)
    blk_j0 = (pick(start_sorted) + local * bm).astype(jnp.int32)
    blk_nv = jnp.where((b >= 0) & (b < used), jnp.minimum(bm, pick(counts) - local * bm), 0).astype(jnp.int32)
    blk_e = e[1:nblk + 1]
    later = (experts[None, :] > experts[:, None]) & (counts[None, :] > 0)
    nxt = jnp.min(jnp.where(later, experts[None, :], N_EXPERTS), axis=1)
    nxt_e = jnp.where(nxt < N_EXPERTS, nxt, -1).astype(jnp.int32)
    y_rows = _expert(order, blk_e, blk_j0, blk_nv, nxt_e, used.reshape(1), h2, w, bm)
    return _combine(y_rows.reshape(-1, LANES), x1, gates, gt2, w, tm_c, tokens_per_batch, alpha)


def _rope_tables(pos):
    half = QK_ROPE // 2
    inv = ROPE_THETA ** (-jnp.arange(half, dtype=F32) / half)
    ang = pos.astype(F32)[:, None] * inv[None, :]
    cos, sin = jnp.cos(ang), jnp.sin(ang)
    cos32 = jnp.concatenate([cos, cos], axis=1)
    sin32 = jnp.concatenate([-sin, sin], axis=1)
    return jnp.tile(cos32, (1, LANES // QK_ROPE)), jnp.tile(sin32, (1, LANES // QK_ROPE))


def _swap_halves(w32):
    shp = w32.shape
    w = w32.reshape(shp[:-1] + (shp[-1] // QK_ROPE, 2, QK_ROPE // 2))
    return w[..., ::-1, :].reshape(shp)


def _prep_weights(l, w_in, conv_w, g_qa, w_qb, g_kva, w_kvb, g_out_conv, g_out_attn, w_out,
                  ln1_g, ln1_b, router_w, router_b, w_gu, b_gu, w_down, b_down, ln2_g, ln2_b):
    w = {}
    wi = w_in[l]
    k_r = wi[:, _O_KR:_O_KR + QK_ROPE]
    rep = LANES // QK_ROPE
    w["w_in"] = jnp.concatenate([wi[:, :_O_KR], jnp.tile(k_r, (1, rep)), jnp.tile(_swap_halves(k_r), (1, rep))],
                                axis=1).astype(BF16)
    w["conv_w"] = conv_w[l]
    w["g_qa"] = g_qa[l].reshape(1, Q_LORA)
    w["g_kva"] = g_kva[l].reshape(1, KV_LORA)
    w["g_oc"] = g_out_conv[l].reshape(1, CONV_WIDTH)
    w["g_oa"] = g_out_attn[l].reshape(1, ATTN_WIDTH)
    wq = w_qb[l].reshape(Q_LORA, N_HEADS, QK_NOPE + QK_ROPE)
    w["wq_nope"] = wq[:, :, :QK_NOPE].reshape(Q_LORA, N_HEADS * QK_NOPE).astype(BF16)
    wq_rope = wq[:, :, QK_NOPE:].reshape(Q_LORA, N_HEADS * QK_ROPE)
    w["wq_rope"] = wq_rope.astype(BF16)
    w["wq_rope_sw"] = _swap_halves(wq_rope).astype(BF16)
    w_uk = jnp.transpose(w_kvb[l][:, :, :QK_NOPE], (1, 2, 0))
    w_uv = jnp.transpose(w_kvb[l][:, :, QK_NOPE:], (1, 0, 2))
    zk = jnp.zeros((QK_NOPE, KV_LORA), F32)
    zv = jnp.zeros((KV_LORA, V_HEAD), F32)
    w["w_uk_pair"] = jnp.stack([
        jnp.concatenate([jnp.concatenate([w_uk[2 * p], zk], axis=1),
                         jnp.concatenate([zk, w_uk[2 * p + 1]], axis=1)], axis=0)
        for p in range(N_HEADS // 2)]).astype(BF16)
    w["w_uv_pair"] = jnp.stack([
        jnp.concatenate([jnp.concatenate([w_uv[2 * p], zv], axis=1),
                         jnp.concatenate([zv, w_uv[2 * p + 1]], axis=1)], axis=0)
        for p in range(N_HEADS // 2)]).astype(BF16)
    w["w_out"] = w_out[l].astype(BF16)
    w["ln1_g"] = ln1_g[l].reshape(1, D_MODEL)
    w["ln1_b"] = ln1_b[l].reshape(1, D_MODEL)
    w["ln2_g"] = ln2_g[l].reshape(1, D_MODEL)
    w["ln2_b"] = ln2_b[l].reshape(1, D_MODEL)
    w["router_w"] = jnp.pad(router_w[l], ((0, 0), (0, LANES - N_EXPERTS))).astype(BF16)
    w["router_b"] = jnp.concatenate([router_b[l], jnp.full((LANES - N_EXPERTS,), NEG, F32)]).reshape(1, LANES)
    w["w_gu"] = w_gu[l]
    w["b_gu"] = b_gu[l].reshape(N_EXPERTS, 1, 2 * D_FF)
    w["w_down"] = w_down[l]
    w["b_down"] = b_down[l].reshape(N_EXPERTS, 1, D_MODEL)
    return w


def _layer(x, mod, conv_prev, past, pos0, w, alpha, *, tm_in, tm_out, bm, tm_c, tq=128, tk=512):
    B, T, _ = x.shape
    sh1, sc1, gt1, sh2, sc2, gt2 = [mod[:, None, i * D_MODEL:(i + 1) * D_MODEL] for i in range(N_MOD)]
    cos_t, sin_t = _rope_tables(pos0 + jnp.arange(T, dtype=jnp.int32))
    a_n, qlat, qrope, kcat, ckv, krope, conv_new = _inproj(x, sc1, sh1, conv_prev, cos_t, sin_t, w, tm_in)
    if past is None:
        b_n = _attn_prompt(qlat, qrope, kcat, w, tq, tk)
    else:
        b_n = _attn_sample(qlat, qrope, kcat, past[0], past[1], w, tk)
    x1, h2, meta, gates, cnt = _outproj(x, a_n, b_n, gt1, sc2, sh2, w, tm_out, alpha)
    N = B * T
    counts = cnt[0, :N_EXPERTS].astype(jnp.int32)
    y = _moe(h2.reshape(N, ROW_TILE, LANES), x1.reshape(N, D_MODEL), meta.reshape(N, LANES)[:, :TOP_K],
             gates.reshape(N, LANES), counts, gt2, w, T, bm, tm_c, alpha)
    return y.reshape(B, T, D_MODEL), ckv, krope, conv_new


def kernel(x_prompt, x_sample, c_prompt, c_sample, cache_ckv, cache_krope, state_conv, w_ada, b_ada, w_in, conv_w, g_qa, w_qb, g_kva, w_kvb, g_out_conv, g_out_attn, w_out, ln1_g, ln1_b, router_w, router_b, w_gu, b_gu, w_down, b_down, ln2_g, ln2_b):
    depth = w_ada.shape[0]
    Bp, Tp, _ = x_prompt.shape
    Bs, Ts, _ = x_sample.shape
    past_len = cache_ckv.shape[2]
    assert Ts == CHUNK and past_len % CHUNK == 0 and Tp % 512 == 0
    alpha = (2.0 * depth) ** 0.25
    xp, xs = x_prompt, x_sample
    outs = [[] for _ in range(6)]
    c_all = jnp.concatenate([c_prompt, c_sample, jnp.zeros((16 - Bp - Bs, D_MODEL), F32)], axis=0)
    for l in range(depth):
        w = _prep_weights(l, w_in, conv_w, g_qa, w_qb, g_kva, w_kvb, g_out_conv, g_out_attn, w_out,
                          ln1_g, ln1_b, router_w, router_b, w_gu, b_gu, w_down, b_down, ln2_g, ln2_b)
        mod = _ada(c_all, w_ada[l], b_ada[l])
        xp, ckv_p, kr_p, cv_p = _layer(xp, mod[:Bp], jnp.zeros((Bp, CONV_K - 1, CONV_WIDTH), F32), None, 0, w, alpha,
                                       tm_in=512, tm_out=256, bm=256, tm_c=256)
        xs, ckv_s, kr_s, cv_s = _layer(xs, mod[Bp:Bp + Bs], state_conv[l], (cache_ckv[l], cache_krope[l]),
                                       past_len, w, alpha, tm_in=Ts, tm_out=Ts, bm=128, tm_c=Ts)
        for o, v in zip(outs, (ckv_p, kr_p, cv_p, ckv_s, kr_s, cv_s)):
            o.append(v)
    return (xp, xs) + tuple(jnp.stack(o) for o in outs)
```

```python
import functools
import math

import jax
import jax.numpy as jnp
from jax import lax
from jax.experimental import pallas as pl
from jax.experimental.pallas import tpu as pltpu

F32 = jnp.float32
BF16 = jnp.bfloat16

D_MODEL = 1024
CHUNK = 64
CONV_WIDTH = 512
CONV_K = 3
N_HEADS = 8
QK_NOPE = 64
QK_ROPE = 32
V_HEAD = 64
Q_LORA = 256
KV_LORA = 128
ATTN_WIDTH = N_HEADS * V_HEAD
ROPE_THETA = 10000.0
ATTN_SCALE = 1.0 / math.sqrt(QK_NOPE + QK_ROPE)
Q_SCALE = ATTN_SCALE * math.log2(math.e)
N_EXPERTS = 32
TOP_K = 4
D_FF = 1024
SWIGLU_LIMIT = 7.0
SWIGLU_ALPHA = 1.702
N_MOD = 6
LN_EPS = 1e-5
RMS_EPS = 1e-6

LANES = 128
ATTN_GROUPS = 1
FFN_PARTS = 6
NEG = -1e30
VMEM_LIMIT = 56 * 1024 * 1024

_O_XB, _O_XC, _O_XV = 0, CONV_WIDTH, 2 * CONV_WIDTH
_O_QA = 3 * CONV_WIDTH
_O_KVA = _O_QA + Q_LORA
_O_KR = _O_KVA + KV_LORA
_O_KRS = _O_KR + LANES
IN_COLS_EXT = _O_KRS + LANES


def _cparams(sem):
    return pltpu.CompilerParams(dimension_semantics=sem, vmem_limit_bytes=VMEM_LIMIT)


def _layernorm(x):
    mu = jnp.mean(x, axis=-1, keepdims=True)
    xc = x - mu
    var = jnp.mean(xc * xc, axis=-1, keepdims=True)
    return xc * lax.rsqrt(var + LN_EPS)


def _rms(x):
    return x * lax.rsqrt(jnp.mean(x * x, axis=-1, keepdims=True) + RMS_EPS)


def _sigmoid(x):
    return 1.0 / (1.0 + jnp.exp(-x))


ROW_TILE = D_MODEL // LANES


def _rows_from_tiles(ref, n):
    return jnp.concatenate([ref[pl.ds(c, n, stride=ROW_TILE), :] for c in range(ROW_TILE)], axis=1)


def _rows_to_tiles(ref, x):
    n = x.shape[0]
    for c in range(ROW_TILE):
        ref[pl.ds(c, n, stride=ROW_TILE), :] = x[:, c * LANES:(c + 1) * LANES]


def _ada_kernel(c_ref, w_ref, b_ref, o_ref):
    c = c_ref[...]
    s = (c * _sigmoid(c)).astype(BF16)
    o_ref[...] = jnp.dot(s, w_ref[...].astype(BF16), preferred_element_type=F32) + b_ref[...]


def _ada(c_all, w_ada, b_ada):
    rows = c_all.shape[0]
    ncol = w_ada.shape[1]
    tn = 1024
    return pl.pallas_call(
        _ada_kernel,
        grid=(ncol // tn,),
        in_specs=[pl.BlockSpec((rows, D_MODEL), lambda j: (0, 0)),
                  pl.BlockSpec((D_MODEL, tn), lambda j: (0, j)),
                  pl.BlockSpec((1, tn), lambda j: (0, j))],
        out_specs=pl.BlockSpec((rows, tn), lambda j: (0, j)),
        out_shape=jax.ShapeDtypeStruct((rows, ncol), F32),
        compiler_params=_cparams(("arbitrary",)),
        name="ada",
    )(c_all, w_ada, b_ada.reshape(1, ncol))


def _inproj_kernel(x_ref, sc_ref, sh_ref, win_ref, cw_ref, cprev_ref, gqa_ref, gkva_ref, goc_ref,
                   wqn_ref, wqr_ref, wqrs_ref, wuk_ref, cos_ref, sin_ref,
                   an_ref, qlat_ref, qrope_ref, kcat_ref, ckv_ref, krope_ref, cnew_ref,
                   ubuf, *, tm):
    j = pl.program_id(1)
    x = x_ref[0]
    h = _layernorm(x) * (1.0 + sc_ref[0]) + sh_ref[0]
    proj = jnp.dot(h.astype(BF16), win_ref[...], preferred_element_type=F32)
    xb = proj[:, _O_XB:_O_XB + CONV_WIDTH]
    xc = proj[:, _O_XC:_O_XC + CONV_WIDTH]
    xv = proj[:, _O_XV:_O_XV + CONV_WIDTH]
    q_a = proj[:, _O_QA:_O_QA + Q_LORA]
    kv_a = proj[:, _O_KVA:_O_KVA + KV_LORA]
    kr4 = proj[:, _O_KR:_O_KR + LANES]
    kr4s = proj[:, _O_KRS:_O_KRS + LANES]

    u = xc * xv

    @pl.when(j == 0)
    def _():
        ubuf[6:8, :] = cprev_ref[0]

    ubuf[8:8 + tm, :] = u
    conv = (cw_ref[0:1, :] * ubuf[6:6 + tm, :] + cw_ref[1:2, :] * ubuf[7:7 + tm, :]
            + cw_ref[2:3, :] * u)
    ubuf[0:8, :] = ubuf[tm:tm + 8, :]
    cnew_ref[0] = u[tm - (CONV_K - 1):tm, :]
    an_ref[0] = (_rms(xb * conv) * goc_ref[...]).astype(BF16)

    cos = cos_ref[...]
    sin = sin_ref[...]

    ckv = _rms(kv_a) * gkva_ref[...]
    kro4 = kr4 * cos + kr4s * sin
    ckv_ref[0] = ckv
    krope_ref[0] = kro4[:, :QK_ROPE]
    kcat_ref[0] = jnp.concatenate([ckv, kro4], axis=1).astype(BF16)

    qn = (_rms(q_a) * gqa_ref[...]).astype(BF16)
    q_nope = jnp.dot(qn, wqn_ref[...], preferred_element_type=F32)
    xr = jnp.dot(qn, wqr_ref[...], preferred_element_type=F32)
    xrs = jnp.dot(qn, wqrs_ref[...], preferred_element_type=F32)
    for g in range(2):
        sl = slice(g * LANES, (g + 1) * LANES)
        qrope_ref[0, :, sl] = ((xr[:, sl] * cos + xrs[:, sl] * sin) * Q_SCALE).astype(BF16)
    for p in range(N_HEADS // 2):
        qp = q_nope[:, p * LANES:(p + 1) * LANES].astype(BF16)
        ql = jnp.dot(qp, wuk_ref[p], preferred_element_type=F32)
        qlat_ref[0, :, p * 2 * KV_LORA:(p + 1) * 2 * KV_LORA] = (ql * Q_SCALE).astype(BF16)


def _inproj(x, sc1, sh1, conv_prev, cos_t, sin_t, w, tm):
    B, T, _ = x.shape
    nt = T // tm
    full = lambda shp: pl.BlockSpec(shp, lambda b, j: (0,) * len(shp))
    out_shapes = (
        jax.ShapeDtypeStruct((B, T, CONV_WIDTH), BF16),
        jax.ShapeDtypeStruct((B, T, N_HEADS * KV_LORA), BF16),
        jax.ShapeDtypeStruct((B, T, 2 * LANES), BF16),
        jax.ShapeDtypeStruct((B, T, 2 * LANES), BF16),
        jax.ShapeDtypeStruct((B, T, KV_LORA), F32),
        jax.ShapeDtypeStruct((B, T, QK_ROPE), F32),
        jax.ShapeDtypeStruct((B, CONV_K - 1, CONV_WIDTH), F32),
    )
    tile = lambda c: pl.BlockSpec((1, tm, c), lambda b, j: (b, j, 0))
    return pl.pallas_call(
        functools.partial(_inproj_kernel, tm=tm),
        grid=(B, nt),
        in_specs=[
            tile(D_MODEL),
            pl.BlockSpec((1, 1, D_MODEL), lambda b, j: (b, 0, 0)),
            pl.BlockSpec((1, 1, D_MODEL), lambda b, j: (b, 0, 0)),
            full((D_MODEL, IN_COLS_EXT)),
            full((CONV_K, CONV_WIDTH)),
            pl.BlockSpec((1, CONV_K - 1, CONV_WIDTH), lambda b, j: (b, 0, 0)),
            full((1, Q_LORA)), full((1, KV_LORA)), full((1, CONV_WIDTH)),
            full((Q_LORA, N_HEADS * QK_NOPE)), full((Q_LORA, 2 * LANES)), full((Q_LORA, 2 * LANES)),
            full((N_HEADS // 2, LANES, 2 * KV_LORA)),
            pl.BlockSpec((tm, LANES), lambda b, j: (j, 0)),
            pl.BlockSpec((tm, LANES), lambda b, j: (j, 0)),
        ],
        out_specs=(tile(CONV_WIDTH), tile(N_HEADS * KV_LORA), tile(2 * LANES), tile(2 * LANES),
                   tile(KV_LORA), tile(QK_ROPE),
                   pl.BlockSpec((1, CONV_K - 1, CONV_WIDTH), lambda b, j: (b, 0, 0))),
        out_shape=out_shapes,
        scratch_shapes=[pltpu.VMEM((tm + 8, CONV_WIDTH), F32)],
        compiler_params=_cparams(("arbitrary", "arbitrary")),
        name="inproj",
    )(x, sc1, sh1, w["w_in"], w["conv_w"], conv_prev, w["g_qa"], w["g_kva"], w["g_oc"],
      w["wq_nope"], w["wq_rope"], w["wq_rope_sw"], w["w_uk_pair"], cos_t, sin_t)


def _stack_queries(qlat_ref, qrope_ref, qs, tq):
    lane = lax.broadcasted_iota(jnp.int32, (tq, LANES), 1)
    for h in range(N_HEADS):
        g, i = divmod(h, 4)
        rope = qrope_ref[0, :, g * LANES:(g + 1) * LANES]
        keep = (lane >= i * QK_ROPE) & (lane < (i + 1) * QK_ROPE)
        qs[h * tq:(h + 1) * tq, 0:KV_LORA] = qlat_ref[0, :, h * KV_LORA:(h + 1) * KV_LORA]
        qs[h * tq:(h + 1) * tq, KV_LORA:KV_LORA + LANES] = jnp.where(keep, rope, jnp.zeros_like(rope))


def _softmax_step(qs, k, v, m_s, l_s, acc_s, mask=None, groups=1):
    tk = k.shape[0]
    rows = qs.shape[0] // groups
    for g in range(groups):
        r = slice(g * rows, (g + 1) * rows)
        s = lax.dot_general(qs[r, :], k, (((1,), (1,)), ((), ())), preferred_element_type=F32)
        if mask is not None:
            col, limit = mask
            s = jnp.where(col < limit[r], s, NEG)
        m_prev = m_s[r, :]
        m_new = jnp.maximum(m_prev, jnp.max(s, axis=1, keepdims=True))
        alpha = jnp.exp2(m_prev - m_new)
        if tk % LANES == 0:
            p = jnp.exp2(s - jnp.tile(m_new, (1, tk // LANES)))
        else:
            p = jnp.exp2(s - m_new[:, :tk])
        l_s[r, :] = alpha * l_s[r, :] + jnp.sum(p, axis=1, keepdims=True)
        acc_s[r, :] = alpha * acc_s[r, :] + jnp.dot(p.astype(BF16), v, preferred_element_type=F32)
        m_s[r, :] = m_new


def _attn_epilogue(acc_s, l_s, wuv_ref, g_ref, o_ref, tq):
    o = acc_s[...] / l_s[...]
    parts = []
    for p in range(N_HEADS // 2):
        op = jnp.concatenate([o[(2 * p) * tq:(2 * p + 1) * tq], o[(2 * p + 1) * tq:(2 * p + 2) * tq]], axis=1)
        parts.append(jnp.dot(op.astype(BF16), wuv_ref[p], preferred_element_type=F32))
    b = jnp.concatenate(parts, axis=1)
    o_ref[0] = (_rms(b) * g_ref[...]).astype(BF16)


def _attn_prompt_kernel(qlat_ref, qrope_ref, k_ref, wuv_ref, g_ref, o_ref, qs, m_s, l_s, acc_s, *, tq, tk,
                        groups):
    i = pl.program_id(1)
    M = N_HEADS * tq
    _stack_queries(qlat_ref, qrope_ref, qs, tq)
    m_s[...] = jnp.full(m_s.shape, NEG, F32)
    l_s[...] = jnp.zeros(l_s.shape, F32)
    acc_s[...] = jnp.zeros(acc_s.shape, F32)
    q0 = i * tq
    n_full = (q0 + CHUNK) // tk

    def body(t, carry):
        start = pl.multiple_of(t * tk, tk)
        k = k_ref[0, pl.ds(start, tk), :]
        _softmax_step(qs, k, k[:, :KV_LORA], m_s, l_s, acc_s, groups=groups)
        return carry

    lax.fori_loop(0, n_full, body, 0)

    start = pl.multiple_of(n_full * tk, tk)
    k = k_ref[0, pl.ds(start, tk), :]
    col = start + lax.broadcasted_iota(jnp.int32, (1, tk), 1)
    row_t = jnp.bitwise_and(lax.broadcasted_iota(jnp.int32, (M, 1), 0), tq - 1)
    limit = q0 + (jnp.right_shift(row_t, CHUNK.bit_length() - 1) + 1) * CHUNK
    _softmax_step(qs, k, k[:, :KV_LORA], m_s, l_s, acc_s, mask=(col, limit), groups=groups)
    _attn_epilogue(acc_s, l_s, wuv_ref, g_ref, o_ref, tq)


def _attn_prompt(qlat, qrope, kcat, w, tq, tk):
    B, T, _ = qlat.shape
    M = N_HEADS * tq
    return pl.pallas_call(
        functools.partial(_attn_prompt_kernel, tq=tq, tk=tk, groups=ATTN_GROUPS),
        grid=(B, T // tq),
        in_specs=[
            pl.BlockSpec((1, tq, N_HEADS * KV_LORA), lambda b, i: (b, i, 0)),
            pl.BlockSpec((1, tq, 2 * LANES), lambda b, i: (b, i, 0)),
            pl.BlockSpec((1, T, 2 * LANES), lambda b, i: (b, 0, 0)),
            pl.BlockSpec((N_HEADS // 2, 2 * KV_LORA, LANES), lambda b, i: (0, 0, 0)),
            pl.BlockSpec((1, ATTN_WIDTH), lambda b, i: (0, 0)),
        ],
        out_specs=pl.BlockSpec((1, tq, ATTN_WIDTH), lambda b, i: (b, i, 0)),
        out_shape=jax.ShapeDtypeStruct((B, T, ATTN_WIDTH), BF16),
        scratch_shapes=[pltpu.VMEM((M, 2 * LANES), BF16), pltpu.VMEM((M, LANES), F32),
                        pltpu.VMEM((M, LANES), F32), pltpu.VMEM((M, KV_LORA), F32)],
        compiler_params=_cparams(("arbitrary", "arbitrary")),
        name="attn_prompt",
    )(qlat, qrope, kcat, w["w_uv_pair"], w["g_oa"])


def _attn_sample_kernel(qlat_ref, qrope_ref, knew_ref, pckv_ref, pkr_ref, wuv_ref, g_ref, o_ref,
                        qs, m_s, l_s, acc_s, *, tq, tk, n_past):
    _stack_queries(qlat_ref, qrope_ref, qs, tq)
    m_s[...] = jnp.full(m_s.shape, NEG, F32)
    l_s[...] = jnp.zeros(l_s.shape, F32)
    acc_s[...] = jnp.zeros(acc_s.shape, F32)

    def body(t, carry):
        start = pl.multiple_of(t * tk, tk)
        ck = pckv_ref[0, pl.ds(start, tk), :]
        kr = pkr_ref[0, pl.ds(start, tk), :]
        k = jnp.concatenate([ck, kr, kr, kr, kr], axis=1).astype(BF16)
        _softmax_step(qs, k, k[:, :KV_LORA], m_s, l_s, acc_s)
        return carry

    lax.fori_loop(0, n_past // tk, body, 0)
    k = knew_ref[0]
    _softmax_step(qs, k, k[:, :KV_LORA], m_s, l_s, acc_s)
    _attn_epilogue(acc_s, l_s, wuv_ref, g_ref, o_ref, tq)


def _attn_sample(qlat, qrope, kcat, past_ckv, past_krope, w, tk):
    B, T, _ = qlat.shape
    n_past = past_ckv.shape[1]
    M = N_HEADS * T
    per_b = lambda r, c: pl.BlockSpec((1, r, c), lambda b: (b, 0, 0))
    return pl.pallas_call(
        functools.partial(_attn_sample_kernel, tq=T, tk=tk, n_past=n_past),
        grid=(B,),
        in_specs=[per_b(T, N_HEADS * KV_LORA), per_b(T, 2 * LANES), per_b(T, 2 * LANES),
                  per_b(n_past, KV_LORA), per_b(n_past, QK_ROPE),
                  pl.BlockSpec((N_HEADS // 2, 2 * KV_LORA, LANES), lambda b: (0, 0, 0)),
                  pl.BlockSpec((1, ATTN_WIDTH), lambda b: (0, 0))],
        out_specs=per_b(T, ATTN_WIDTH),
        out_shape=jax.ShapeDtypeStruct((B, T, ATTN_WIDTH), BF16),
        scratch_shapes=[pltpu.VMEM((M, 2 * LANES), BF16), pltpu.VMEM((M, LANES), F32),
                        pltpu.VMEM((M, LANES), F32), pltpu.VMEM((M, KV_LORA), F32)],
        compiler_params=_cparams(("arbitrary",)),
        name="attn_sample",
    )(qlat, qrope, kcat, past_ckv, past_krope, w["w_uv_pair"], w["g_oa"])


def _outproj_kernel(x_ref, an_ref, bn_ref, gt1_ref, sc2_ref, sh2_ref, wo_ref, l1g_ref, l1b_ref,
                    rw_ref, rb_ref, x1_ref, h2_ref, meta_ref, gates_ref, cnt_ref, run_s,
                    *, tm, alpha):
    first = (pl.program_id(0) == 0) & (pl.program_id(1) == 0)

    @pl.when(first)
    def _():
        run_s[...] = jnp.zeros(run_s.shape, F32)

    m = (jnp.dot(an_ref[0], wo_ref[0:CONV_WIDTH, :], preferred_element_type=F32)
         + jnp.dot(bn_ref[0], wo_ref[CONV_WIDTH:, :], preferred_element_type=F32))
    x1 = _layernorm(alpha * x_ref[0] + gt1_ref[0] * m) * l1g_ref[...] + l1b_ref[...]
    x1_ref[0] = x1
    h2 = _layernorm(x1) * (1.0 + sc2_ref[0]) + sh2_ref[0]
    _rows_to_tiles(h2_ref, h2)

    logits = jnp.dot(h2.astype(BF16), rw_ref[...], preferred_element_type=F32) + rb_ref[...]
    lane = lax.broadcasted_iota(jnp.int32, (tm, LANES), 1)
    lane_f = lane.astype(F32)
    lg = logits
    vals, sels = [], []
    chosen = jnp.zeros((tm, LANES), F32)
    for _ in range(TOP_K):
        mx = jnp.max(lg, axis=1, keepdims=True)
        idx = jnp.min(jnp.where(lg == mx, lane_f, float(LANES)), axis=1, keepdims=True)
        sel = lane_f == idx
        vals.append(mx)
        sels.append(idx)
        chosen = jnp.where(sel, 1.0, chosen)
        lg = jnp.where(sel, NEG, lg)

    es = [jnp.exp(v - vals[0]) for v in vals]
    denom = es[0] + es[1] + es[2] + es[3]

    run_s[0:1, :] = run_s[0:1, :] + jnp.sum(chosen, axis=0, keepdims=True)
    cnt_ref[...] = jnp.broadcast_to(run_s[0:1, :], cnt_ref.shape)

    meta = jnp.zeros((tm, LANES), jnp.int32)
    gates = jnp.zeros((tm, LANES), F32)
    for k in range(TOP_K):
        meta = jnp.where(lane == k, sels[k].astype(jnp.int32), meta)
        gates = jnp.where(lane == k, es[k] / denom, gates)
    meta_ref[0] = meta
    gates_ref[0] = gates


def _outproj(x, a_n, b_n, gt1, sc2, sh2, w, tm, alpha):
    B, T, _ = x.shape
    tile = lambda c: pl.BlockSpec((1, tm, c), lambda b, j: (b, j, 0))
    modv = pl.BlockSpec((1, 1, D_MODEL), lambda b, j: (b, 0, 0))
    full = lambda shp: pl.BlockSpec(shp, lambda b, j: (0,) * len(shp))
    return pl.pallas_call(
        functools.partial(_outproj_kernel, tm=tm, alpha=alpha),
        grid=(B, T // tm),
        in_specs=[tile(D_MODEL), tile(CONV_WIDTH), tile(ATTN_WIDTH), modv, modv, modv,
                  full((CONV_WIDTH + ATTN_WIDTH, D_MODEL)), full((1, D_MODEL)), full((1, D_MODEL)),
                  full((D_MODEL, LANES)), full((1, LANES))],
        out_specs=(tile(D_MODEL), pl.BlockSpec((tm * ROW_TILE, LANES), lambda b, j: (b * (T // tm) + j, 0)),
                   tile(LANES), tile(LANES), pl.BlockSpec((8, LANES), lambda b, j: (0, 0))),
        out_shape=(jax.ShapeDtypeStruct((B, T, D_MODEL), F32),
                   jax.ShapeDtypeStruct((B * T * ROW_TILE, LANES), F32),
                   jax.ShapeDtypeStruct((B, T, LANES), jnp.int32), jax.ShapeDtypeStruct((B, T, LANES), F32),
                   jax.ShapeDtypeStruct((8, LANES), F32)),
        scratch_shapes=[pltpu.VMEM((8, LANES), F32)],
        compiler_params=_cparams(("arbitrary", "arbitrary")),
        name="outproj",
    )(x, a_n, b_n, gt1, sc2, sh2, w["w_out"], w["ln1_g"], w["ln1_b"], w["router_w"], w["router_b"])


def _expert_kernel(order_ref, blk_e_ref, blk_j0_ref, blk_nv_ref, nxt_e_ref, used_ref,
                   h_ref, wgu_hbm, bgu_ref, wd_hbm, bd_ref, y_ref,
                   x0, x1, y0, y1, xb, act_s, wgu_st, wd_st, wgu_bf, wd_bf, gsem, ssem, wsem, *, bm, n_tok):
    s = pl.program_id(0)
    used = used_ref[0]
    dump0 = TOP_K * n_tok

    def weight_copies(e):
        return (pltpu.make_async_copy(wgu_hbm.at[e], wgu_st, wsem.at[0]),
                pltpu.make_async_copy(wd_hbm.at[e], wd_st, wsem.at[1]))

    def gather_start(j0, xbuf, sem, r, prio=0):
        tok = jnp.right_shift(order_ref[j0 + r], 2)
        pltpu.make_async_copy(h_ref.at[tok], xbuf.at[pl.ds(r * ROW_TILE, ROW_TILE)], sem).start(priority=prio)

    def scatter_start(j0, nv, ybuf, sem, r, odd, prio=0):
        o = order_ref[j0 + r]
        real = jnp.bitwise_and(o, TOP_K - 1) * n_tok + jnp.right_shift(o, 2)
        dump = dump0 + odd * bm + r
        valid = jnp.right_shift(r - nv, 31)
        dst = dump + jnp.bitwise_and(valid, real - dump)
        pltpu.make_async_copy(ybuf.at[pl.ds(r * ROW_TILE, ROW_TILE)], y_ref.at[dst], sem).start(priority=prio)

    def block_wait(buf, sem):
        pltpu.make_async_copy(buf, buf, sem).wait()

    def switch_weights(b):
        prev = blk_e_ref[jnp.maximum(b - 1, 0)]
        e = blk_e_ref[b]

        @pl.when((b == 0) | (e != prev))
        def _():
            for c in weight_copies(e):
                c.wait()
            wgu_bf[...] = wgu_st[...].astype(BF16)
            wd_bf[...] = wd_st[...].astype(BF16)

            @pl.when(nxt_e_ref[e] >= 0)
            def _():
                for c in weight_copies(nxt_e_ref[e]):
                    c.start()

    def block(b, x_in, y_out, g_next, x_next, gsem_next, s_prev, y_prev, ssem_prev, prev_odd, wait_y_free):
        gj0 = blk_j0_ref[g_next + 1]
        sj0 = blk_j0_ref[s_prev + 1]
        snv = blk_nv_ref[s_prev + 1]
        def issue(part, after=None):
            zero = 0
            if after is not None:
                bits = pltpu.bitcast(jnp.abs(after[0:8, 0:LANES]), jnp.int32)
                zero = jnp.minimum(bits[0, 0], 0)
            half = FFN_PARTS // 2
            q = part % half
            for r in range(q * bm // half, (q + 1) * bm // half):
                if part < half:
                    gather_start(gj0 + zero, x_next, gsem_next, r, prio=r % 2)
                else:
                    scatter_start(sj0 + zero, snv, y_prev, ssem_prev, r, prev_odd, prio=r % 2)

        xb[...] = _rows_from_tiles(x_in, bm).astype(BF16)
        e = blk_e_ref[b]
        cw = D_FF // 4
        prev = None
        for c in range(4):
            issue(c, prev)
            gs, ls = slice(c * cw, (c + 1) * cw), slice(D_FF + c * cw, D_FF + (c + 1) * cw)
            g = jnp.dot(xb[...], wgu_bf[:, gs], preferred_element_type=F32) + bgu_ref[e][:, gs]
            lin = jnp.dot(xb[...], wgu_bf[:, ls], preferred_element_type=F32) + bgu_ref[e][:, ls]
            prev = g
            g = jnp.minimum(g, SWIGLU_LIMIT)
            lin = jnp.clip(lin, -SWIGLU_LIMIT, SWIGLU_LIMIT)
            act_s[:, gs] = (g * _sigmoid(SWIGLU_ALPHA * g) * (lin + 1.0)).astype(BF16)
        hw = D_MODEL // 2
        wait_y_free()
        for h in range(2):
            issue(4 + h, prev)
            y = jnp.dot(act_s[...], wd_bf[:, h * hw:(h + 1) * hw], preferred_element_type=F32)
            y = y + bd_ref[e][:, h * hw:(h + 1) * hw]
            prev = y
            for c in range(hw // LANES):
                y_out[pl.ds(h * (hw // LANES) + c, bm, stride=ROW_TILE), :] = y[:, c * LANES:(c + 1) * LANES]

    b0 = 2 * s
    b1 = b0 + 1

    @pl.when(b0 < used)
    def _():
        @pl.when(s == 0)
        def _():
            for c in weight_copies(blk_e_ref[0]):
                c.start()
            y1[...] = jnp.zeros(y1.shape, F32)

            def first(r, c):
                scatter_start(0, 0, y1, ssem.at[0], r, 0)
                gather_start(blk_j0_ref[1], x0, gsem.at[0], r)
                return c

            lax.fori_loop(0, bm, first, 0)
            block_wait(y1, ssem.at[0])

        def y0_free():
            @pl.when(s > 0)
            def _():
                block_wait(y0, ssem.at[0])

        def y1_free():
            block_wait(y1, ssem.at[1])

        switch_weights(b0)
        block_wait(x0, gsem.at[0])
        block(b0, x0, y0, b1, x1, gsem.at[1], b0 - 1, y1, ssem.at[1], 1, y0_free)

        switch_weights(b1)
        block_wait(x1, gsem.at[1])
        block(b1, x1, y1, b0 + 2, x0, gsem.at[0], b0, y0, ssem.at[0], 0, y1_free)

        @pl.when(b0 + 2 >= used)
        def _():
            def last(r, c):
                scatter_start(blk_j0_ref[b1 + 1], blk_nv_ref[b1 + 1], y1, ssem.at[1], r, 1)
                return c

            lax.fori_loop(0, bm, last, 0)
            block_wait(y0, ssem.at[0])
            block_wait(y1, ssem.at[1])
            block_wait(x0, gsem.at[0])


def _expert(order, blk_e, blk_j0, blk_nv, nxt_e, used, h2, w, bm):
    n_tok = h2.shape[0]
    nblk = blk_e.shape[0]
    f32buf = lambda shp: pltpu.VMEM(shp, F32)
    return pl.pallas_call(
        functools.partial(_expert_kernel, bm=bm, n_tok=n_tok),
        grid_spec=pltpu.PrefetchScalarGridSpec(
            num_scalar_prefetch=6,
            grid=(nblk // 2,),
            in_specs=[
                pl.BlockSpec(memory_space=pl.ANY),
                pl.BlockSpec(memory_space=pl.ANY),
                pl.BlockSpec((N_EXPERTS, 1, 2 * D_FF), lambda s, *_: (0, 0, 0)),
                pl.BlockSpec(memory_space=pl.ANY),
                pl.BlockSpec((N_EXPERTS, 1, D_MODEL), lambda s, *_: (0, 0, 0)),
            ],
            out_specs=pl.BlockSpec(memory_space=pl.ANY),
            scratch_shapes=[f32buf((bm * ROW_TILE, LANES)), f32buf((bm * ROW_TILE, LANES)),
                            f32buf((bm * ROW_TILE, LANES)), f32buf((bm * ROW_TILE, LANES)),
                            pltpu.VMEM((bm, D_MODEL), BF16), pltpu.VMEM((bm, D_FF), BF16),
                            f32buf((D_MODEL, 2 * D_FF)), f32buf((D_FF, D_MODEL)),
                            pltpu.VMEM((D_MODEL, 2 * D_FF), BF16), pltpu.VMEM((D_FF, D_MODEL), BF16),
                            pltpu.SemaphoreType.DMA((2,)), pltpu.SemaphoreType.DMA((2,)),
                            pltpu.SemaphoreType.DMA((2,))],
        ),
        out_shape=jax.ShapeDtypeStruct((TOP_K * n_tok + 2 * bm, ROW_TILE, LANES), F32),
        compiler_params=_cparams(("arbitrary",)),
        name="moe_expert",
    )(order, blk_e, blk_j0, blk_nv, nxt_e, used, h2, w["w_gu"], w["b_gu"], w["w_down"], w["b_down"])


def _combine_kernel(y0_ref, y1_ref, y2_ref, y3_ref, x1_ref, gates_ref, gt2_ref, l2g_ref, l2b_ref, o_ref, *, alpha):
    gates = gates_ref[...]
    tm = gates.shape[0]
    f = gates[:, 0:1] * _rows_from_tiles(y0_ref, tm)
    for k, y_ref in enumerate((y1_ref, y2_ref, y3_ref), start=1):
        f = f + gates[:, k:k + 1] * _rows_from_tiles(y_ref, tm)
    o_ref[...] = _layernorm(alpha * x1_ref[...] + gt2_ref[0] * f) * l2g_ref[...] + l2b_ref[...]


def _combine(y_rows, x1, gates, gt2, w, tm, tokens_per_batch, alpha):
    N = x1.shape[0]
    per_b = tokens_per_batch // tm
    nt = N // tm
    slot = lambda k: pl.BlockSpec((tm * ROW_TILE, LANES), lambda i: (k * nt + i, 0))
    return pl.pallas_call(
        functools.partial(_combine_kernel, alpha=alpha),
        grid=(nt,),
        in_specs=[slot(0), slot(1), slot(2), slot(3),
                  pl.BlockSpec((tm, D_MODEL), lambda i: (i, 0)),
                  pl.BlockSpec((tm, LANES), lambda i: (i, 0)),
                  pl.BlockSpec((1, 1, D_MODEL), lambda i: (i // per_b, 0, 0)),
                  pl.BlockSpec((1, D_MODEL), lambda i: (0, 0)),
                  pl.BlockSpec((1, D_MODEL), lambda i: (0, 0))],
        out_specs=pl.BlockSpec((tm, D_MODEL), lambda i: (i, 0)),
        out_shape=jax.ShapeDtypeStruct((N, D_MODEL), F32),
        compiler_params=_cparams(("arbitrary",)),
        name="moe_combine",
    )(y_rows, y_rows, y_rows, y_rows, x1, gates, gt2, w["ln2_g"], w["ln2_b"])


def _moe(h2, x1, idx, gates, counts, gt2, w, tokens_per_batch, bm, tm_c, alpha):
    N = h2.shape[0]
    n_rows = N * TOP_K
    nblk = (n_rows + N_EXPERTS * (bm - 1)) // bm
    nblk += nblk % 2
    experts = jnp.arange(N_EXPERTS, dtype=jnp.int32)
    order = jnp.concatenate([jnp.argsort(idx.reshape(-1)).astype(jnp.int32), jnp.zeros((bm,), jnp.int32)])
    nb_e = (counts + bm - 1) // bm
    blk_end = jnp.cumsum(nb_e)
    first_blk = blk_end - nb_e
    start_sorted = jnp.cumsum(counts) - counts
    used = blk_end[-1].astype(jnp.int32)
    b = jnp.arange(-1, nblk + 1, dtype=jnp.int32)
    bc = jnp.clip(b, 0, used - 1)
    e = jnp.minimum(jnp.sum(blk_end[None, :] <= bc[:, None], axis=1), N_EXPERTS - 1).astype(jnp.int32)
    pick = lambda table: jnp.sum(jnp.where(e[:, None] == experts, table, 0), axis=1)
    local = bc - pick(first_blk)
    blk_j0 = (pick(start_sorted) + local * bm).astype(jnp.int32)
    blk_nv = jnp.where((b >= 0) & (b < used), jnp.minimum(bm, pick(counts) - local * bm), 0).astype(jnp.int32)
    blk_e = e[1:nblk + 1]
    later = (experts[None, :] > experts[:, None]) & (counts[None, :] > 0)
    nxt = jnp.min(jnp.where(later, experts[None, :], N_EXPERTS), axis=1)
    nxt_e = jnp.where(nxt < N_EXPERTS, nxt, -1).astype(jnp.int32)
    y_rows = _expert(order, blk_e, blk_j0, blk_nv, nxt_e, used.reshape(1), h2, w, bm)
    return _combine(y_rows.reshape(-1, LANES), x1, gates, gt2, w, tm_c, tokens_per_batch, alpha)


def _rope_tables(pos):
    half = QK_ROPE // 2
    inv = ROPE_THETA ** (-jnp.arange(half, dtype=F32) / half)
    ang = pos.astype(F32)[:, None] * inv[None, :]
    cos, sin = jnp.cos(ang), jnp.sin(ang)
    cos32 = jnp.concatenate([cos, cos], axis=1)
    sin32 = jnp.concatenate([-sin, sin], axis=1)
    return jnp.tile(cos32, (1, LANES // QK_ROPE)), jnp.tile(sin32, (1, LANES // QK_ROPE))


def _swap_halves(w32):
    shp = w32.shape
    w = w32.reshape(shp[:-1] + (shp[-1] // QK_ROPE, 2, QK_ROPE // 2))
    return w[..., ::-1, :].reshape(shp)


def _prep_weights(l, w_in, conv_w, g_qa, w_qb, g_kva, w_kvb, g_out_conv, g_out_attn, w_out,
                  ln1_g, ln1_b, router_w, router_b, w_gu, b_gu, w_down, b_down, ln2_g, ln2_b):
    w = {}
    wi = w_in[l]
    k_r = wi[:, _O_KR:_O_KR + QK_ROPE]
    rep = LANES // QK_ROPE
    w["w_in"] = jnp.concatenate([wi[:, :_O_KR], jnp.tile(k_r, (1, rep)), jnp.tile(_swap_halves(k_r), (1, rep))],
                                axis=1).astype(BF16)
    w["conv_w"] = conv_w[l]
    w["g_qa"] = g_qa[l].reshape(1, Q_LORA)
    w["g_kva"] = g_kva[l].reshape(1, KV_LORA)
    w["g_oc"] = g_out_conv[l].reshape(1, CONV_WIDTH)
    w["g_oa"] = g_out_attn[l].reshape(1, ATTN_WIDTH)
    wq = w_qb[l].reshape(Q_LORA, N_HEADS, QK_NOPE + QK_ROPE)
    w["wq_nope"] = wq[:, :, :QK_NOPE].reshape(Q_LORA, N_HEADS * QK_NOPE).astype(BF16)
    wq_rope = wq[:, :, QK_NOPE:].reshape(Q_LORA, N_HEADS * QK_ROPE)
    w["wq_rope"] = wq_rope.astype(BF16)
    w["wq_rope_sw"] = _swap_halves(wq_rope).astype(BF16)
    w_uk = jnp.transpose(w_kvb[l][:, :, :QK_NOPE], (1, 2, 0))
    w_uv = jnp.transpose(w_kvb[l][:, :, QK_NOPE:], (1, 0, 2))
    zk = jnp.zeros((QK_NOPE, KV_LORA), F32)
    zv = jnp.zeros((KV_LORA, V_HEAD), F32)
    w["w_uk_pair"] = jnp.stack([
        jnp.concatenate([jnp.concatenate([w_uk[2 * p], zk], axis=1),
                         jnp.concatenate([zk, w_uk[2 * p + 1]], axis=1)], axis=0)
        for p in range(N_HEADS // 2)]).astype(BF16)
    w["w_uv_pair"] = jnp.stack([
        jnp.concatenate([jnp.concatenate([w_uv[2 * p], zv], axis=1),
                         jnp.concatenate([zv, w_uv[2 * p + 1]], axis=1)], axis=0)
        for p in range(N_HEADS // 2)]).astype(BF16)
    w["w_out"] = w_out[l].astype(BF16)
    w["ln1_g"] = ln1_g[l].reshape(1, D_MODEL)
    w["ln1_b"] = ln1_b[l].reshape(1, D_MODEL)
    w["ln2_g"] = ln2_g[l].reshape(1, D_MODEL)
    w["ln2_b"] = ln2_b[l].reshape(1, D_MODEL)
    w["router_w"] = jnp.pad(router_w[l], ((0, 0), (0, LANES - N_EXPERTS))).astype(BF16)
    w["router_b"] = jnp.concatenate([router_b[l], jnp.full((LANES - N_EXPERTS,), NEG, F32)]).reshape(1, LANES)
    w["w_gu"] = w_gu[l]
    w["b_gu"] = b_gu[l].reshape(N_EXPERTS, 1, 2 * D_FF)
    w["w_down"] = w_down[l]
    w["b_down"] = b_down[l].reshape(N_EXPERTS, 1, D_MODEL)
    return w


def _layer(x, mod, conv_prev, past, pos0, w, alpha, *, tm_in, tm_out, bm, tm_c, tq=128, tk=512):
    B, T, _ = x.shape
    sh1, sc1, gt1, sh2, sc2, gt2 = [mod[:, None, i * D_MODEL:(i + 1) * D_MODEL] for i in range(N_MOD)]
    cos_t, sin_t = _rope_tables(pos0 + jnp.arange(T, dtype=jnp.int32))
    a_n, qlat, qrope, kcat, ckv, krope, conv_new = _inproj(x, sc1, sh1, conv_prev, cos_t, sin_t, w, tm_in)
    if past is None:
        b_n = _attn_prompt(qlat, qrope, kcat, w, tq, tk)
    else:
        b_n = _attn_sample(qlat, qrope, kcat, past[0], past[1], w, tk)
    x1, h2, meta, gates, cnt = _outproj(x, a_n, b_n, gt1, sc2, sh2, w, tm_out, alpha)
    N = B * T
    counts = cnt[0, :N_EXPERTS].astype(jnp.int32)
    y = _moe(h2.reshape(N, ROW_TILE, LANES), x1.reshape(N, D_MODEL), meta.reshape(N, LANES)[:, :TOP_K],
             gates.reshape(N, LANES), counts, gt2, w, T, bm, tm_c, alpha)
    return y.reshape(B, T, D_MODEL), ckv, krope, conv_new


def kernel(x_prompt, x_sample, c_prompt, c_sample, cache_ckv, cache_krope, state_conv, w_ada, b_ada, w_in, conv_w, g_qa, w_qb, g_kva, w_kvb, g_out_conv, g_out_attn, w_out, ln1_g, ln1_b, router_w, router_b, w_gu, b_gu, w_down, b_down, ln2_g, ln2_b):
    depth = w_ada.shape[0]
    Bp, Tp, _ = x_prompt.shape
    Bs, Ts, _ = x_sample.shape
    past_len = cache_ckv.shape[2]
    assert Ts == CHUNK and past_len % CHUNK == 0 and Tp % 512 == 0
    alpha = (2.0 * depth) ** 0.25
    xp, xs = x_prompt, x_sample
    outs = [[] for _ in range(6)]
    c_all = jnp.concatenate([c_prompt, c_sample, jnp.zeros((16 - Bp - Bs, D_MODEL), F32)], axis=0)
    for l in range(depth):
        w = _prep_weights(l, w_in, conv_w, g_qa, w_qb, g_kva, w_kvb, g_out_conv, g_out_attn, w_out,
                          ln1_g, ln1_b, router_w, router_b, w_gu, b_gu, w_down, b_down, ln2_g, ln2_b)
        mod = _ada(c_all, w_ada[l], b_ada[l])
        xp, ckv_p, kr_p, cv_p = _layer(xp, mod[:Bp], jnp.zeros((Bp, CONV_K - 1, CONV_WIDTH), F32), None, 0, w, alpha,
                                       tm_in=512, tm_out=256, bm=256, tm_c=256)
        xs, ckv_s, kr_s, cv_s = _layer(xs, mod[Bp:Bp + Bs], state_conv[l], (cache_ckv[l], cache_krope[l]),
                                       past_len, w, alpha, tm_in=Ts, tm_out=Ts, bm=128, tm_c=Ts)
        for o, v in zip(outs, (ckv_p, kr_p, cv_p, ckv_s, kr_s, cv_s)):
            o.append(v)
    return (xp, xs) + tuple(jnp.stack(o) for o in outs)
```

```python
import functools
import math

import jax
import jax.numpy as jnp
from jax import lax
from jax.experimental import pallas as pl
from jax.experimental.pallas import tpu as pltpu

F32 = jnp.float32
BF16 = jnp.bfloat16

D_MODEL = 1024
CHUNK = 64
CONV_WIDTH = 512
CONV_K = 3
N_HEADS = 8
QK_NOPE = 64
QK_ROPE = 32
V_HEAD = 64
Q_LORA = 256
KV_LORA = 128
ATTN_WIDTH = N_HEADS * V_HEAD
ROPE_THETA = 10000.0
ATTN_SCALE = 1.0 / math.sqrt(QK_NOPE + QK_ROPE)
Q_SCALE = ATTN_SCALE * math.log2(math.e)
N_EXPERTS = 32
TOP_K = 4
D_FF = 1024
SWIGLU_LIMIT = 7.0
SWIGLU_ALPHA = 1.702
N_MOD = 6
LN_EPS = 1e-5
RMS_EPS = 1e-6

LANES = 128
ATTN_GROUPS = 1
FFN_PARTS = 6
NEG = -1e30
VMEM_LIMIT = 56 * 1024 * 1024

_O_XB, _O_XC, _O_XV = 0, CONV_WIDTH, 2 * CONV_WIDTH
_O_QA = 3 * CONV_WIDTH
_O_KVA = _O_QA + Q_LORA
_O_KR = _O_KVA + KV_LORA
_O_KRS = _O_KR + LANES
IN_COLS_EXT = _O_KRS + LANES


def _cparams(sem):
    return pltpu.CompilerParams(dimension_semantics=sem, vmem_limit_bytes=VMEM_LIMIT)


def _layernorm(x):
    mu = jnp.mean(x, axis=-1, keepdims=True)
    xc = x - mu
    var = jnp.mean(xc * xc, axis=-1, keepdims=True)
    return xc * lax.rsqrt(var + LN_EPS)


def _rms(x):
    return x * lax.rsqrt(jnp.mean(x * x, axis=-1, keepdims=True) + RMS_EPS)


def _sigmoid(x):
    return 1.0 / (1.0 + jnp.exp(-x))


ROW_TILE = D_MODEL // LANES


def _rows_from_tiles(ref, n):
    return jnp.concatenate([ref[pl.ds(c, n, stride=ROW_TILE), :] for c in range(ROW_TILE)], axis=1)


def _rows_to_tiles(ref, x):
    n = x.shape[0]
    for c in range(ROW_TILE):
        ref[pl.ds(c, n, stride=ROW_TILE), :] = x[:, c * LANES:(c + 1) * LANES]


def _ada_kernel(c_ref, w_ref, b_ref, o_ref):
    c = c_ref[...]
    s = (c * _sigmoid(c)).astype(BF16)
    o_ref[...] = jnp.dot(s, w_ref[...].astype(BF16), preferred_element_type=F32) + b_ref[...]


def _ada(c_all, w_ada, b_ada):
    rows = c_all.shape[0]
    ncol = w_ada.shape[1]
    tn = 1024
    return pl.pallas_call(
        _ada_kernel,
        grid=(ncol // tn,),
        in_specs=[pl.BlockSpec((rows, D_MODEL), lambda j: (0, 0)),
                  pl.BlockSpec((D_MODEL, tn), lambda j: (0, j)),
                  pl.BlockSpec((1, tn), lambda j: (0, j))],
        out_specs=pl.BlockSpec((rows, tn), lambda j: (0, j)),
        out_shape=jax.ShapeDtypeStruct((rows, ncol), F32),
        compiler_params=_cparams(("arbitrary",)),
        name="ada",
    )(c_all, w_ada, b_ada.reshape(1, ncol))


def _inproj_kernel(x_ref, sc_ref, sh_ref, win_ref, cw_ref, cprev_ref, gqa_ref, gkva_ref, goc_ref,
                   wqn_ref, wqr_ref, wqrs_ref, wuk_ref, cos_ref, sin_ref,
                   an_ref, qlat_ref, qrope_ref, kcat_ref, ckv_ref, krope_ref, cnew_ref,
                   *rest, tm, with_vt):
    vt_ref, ubuf = rest if with_vt else (None, rest[0])
    j = pl.program_id(1)
    x = x_ref[0]
    h = _layernorm(x) * (1.0 + sc_ref[0]) + sh_ref[0]
    proj = jnp.dot(h.astype(BF16), win_ref[...], preferred_element_type=F32)
    xb = proj[:, _O_XB:_O_XB + CONV_WIDTH]
    xc = proj[:, _O_XC:_O_XC + CONV_WIDTH]
    xv = proj[:, _O_XV:_O_XV + CONV_WIDTH]
    q_a = proj[:, _O_QA:_O_QA + Q_LORA]
    kv_a = proj[:, _O_KVA:_O_KVA + KV_LORA]
    kr4 = proj[:, _O_KR:_O_KR + LANES]
    kr4s = proj[:, _O_KRS:_O_KRS + LANES]

    u = xc * xv

    @pl.when(j == 0)
    def _():
        ubuf[6:8, :] = cprev_ref[0]

    ubuf[8:8 + tm, :] = u
    conv = (cw_ref[0:1, :] * ubuf[6:6 + tm, :] + cw_ref[1:2, :] * ubuf[7:7 + tm, :]
            + cw_ref[2:3, :] * u)
    ubuf[0:8, :] = ubuf[tm:tm + 8, :]
    cnew_ref[0] = u[tm - (CONV_K - 1):tm, :]
    an_ref[0] = (_rms(xb * conv) * goc_ref[...]).astype(BF16)

    cos = cos_ref[...]
    sin = sin_ref[...]

    ckv = _rms(kv_a) * gkva_ref[...]
    kro4 = kr4 * cos + kr4s * sin
    ckv_ref[0] = ckv
    krope_ref[0] = kro4[:, :QK_ROPE]
    kcat_ref[0] = jnp.concatenate([ckv, kro4], axis=1).astype(BF16)
    if with_vt:
        vt_ref[0] = ckv.T.astype(BF16)

    qn = (_rms(q_a) * gqa_ref[...]).astype(BF16)
    q_nope = jnp.dot(qn, wqn_ref[...], preferred_element_type=F32)
    xr = jnp.dot(qn, wqr_ref[...], preferred_element_type=F32)
    xrs = jnp.dot(qn, wqrs_ref[...], preferred_element_type=F32)
    for g in range(2):
        sl = slice(g * LANES, (g + 1) * LANES)
        qrope_ref[0, :, sl] = ((xr[:, sl] * cos + xrs[:, sl] * sin) * Q_SCALE).astype(BF16)
    for p in range(N_HEADS // 2):
        qp = q_nope[:, p * LANES:(p + 1) * LANES].astype(BF16)
        ql = jnp.dot(qp, wuk_ref[p], preferred_element_type=F32)
        qlat_ref[0, :, p * 2 * KV_LORA:(p + 1) * 2 * KV_LORA] = (ql * Q_SCALE).astype(BF16)


def _inproj(x, sc1, sh1, conv_prev, cos_t, sin_t, w, tm, with_vt):
    B, T, _ = x.shape
    nt = T // tm
    full = lambda shp: pl.BlockSpec(shp, lambda b, j: (0,) * len(shp))
    vt_shape = (jax.ShapeDtypeStruct((B, KV_LORA, T), BF16),) if with_vt else ()
    vt_spec = (pl.BlockSpec((1, KV_LORA, tm), lambda b, j: (b, 0, j)),) if with_vt else ()
    out_shapes = (
        jax.ShapeDtypeStruct((B, T, CONV_WIDTH), BF16),
        jax.ShapeDtypeStruct((B, T, N_HEADS * KV_LORA), BF16),
        jax.ShapeDtypeStruct((B, T, 2 * LANES), BF16),
        jax.ShapeDtypeStruct((B, T, 2 * LANES), BF16),
        jax.ShapeDtypeStruct((B, T, KV_LORA), F32),
        jax.ShapeDtypeStruct((B, T, QK_ROPE), F32),
        jax.ShapeDtypeStruct((B, CONV_K - 1, CONV_WIDTH), F32),
    ) + vt_shape
    tile = lambda c: pl.BlockSpec((1, tm, c), lambda b, j: (b, j, 0))
    return pl.pallas_call(
        functools.partial(_inproj_kernel, tm=tm, with_vt=with_vt),
        grid=(B, nt),
        in_specs=[
            tile(D_MODEL),
            pl.BlockSpec((1, 1, D_MODEL), lambda b, j: (b, 0, 0)),
            pl.BlockSpec((1, 1, D_MODEL), lambda b, j: (b, 0, 0)),
            full((D_MODEL, IN_COLS_EXT)),
            full((CONV_K, CONV_WIDTH)),
            pl.BlockSpec((1, CONV_K - 1, CONV_WIDTH), lambda b, j: (b, 0, 0)),
            full((1, Q_LORA)), full((1, KV_LORA)), full((1, CONV_WIDTH)),
            full((Q_LORA, N_HEADS * QK_NOPE)), full((Q_LORA, 2 * LANES)), full((Q_LORA, 2 * LANES)),
            full((N_HEADS // 2, LANES, 2 * KV_LORA)),
            pl.BlockSpec((tm, LANES), lambda b, j: (j, 0)),
            pl.BlockSpec((tm, LANES), lambda b, j: (j, 0)),
        ],
        out_specs=(tile(CONV_WIDTH), tile(N_HEADS * KV_LORA), tile(2 * LANES), tile(2 * LANES),
                   tile(KV_LORA), tile(QK_ROPE),
                   pl.BlockSpec((1, CONV_K - 1, CONV_WIDTH), lambda b, j: (b, 0, 0))) + vt_spec,
        out_shape=out_shapes,
        scratch_shapes=[pltpu.VMEM((tm + 8, CONV_WIDTH), F32)],
        compiler_params=_cparams(("arbitrary", "arbitrary")),
        name="inproj",
    )(x, sc1, sh1, w["w_in"], w["conv_w"], conv_prev, w["g_qa"], w["g_kva"], w["g_oc"],
      w["wq_nope"], w["wq_rope"], w["wq_rope_sw"], w["w_uk_pair"], cos_t, sin_t)


def _stack_queries(qlat_ref, qrope_ref, qs, tq):
    lane = lax.broadcasted_iota(jnp.int32, (tq, LANES), 1)
    for h in range(N_HEADS):
        g, i = divmod(h, 4)
        rope = qrope_ref[0, :, g * LANES:(g + 1) * LANES]
        keep = (lane >= i * QK_ROPE) & (lane < (i + 1) * QK_ROPE)
        qs[h * tq:(h + 1) * tq, 0:KV_LORA] = qlat_ref[0, :, h * KV_LORA:(h + 1) * KV_LORA]
        qs[h * tq:(h + 1) * tq, KV_LORA:KV_LORA + LANES] = jnp.where(keep, rope, jnp.zeros_like(rope))


def _softmax_step(qs, k, v, m_s, l_s, acc_s, mask=None, groups=1):
    tk = k.shape[0]
    rows = qs.shape[0] // groups
    for g in range(groups):
        r = slice(g * rows, (g + 1) * rows)
        s = lax.dot_general(qs[r, :], k, (((1,), (1,)), ((), ())), preferred_element_type=F32)
        if mask is not None:
            col, limit = mask
            s = jnp.where(col < limit[r], s, NEG)
        m_prev = m_s[r, :]
        m_new = jnp.maximum(m_prev, jnp.max(s, axis=1, keepdims=True))
        alpha = jnp.exp2(m_prev - m_new)
        if tk % LANES == 0:
            p = jnp.exp2(s - jnp.tile(m_new, (1, tk // LANES)))
        else:
            p = jnp.exp2(s - m_new[:, :tk])
        l_s[r, :] = alpha * l_s[r, :] + jnp.sum(p, axis=1, keepdims=True)
        acc_s[r, :] = alpha * acc_s[r, :] + jnp.dot(p.astype(BF16), v, preferred_element_type=F32)
        m_s[r, :] = m_new


def _attn_epilogue(acc_s, l_s, wuv_ref, g_ref, o_ref, tq):
    o = acc_s[...] / l_s[...]
    parts = []
    for p in range(N_HEADS // 2):
        op = jnp.concatenate([o[(2 * p) * tq:(2 * p + 1) * tq], o[(2 * p + 1) * tq:(2 * p + 2) * tq]], axis=1)
        parts.append(jnp.dot(op.astype(BF16), wuv_ref[p], preferred_element_type=F32))
    b = jnp.concatenate(parts, axis=1)
    o_ref[0] = (_rms(b) * g_ref[...]).astype(BF16)


def _attn_prompt_kernel(qlat_ref, qrope_ref, k_ref, vt_ref, wuvt_ref, g_ref, o_ref, qs, m_s, l_s, acc_s, sa, sb,
                        *, tq, tk):
    i = pl.program_id(1)
    M = N_HEADS * tq
    _stack_queries(qlat_ref, qrope_ref, qs, tq)
    m_s[...] = jnp.full(m_s.shape, NEG, F32)
    l_s[...] = jnp.zeros(l_s.shape, F32)
    acc_s[...] = jnp.zeros(acc_s.shape, F32)
    q0 = i * tq
    n_full = (q0 + CHUNK) // tk

    def scores(t, dst):
        k = k_ref[0, pl.ds(pl.multiple_of(t * tk, tk), tk), :]
        dst[...] = lax.dot_general(k, qs[...], (((1,), (1,)), ((), ())), preferred_element_type=F32)

    def update(t, src, limit=None):
        start = pl.multiple_of(t * tk, tk)
        vt = vt_ref[0, :, pl.ds(start, tk)]
        s = src[...]
        if limit is not None:
            kpos = start + lax.broadcasted_iota(jnp.int32, (tk, 1), 0)
            s = jnp.where(kpos < limit, s, NEG)
        m_prev = m_s[...]
        m_new = jnp.maximum(m_prev, jnp.max(s, axis=0, keepdims=True))
        alpha = jnp.exp2(m_prev - m_new)
        p = jnp.exp2(s - m_new)
        l_s[...] = alpha * l_s[...] + jnp.sum(p, axis=0, keepdims=True)
        acc_s[...] = alpha * acc_s[...] + jnp.dot(vt, p.astype(BF16), preferred_element_type=F32)
        m_s[...] = m_new

    scores(0, sa)

    def body(j, carry):
        t = 2 * j
        scores(t + 1, sb)
        update(t, sa)
        scores(t + 2, sa)
        update(t + 1, sb)
        return carry

    lax.fori_loop(0, n_full // 2, body, 0)

    col_t = jnp.bitwise_and(lax.broadcasted_iota(jnp.int32, (1, M), 1), tq - 1)
    limit = q0 + (jnp.right_shift(col_t, CHUNK.bit_length() - 1) + 1) * CHUNK
    odd = n_full % 2

    @pl.when(odd == 0)
    def _():
        update(n_full, sa, limit)

    @pl.when(odd == 1)
    def _():
        scores(n_full, sb)
        update(n_full - 1, sa)
        update(n_full, sb, limit)

    o_t = acc_s[...] / l_s[...]
    parts = []
    for p in range(N_HEADS // 2):
        pair = jnp.concatenate([o_t[:, (2 * p) * tq:(2 * p + 1) * tq], o_t[:, (2 * p + 1) * tq:(2 * p + 2) * tq]],
                               axis=0)
        parts.append(jnp.dot(wuvt_ref[p], pair.astype(BF16), preferred_element_type=F32))
    b = jnp.concatenate(parts, axis=0).T
    o_ref[0] = (_rms(b) * g_ref[...]).astype(BF16)


def _attn_prompt(qlat, qrope, kcat, vt, w, tq, tk):
    B, T, _ = qlat.shape
    M = N_HEADS * tq
    return pl.pallas_call(
        functools.partial(_attn_prompt_kernel, tq=tq, tk=tk),
        grid=(B, T // tq),
        in_specs=[
            pl.BlockSpec((1, tq, N_HEADS * KV_LORA), lambda b, i: (b, i, 0)),
            pl.BlockSpec((1, tq, 2 * LANES), lambda b, i: (b, i, 0)),
            pl.BlockSpec((1, T, 2 * LANES), lambda b, i: (b, 0, 0)),
            pl.BlockSpec((1, KV_LORA, T), lambda b, i: (b, 0, 0)),
            pl.BlockSpec((N_HEADS // 2, LANES, 2 * KV_LORA), lambda b, i: (0, 0, 0)),
            pl.BlockSpec((1, ATTN_WIDTH), lambda b, i: (0, 0)),
        ],
        out_specs=pl.BlockSpec((1, tq, ATTN_WIDTH), lambda b, i: (b, i, 0)),
        out_shape=jax.ShapeDtypeStruct((B, T, ATTN_WIDTH), BF16),
        scratch_shapes=[pltpu.VMEM((M, 2 * LANES), BF16), pltpu.VMEM((1, M), F32),
                        pltpu.VMEM((1, M), F32), pltpu.VMEM((KV_LORA, M), F32),
                        pltpu.VMEM((tk, M), F32), pltpu.VMEM((tk, M), F32)],
        compiler_params=_cparams(("arbitrary", "arbitrary")),
        name="attn_prompt",
    )(qlat, qrope, kcat, vt, jnp.swapaxes(w["w_uv_pair"], 1, 2), w["g_oa"])


def _attn_sample_kernel(qlat_ref, qrope_ref, knew_ref, pckv_ref, pkr_ref, wuv_ref, g_ref, o_ref,
                        qs, m_s, l_s, acc_s, *, tq, tk, n_past):
    _stack_queries(qlat_ref, qrope_ref, qs, tq)
    m_s[...] = jnp.full(m_s.shape, NEG, F32)
    l_s[...] = jnp.zeros(l_s.shape, F32)
    acc_s[...] = jnp.zeros(acc_s.shape, F32)

    def body(t, carry):
        start = pl.multiple_of(t * tk, tk)
        ck = pckv_ref[0, pl.ds(start, tk), :]
        kr = pkr_ref[0, pl.ds(start, tk), :]
        k = jnp.concatenate([ck, kr, kr, kr, kr], axis=1).astype(BF16)
        _softmax_step(qs, k, k[:, :KV_LORA], m_s, l_s, acc_s)
        return carry

    lax.fori_loop(0, n_past // tk, body, 0)
    k = knew_ref[0]
    _softmax_step(qs, k, k[:, :KV_LORA], m_s, l_s, acc_s)
    _attn_epilogue(acc_s, l_s, wuv_ref, g_ref, o_ref, tq)


def _attn_sample(qlat, qrope, kcat, past_ckv, past_krope, w, tk):
    B, T, _ = qlat.shape
    n_past = past_ckv.shape[1]
    M = N_HEADS * T
    per_b = lambda r, c: pl.BlockSpec((1, r, c), lambda b: (b, 0, 0))
    return pl.pallas_call(
        functools.partial(_attn_sample_kernel, tq=T, tk=tk, n_past=n_past),
        grid=(B,),
        in_specs=[per_b(T, N_HEADS * KV_LORA), per_b(T, 2 * LANES), per_b(T, 2 * LANES),
                  per_b(n_past, KV_LORA), per_b(n_past, QK_ROPE),
                  pl.BlockSpec((N_HEADS // 2, 2 * KV_LORA, LANES), lambda b: (0, 0, 0)),
                  pl.BlockSpec((1, ATTN_WIDTH), lambda b: (0, 0))],
        out_specs=per_b(T, ATTN_WIDTH),
        out_shape=jax.ShapeDtypeStruct((B, T, ATTN_WIDTH), BF16),
        scratch_shapes=[pltpu.VMEM((M, 2 * LANES), BF16), pltpu.VMEM((M, LANES), F32),
                        pltpu.VMEM((M, LANES), F32), pltpu.VMEM((M, KV_LORA), F32)],
        compiler_params=_cparams(("arbitrary",)),
        name="attn_sample",
    )(qlat, qrope, kcat, past_ckv, past_krope, w["w_uv_pair"], w["g_oa"])


def _outproj_kernel(x_ref, an_ref, bn_ref, gt1_ref, sc2_ref, sh2_ref, wo_ref, l1g_ref, l1b_ref,
                    rw_ref, rb_ref, x1_ref, h2_ref, meta_ref, gates_ref, cnt_ref, run_s,
                    *, tm, alpha):
    first = (pl.program_id(0) == 0) & (pl.program_id(1) == 0)

    @pl.when(first)
    def _():
        run_s[...] = jnp.zeros(run_s.shape, F32)

    m = (jnp.dot(an_ref[0], wo_ref[0:CONV_WIDTH, :], preferred_element_type=F32)
         + jnp.dot(bn_ref[0], wo_ref[CONV_WIDTH:, :], preferred_element_type=F32))
    x1 = _layernorm(alpha * x_ref[0] + gt1_ref[0] * m) * l1g_ref[...] + l1b_ref[...]
    x1_ref[0] = x1
    h2 = _layernorm(x1) * (1.0 + sc2_ref[0]) + sh2_ref[0]
    _rows_to_tiles(h2_ref, h2)

    logits = jnp.dot(h2.astype(BF16), rw_ref[...], preferred_element_type=F32) + rb_ref[...]
    lane = lax.broadcasted_iota(jnp.int32, (tm, LANES), 1)
    lane_f = lane.astype(F32)
    lg = logits
    vals, sels = [], []
    chosen = jnp.zeros((tm, LANES), F32)
    for _ in range(TOP_K):
        mx = jnp.max(lg, axis=1, keepdims=True)
        idx = jnp.min(jnp.where(lg == mx, lane_f, float(LANES)), axis=1, keepdims=True)
        sel = lane_f == idx
        vals.append(mx)
        sels.append(idx)
        chosen = jnp.where(sel, 1.0, chosen)
        lg = jnp.where(sel, NEG, lg)

    es = [jnp.exp(v - vals[0]) for v in vals]
    denom = es[0] + es[1] + es[2] + es[3]

    run_s[0:1, :] = run_s[0:1, :] + jnp.sum(chosen, axis=0, keepdims=True)
    cnt_ref[...] = jnp.broadcast_to(run_s[0:1, :], cnt_ref.shape)

    meta = jnp.zeros((tm, LANES), jnp.int32)
    gates = jnp.zeros((tm, LANES), F32)
    for k in range(TOP_K):
        meta = jnp.where(lane == k, sels[k].astype(jnp.int32), meta)
        gates = jnp.where(lane == k, es[k] / denom, gates)
    meta_ref[0] = meta
    gates_ref[0] = gates


def _outproj(x, a_n, b_n, gt1, sc2, sh2, w, tm, alpha):
    B, T, _ = x.shape
    tile = lambda c: pl.BlockSpec((1, tm, c), lambda b, j: (b, j, 0))
    modv = pl.BlockSpec((1, 1, D_MODEL), lambda b, j: (b, 0, 0))
    full = lambda shp: pl.BlockSpec(shp, lambda b, j: (0,) * len(shp))
    return pl.pallas_call(
        functools.partial(_outproj_kernel, tm=tm, alpha=alpha),
        grid=(B, T // tm),
        in_specs=[tile(D_MODEL), tile(CONV_WIDTH), tile(ATTN_WIDTH), modv, modv, modv,
                  full((CONV_WIDTH + ATTN_WIDTH, D_MODEL)), full((1, D_MODEL)), full((1, D_MODEL)),
                  full((D_MODEL, LANES)), full((1, LANES))],
        out_specs=(tile(D_MODEL), pl.BlockSpec((tm * ROW_TILE, LANES), lambda b, j: (b * (T // tm) + j, 0)),
                   tile(LANES), tile(LANES), pl.BlockSpec((8, LANES), lambda b, j: (0, 0))),
        out_shape=(jax.ShapeDtypeStruct((B, T, D_MODEL), F32),
                   jax.ShapeDtypeStruct((B * T * ROW_TILE, LANES), F32),
                   jax.ShapeDtypeStruct((B, T, LANES), jnp.int32), jax.ShapeDtypeStruct((B, T, LANES), F32),
                   jax.ShapeDtypeStruct((8, LANES), F32)),
        scratch_shapes=[pltpu.VMEM((8, LANES), F32)],
        compiler_params=_cparams(("arbitrary", "arbitrary")),
        name="outproj",
    )(x, a_n, b_n, gt1, sc2, sh2, w["w_out"], w["ln1_g"], w["ln1_b"], w["router_w"], w["router_b"])


def _expert_kernel(order_ref, blk_e_ref, blk_j0_ref, blk_nv_ref, nxt_e_ref, used_ref,
                   h_ref, wgu_hbm, bgu_ref, wd_hbm, bd_ref, y_ref,
                   x0, x1, y0, y1, xb, act_s, wgu_st, wd_st, wgu_bf, wd_bf, gsem, ssem, wsem, *, bm, n_tok):
    s = pl.program_id(0)
    used = used_ref[0]
    dump0 = TOP_K * n_tok

    def weight_copies(e):
        return (pltpu.make_async_copy(wgu_hbm.at[e], wgu_st, wsem.at[0]),
                pltpu.make_async_copy(wd_hbm.at[e], wd_st, wsem.at[1]))

    def gather_start(j0, xbuf, sem, r, prio=0):
        tok = jnp.right_shift(order_ref[j0 + r], 2)
        pltpu.make_async_copy(h_ref.at[tok], xbuf.at[pl.ds(r * ROW_TILE, ROW_TILE)], sem).start(priority=prio)

    def scatter_start(j0, nv, ybuf, sem, r, odd, prio=0):
        o = order_ref[j0 + r]
        real = jnp.bitwise_and(o, TOP_K - 1) * n_tok + jnp.right_shift(o, 2)
        dump = dump0 + odd * bm + r
        valid = jnp.right_shift(r - nv, 31)
        dst = dump + jnp.bitwise_and(valid, real - dump)
        pltpu.make_async_copy(ybuf.at[pl.ds(r * ROW_TILE, ROW_TILE)], y_ref.at[dst], sem).start(priority=prio)

    def block_wait(buf, sem):
        pltpu.make_async_copy(buf, buf, sem).wait()

    def switch_weights(b):
        prev = blk_e_ref[jnp.maximum(b - 1, 0)]
        e = blk_e_ref[b]

        @pl.when((b == 0) | (e != prev))
        def _():
            for c in weight_copies(e):
                c.wait()
            wgu_bf[...] = wgu_st[...].astype(BF16)
            wd_bf[...] = wd_st[...].astype(BF16)

            @pl.when(nxt_e_ref[e] >= 0)
            def _():
                for c in weight_copies(nxt_e_ref[e]):
                    c.start()

    def block(b, x_in, y_out, g_next, x_next, gsem_next, s_prev, y_prev, ssem_prev, prev_odd, wait_y_free):
        gj0 = blk_j0_ref[g_next + 1]
        sj0 = blk_j0_ref[s_prev + 1]
        snv = blk_nv_ref[s_prev + 1]
        def issue(part, after=None):
            zero = 0
            if after is not None:
                bits = pltpu.bitcast(jnp.abs(after[0:8, 0:LANES]), jnp.int32)
                zero = jnp.minimum(bits[0, 0], 0)
            half = FFN_PARTS // 2
            q = part % half
            for r in range(q * bm // half, (q + 1) * bm // half):
                if part < half:
                    gather_start(gj0 + zero, x_next, gsem_next, r, prio=r % 2)
                else:
                    scatter_start(sj0 + zero, snv, y_prev, ssem_prev, r, prev_odd, prio=r % 2)

        xb[...] = _rows_from_tiles(x_in, bm).astype(BF16)
        e = blk_e_ref[b]
        cw = D_FF // 4
        prev = None
        for c in range(4):
            issue(c, prev)
            gs, ls = slice(c * cw, (c + 1) * cw), slice(D_FF + c * cw, D_FF + (c + 1) * cw)
            g = jnp.dot(xb[...], wgu_bf[:, gs], preferred_element_type=F32) + bgu_ref[e][:, gs]
            lin = jnp.dot(xb[...], wgu_bf[:, ls], preferred_element_type=F32) + bgu_ref[e][:, ls]
            prev = g
            g = jnp.minimum(g, SWIGLU_LIMIT)
            lin = jnp.clip(lin, -SWIGLU_LIMIT, SWIGLU_LIMIT)
            act_s[:, gs] = (g * _sigmoid(SWIGLU_ALPHA * g) * (lin + 1.0)).astype(BF16)
        hw = D_MODEL // 2
        wait_y_free()
        for h in range(2):
            issue(4 + h, prev)
            y = jnp.dot(act_s[...], wd_bf[:, h * hw:(h + 1) * hw], preferred_element_type=F32)
            y = y + bd_ref[e][:, h * hw:(h + 1) * hw]
            prev = y
            for c in range(hw // LANES):
                y_out[pl.ds(h * (hw // LANES) + c, bm, stride=ROW_TILE), :] = y[:, c * LANES:(c + 1) * LANES]

    b0 = 2 * s
    b1 = b0 + 1

    @pl.when(b0 < used)
    def _():
        @pl.when(s == 0)
        def _():
            for c in weight_copies(blk_e_ref[0]):
                c.start()
            y1[...] = jnp.zeros(y1.shape, F32)

            def first(r, c):
                scatter_start(0, 0, y1, ssem.at[0], r, 0)
                gather_start(blk_j0_ref[1], x0, gsem.at[0], r)
                return c

            lax.fori_loop(0, bm, first, 0)
            block_wait(y1, ssem.at[0])

        def y0_free():
            @pl.when(s > 0)
            def _():
                block_wait(y0, ssem.at[0])

        def y1_free():
            block_wait(y1, ssem.at[1])

        switch_weights(b0)
        block_wait(x0, gsem.at[0])
        block(b0, x0, y0, b1, x1, gsem.at[1], b0 - 1, y1, ssem.at[1], 1, y0_free)

        switch_weights(b1)
        block_wait(x1, gsem.at[1])
        block(b1, x1, y1, b0 + 2, x0, gsem.at[0], b0, y0, ssem.at[0], 0, y1_free)

        @pl.when(b0 + 2 >= used)
        def _():
            def last(r, c):
                scatter_start(blk_j0_ref[b1 + 1], blk_nv_ref[b1 + 1], y1, ssem.at[1], r, 1)
                return c

            lax.fori_loop(0, bm, last, 0)
            block_wait(y0, ssem.at[0])
            block_wait(y1, ssem.at[1])
            block_wait(x0, gsem.at[0])


def _expert(order, blk_e, blk_j0, blk_nv, nxt_e, used, h2, w, bm):
    n_tok = h2.shape[0]
    nblk = blk_e.shape[0]
    f32buf = lambda shp: pltpu.VMEM(shp, F32)
    return pl.pallas_call(
        functools.partial(_expert_kernel, bm=bm, n_tok=n_tok),
        grid_spec=pltpu.PrefetchScalarGridSpec(
            num_scalar_prefetch=6,
            grid=(nblk // 2,),
            in_specs=[
                pl.BlockSpec(memory_space=pl.ANY),
                pl.BlockSpec(memory_space=pl.ANY),
                pl.BlockSpec((N_EXPERTS, 1, 2 * D_FF), lambda s, *_: (0, 0, 0)),
                pl.BlockSpec(memory_space=pl.ANY),
                pl.BlockSpec((N_EXPERTS, 1, D_MODEL), lambda s, *_: (0, 0, 0)),
            ],
            out_specs=pl.BlockSpec(memory_space=pl.ANY),
            scratch_shapes=[f32buf((bm * ROW_TILE, LANES)), f32buf((bm * ROW_TILE, LANES)),
                            f32buf((bm * ROW_TILE, LANES)), f32buf((bm * ROW_TILE, LANES)),
                            pltpu.VMEM((bm, D_MODEL), BF16), pltpu.VMEM((bm, D_FF), BF16),
                            f32buf((D_MODEL, 2 * D_FF)), f32buf((D_FF, D_MODEL)),
                            pltpu.VMEM((D_MODEL, 2 * D_FF), BF16), pltpu.VMEM((D_FF, D_MODEL), BF16),
                            pltpu.SemaphoreType.DMA((2,)), pltpu.SemaphoreType.DMA((2,)),
                            pltpu.SemaphoreType.DMA((2,))],
        ),
        out_shape=jax.ShapeDtypeStruct((TOP_K * n_tok + 2 * bm, ROW_TILE, LANES), F32),
        compiler_params=_cparams(("arbitrary",)),
        name="moe_expert",
    )(order, blk_e, blk_j0, blk_nv, nxt_e, used, h2, w["w_gu"], w["b_gu"], w["w_down"], w["b_down"])


def _combine_kernel(y0_ref, y1_ref, y2_ref, y3_ref, x1_ref, gates_ref, gt2_ref, l2g_ref, l2b_ref, o_ref, *, alpha):
    gates = gates_ref[...]
    tm = gates.shape[0]
    f = gates[:, 0:1] * _rows_from_tiles(y0_ref, tm)
    for k, y_ref in enumerate((y1_ref, y2_ref, y3_ref), start=1):
        f = f + gates[:, k:k + 1] * _rows_from_tiles(y_ref, tm)
    o_ref[...] = _layernorm(alpha * x1_ref[...] + gt2_ref[0] * f) * l2g_ref[...] + l2b_ref[...]


def _combine(y_rows, x1, gates, gt2, w, tm, tokens_per_batch, alpha):
    N = x1.shape[0]
    per_b = tokens_per_batch // tm
    nt = N // tm
    slot = lambda k: pl.BlockSpec((tm * ROW_TILE, LANES), lambda i: (k * nt + i, 0))
    return pl.pallas_call(
        functools.partial(_combine_kernel, alpha=alpha),
        grid=(nt,),
        in_specs=[slot(0), slot(1), slot(2), slot(3),
                  pl.BlockSpec((tm, D_MODEL), lambda i: (i, 0)),
                  pl.BlockSpec((tm, LANES), lambda i: (i, 0)),
                  pl.BlockSpec((1, 1, D_MODEL), lambda i: (i // per_b, 0, 0)),
                  pl.BlockSpec((1, D_MODEL), lambda i: (0, 0)),
                  pl.BlockSpec((1, D_MODEL), lambda i: (0, 0))],
        out_specs=pl.BlockSpec((tm, D_MODEL), lambda i: (i, 0)),
        out_shape=jax.ShapeDtypeStruct((N, D_MODEL), F32),
        compiler_params=_cparams(("arbitrary",)),
        name="moe_combine",
    )(y_rows, y_rows, y_rows, y_rows, x1, gates, gt2, w["ln2_g"], w["ln2_b"])


def _moe(h2, x1, idx, gates, counts, gt2, w, tokens_per_batch, bm, tm_c, alpha):
    N = h2.shape[0]
    n_rows = N * TOP_K
    nblk = (n_rows + N_EXPERTS * (bm - 1)) // bm
    nblk += nblk % 2
    experts = jnp.arange(N_EXPERTS, dtype=jnp.int32)
    order = jnp.concatenate([jnp.argsort(idx.reshape(-1)).astype(jnp.int32), jnp.zeros((bm,), jnp.int32)])
    nb_e = (counts + bm - 1) // bm
    blk_end = jnp.cumsum(nb_e)
    first_blk = blk_end - nb_e
    start_sorted = jnp.cumsum(counts) - counts
    used = blk_end[-1].astype(jnp.int32)
    b = jnp.arange(-1, nblk + 1, dtype=jnp.int32)
    bc = jnp.clip(b, 0, used - 1)
    e = jnp.minimum(jnp.sum(blk_end[None, :] <= bc[:, None], axis=1), N_EXPERTS - 1).astype(jnp.int32)
    pick = lambda table: jnp.sum(jnp.where(e[:, None] == experts, table, 0), axis=1)
    local = bc - pick(first_blk)
    blk_j0 = (pick(start_sorted) + local * bm).astype(jnp.int32)
    blk_nv = jnp.where((b >= 0) & (b < used), jnp.minimum(bm, pick(counts) - local * bm), 0).astype(jnp.int32)
    blk_e = e[1:nblk + 1]
    later = (experts[None, :] > experts[:, None]) & (counts[None, :] > 0)
    nxt = jnp.min(jnp.where(later, experts[None, :], N_EXPERTS), axis=1)
    nxt_e = jnp.where(nxt < N_EXPERTS, nxt, -1).astype(jnp.int32)
    y_rows = _expert(order, blk_e, blk_j0, blk_nv, nxt_e, used.reshape(1), h2, w, bm)
    return _combine(y_rows.reshape(-1, LANES), x1, gates, gt2, w, tm_c, tokens_per_batch, alpha)


def _rope_tables(pos):
    half = QK_ROPE // 2
    inv = ROPE_THETA ** (-jnp.arange(half, dtype=F32) / half)
    ang = pos.astype(F32)[:, None] * inv[None, :]
    cos, sin = jnp.cos(ang), jnp.sin(ang)
    cos32 = jnp.concatenate([cos, cos], axis=1)
    sin32 = jnp.concatenate([-sin, sin], axis=1)
    return jnp.tile(cos32, (1, LANES // QK_ROPE)), jnp.tile(sin32, (1, LANES // QK_ROPE))


def _swap_halves(w32):
    shp = w32.shape
    w = w32.reshape(shp[:-1] + (shp[-1] // QK_ROPE, 2, QK_ROPE // 2))
    return w[..., ::-1, :].reshape(shp)


def _prep_weights(l, w_in, conv_w, g_qa, w_qb, g_kva, w_kvb, g_out_conv, g_out_attn, w_out,
                  ln1_g, ln1_b, router_w, router_b, w_gu, b_gu, w_down, b_down, ln2_g, ln2_b):
    w = {}
    wi = w_in[l]
    k_r = wi[:, _O_KR:_O_KR + QK_ROPE]
    rep = LANES // QK_ROPE
    w["w_in"] = jnp.concatenate([wi[:, :_O_KR], jnp.tile(k_r, (1, rep)), jnp.tile(_swap_halves(k_r), (1, rep))],
                                axis=1).astype(BF16)
    w["conv_w"] = conv_w[l]
    w["g_qa"] = g_qa[l].reshape(1, Q_LORA)
    w["g_kva"] = g_kva[l].reshape(1, KV_LORA)
    w["g_oc"] = g_out_conv[l].reshape(1, CONV_WIDTH)
    w["g_oa"] = g_out_attn[l].reshape(1, ATTN_WIDTH)
    wq = w_qb[l].reshape(Q_LORA, N_HEADS, QK_NOPE + QK_ROPE)
    w["wq_nope"] = wq[:, :, :QK_NOPE].reshape(Q_LORA, N_HEADS * QK_NOPE).astype(BF16)
    wq_rope = wq[:, :, QK_NOPE:].reshape(Q_LORA, N_HEADS * QK_ROPE)
    w["wq_rope"] = wq_rope.astype(BF16)
    w["wq_rope_sw"] = _swap_halves(wq_rope).astype(BF16)
    w_uk = jnp.transpose(w_kvb[l][:, :, :QK_NOPE], (1, 2, 0))
    w_uv = jnp.transpose(w_kvb[l][:, :, QK_NOPE:], (1, 0, 2))
    zk = jnp.zeros((QK_NOPE, KV_LORA), F32)
    zv = jnp.zeros((KV_LORA, V_HEAD), F32)
    w["w_uk_pair"] = jnp.stack([
        jnp.concatenate([jnp.concatenate([w_uk[2 * p], zk], axis=1),
                         jnp.concatenate([zk, w_uk[2 * p + 1]], axis=1)], axis=0)
        for p in range(N_HEADS // 2)]).astype(BF16)
    w["w_uv_pair"] = jnp.stack([
        jnp.concatenate([jnp.concatenate([w_uv[2 * p], zv], axis=1),
                         jnp.concatenate([zv, w_uv[2 * p + 1]], axis=1)], axis=0)
        for p in range(N_HEADS // 2)]).astype(BF16)
    w["w_out"] = w_out[l].astype(BF16)
    w["ln1_g"] = ln1_g[l].reshape(1, D_MODEL)
    w["ln1_b"] = ln1_b[l].reshape(1, D_MODEL)
    w["ln2_g"] = ln2_g[l].reshape(1, D_MODEL)
    w["ln2_b"] = ln2_b[l].reshape(1, D_MODEL)
    w["router_w"] = jnp.pad(router_w[l], ((0, 0), (0, LANES - N_EXPERTS))).astype(BF16)
    w["router_b"] = jnp.concatenate([router_b[l], jnp.full((LANES - N_EXPERTS,), NEG, F32)]).reshape(1, LANES)
    w["w_gu"] = w_gu[l]
    w["b_gu"] = b_gu[l].reshape(N_EXPERTS, 1, 2 * D_FF)
    w["w_down"] = w_down[l]
    w["b_down"] = b_down[l].reshape(N_EXPERTS, 1, D_MODEL)
    return w


def _layer(x, mod, conv_prev, past, pos0, w, alpha, *, tm_in, tm_out, bm, tm_c, tq=128, tk=512):
    B, T, _ = x.shape
    sh1, sc1, gt1, sh2, sc2, gt2 = [mod[:, None, i * D_MODEL:(i + 1) * D_MODEL] for i in range(N_MOD)]
    cos_t, sin_t = _rope_tables(pos0 + jnp.arange(T, dtype=jnp.int32))
    a_n, qlat, qrope, kcat, ckv, krope, conv_new, *vt = _inproj(x, sc1, sh1, conv_prev, cos_t, sin_t, w, tm_in,
                                                                 with_vt=past is None)
    if past is None:
        b_n = _attn_prompt(qlat, qrope, kcat, vt[0], w, tq, tk)
    else:
        b_n = _attn_sample(qlat, qrope, kcat, past[0], past[1], w, tk)
    x1, h2, meta, gates, cnt = _outproj(x, a_n, b_n, gt1, sc2, sh2, w, tm_out, alpha)
    N = B * T
    counts = cnt[0, :N_EXPERTS].astype(jnp.int32)
    y = _moe(h2.reshape(N, ROW_TILE, LANES), x1.reshape(N, D_MODEL), meta.reshape(N, LANES)[:, :TOP_K],
             gates.reshape(N, LANES), counts, gt2, w, T, bm, tm_c, alpha)
    return y.reshape(B, T, D_MODEL), ckv, krope, conv_new


def kernel(x_prompt, x_sample, c_prompt, c_sample, cache_ckv, cache_krope, state_conv, w_ada, b_ada, w_in, conv_w, g_qa, w_qb, g_kva, w_kvb, g_out_conv, g_out_attn, w_out, ln1_g, ln1_b, router_w, router_b, w_gu, b_gu, w_down, b_down, ln2_g, ln2_b):
    depth = w_ada.shape[0]
    Bp, Tp, _ = x_prompt.shape
    Bs, Ts, _ = x_sample.shape
    past_len = cache_ckv.shape[2]
    assert Ts == CHUNK and past_len % CHUNK == 0 and Tp % 512 == 0
    alpha = (2.0 * depth) ** 0.25
    xp, xs = x_prompt, x_sample
    outs = [[] for _ in range(6)]
    c_all = jnp.concatenate([c_prompt, c_sample, jnp.zeros((16 - Bp - Bs, D_MODEL), F32)], axis=0)
    for l in range(depth):
        w = _prep_weights(l, w_in, conv_w, g_qa, w_qb, g_kva, w_kvb, g_out_conv, g_out_attn, w_out,
                          ln1_g, ln1_b, router_w, router_b, w_gu, b_gu, w_down, b_down, ln2_g, ln2_b)
        mod = _ada(c_all, w_ada[l], b_ada[l])
        xp, ckv_p, kr_p, cv_p = _layer(xp, mod[:Bp], jnp.zeros((Bp, CONV_K - 1, CONV_WIDTH), F32), None, 0, w, alpha,
                                       tm_in=512, tm_out=256, bm=256, tm_c=256)
        xs, ckv_s, kr_s, cv_s = _layer(xs, mod[Bp:Bp + Bs], state_conv[l], (cache_ckv[l], cache_krope[l]),
                                       past_len, w, alpha, tm_in=Ts, tm_out=Ts, bm=128, tm_c=Ts)
        for o, v in zip(outs, (ckv_p, kr_p, cv_p, ckv_s, kr_s, cv_s)):
            o.append(v)
    return (xp, xs) + tuple(jnp.stack(o) for o in outs)
```

```python
import functools
import math

import jax
import jax.numpy as jnp
from jax import lax
from jax.experimental import pallas as pl
from jax.experimental.pallas import tpu as pltpu

F32 = jnp.float32
BF16 = jnp.bfloat16

D_MODEL = 1024
CHUNK = 64
CONV_WIDTH = 512
CONV_K = 3
N_HEADS = 8
QK_NOPE = 64
QK_ROPE = 32
V_HEAD = 64
Q_LORA = 256
KV_LORA = 128
ATTN_WIDTH = N_HEADS * V_HEAD
ROPE_THETA = 10000.0
ATTN_SCALE = 1.0 / math.sqrt(QK_NOPE + QK_ROPE)
Q_SCALE = ATTN_SCALE * math.log2(math.e)
N_EXPERTS = 32
TOP_K = 4
D_FF = 1024
SWIGLU_LIMIT = 7.0
SWIGLU_ALPHA = 1.702
N_MOD = 6
LN_EPS = 1e-5
RMS_EPS = 1e-6

LANES = 128
ATTN_GROUPS = 1
FFN_PARTS = 6
NEG = -1e30
VMEM_LIMIT = 56 * 1024 * 1024

_O_XB, _O_XC, _O_XV = 0, CONV_WIDTH, 2 * CONV_WIDTH
_O_QA = 3 * CONV_WIDTH
_O_KVA = _O_QA + Q_LORA
_O_KR = _O_KVA + KV_LORA
_O_KRS = _O_KR + LANES
IN_COLS_EXT = _O_KRS + LANES


def _cparams(sem):
    return pltpu.CompilerParams(dimension_semantics=sem, vmem_limit_bytes=VMEM_LIMIT)


def _layernorm(x):
    mu = jnp.mean(x, axis=-1, keepdims=True)
    xc = x - mu
    var = jnp.mean(xc * xc, axis=-1, keepdims=True)
    return xc * lax.rsqrt(var + LN_EPS)


def _rms(x):
    return x * lax.rsqrt(jnp.mean(x * x, axis=-1, keepdims=True) + RMS_EPS)


def _sigmoid(x):
    return 1.0 / (1.0 + jnp.exp(-x))


ROW_TILE = D_MODEL // LANES


def _rows_from_tiles(ref, n):
    return jnp.concatenate([ref[pl.ds(c, n, stride=ROW_TILE), :] for c in range(ROW_TILE)], axis=1)


def _rows_to_tiles(ref, x):
    n = x.shape[0]
    for c in range(ROW_TILE):
        ref[pl.ds(c, n, stride=ROW_TILE), :] = x[:, c * LANES:(c + 1) * LANES]


def _ada_kernel(c_ref, w_ref, b_ref, o_ref):
    c = c_ref[...]
    s = (c * _sigmoid(c)).astype(BF16)
    o_ref[...] = jnp.dot(s, w_ref[...].astype(BF16), preferred_element_type=F32) + b_ref[...]


def _ada(c_all, w_ada, b_ada):
    rows = c_all.shape[0]
    ncol = w_ada.shape[1]
    tn = 1024
    return pl.pallas_call(
        _ada_kernel,
        grid=(ncol // tn,),
        in_specs=[pl.BlockSpec((rows, D_MODEL), lambda j: (0, 0)),
                  pl.BlockSpec((D_MODEL, tn), lambda j: (0, j)),
                  pl.BlockSpec((1, tn), lambda j: (0, j))],
        out_specs=pl.BlockSpec((rows, tn), lambda j: (0, j)),
        out_shape=jax.ShapeDtypeStruct((rows, ncol), F32),
        compiler_params=_cparams(("arbitrary",)),
        name="ada",
    )(c_all, w_ada, b_ada.reshape(1, ncol))


def _inproj_kernel(x_ref, sc_ref, sh_ref, win_ref, cw_ref, cprev_ref, gqa_ref, gkva_ref, goc_ref,
                   wqn_ref, wqr_ref, wqrs_ref, wuk_ref, cos_ref, sin_ref,
                   an_ref, qlat_ref, qrope_ref, kcat_ref, ckv_ref, krope_ref, cnew_ref,
                   *rest, tm, with_vt):
    vt_ref, ubuf = rest if with_vt else (None, rest[0])
    j = pl.program_id(1)
    x = x_ref[0]
    h = _layernorm(x) * (1.0 + sc_ref[0]) + sh_ref[0]
    proj = jnp.dot(h.astype(BF16), win_ref[...], preferred_element_type=F32)
    xb = proj[:, _O_XB:_O_XB + CONV_WIDTH]
    xc = proj[:, _O_XC:_O_XC + CONV_WIDTH]
    xv = proj[:, _O_XV:_O_XV + CONV_WIDTH]
    q_a = proj[:, _O_QA:_O_QA + Q_LORA]
    kv_a = proj[:, _O_KVA:_O_KVA + KV_LORA]
    kr4 = proj[:, _O_KR:_O_KR + LANES]
    kr4s = proj[:, _O_KRS:_O_KRS + LANES]

    u = xc * xv

    @pl.when(j == 0)
    def _():
        ubuf[6:8, :] = cprev_ref[0]

    ubuf[8:8 + tm, :] = u
    conv = (cw_ref[0:1, :] * ubuf[6:6 + tm, :] + cw_ref[1:2, :] * ubuf[7:7 + tm, :]
            + cw_ref[2:3, :] * u)
    ubuf[0:8, :] = ubuf[tm:tm + 8, :]
    cnew_ref[0] = u[tm - (CONV_K - 1):tm, :]
    an_ref[0] = (_rms(xb * conv) * goc_ref[...]).astype(BF16)

    cos = cos_ref[...]
    sin = sin_ref[...]

    ckv = _rms(kv_a) * gkva_ref[...]
    kro4 = kr4 * cos + kr4s * sin
    ckv_ref[0] = ckv
    krope_ref[0] = kro4[:, :QK_ROPE]
    kcat_ref[0] = jnp.concatenate([ckv, kro4], axis=1).astype(BF16)
    if with_vt:
        vt_ref[0] = ckv.T.astype(BF16)

    qn = (_rms(q_a) * gqa_ref[...]).astype(BF16)
    q_nope = jnp.dot(qn, wqn_ref[...], preferred_element_type=F32)
    xr = jnp.dot(qn, wqr_ref[...], preferred_element_type=F32)
    xrs = jnp.dot(qn, wqrs_ref[...], preferred_element_type=F32)
    for g in range(2):
        sl = slice(g * LANES, (g + 1) * LANES)
        qrope_ref[0, :, sl] = ((xr[:, sl] * cos + xrs[:, sl] * sin) * Q_SCALE).astype(BF16)
    for p in range(N_HEADS // 2):
        qp = q_nope[:, p * LANES:(p + 1) * LANES].astype(BF16)
        ql = jnp.dot(qp, wuk_ref[p], preferred_element_type=F32)
        qlat_ref[0, :, p * 2 * KV_LORA:(p + 1) * 2 * KV_LORA] = (ql * Q_SCALE).astype(BF16)


def _inproj(x, sc1, sh1, conv_prev, cos_t, sin_t, w, tm, with_vt):
    B, T, _ = x.shape
    nt = T // tm
    full = lambda shp: pl.BlockSpec(shp, lambda b, j: (0,) * len(shp))
    vt_shape = (jax.ShapeDtypeStruct((B, KV_LORA, T), BF16),) if with_vt else ()
    vt_spec = (pl.BlockSpec((1, KV_LORA, tm), lambda b, j: (b, 0, j)),) if with_vt else ()
    out_shapes = (
        jax.ShapeDtypeStruct((B, T, CONV_WIDTH), BF16),
        jax.ShapeDtypeStruct((B, T, N_HEADS * KV_LORA), BF16),
        jax.ShapeDtypeStruct((B, T, 2 * LANES), BF16),
        jax.ShapeDtypeStruct((B, T, 2 * LANES), BF16),
        jax.ShapeDtypeStruct((B, T, KV_LORA), F32),
        jax.ShapeDtypeStruct((B, T, QK_ROPE), F32),
        jax.ShapeDtypeStruct((B, CONV_K - 1, CONV_WIDTH), F32),
    ) + vt_shape
    tile = lambda c: pl.BlockSpec((1, tm, c), lambda b, j: (b, j, 0))
    return pl.pallas_call(
        functools.partial(_inproj_kernel, tm=tm, with_vt=with_vt),
        grid=(B, nt),
        in_specs=[
            tile(D_MODEL),
            pl.BlockSpec((1, 1, D_MODEL), lambda b, j: (b, 0, 0)),
            pl.BlockSpec((1, 1, D_MODEL), lambda b, j: (b, 0, 0)),
            full((D_MODEL, IN_COLS_EXT)),
            full((CONV_K, CONV_WIDTH)),
            pl.BlockSpec((1, CONV_K - 1, CONV_WIDTH), lambda b, j: (b, 0, 0)),
            full((1, Q_LORA)), full((1, KV_LORA)), full((1, CONV_WIDTH)),
            full((Q_LORA, N_HEADS * QK_NOPE)), full((Q_LORA, 2 * LANES)), full((Q_LORA, 2 * LANES)),
            full((N_HEADS // 2, LANES, 2 * KV_LORA)),
            pl.BlockSpec((tm, LANES), lambda b, j: (j, 0)),
            pl.BlockSpec((tm, LANES), lambda b, j: (j, 0)),
        ],
        out_specs=(tile(CONV_WIDTH), tile(N_HEADS * KV_LORA), tile(2 * LANES), tile(2 * LANES),
                   tile(KV_LORA), tile(QK_ROPE),
                   pl.BlockSpec((1, CONV_K - 1, CONV_WIDTH), lambda b, j: (b, 0, 0))) + vt_spec,
        out_shape=out_shapes,
        scratch_shapes=[pltpu.VMEM((tm + 8, CONV_WIDTH), F32)],
        compiler_params=_cparams(("arbitrary", "arbitrary")),
        name="inproj",
    )(x, sc1, sh1, w["w_in"], w["conv_w"], conv_prev, w["g_qa"], w["g_kva"], w["g_oc"],
      w["wq_nope"], w["wq_rope"], w["wq_rope_sw"], w["w_uk_pair"], cos_t, sin_t)


def _stack_queries(qlat_ref, qrope_ref, qs, tq):
    lane = lax.broadcasted_iota(jnp.int32, (tq, LANES), 1)
    for h in range(N_HEADS):
        g, i = divmod(h, 4)
        rope = qrope_ref[0, :, g * LANES:(g + 1) * LANES]
        keep = (lane >= i * QK_ROPE) & (lane < (i + 1) * QK_ROPE)
        qs[h * tq:(h + 1) * tq, 0:KV_LORA] = qlat_ref[0, :, h * KV_LORA:(h + 1) * KV_LORA]
        qs[h * tq:(h + 1) * tq, KV_LORA:KV_LORA + LANES] = jnp.where(keep, rope, jnp.zeros_like(rope))


def _softmax_step(qs, k, v, m_s, l_s, acc_s, mask=None, groups=1):
    tk = k.shape[0]
    rows = qs.shape[0] // groups
    for g in range(groups):
        r = slice(g * rows, (g + 1) * rows)
        s = lax.dot_general(qs[r, :], k, (((1,), (1,)), ((), ())), preferred_element_type=F32)
        if mask is not None:
            col, limit = mask
            s = jnp.where(col < limit[r], s, NEG)
        m_prev = m_s[r, :]
        m_new = jnp.maximum(m_prev, jnp.max(s, axis=1, keepdims=True))
        alpha = jnp.exp2(m_prev - m_new)
        if tk % LANES == 0:
            p = jnp.exp2(s - jnp.tile(m_new, (1, tk // LANES)))
        else:
            p = jnp.exp2(s - m_new[:, :tk])
        l_s[r, :] = alpha * l_s[r, :] + jnp.sum(p, axis=1, keepdims=True)
        acc_s[r, :] = alpha * acc_s[r, :] + jnp.dot(p.astype(BF16), v, preferred_element_type=F32)
        m_s[r, :] = m_new


def _attn_epilogue(acc_s, l_s, wuv_ref, g_ref, o_ref, tq):
    o = acc_s[...] / l_s[...]
    parts = []
    for p in range(N_HEADS // 2):
        op = jnp.concatenate([o[(2 * p) * tq:(2 * p + 1) * tq], o[(2 * p + 1) * tq:(2 * p + 2) * tq]], axis=1)
        parts.append(jnp.dot(op.astype(BF16), wuv_ref[p], preferred_element_type=F32))
    b = jnp.concatenate(parts, axis=1)
    o_ref[0] = (_rms(b) * g_ref[...]).astype(BF16)


def _attn_prompt_kernel(qlat_ref, qrope_ref, k_ref, vt_ref, wuvt_ref, g_ref, o_ref, qs, m_s, l_s, acc_s, sa, sb,
                        *, tq, tk):
    i = pl.program_id(1)
    M = N_HEADS * tq
    _stack_queries(qlat_ref, qrope_ref, qs, tq)
    m_s[...] = jnp.full(m_s.shape, NEG, F32)
    l_s[...] = jnp.zeros(l_s.shape, F32)
    acc_s[...] = jnp.zeros(acc_s.shape, F32)
    q0 = i * tq
    n_full = (q0 + CHUNK) // tk

    def scores(t, dst):
        k = k_ref[0, pl.ds(pl.multiple_of(t * tk, tk), tk), :]
        dst[...] = lax.dot_general(k, qs[...], (((1,), (1,)), ((), ())), preferred_element_type=F32)

    def update(t, src, limit=None):
        start = pl.multiple_of(t * tk, tk)
        vt = vt_ref[0, :, pl.ds(start, tk)]
        s = src[...]
        if limit is not None:
            kpos = start + lax.broadcasted_iota(jnp.int32, (tk, 1), 0)
            s = jnp.where(kpos < limit, s, NEG)
        m_prev = m_s[...]
        m_new = jnp.maximum(m_prev, jnp.max(s, axis=0, keepdims=True))
        alpha = jnp.exp2(m_prev - m_new)
        p = jnp.exp2(s - m_new)
        l_s[...] = alpha * l_s[...] + jnp.sum(p, axis=0, keepdims=True)
        acc_s[...] = alpha * acc_s[...] + jnp.dot(vt, p.astype(BF16), preferred_element_type=F32)
        m_s[...] = m_new

    scores(0, sa)

    def body(j, carry):
        t = 2 * j
        scores(t + 1, sb)
        update(t, sa)
        scores(t + 2, sa)
        update(t + 1, sb)
        return carry

    lax.fori_loop(0, n_full // 2, body, 0)

    col_t = jnp.bitwise_and(lax.broadcasted_iota(jnp.int32, (1, M), 1), tq - 1)
    limit = q0 + (jnp.right_shift(col_t, CHUNK.bit_length() - 1) + 1) * CHUNK
    odd = n_full % 2

    @pl.when(odd == 0)
    def _():
        update(n_full, sa, limit)

    @pl.when(odd == 1)
    def _():
        scores(n_full, sb)
        update(n_full - 1, sa)
        update(n_full, sb, limit)

    o_t = acc_s[...] / l_s[...]
    parts = []
    for p in range(N_HEADS // 2):
        pair = jnp.concatenate([o_t[:, (2 * p) * tq:(2 * p + 1) * tq], o_t[:, (2 * p + 1) * tq:(2 * p + 2) * tq]],
                               axis=0)
        parts.append(jnp.dot(wuvt_ref[p], pair.astype(BF16), preferred_element_type=F32))
    b = jnp.concatenate(parts, axis=0).T
    o_ref[0] = (_rms(b) * g_ref[...]).astype(BF16)


def _attn_prompt(qlat, qrope, kcat, vt, w, tq, tk):
    B, T, _ = qlat.shape
    M = N_HEADS * tq
    return pl.pallas_call(
        functools.partial(_attn_prompt_kernel, tq=tq, tk=tk),
        grid=(B, T // tq),
        in_specs=[
            pl.BlockSpec((1, tq, N_HEADS * KV_LORA), lambda b, i: (b, i, 0)),
            pl.BlockSpec((1, tq, 2 * LANES), lambda b, i: (b, i, 0)),
            pl.BlockSpec((1, T, 2 * LANES), lambda b, i: (b, 0, 0)),
            pl.BlockSpec((1, KV_LORA, T), lambda b, i: (b, 0, 0)),
            pl.BlockSpec((N_HEADS // 2, LANES, 2 * KV_LORA), lambda b, i: (0, 0, 0)),
            pl.BlockSpec((1, ATTN_WIDTH), lambda b, i: (0, 0)),
        ],
        out_specs=pl.BlockSpec((1, tq, ATTN_WIDTH), lambda b, i: (b, i, 0)),
        out_shape=jax.ShapeDtypeStruct((B, T, ATTN_WIDTH), BF16),
        scratch_shapes=[pltpu.VMEM((M, 2 * LANES), BF16), pltpu.VMEM((1, M), F32),
                        pltpu.VMEM((1, M), F32), pltpu.VMEM((KV_LORA, M), F32),
                        pltpu.VMEM((tk, M), F32), pltpu.VMEM((tk, M), F32)],
        compiler_params=_cparams(("arbitrary", "arbitrary")),
        name="attn_prompt",
    )(qlat, qrope, kcat, vt, jnp.swapaxes(w["w_uv_pair"], 1, 2), w["g_oa"])


def _attn_sample_kernel(qlat_ref, qrope_ref, knew_ref, pckv_ref, pkr_ref, wuv_ref, g_ref, o_ref,
                        qs, m_s, l_s, acc_s, *, tq, tk, n_past):
    _stack_queries(qlat_ref, qrope_ref, qs, tq)
    m_s[...] = jnp.full(m_s.shape, NEG, F32)
    l_s[...] = jnp.zeros(l_s.shape, F32)
    acc_s[...] = jnp.zeros(acc_s.shape, F32)

    def body(t, carry):
        start = pl.multiple_of(t * tk, tk)
        ck = pckv_ref[0, pl.ds(start, tk), :]
        kr = pkr_ref[0, pl.ds(start, tk), :]
        k = jnp.concatenate([ck, kr, kr, kr, kr], axis=1).astype(BF16)
        _softmax_step(qs, k, k[:, :KV_LORA], m_s, l_s, acc_s)
        return carry

    lax.fori_loop(0, n_past // tk, body, 0)
    k = knew_ref[0]
    _softmax_step(qs, k, k[:, :KV_LORA], m_s, l_s, acc_s)
    _attn_epilogue(acc_s, l_s, wuv_ref, g_ref, o_ref, tq)


def _attn_sample(qlat, qrope, kcat, past_ckv, past_krope, w, tk):
    B, T, _ = qlat.shape
    n_past = past_ckv.shape[1]
    M = N_HEADS * T
    per_b = lambda r, c: pl.BlockSpec((1, r, c), lambda b: (b, 0, 0))
    return pl.pallas_call(
        functools.partial(_attn_sample_kernel, tq=T, tk=tk, n_past=n_past),
        grid=(B,),
        in_specs=[per_b(T, N_HEADS * KV_LORA), per_b(T, 2 * LANES), per_b(T, 2 * LANES),
                  per_b(n_past, KV_LORA), per_b(n_past, QK_ROPE),
                  pl.BlockSpec((N_HEADS // 2, 2 * KV_LORA, LANES), lambda b: (0, 0, 0)),
                  pl.BlockSpec((1, ATTN_WIDTH), lambda b: (0, 0))],
        out_specs=per_b(T, ATTN_WIDTH),
        out_shape=jax.ShapeDtypeStruct((B, T, ATTN_WIDTH), BF16),
        scratch_shapes=[pltpu.VMEM((M, 2 * LANES), BF16), pltpu.VMEM((M, LANES), F32),
                        pltpu.VMEM((M, LANES), F32), pltpu.VMEM((M, KV_LORA), F32)],
        compiler_params=_cparams(("arbitrary",)),
        name="attn_sample",
    )(qlat, qrope, kcat, past_ckv, past_krope, w["w_uv_pair"], w["g_oa"])


def _outproj_kernel(x_ref, an_ref, bn_ref, gt1_ref, sc2_ref, sh2_ref, wo_ref, l1g_ref, l1b_ref,
                    rw_ref, rb_ref, h2_all_ref, x1_ref, h2_ref, meta_ref, gates_ref, cnt_ref, run_s,
                    *, tm, alpha):
    del h2_all_ref
    first = (pl.program_id(0) == 0) & (pl.program_id(1) == 0)

    @pl.when(first)
    def _():
        run_s[...] = jnp.zeros(run_s.shape, F32)

    m = (jnp.dot(an_ref[0], wo_ref[0:CONV_WIDTH, :], preferred_element_type=F32)
         + jnp.dot(bn_ref[0], wo_ref[CONV_WIDTH:, :], preferred_element_type=F32))
    x1 = _layernorm(alpha * x_ref[0] + gt1_ref[0] * m) * l1g_ref[...] + l1b_ref[...]
    x1_ref[0] = x1
    h2 = _layernorm(x1) * (1.0 + sc2_ref[0]) + sh2_ref[0]
    _rows_to_tiles(h2_ref, h2)

    logits = jnp.dot(h2.astype(BF16), rw_ref[...], preferred_element_type=F32) + rb_ref[...]
    lane = lax.broadcasted_iota(jnp.int32, (tm, LANES), 1)
    lane_f = lane.astype(F32)
    lg = logits
    vals, sels = [], []
    chosen = jnp.zeros((tm, LANES), F32)
    for _ in range(TOP_K):
        mx = jnp.max(lg, axis=1, keepdims=True)
        idx = jnp.min(jnp.where(lg == mx, lane_f, float(LANES)), axis=1, keepdims=True)
        sel = lane_f == idx
        vals.append(mx)
        sels.append(idx)
        chosen = jnp.where(sel, 1.0, chosen)
        lg = jnp.where(sel, NEG, lg)

    es = [jnp.exp(v - vals[0]) for v in vals]
    denom = es[0] + es[1] + es[2] + es[3]

    run_s[0:1, :] = run_s[0:1, :] + jnp.sum(chosen, axis=0, keepdims=True)
    cnt_ref[...] = jnp.broadcast_to(run_s[0:1, :], cnt_ref.shape)

    meta = jnp.zeros((tm, LANES), jnp.int32)
    gates = jnp.zeros((tm, LANES), F32)
    for k in range(TOP_K):
        meta = jnp.where(lane == k, sels[k].astype(jnp.int32), meta)
        gates = jnp.where(lane == k, es[k] / denom, gates)
    meta_ref[0] = meta
    gates_ref[0] = gates


def _outproj(x, a_n, b_n, gt1, sc2, sh2, h2_all, tok0, w, tm, alpha):
    B, T, _ = x.shape
    blk0 = tok0 // tm
    tile = lambda c: pl.BlockSpec((1, tm, c), lambda b, j: (b, j, 0))
    modv = pl.BlockSpec((1, 1, D_MODEL), lambda b, j: (b, 0, 0))
    full = lambda shp: pl.BlockSpec(shp, lambda b, j: (0,) * len(shp))
    return pl.pallas_call(
        functools.partial(_outproj_kernel, tm=tm, alpha=alpha),
        grid=(B, T // tm),
        in_specs=[tile(D_MODEL), tile(CONV_WIDTH), tile(ATTN_WIDTH), modv, modv, modv,
                  full((CONV_WIDTH + ATTN_WIDTH, D_MODEL)), full((1, D_MODEL)), full((1, D_MODEL)),
                  full((D_MODEL, LANES)), full((1, LANES)), pl.BlockSpec(memory_space=pl.ANY)],
        out_specs=(tile(D_MODEL),
                   pl.BlockSpec((tm * ROW_TILE, LANES), lambda b, j: (blk0 + b * (T // tm) + j, 0)),
                   tile(LANES), tile(LANES), pl.BlockSpec((8, LANES), lambda b, j: (0, 0))),
        out_shape=(jax.ShapeDtypeStruct((B, T, D_MODEL), F32),
                   jax.ShapeDtypeStruct(h2_all.shape, F32),
                   jax.ShapeDtypeStruct((B, T, LANES), jnp.int32), jax.ShapeDtypeStruct((B, T, LANES), F32),
                   jax.ShapeDtypeStruct((8, LANES), F32)),
        scratch_shapes=[pltpu.VMEM((8, LANES), F32)],
        input_output_aliases={11: 1},
        compiler_params=_cparams(("arbitrary", "arbitrary")),
        name="outproj",
    )(x, a_n, b_n, gt1, sc2, sh2, w["w_out"], w["ln1_g"], w["ln1_b"], w["router_w"], w["router_b"], h2_all)


def _expert_kernel(order_ref, blk_e_ref, blk_j0_ref, blk_nv_ref, nxt_e_ref, used_ref,
                   h_ref, wgu_hbm, bgu_ref, wd_hbm, bd_ref, y_ref,
                   x0, x1, y0, y1, xb, act_s, wgu_st, wd_st, wgu_bf, wd_bf, gsem, ssem, wsem, *, bm, n_tok):
    s = pl.program_id(0)
    used = used_ref[0]
    dump0 = TOP_K * n_tok

    def weight_copies(e):
        return (pltpu.make_async_copy(wgu_hbm.at[e], wgu_st, wsem.at[0]),
                pltpu.make_async_copy(wd_hbm.at[e], wd_st, wsem.at[1]))

    def gather_start(j0, xbuf, sem, r, prio=0):
        tok = jnp.right_shift(order_ref[j0 + r], 2)
        pltpu.make_async_copy(h_ref.at[tok], xbuf.at[pl.ds(r * ROW_TILE, ROW_TILE)], sem).start(priority=prio)

    def scatter_start(j0, nv, ybuf, sem, r, odd, prio=0):
        o = order_ref[j0 + r]
        real = jnp.bitwise_and(o, TOP_K - 1) * n_tok + jnp.right_shift(o, 2)
        dump = dump0 + odd * bm + r
        valid = jnp.right_shift(r - nv, 31)
        dst = dump + jnp.bitwise_and(valid, real - dump)
        pltpu.make_async_copy(ybuf.at[pl.ds(r * ROW_TILE, ROW_TILE)], y_ref.at[dst], sem).start(priority=prio)

    def block_wait(buf, sem):
        pltpu.make_async_copy(buf, buf, sem).wait()

    def switch_weights(b):
        prev = blk_e_ref[jnp.maximum(b - 1, 0)]
        e = blk_e_ref[b]

        @pl.when((b == 0) | (e != prev))
        def _():
            for c in weight_copies(e):
                c.wait()
            wgu_bf[...] = wgu_st[...].astype(BF16)
            wd_bf[...] = wd_st[...].astype(BF16)

            @pl.when(nxt_e_ref[e] >= 0)
            def _():
                for c in weight_copies(nxt_e_ref[e]):
                    c.start()

    def block(b, x_in, y_out, g_next, x_next, gsem_next, s_prev, y_prev, ssem_prev, prev_odd, wait_y_free):
        gj0 = blk_j0_ref[g_next + 1]
        sj0 = blk_j0_ref[s_prev + 1]
        snv = blk_nv_ref[s_prev + 1]
        def issue(part, after=None):
            zero = 0
            if after is not None:
                bits = pltpu.bitcast(jnp.abs(after[0:8, 0:LANES]), jnp.int32)
                zero = jnp.minimum(bits[0, 0], 0)
            half = FFN_PARTS // 2
            q = part % half
            for r in range(q * bm // half, (q + 1) * bm // half):
                if part < half:
                    gather_start(gj0 + zero, x_next, gsem_next, r, prio=r % 2)
                else:
                    scatter_start(sj0 + zero, snv, y_prev, ssem_prev, r, prev_odd, prio=r % 2)

        xb[...] = _rows_from_tiles(x_in, bm).astype(BF16)
        e = blk_e_ref[b]
        cw = D_FF // 4
        prev = None
        for c in range(4):
            issue(c, prev)
            gs, ls = slice(c * cw, (c + 1) * cw), slice(D_FF + c * cw, D_FF + (c + 1) * cw)
            g = jnp.dot(xb[...], wgu_bf[:, gs], preferred_element_type=F32) + bgu_ref[e][:, gs]
            lin = jnp.dot(xb[...], wgu_bf[:, ls], preferred_element_type=F32) + bgu_ref[e][:, ls]
            prev = g
            g = jnp.minimum(g, SWIGLU_LIMIT)
            lin = jnp.clip(lin, -SWIGLU_LIMIT, SWIGLU_LIMIT)
            act_s[:, gs] = (g * _sigmoid(SWIGLU_ALPHA * g) * (lin + 1.0)).astype(BF16)
        hw = D_MODEL // 2
        wait_y_free()
        for h in range(2):
            issue(4 + h, prev)
            y = jnp.dot(act_s[...], wd_bf[:, h * hw:(h + 1) * hw], preferred_element_type=F32)
            y = y + bd_ref[e][:, h * hw:(h + 1) * hw]
            prev = y
            for c in range(hw // LANES):
                y_out[pl.ds(h * (hw // LANES) + c, bm, stride=ROW_TILE), :] = y[:, c * LANES:(c + 1) * LANES]

    b0 = 2 * s
    b1 = b0 + 1

    @pl.when(b0 < used)
    def _():
        @pl.when(s == 0)
        def _():
            for c in weight_copies(blk_e_ref[0]):
                c.start()
            y1[...] = jnp.zeros(y1.shape, F32)

            def first(r, c):
                scatter_start(0, 0, y1, ssem.at[0], r, 0)
                gather_start(blk_j0_ref[1], x0, gsem.at[0], r)
                return c

            lax.fori_loop(0, bm, first, 0)
            block_wait(y1, ssem.at[0])

        def y0_free():
            @pl.when(s > 0)
            def _():
                block_wait(y0, ssem.at[0])

        def y1_free():
            block_wait(y1, ssem.at[1])

        switch_weights(b0)
        block_wait(x0, gsem.at[0])
        block(b0, x0, y0, b1, x1, gsem.at[1], b0 - 1, y1, ssem.at[1], 1, y0_free)

        switch_weights(b1)
        block_wait(x1, gsem.at[1])
        block(b1, x1, y1, b0 + 2, x0, gsem.at[0], b0, y0, ssem.at[0], 0, y1_free)

        @pl.when(b0 + 2 >= used)
        def _():
            def last(r, c):
                scatter_start(blk_j0_ref[b1 + 1], blk_nv_ref[b1 + 1], y1, ssem.at[1], r, 1)
                return c

            lax.fori_loop(0, bm, last, 0)
            block_wait(y0, ssem.at[0])
            block_wait(y1, ssem.at[1])
            block_wait(x0, gsem.at[0])


def _expert(order, blk_e, blk_j0, blk_nv, nxt_e, used, h2, w, bm):
    n_tok = h2.shape[0]
    nblk = blk_e.shape[0]
    f32buf = lambda shp: pltpu.VMEM(shp, F32)
    return pl.pallas_call(
        functools.partial(_expert_kernel, bm=bm, n_tok=n_tok),
        grid_spec=pltpu.PrefetchScalarGridSpec(
            num_scalar_prefetch=6,
            grid=(nblk // 2,),
            in_specs=[
                pl.BlockSpec(memory_space=pl.ANY),
                pl.BlockSpec(memory_space=pl.ANY),
                pl.BlockSpec((N_EXPERTS, 1, 2 * D_FF), lambda s, *_: (0, 0, 0)),
                pl.BlockSpec(memory_space=pl.ANY),
                pl.BlockSpec((N_EXPERTS, 1, D_MODEL), lambda s, *_: (0, 0, 0)),
            ],
            out_specs=pl.BlockSpec(memory_space=pl.ANY),
            scratch_shapes=[f32buf((bm * ROW_TILE, LANES)), f32buf((bm * ROW_TILE, LANES)),
                            f32buf((bm * ROW_TILE, LANES)), f32buf((bm * ROW_TILE, LANES)),
                            pltpu.VMEM((bm, D_MODEL), BF16), pltpu.VMEM((bm, D_FF), BF16),
                            f32buf((D_MODEL, 2 * D_FF)), f32buf((D_FF, D_MODEL)),
                            pltpu.VMEM((D_MODEL, 2 * D_FF), BF16), pltpu.VMEM((D_FF, D_MODEL), BF16),
                            pltpu.SemaphoreType.DMA((2,)), pltpu.SemaphoreType.DMA((2,)),
                            pltpu.SemaphoreType.DMA((2,))],
        ),
        out_shape=jax.ShapeDtypeStruct((TOP_K * n_tok + 2 * bm, ROW_TILE, LANES), F32),
        compiler_params=_cparams(("arbitrary",)),
        name="moe_expert",
    )(order, blk_e, blk_j0, blk_nv, nxt_e, used, h2, w["w_gu"], w["b_gu"], w["w_down"], w["b_down"])


def _combine_kernel(y0_ref, y1_ref, y2_ref, y3_ref, x1_ref, gates_ref, gt2_ref, l2g_ref, l2b_ref, o_ref, *, alpha):
    gates = gates_ref[...]
    tm = gates.shape[0]
    f = gates[:, 0:1] * _rows_from_tiles(y0_ref, tm)
    for k, y_ref in enumerate((y1_ref, y2_ref, y3_ref), start=1):
        f = f + gates[:, k:k + 1] * _rows_from_tiles(y_ref, tm)
    o_ref[...] = _layernorm(alpha * x1_ref[...] + gt2_ref[0] * f) * l2g_ref[...] + l2b_ref[...]


def _combine(y_rows, x1, gates, gt2, w, tm, tokens_per_batch, alpha, n_tok, tok0):
    N = x1.shape[0]
    per_b = tokens_per_batch // tm
    nt = N // tm
    assert n_tok % tm == 0 and tok0 % tm == 0
    slot = lambda k: pl.BlockSpec((tm * ROW_TILE, LANES), lambda i: ((k * n_tok + tok0) // tm + i, 0))
    return pl.pallas_call(
        functools.partial(_combine_kernel, alpha=alpha),
        grid=(nt,),
        in_specs=[slot(0), slot(1), slot(2), slot(3),
                  pl.BlockSpec((tm, D_MODEL), lambda i: (i, 0)),
                  pl.BlockSpec((tm, LANES), lambda i: (i, 0)),
                  pl.BlockSpec((1, 1, D_MODEL), lambda i: (i // per_b, 0, 0)),
                  pl.BlockSpec((1, D_MODEL), lambda i: (0, 0)),
                  pl.BlockSpec((1, D_MODEL), lambda i: (0, 0))],
        out_specs=pl.BlockSpec((tm, D_MODEL), lambda i: (i, 0)),
        out_shape=jax.ShapeDtypeStruct((N, D_MODEL), F32),
        compiler_params=_cparams(("arbitrary",)),
        name="moe_combine",
    )(y_rows, y_rows, y_rows, y_rows, x1, gates, gt2, w["ln2_g"], w["ln2_b"])


def _moe_experts(h2, idx, counts, w, bm):
    N = h2.shape[0]
    n_rows = N * TOP_K
    nblk = (n_rows + N_EXPERTS * (bm - 1)) // bm
    nblk += nblk % 2
    experts = jnp.arange(N_EXPERTS, dtype=jnp.int32)
    order = jnp.concatenate([jnp.argsort(idx.reshape(-1)).astype(jnp.int32), jnp.zeros((bm,), jnp.int32)])
    nb_e = (counts + bm - 1) // bm
    blk_end = jnp.cumsum(nb_e)
    first_blk = blk_end - nb_e
    start_sorted = jnp.cumsum(counts) - counts
    used = blk_end[-1].astype(jnp.int32)
    b = jnp.arange(-1, nblk + 1, dtype=jnp.int32)
    bc = jnp.clip(b, 0, used - 1)
    e = jnp.minimum(jnp.sum(blk_end[None, :] <= bc[:, None], axis=1), N_EXPERTS - 1).astype(jnp.int32)
    pick = lambda table: jnp.sum(jnp.where(e[:, None] == experts, table, 0), axis=1)
    local = bc - pick(first_blk)
    blk_j0 = (pick(start_sorted) + local * bm).astype(jnp.int32)
    blk_nv = jnp.where((b >= 0) & (b < used), jnp.minimum(bm, pick(counts) - local * bm), 0).astype(jnp.int32)
    blk_e = e[1:nblk + 1]
    later = (experts[None, :] > experts[:, None]) & (counts[None, :] > 0)
    nxt = jnp.min(jnp.where(later, experts[None, :], N_EXPERTS), axis=1)
    nxt_e = jnp.where(nxt < N_EXPERTS, nxt, -1).astype(jnp.int32)
    y_rows = _expert(order, blk_e, blk_j0, blk_nv, nxt_e, used.reshape(1), h2, w, bm)
    return y_rows.reshape(-1, LANES)


def _rope_tables(pos):
    half = QK_ROPE // 2
    inv = ROPE_THETA ** (-jnp.arange(half, dtype=F32) / half)
    ang = pos.astype(F32)[:, None] * inv[None, :]
    cos, sin = jnp.cos(ang), jnp.sin(ang)
    cos32 = jnp.concatenate([cos, cos], axis=1)
    sin32 = jnp.concatenate([-sin, sin], axis=1)
    return jnp.tile(cos32, (1, LANES // QK_ROPE)), jnp.tile(sin32, (1, LANES // QK_ROPE))


def _swap_halves(w32):
    shp = w32.shape
    w = w32.reshape(shp[:-1] + (shp[-1] // QK_ROPE, 2, QK_ROPE // 2))
    return w[..., ::-1, :].reshape(shp)


def _prep_weights(l, w_in, conv_w, g_qa, w_qb, g_kva, w_kvb, g_out_conv, g_out_attn, w_out,
                  ln1_g, ln1_b, router_w, router_b, w_gu, b_gu, w_down, b_down, ln2_g, ln2_b):
    w = {}
    wi = w_in[l]
    k_r = wi[:, _O_KR:_O_KR + QK_ROPE]
    rep = LANES // QK_ROPE
    w["w_in"] = jnp.concatenate([wi[:, :_O_KR], jnp.tile(k_r, (1, rep)), jnp.tile(_swap_halves(k_r), (1, rep))],
                                axis=1).astype(BF16)
    w["conv_w"] = conv_w[l]
    w["g_qa"] = g_qa[l].reshape(1, Q_LORA)
    w["g_kva"] = g_kva[l].reshape(1, KV_LORA)
    w["g_oc"] = g_out_conv[l].reshape(1, CONV_WIDTH)
    w["g_oa"] = g_out_attn[l].reshape(1, ATTN_WIDTH)
    wq = w_qb[l].reshape(Q_LORA, N_HEADS, QK_NOPE + QK_ROPE)
    w["wq_nope"] = wq[:, :, :QK_NOPE].reshape(Q_LORA, N_HEADS * QK_NOPE).astype(BF16)
    wq_rope = wq[:, :, QK_NOPE:].reshape(Q_LORA, N_HEADS * QK_ROPE)
    w["wq_rope"] = wq_rope.astype(BF16)
    w["wq_rope_sw"] = _swap_halves(wq_rope).astype(BF16)
    w_uk = jnp.transpose(w_kvb[l][:, :, :QK_NOPE], (1, 2, 0))
    w_uv = jnp.transpose(w_kvb[l][:, :, QK_NOPE:], (1, 0, 2))
    zk = jnp.zeros((QK_NOPE, KV_LORA), F32)
    zv = jnp.zeros((KV_LORA, V_HEAD), F32)
    w["w_uk_pair"] = jnp.stack([
        jnp.concatenate([jnp.concatenate([w_uk[2 * p], zk], axis=1),
                         jnp.concatenate([zk, w_uk[2 * p + 1]], axis=1)], axis=0)
        for p in range(N_HEADS // 2)]).astype(BF16)
    w["w_uv_pair"] = jnp.stack([
        jnp.concatenate([jnp.concatenate([w_uv[2 * p], zv], axis=1),
                         jnp.concatenate([zv, w_uv[2 * p + 1]], axis=1)], axis=0)
        for p in range(N_HEADS // 2)]).astype(BF16)
    w["w_out"] = w_out[l].astype(BF16)
    w["ln1_g"] = ln1_g[l].reshape(1, D_MODEL)
    w["ln1_b"] = ln1_b[l].reshape(1, D_MODEL)
    w["ln2_g"] = ln2_g[l].reshape(1, D_MODEL)
    w["ln2_b"] = ln2_b[l].reshape(1, D_MODEL)
    w["router_w"] = jnp.pad(router_w[l], ((0, 0), (0, LANES - N_EXPERTS))).astype(BF16)
    w["router_b"] = jnp.concatenate([router_b[l], jnp.full((LANES - N_EXPERTS,), NEG, F32)]).reshape(1, LANES)
    w["w_gu"] = w_gu[l]
    w["b_gu"] = b_gu[l].reshape(N_EXPERTS, 1, 2 * D_FF)
    w["w_down"] = w_down[l]
    w["b_down"] = b_down[l].reshape(N_EXPERTS, 1, D_MODEL)
    return w


def _mixer(x, mod, conv_prev, past, pos0, h2_all, tok0, w, alpha, *, tm_in, tm_out, tq=128, tk=512):
    B, T, _ = x.shape
    sh1, sc1, gt1, sh2, sc2, gt2 = [mod[:, None, i * D_MODEL:(i + 1) * D_MODEL] for i in range(N_MOD)]
    cos_t, sin_t = _rope_tables(pos0 + jnp.arange(T, dtype=jnp.int32))
    a_n, qlat, qrope, kcat, ckv, krope, conv_new, *vt = _inproj(x, sc1, sh1, conv_prev, cos_t, sin_t, w, tm_in,
                                                                 with_vt=past is None)
    if past is None:
        b_n = _attn_prompt(qlat, qrope, kcat, vt[0], w, tq, tk)
    else:
        b_n = _attn_sample(qlat, qrope, kcat, past[0], past[1], w, tk)
    x1, h2_all, meta, gates, cnt = _outproj(x, a_n, b_n, gt1, sc2, sh2, h2_all, tok0, w, tm_out, alpha)
    N = B * T
    route = dict(x1=x1.reshape(N, D_MODEL), idx=meta.reshape(N, LANES)[:, :TOP_K], gates=gates.reshape(N, LANES),
                 counts=cnt[0, :N_EXPERTS].astype(jnp.int32), gt2=gt2, shape=(B, T))
    return h2_all, route, ckv, krope, conv_new


def kernel(x_prompt, x_sample, c_prompt, c_sample, cache_ckv, cache_krope, state_conv, w_ada, b_ada, w_in, conv_w, g_qa, w_qb, g_kva, w_kvb, g_out_conv, g_out_attn, w_out, ln1_g, ln1_b, router_w, router_b, w_gu, b_gu, w_down, b_down, ln2_g, ln2_b):
    depth = w_ada.shape[0]
    Bp, Tp, _ = x_prompt.shape
    Bs, Ts, _ = x_sample.shape
    past_len = cache_ckv.shape[2]
    assert Ts == CHUNK and past_len % CHUNK == 0 and Tp % 512 == 0
    alpha = (2.0 * depth) ** 0.25
    xp, xs = x_prompt, x_sample
    outs = [[] for _ in range(6)]
    c_all = jnp.concatenate([c_prompt, c_sample, jnp.zeros((16 - Bp - Bs, D_MODEL), F32)], axis=0)
    for l in range(depth):
        w = _prep_weights(l, w_in, conv_w, g_qa, w_qb, g_kva, w_kvb, g_out_conv, g_out_attn, w_out,
                          ln1_g, ln1_b, router_w, router_b, w_gu, b_gu, w_down, b_down, ln2_g, ln2_b)
        mod = _ada(c_all, w_ada[l], b_ada[l])
        n_p, n_s = Bp * Tp, Bs * Ts
        n_tok = n_p + n_s
        h2_all = jnp.zeros((n_tok * ROW_TILE, LANES), F32)
        h2_all, rp, ckv_p, kr_p, cv_p = _mixer(xp, mod[:Bp], jnp.zeros((Bp, CONV_K - 1, CONV_WIDTH), F32), None, 0,
                                               h2_all, 0, w, alpha, tm_in=512, tm_out=256)
        h2_all, rs, ckv_s, kr_s, cv_s = _mixer(xs, mod[Bp:Bp + Bs], state_conv[l], (cache_ckv[l], cache_krope[l]),
                                               past_len, h2_all, n_p, w, alpha, tm_in=Ts, tm_out=Ts)
        y_rows = _moe_experts(h2_all.reshape(n_tok, ROW_TILE, LANES), jnp.concatenate([rp["idx"], rs["idx"]]),
                              rp["counts"] + rs["counts"], w, bm=256)
        xp = _combine(y_rows, rp["x1"], rp["gates"], rp["gt2"], w, 256, Tp, alpha, n_tok, 0).reshape(rp["shape"] + (D_MODEL,))
        xs = _combine(y_rows, rs["x1"], rs["gates"], rs["gt2"], w, Ts, Ts, alpha, n_tok, n_p).reshape(rs["shape"] + (D_MODEL,))
        for o, v in zip(outs, (ckv_p, kr_p, cv_p, ckv_s, kr_s, cv_s)):
            o.append(v)
    return (xp, xs) + tuple(jnp.stack(o) for o in outs)
```

```python
import functools
import math

import jax
import jax.numpy as jnp
from jax import lax
from jax.experimental import pallas as pl
from jax.experimental.pallas import tpu as pltpu

F32 = jnp.float32
BF16 = jnp.bfloat16

D_MODEL = 1024
CHUNK = 64
CONV_WIDTH = 512
CONV_K = 3
N_HEADS = 8
QK_NOPE = 64
QK_ROPE = 32
V_HEAD = 64
Q_LORA = 256
KV_LORA = 128
ATTN_WIDTH = N_HEADS * V_HEAD
ROPE_THETA = 10000.0
ATTN_SCALE = 1.0 / math.sqrt(QK_NOPE + QK_ROPE)
Q_SCALE = ATTN_SCALE * math.log2(math.e)
N_EXPERTS = 32
TOP_K = 4
D_FF = 1024
SWIGLU_LIMIT = 7.0
SWIGLU_ALPHA = 1.702
N_MOD = 6
LN_EPS = 1e-5
RMS_EPS = 1e-6

LANES = 128
ATTN_GROUPS = 1
FFN_PARTS = 6
NEG = -1e30
VMEM_LIMIT = 56 * 1024 * 1024

_O_XB, _O_XC, _O_XV = 0, CONV_WIDTH, 2 * CONV_WIDTH
_O_QA = 3 * CONV_WIDTH
_O_KVA = _O_QA + Q_LORA
_O_KR = _O_KVA + KV_LORA
_O_KRS = _O_KR + LANES
IN_COLS_EXT = _O_KRS + LANES


def _cparams(sem):
    return pltpu.CompilerParams(dimension_semantics=sem, vmem_limit_bytes=VMEM_LIMIT)


def _layernorm(x):
    mu = jnp.mean(x, axis=-1, keepdims=True)
    xc = x - mu
    var = jnp.mean(xc * xc, axis=-1, keepdims=True)
    return xc * lax.rsqrt(var + LN_EPS)


def _rms(x):
    return x * lax.rsqrt(jnp.mean(x * x, axis=-1, keepdims=True) + RMS_EPS)


def _sigmoid(x):
    return 1.0 / (1.0 + jnp.exp(-x))


ROW_TILE = D_MODEL // LANES


def _rows_from_tiles(ref, n):
    return jnp.concatenate([ref[pl.ds(c, n, stride=ROW_TILE), :] for c in range(ROW_TILE)], axis=1)


def _rows_to_tiles(ref, x):
    n = x.shape[0]
    for c in range(ROW_TILE):
        ref[pl.ds(c, n, stride=ROW_TILE), :] = x[:, c * LANES:(c + 1) * LANES]


def _ada_kernel(c_ref, w_ref, b_ref, o_ref):
    c = c_ref[...]
    s = (c * _sigmoid(c)).astype(BF16)
    o_ref[...] = jnp.dot(s, w_ref[...].astype(BF16), preferred_element_type=F32) + b_ref[...]


def _ada(c_all, w_ada, b_ada):
    rows = c_all.shape[0]
    ncol = w_ada.shape[1]
    tn = 1024
    return pl.pallas_call(
        _ada_kernel,
        grid=(ncol // tn,),
        in_specs=[pl.BlockSpec((rows, D_MODEL), lambda j: (0, 0)),
                  pl.BlockSpec((D_MODEL, tn), lambda j: (0, j)),
                  pl.BlockSpec((1, tn), lambda j: (0, j))],
        out_specs=pl.BlockSpec((rows, tn), lambda j: (0, j)),
        out_shape=jax.ShapeDtypeStruct((rows, ncol), F32),
        compiler_params=_cparams(("arbitrary",)),
        name="ada",
    )(c_all, w_ada, b_ada.reshape(1, ncol))


def _inproj_kernel(x_ref, sc_ref, sh_ref, win_ref, cw_ref, cprev_ref, gqa_ref, gkva_ref, goc_ref,
                   wqn_ref, wqr_ref, wqrs_ref, wuk_ref, cos_ref, sin_ref,
                   an_ref, qlat_ref, qrope_ref, kcat_ref, ckv_ref, krope_ref, cnew_ref,
                   *rest, tm, with_vt):
    vt_ref, ubuf = rest if with_vt else (None, rest[0])
    j = pl.program_id(1)
    x = x_ref[0]
    h = _layernorm(x) * (1.0 + sc_ref[0]) + sh_ref[0]
    proj = jnp.dot(h.astype(BF16), win_ref[...], preferred_element_type=F32)
    xb = proj[:, _O_XB:_O_XB + CONV_WIDTH]
    xc = proj[:, _O_XC:_O_XC + CONV_WIDTH]
    xv = proj[:, _O_XV:_O_XV + CONV_WIDTH]
    q_a = proj[:, _O_QA:_O_QA + Q_LORA]
    kv_a = proj[:, _O_KVA:_O_KVA + KV_LORA]
    kr4 = proj[:, _O_KR:_O_KR + LANES]
    kr4s = proj[:, _O_KRS:_O_KRS + LANES]

    u = xc * xv

    @pl.when(j == 0)
    def _():
        ubuf[6:8, :] = cprev_ref[0]

    ubuf[8:8 + tm, :] = u
    conv = (cw_ref[0:1, :] * ubuf[6:6 + tm, :] + cw_ref[1:2, :] * ubuf[7:7 + tm, :]
            + cw_ref[2:3, :] * u)
    ubuf[0:8, :] = ubuf[tm:tm + 8, :]
    cnew_ref[0] = u[tm - (CONV_K - 1):tm, :]
    an_ref[0] = (_rms(xb * conv) * goc_ref[...]).astype(BF16)

    cos = cos_ref[...]
    sin = sin_ref[...]

    ckv = _rms(kv_a) * gkva_ref[...]
    kro4 = kr4 * cos + kr4s * sin
    ckv_ref[0] = ckv
    krope_ref[0] = kro4[:, :QK_ROPE]
    kcat_ref[0] = jnp.concatenate([ckv, kro4], axis=1).astype(BF16)
    if with_vt:
        vt_ref[0] = ckv.T.astype(BF16)

    qn = (_rms(q_a) * gqa_ref[...]).astype(BF16)
    q_nope = jnp.dot(qn, wqn_ref[...], preferred_element_type=F32)
    xr = jnp.dot(qn, wqr_ref[...], preferred_element_type=F32)
    xrs = jnp.dot(qn, wqrs_ref[...], preferred_element_type=F32)
    for g in range(2):
        sl = slice(g * LANES, (g + 1) * LANES)
        qrope_ref[0, :, sl] = ((xr[:, sl] * cos + xrs[:, sl] * sin) * Q_SCALE).astype(BF16)
    for p in range(N_HEADS // 2):
        qp = q_nope[:, p * LANES:(p + 1) * LANES].astype(BF16)
        ql = jnp.dot(qp, wuk_ref[p], preferred_element_type=F32)
        qlat_ref[0, :, p * 2 * KV_LORA:(p + 1) * 2 * KV_LORA] = (ql * Q_SCALE).astype(BF16)


def _inproj(x, sc1, sh1, conv_prev, cos_t, sin_t, w, tm, with_vt):
    B, T, _ = x.shape
    nt = T // tm
    full = lambda shp: pl.BlockSpec(shp, lambda b, j: (0,) * len(shp))
    vt_shape = (jax.ShapeDtypeStruct((B, KV_LORA, T), BF16),) if with_vt else ()
    vt_spec = (pl.BlockSpec((1, KV_LORA, tm), lambda b, j: (b, 0, j)),) if with_vt else ()
    out_shapes = (
        jax.ShapeDtypeStruct((B, T, CONV_WIDTH), BF16),
        jax.ShapeDtypeStruct((B, T, N_HEADS * KV_LORA), BF16),
        jax.ShapeDtypeStruct((B, T, 2 * LANES), BF16),
        jax.ShapeDtypeStruct((B, T, 2 * LANES), BF16),
        jax.ShapeDtypeStruct((B, T, KV_LORA), F32),
        jax.ShapeDtypeStruct((B, T, QK_ROPE), F32),
        jax.ShapeDtypeStruct((B, CONV_K - 1, CONV_WIDTH), F32),
    ) + vt_shape
    tile = lambda c: pl.BlockSpec((1, tm, c), lambda b, j: (b, j, 0))
    return pl.pallas_call(
        functools.partial(_inproj_kernel, tm=tm, with_vt=with_vt),
        grid=(B, nt),
        in_specs=[
            tile(D_MODEL),
            pl.BlockSpec((1, 1, D_MODEL), lambda b, j: (b, 0, 0)),
            pl.BlockSpec((1, 1, D_MODEL), lambda b, j: (b, 0, 0)),
            full((D_MODEL, IN_COLS_EXT)),
            full((CONV_K, CONV_WIDTH)),
            pl.BlockSpec((1, CONV_K - 1, CONV_WIDTH), lambda b, j: (b, 0, 0)),
            full((1, Q_LORA)), full((1, KV_LORA)), full((1, CONV_WIDTH)),
            full((Q_LORA, N_HEADS * QK_NOPE)), full((Q_LORA, 2 * LANES)), full((Q_LORA, 2 * LANES)),
            full((N_HEADS // 2, LANES, 2 * KV_LORA)),
            pl.BlockSpec((tm, LANES), lambda b, j: (j, 0)),
            pl.BlockSpec((tm, LANES), lambda b, j: (j, 0)),
        ],
        out_specs=(tile(CONV_WIDTH), tile(N_HEADS * KV_LORA), tile(2 * LANES), tile(2 * LANES),
                   tile(KV_LORA), tile(QK_ROPE),
                   pl.BlockSpec((1, CONV_K - 1, CONV_WIDTH), lambda b, j: (b, 0, 0))) + vt_spec,
        out_shape=out_shapes,
        scratch_shapes=[pltpu.VMEM((tm + 8, CONV_WIDTH), F32)],
        compiler_params=_cparams(("arbitrary", "arbitrary")),
        name="inproj",
    )(x, sc1, sh1, w["w_in"], w["conv_w"], conv_prev, w["g_qa"], w["g_kva"], w["g_oc"],
      w["wq_nope"], w["wq_rope"], w["wq_rope_sw"], w["w_uk_pair"], cos_t, sin_t)


def _stack_queries(qlat_ref, qrope_ref, qs, tq):
    lane = lax.broadcasted_iota(jnp.int32, (tq, LANES), 1)
    for h in range(N_HEADS):
        g, i = divmod(h, 4)
        rope = qrope_ref[0, :, g * LANES:(g + 1) * LANES]
        keep = (lane >= i * QK_ROPE) & (lane < (i + 1) * QK_ROPE)
        qs[h * tq:(h + 1) * tq, 0:KV_LORA] = qlat_ref[0, :, h * KV_LORA:(h + 1) * KV_LORA]
        qs[h * tq:(h + 1) * tq, KV_LORA:KV_LORA + LANES] = jnp.where(keep, rope, jnp.zeros_like(rope))


def _softmax_step(qs, k, v, m_s, l_s, acc_s, mask=None, groups=1):
    tk = k.shape[0]
    rows = qs.shape[0] // groups
    for g in range(groups):
        r = slice(g * rows, (g + 1) * rows)
        s = lax.dot_general(qs[r, :], k, (((1,), (1,)), ((), ())), preferred_element_type=F32)
        if mask is not None:
            col, limit = mask
            s = jnp.where(col < limit[r], s, NEG)
        m_prev = m_s[r, :]
        m_new = jnp.maximum(m_prev, jnp.max(s, axis=1, keepdims=True))
        alpha = jnp.exp2(m_prev - m_new)
        if tk % LANES == 0:
            p = jnp.exp2(s - jnp.tile(m_new, (1, tk // LANES)))
        else:
            p = jnp.exp2(s - m_new[:, :tk])
        l_s[r, :] = alpha * l_s[r, :] + jnp.sum(p, axis=1, keepdims=True)
        acc_s[r, :] = alpha * acc_s[r, :] + jnp.dot(p.astype(BF16), v, preferred_element_type=F32)
        m_s[r, :] = m_new


def _attn_epilogue(acc_s, l_s, wuv_ref, g_ref, o_ref, tq):
    o = acc_s[...] / l_s[...]
    parts = []
    for p in range(N_HEADS // 2):
        op = jnp.concatenate([o[(2 * p) * tq:(2 * p + 1) * tq], o[(2 * p + 1) * tq:(2 * p + 2) * tq]], axis=1)
        parts.append(jnp.dot(op.astype(BF16), wuv_ref[p], preferred_element_type=F32))
    b = jnp.concatenate(parts, axis=1)
    o_ref[0] = (_rms(b) * g_ref[...]).astype(BF16)


def _attn_prompt_kernel(qlat_ref, qrope_ref, k_ref, vt_ref, wuvt_ref, g_ref, o_ref, qs, m_s, l_s, acc_s, sa, sb,
                        *, tq, tk):
    i = pl.program_id(1)
    M = N_HEADS * tq
    _stack_queries(qlat_ref, qrope_ref, qs, tq)
    m_s[...] = jnp.full(m_s.shape, NEG, F32)
    l_s[...] = jnp.zeros(l_s.shape, F32)
    acc_s[...] = jnp.zeros(acc_s.shape, F32)
    q0 = i * tq
    n_full = (q0 + CHUNK) // tk

    def scores(t, dst):
        k = k_ref[0, pl.ds(pl.multiple_of(t * tk, tk), tk), :]
        dst[...] = lax.dot_general(k, qs[...], (((1,), (1,)), ((), ())), preferred_element_type=F32)

    def update(t, src, limit=None):
        start = pl.multiple_of(t * tk, tk)
        vt = vt_ref[0, :, pl.ds(start, tk)]
        s = src[...]
        if limit is not None:
            kpos = start + lax.broadcasted_iota(jnp.int32, (tk, 1), 0)
            s = jnp.where(kpos < limit, s, NEG)
        m_prev = m_s[...]
        m_new = jnp.maximum(m_prev, jnp.max(s, axis=0, keepdims=True))
        alpha = jnp.exp2(m_prev - m_new)
        p = jnp.exp2(s - m_new)
        l_s[...] = alpha * l_s[...] + jnp.sum(p, axis=0, keepdims=True)
        acc_s[...] = alpha * acc_s[...] + jnp.dot(vt, p.astype(BF16), preferred_element_type=F32)
        m_s[...] = m_new

    scores(0, sa)

    def body(j, carry):
        t = 2 * j
        scores(t + 1, sb)
        update(t, sa)
        scores(t + 2, sa)
        update(t + 1, sb)
        return carry

    lax.fori_loop(0, n_full // 2, body, 0)

    col_t = jnp.bitwise_and(lax.broadcasted_iota(jnp.int32, (1, M), 1), tq - 1)
    limit = q0 + (jnp.right_shift(col_t, CHUNK.bit_length() - 1) + 1) * CHUNK
    odd = n_full % 2

    @pl.when(odd == 0)
    def _():
        update(n_full, sa, limit)

    @pl.when(odd == 1)
    def _():
        scores(n_full, sb)
        update(n_full - 1, sa)
        update(n_full, sb, limit)

    o_t = acc_s[...] / l_s[...]
    parts = []
    for p in range(N_HEADS // 2):
        pair = jnp.concatenate([o_t[:, (2 * p) * tq:(2 * p + 1) * tq], o_t[:, (2 * p + 1) * tq:(2 * p + 2) * tq]],
                               axis=0)
        parts.append(jnp.dot(wuvt_ref[p], pair.astype(BF16), preferred_element_type=F32))
    b = jnp.concatenate(parts, axis=0).T
    o_ref[0] = (_rms(b) * g_ref[...]).astype(BF16)


def _attn_prompt(qlat, qrope, kcat, vt, w, tq, tk):
    B, T, _ = qlat.shape
    M = N_HEADS * tq
    return pl.pallas_call(
        functools.partial(_attn_prompt_kernel, tq=tq, tk=tk),
        grid=(B, T // tq),
        in_specs=[
            pl.BlockSpec((1, tq, N_HEADS * KV_LORA), lambda b, i: (b, i, 0)),
            pl.BlockSpec((1, tq, 2 * LANES), lambda b, i: (b, i, 0)),
            pl.BlockSpec((1, T, 2 * LANES), lambda b, i: (b, 0, 0)),
            pl.BlockSpec((1, KV_LORA, T), lambda b, i: (b, 0, 0)),
            pl.BlockSpec((N_HEADS // 2, LANES, 2 * KV_LORA), lambda b, i: (0, 0, 0)),
            pl.BlockSpec((1, ATTN_WIDTH), lambda b, i: (0, 0)),
        ],
        out_specs=pl.BlockSpec((1, tq, ATTN_WIDTH), lambda b, i: (b, i, 0)),
        out_shape=jax.ShapeDtypeStruct((B, T, ATTN_WIDTH), BF16),
        scratch_shapes=[pltpu.VMEM((M, 2 * LANES), BF16), pltpu.VMEM((1, M), F32),
                        pltpu.VMEM((1, M), F32), pltpu.VMEM((KV_LORA, M), F32),
                        pltpu.VMEM((tk, M), F32), pltpu.VMEM((tk, M), F32)],
        compiler_params=_cparams(("arbitrary", "arbitrary")),
        name="attn_prompt",
    )(qlat, qrope, kcat, vt, jnp.swapaxes(w["w_uv_pair"], 1, 2), w["g_oa"])


def _attn_sample_kernel(qlat_ref, qrope_ref, knew_ref, pckv_ref, pkr_ref, wuv_ref, g_ref, o_ref,
                        qs, m_s, l_s, acc_s, *, tq, tk, n_past):
    _stack_queries(qlat_ref, qrope_ref, qs, tq)
    m_s[...] = jnp.full(m_s.shape, NEG, F32)
    l_s[...] = jnp.zeros(l_s.shape, F32)
    acc_s[...] = jnp.zeros(acc_s.shape, F32)

    def body(t, carry):
        start = pl.multiple_of(t * tk, tk)
        ck = pckv_ref[0, pl.ds(start, tk), :]
        kr = pkr_ref[0, pl.ds(start, tk), :]
        k = jnp.concatenate([ck, kr, kr, kr, kr], axis=1).astype(BF16)
        _softmax_step(qs, k, k[:, :KV_LORA], m_s, l_s, acc_s)
        return carry

    lax.fori_loop(0, n_past // tk, body, 0)
    k = knew_ref[0]
    _softmax_step(qs, k, k[:, :KV_LORA], m_s, l_s, acc_s)
    _attn_epilogue(acc_s, l_s, wuv_ref, g_ref, o_ref, tq)


def _attn_sample(qlat, qrope, kcat, past_ckv, past_krope, w, tk):
    B, T, _ = qlat.shape
    n_past = past_ckv.shape[1]
    M = N_HEADS * T
    per_b = lambda r, c: pl.BlockSpec((1, r, c), lambda b: (b, 0, 0))
    return pl.pallas_call(
        functools.partial(_attn_sample_kernel, tq=T, tk=tk, n_past=n_past),
        grid=(B,),
        in_specs=[per_b(T, N_HEADS * KV_LORA), per_b(T, 2 * LANES), per_b(T, 2 * LANES),
                  per_b(n_past, KV_LORA), per_b(n_past, QK_ROPE),
                  pl.BlockSpec((N_HEADS // 2, 2 * KV_LORA, LANES), lambda b: (0, 0, 0)),
                  pl.BlockSpec((1, ATTN_WIDTH), lambda b: (0, 0))],
        out_specs=per_b(T, ATTN_WIDTH),
        out_shape=jax.ShapeDtypeStruct((B, T, ATTN_WIDTH), BF16),
        scratch_shapes=[pltpu.VMEM((M, 2 * LANES), BF16), pltpu.VMEM((M, LANES), F32),
                        pltpu.VMEM((M, LANES), F32), pltpu.VMEM((M, KV_LORA), F32)],
        compiler_params=_cparams(("arbitrary",)),
        name="attn_sample",
    )(qlat, qrope, kcat, past_ckv, past_krope, w["w_uv_pair"], w["g_oa"])


def _outproj_kernel(x_ref, an_ref, bn_ref, gt1_ref, sc2_ref, sh2_ref, wo_ref, l1g_ref, l1b_ref,
                    rw_ref, rb_ref, h2_all_ref, x1_ref, h2_ref, meta_ref, gates_ref, cnt_ref, run_s,
                    *, tm, alpha):
    del h2_all_ref
    first = (pl.program_id(0) == 0) & (pl.program_id(1) == 0)

    @pl.when(first)
    def _():
        run_s[...] = jnp.zeros(run_s.shape, F32)

    m = (jnp.dot(an_ref[0], wo_ref[0:CONV_WIDTH, :], preferred_element_type=F32)
         + jnp.dot(bn_ref[0], wo_ref[CONV_WIDTH:, :], preferred_element_type=F32))
    x1 = _layernorm(alpha * x_ref[0] + gt1_ref[0] * m) * l1g_ref[...] + l1b_ref[...]
    x1_ref[0] = x1
    h2 = _layernorm(x1) * (1.0 + sc2_ref[0]) + sh2_ref[0]
    _rows_to_tiles(h2_ref, h2)

    logits = jnp.dot(h2.astype(BF16), rw_ref[...], preferred_element_type=F32) + rb_ref[...]
    lane = lax.broadcasted_iota(jnp.int32, (tm, LANES), 1)
    lane_f = lane.astype(F32)
    lg = logits
    vals, sels = [], []
    chosen = jnp.zeros((tm, LANES), F32)
    for _ in range(TOP_K):
        mx = jnp.max(lg, axis=1, keepdims=True)
        idx = jnp.min(jnp.where(lg == mx, lane_f, float(LANES)), axis=1, keepdims=True)
        sel = lane_f == idx
        vals.append(mx)
        sels.append(idx)
        chosen = jnp.where(sel, 1.0, chosen)
        lg = jnp.where(sel, NEG, lg)

    es = [jnp.exp(v - vals[0]) for v in vals]
    denom = es[0] + es[1] + es[2] + es[3]

    run_s[0:1, :] = run_s[0:1, :] + jnp.sum(chosen, axis=0, keepdims=True)
    cnt_ref[...] = jnp.broadcast_to(run_s[0:1, :], cnt_ref.shape)

    meta = jnp.zeros((tm, LANES), jnp.int32)
    gates = jnp.zeros((tm, LANES), F32)
    for k in range(TOP_K):
        meta = jnp.where(lane == k, sels[k].astype(jnp.int32), meta)
        gates = jnp.where(lane == k, es[k] / denom, gates)
    meta_ref[0] = meta
    gates_ref[0] = gates


def _outproj(x, a_n, b_n, gt1, sc2, sh2, h2_all, tok0, w, tm, alpha):
    B, T, _ = x.shape
    blk0 = tok0 // tm
    tile = lambda c: pl.BlockSpec((1, tm, c), lambda b, j: (b, j, 0))
    modv = pl.BlockSpec((1, 1, D_MODEL), lambda b, j: (b, 0, 0))
    full = lambda shp: pl.BlockSpec(shp, lambda b, j: (0,) * len(shp))
    return pl.pallas_call(
        functools.partial(_outproj_kernel, tm=tm, alpha=alpha),
        grid=(B, T // tm),
        in_specs=[tile(D_MODEL), tile(CONV_WIDTH), tile(ATTN_WIDTH), modv, modv, modv,
                  full((CONV_WIDTH + ATTN_WIDTH, D_MODEL)), full((1, D_MODEL)), full((1, D_MODEL)),
                  full((D_MODEL, LANES)), full((1, LANES)), pl.BlockSpec(memory_space=pl.ANY)],
        out_specs=(tile(D_MODEL),
                   pl.BlockSpec((tm * ROW_TILE, LANES), lambda b, j: (blk0 + b * (T // tm) + j, 0)),
                   tile(LANES), tile(LANES), pl.BlockSpec((8, LANES), lambda b, j: (0, 0))),
        out_shape=(jax.ShapeDtypeStruct((B, T, D_MODEL), F32),
                   jax.ShapeDtypeStruct(h2_all.shape, F32),
                   jax.ShapeDtypeStruct((B, T, LANES), jnp.int32), jax.ShapeDtypeStruct((B, T, LANES), F32),
                   jax.ShapeDtypeStruct((8, LANES), F32)),
        scratch_shapes=[pltpu.VMEM((8, LANES), F32)],
        input_output_aliases={11: 1},
        compiler_params=_cparams(("arbitrary", "arbitrary")),
        name="outproj",
    )(x, a_n, b_n, gt1, sc2, sh2, w["w_out"], w["ln1_g"], w["ln1_b"], w["router_w"], w["router_b"], h2_all)


def _expert_kernel(order_ref, blk_e_ref, blk_j0_ref, blk_nv_ref, nxt_e_ref, used_ref,
                   h_ref, wgu_hbm, bgu_ref, wd_hbm, bd_ref, y_ref,
                   x0, x1, y0, y1, xb, act_s, wgu_st, wd_st, wgu_bf, wd_bf, gsem, ssem, wsem, *, bm, n_tok):
    s = pl.program_id(0)
    used = used_ref[0]
    dump0 = TOP_K * n_tok

    def weight_copies(e):
        return (pltpu.make_async_copy(wgu_hbm.at[e], wgu_st, wsem.at[0]),
                pltpu.make_async_copy(wd_hbm.at[e], wd_st, wsem.at[1]))

    def gather_start(j0, xbuf, sem, r, prio=0):
        tok = jnp.right_shift(order_ref[j0 + r], 2)
        pltpu.make_async_copy(h_ref.at[tok], xbuf.at[pl.ds(r * ROW_TILE, ROW_TILE)], sem).start(priority=prio)

    def scatter_start(j0, nv, ybuf, sem, r, odd, prio=0):
        o = order_ref[j0 + r]
        real = jnp.bitwise_and(o, TOP_K - 1) * n_tok + jnp.right_shift(o, 2)
        dump = dump0 + odd * bm + r
        valid = jnp.right_shift(r - nv, 31)
        dst = dump + jnp.bitwise_and(valid, real - dump)
        pltpu.make_async_copy(ybuf.at[pl.ds(r * ROW_TILE, ROW_TILE)], y_ref.at[dst], sem).start(priority=prio)

    def block_wait(buf, sem):
        pltpu.make_async_copy(buf, buf, sem).wait()

    def switch_weights(b):
        prev = blk_e_ref[jnp.maximum(b - 1, 0)]
        e = blk_e_ref[b]

        @pl.when((b == 0) | (e != prev))
        def _():
            for c in weight_copies(e):
                c.wait()
            wgu_bf[...] = wgu_st[...].astype(BF16)
            wd_bf[...] = wd_st[...].astype(BF16)

            @pl.when(nxt_e_ref[e] >= 0)
            def _():
                for c in weight_copies(nxt_e_ref[e]):
                    c.start()

    def block(b, x_in, y_out, g_next, x_next, gsem_next, s_prev, y_prev, ssem_prev, prev_odd, wait_y_free):
        gj0 = blk_j0_ref[g_next + 1]
        sj0 = blk_j0_ref[s_prev + 1]
        snv = blk_nv_ref[s_prev + 1]
        def issue(part, after=None):
            zero = 0
            if after is not None:
                bits = pltpu.bitcast(jnp.abs(after[0:8, 0:LANES]), jnp.int32)
                zero = jnp.minimum(bits[0, 0], 0)
            half = FFN_PARTS // 2
            q = part % half
            for r in range(q * bm // half, (q + 1) * bm // half):
                if part < half:
                    gather_start(gj0 + zero, x_next, gsem_next, r, prio=r % 2)
                else:
                    scatter_start(sj0 + zero, snv, y_prev, ssem_prev, r, prev_odd, prio=r % 2)

        xb[...] = _rows_from_tiles(x_in, bm).astype(BF16)
        e = blk_e_ref[b]
        cw = D_FF // 4
        prev = None
        for c in range(4):
            issue(c, prev)
            gs, ls = slice(c * cw, (c + 1) * cw), slice(D_FF + c * cw, D_FF + (c + 1) * cw)
            g = jnp.dot(xb[...], wgu_bf[:, gs], preferred_element_type=F32) + bgu_ref[e][:, gs]
            lin = jnp.dot(xb[...], wgu_bf[:, ls], preferred_element_type=F32) + bgu_ref[e][:, ls]
            prev = g
            g = jnp.minimum(g, SWIGLU_LIMIT)
            lin = jnp.clip(lin, -SWIGLU_LIMIT, SWIGLU_LIMIT)
            act_s[:, gs] = (g * _sigmoid(SWIGLU_ALPHA * g) * (lin + 1.0)).astype(BF16)
        hw = D_MODEL // 2
        wait_y_free()
        for h in range(2):
            issue(4 + h, prev)
            y = jnp.dot(act_s[...], wd_bf[:, h * hw:(h + 1) * hw], preferred_element_type=F32)
            y = y + bd_ref[e][:, h * hw:(h + 1) * hw]
            prev = y
            for c in range(hw // LANES):
                y_out[pl.ds(h * (hw // LANES) + c, bm, stride=ROW_TILE), :] = y[:, c * LANES:(c + 1) * LANES]

    b0 = 2 * s
    b1 = b0 + 1

    @pl.when(b0 < used)
    def _():
        @pl.when(s == 0)
        def _():
            for c in weight_copies(blk_e_ref[0]):
                c.start()
            y1[...] = jnp.zeros(y1.shape, F32)

            def first(r, c):
                scatter_start(0, 0, y1, ssem.at[0], r, 0)
                gather_start(blk_j0_ref[1], x0, gsem.at[0], r)
                return c

            lax.fori_loop(0, bm, first, 0)
            block_wait(y1, ssem.at[0])

        def y0_free():
            @pl.when(s > 0)
            def _():
                block_wait(y0, ssem.at[0])

        def y1_free():
            block_wait(y1, ssem.at[1])

        switch_weights(b0)
        block_wait(x0, gsem.at[0])
        block(b0, x0, y0, b1, x1, gsem.at[1], b0 - 1, y1, ssem.at[1], 1, y0_free)

        switch_weights(b1)
        block_wait(x1, gsem.at[1])
        block(b1, x1, y1, b0 + 2, x0, gsem.at[0], b0, y0, ssem.at[0], 0, y1_free)

        @pl.when(b0 + 2 >= used)
        def _():
            def last(r, c):
                scatter_start(blk_j0_ref[b1 + 1], blk_nv_ref[b1 + 1], y1, ssem.at[1], r, 1)
                return c

            lax.fori_loop(0, bm, last, 0)
            block_wait(y0, ssem.at[0])
            block_wait(y1, ssem.at[1])
            block_wait(x0, gsem.at[0])


def _expert(order, blk_e, blk_j0, blk_nv, nxt_e, used, h2, w, bm):
    n_tok = h2.shape[0]
    nblk = blk_e.shape[0]
    f32buf = lambda shp: pltpu.VMEM(shp, F32)
    return pl.pallas_call(
        functools.partial(_expert_kernel, bm=bm, n_tok=n_tok),
        grid_spec=pltpu.PrefetchScalarGridSpec(
            num_scalar_prefetch=6,
            grid=(nblk // 2,),
            in_specs=[
                pl.BlockSpec(memory_space=pl.ANY),
                pl.BlockSpec(memory_space=pl.ANY),
                pl.BlockSpec((N_EXPERTS, 1, 2 * D_FF), lambda s, *_: (0, 0, 0)),
                pl.BlockSpec(memory_space=pl.ANY),
                pl.BlockSpec((N_EXPERTS, 1, D_MODEL), lambda s, *_: (0, 0, 0)),
            ],
            out_specs=pl.BlockSpec(memory_space=pl.ANY),
            scratch_shapes=[f32buf((bm * ROW_TILE, LANES)), f32buf((bm * ROW_TILE, LANES)),
                            f32buf((bm * ROW_TILE, LANES)), f32buf((bm * ROW_TILE, LANES)),
                            pltpu.VMEM((bm, D_MODEL), BF16), pltpu.VMEM((bm, D_FF), BF16),
                            f32buf((D_MODEL, 2 * D_FF)), f32buf((D_FF, D_MODEL)),
                            pltpu.VMEM((D_MODEL, 2 * D_FF), BF16), pltpu.VMEM((D_FF, D_MODEL), BF16),
                            pltpu.SemaphoreType.DMA((2,)), pltpu.SemaphoreType.DMA((2,)),
                            pltpu.SemaphoreType.DMA((2,))],
        ),
        out_shape=jax.ShapeDtypeStruct((TOP_K * n_tok + 2 * bm, ROW_TILE, LANES), F32),
        compiler_params=_cparams(("arbitrary",)),
        name="moe_expert",
    )(order, blk_e, blk_j0, blk_nv, nxt_e, used, h2, w["w_gu"], w["b_gu"], w["w_down"], w["b_down"])


def _combine_kernel(y0_ref, y1_ref, y2_ref, y3_ref, x1_ref, gates_ref, gt2_ref, l2g_ref, l2b_ref, o_ref, *, alpha):
    gates = gates_ref[...]
    tm = gates.shape[0]
    f = gates[:, 0:1] * _rows_from_tiles(y0_ref, tm)
    for k, y_ref in enumerate((y1_ref, y2_ref, y3_ref), start=1):
        f = f + gates[:, k:k + 1] * _rows_from_tiles(y_ref, tm)
    o_ref[...] = _layernorm(alpha * x1_ref[...] + gt2_ref[0] * f) * l2g_ref[...] + l2b_ref[...]


def _combine(y_rows, x1, gates, gt2, w, tm, tokens_per_batch, alpha, n_tok, tok0):
    N = x1.shape[0]
    per_b = tokens_per_batch // tm
    nt = N // tm
    assert n_tok % tm == 0 and tok0 % tm == 0
    slot = lambda k: pl.BlockSpec((tm * ROW_TILE, LANES), lambda i: ((k * n_tok + tok0) // tm + i, 0))
    return pl.pallas_call(
        functools.partial(_combine_kernel, alpha=alpha),
        grid=(nt,),
        in_specs=[slot(0), slot(1), slot(2), slot(3),
                  pl.BlockSpec((tm, D_MODEL), lambda i: (i, 0)),
                  pl.BlockSpec((tm, LANES), lambda i: (i, 0)),
                  pl.BlockSpec((1, 1, D_MODEL), lambda i: (i // per_b, 0, 0)),
                  pl.BlockSpec((1, D_MODEL), lambda i: (0, 0)),
                  pl.BlockSpec((1, D_MODEL), lambda i: (0, 0))],
        out_specs=pl.BlockSpec((tm, D_MODEL), lambda i: (i, 0)),
        out_shape=jax.ShapeDtypeStruct((N, D_MODEL), F32),
        compiler_params=_cparams(("arbitrary",)),
        name="moe_combine",
    )(y_rows, y_rows, y_rows, y_rows, x1, gates, gt2, w["ln2_g"], w["ln2_b"])


def _moe_experts(h2, idx, counts, w, bm):
    N = h2.shape[0]
    n_rows = N * TOP_K
    nblk = (n_rows + N_EXPERTS * (bm - 1)) // bm
    nblk += nblk % 2
    experts = jnp.arange(N_EXPERTS, dtype=jnp.int32)
    id_bits = (n_rows - 1).bit_length()
    keys = jnp.left_shift(idx.reshape(-1), id_bits) + jnp.arange(n_rows, dtype=jnp.int32)
    order = jnp.bitwise_and(jnp.sort(keys), (1 << id_bits) - 1)
    order = jnp.concatenate([order, jnp.zeros((bm,), jnp.int32)])
    nb_e = (counts + bm - 1) // bm
    blk_end = jnp.cumsum(nb_e)
    first_blk = blk_end - nb_e
    start_sorted = jnp.cumsum(counts) - counts
    used = blk_end[-1].astype(jnp.int32)
    b = jnp.arange(-1, nblk + 1, dtype=jnp.int32)
    bc = jnp.clip(b, 0, used - 1)
    e = jnp.minimum(jnp.sum(blk_end[None, :] <= bc[:, None], axis=1), N_EXPERTS - 1).astype(jnp.int32)
    pick = lambda table: jnp.sum(jnp.where(e[:, None] == experts, table, 0), axis=1)
    local = bc - pick(first_blk)
    blk_j0 = (pick(start_sorted) + local * bm).astype(jnp.int32)
    blk_nv = jnp.where((b >= 0) & (b < used), jnp.minimum(bm, pick(counts) - local * bm), 0).astype(jnp.int32)
    blk_e = e[1:nblk + 1]
    later = (experts[None, :] > experts[:, None]) & (counts[None, :] > 0)
    nxt = jnp.min(jnp.where(later, experts[None, :], N_EXPERTS), axis=1)
    nxt_e = jnp.where(nxt < N_EXPERTS, nxt, -1).astype(jnp.int32)
    y_rows = _expert(order, blk_e, blk_j0, blk_nv, nxt_e, used.reshape(1), h2, w, bm)
    return y_rows.reshape(-1, LANES)


def _rope_tables(pos):
    half = QK_ROPE // 2
    inv = ROPE_THETA ** (-jnp.arange(half, dtype=F32) / half)
    ang = pos.astype(F32)[:, None] * inv[None, :]
    cos, sin = jnp.cos(ang), jnp.sin(ang)
    cos32 = jnp.concatenate([cos, cos], axis=1)
    sin32 = jnp.concatenate([-sin, sin], axis=1)
    return jnp.tile(cos32, (1, LANES // QK_ROPE)), jnp.tile(sin32, (1, LANES // QK_ROPE))


def _swap_halves(w32):
    shp = w32.shape
    w = w32.reshape(shp[:-1] + (shp[-1] // QK_ROPE, 2, QK_ROPE // 2))
    return w[..., ::-1, :].reshape(shp)


def _prep_weights(l, w_in, conv_w, g_qa, w_qb, g_kva, w_kvb, g_out_conv, g_out_attn, w_out,
                  ln1_g, ln1_b, router_w, router_b, w_gu, b_gu, w_down, b_down, ln2_g, ln2_b):
    w = {}
    wi = w_in[l]
    k_r = wi[:, _O_KR:_O_KR + QK_ROPE]
    rep = LANES // QK_ROPE
    w["w_in"] = jnp.concatenate([wi[:, :_O_KR], jnp.tile(k_r, (1, rep)), jnp.tile(_swap_halves(k_r), (1, rep))],
                                axis=1).astype(BF16)
    w["conv_w"] = conv_w[l]
    w["g_qa"] = g_qa[l].reshape(1, Q_LORA)
    w["g_kva"] = g_kva[l].reshape(1, KV_LORA)
    w["g_oc"] = g_out_conv[l].reshape(1, CONV_WIDTH)
    w["g_oa"] = g_out_attn[l].reshape(1, ATTN_WIDTH)
    wq = w_qb[l].reshape(Q_LORA, N_HEADS, QK_NOPE + QK_ROPE)
    w["wq_nope"] = wq[:, :, :QK_NOPE].reshape(Q_LORA, N_HEADS * QK_NOPE).astype(BF16)
    wq_rope = wq[:, :, QK_NOPE:].reshape(Q_LORA, N_HEADS * QK_ROPE)
    w["wq_rope"] = wq_rope.astype(BF16)
    w["wq_rope_sw"] = _swap_halves(wq_rope).astype(BF16)
    w_uk = jnp.transpose(w_kvb[l][:, :, :QK_NOPE], (1, 2, 0))
    w_uv = jnp.transpose(w_kvb[l][:, :, QK_NOPE:], (1, 0, 2))
    zk = jnp.zeros((QK_NOPE, KV_LORA), F32)
    zv = jnp.zeros((KV_LORA, V_HEAD), F32)
    w["w_uk_pair"] = jnp.stack([
        jnp.concatenate([jnp.concatenate([w_uk[2 * p], zk], axis=1),
                         jnp.concatenate([zk, w_uk[2 * p + 1]], axis=1)], axis=0)
        for p in range(N_HEADS // 2)]).astype(BF16)
    w["w_uv_pair"] = jnp.stack([
        jnp.concatenate([jnp.concatenate([w_uv[2 * p], zv], axis=1),
                         jnp.concatenate([zv, w_uv[2 * p + 1]], axis=1)], axis=0)
        for p in range(N_HEADS // 2)]).astype(BF16)
    w["w_out"] = w_out[l].astype(BF16)
    w["ln1_g"] = ln1_g[l].reshape(1, D_MODEL)
    w["ln1_b"] = ln1_b[l].reshape(1, D_MODEL)
    w["ln2_g"] = ln2_g[l].reshape(1, D_MODEL)
    w["ln2_b"] = ln2_b[l].reshape(1, D_MODEL)
    w["router_w"] = jnp.pad(router_w[l], ((0, 0), (0, LANES - N_EXPERTS))).astype(BF16)
    w["router_b"] = jnp.concatenate([router_b[l], jnp.full((LANES - N_EXPERTS,), NEG, F32)]).reshape(1, LANES)
    w["w_gu"] = w_gu[l]
    w["b_gu"] = b_gu[l].reshape(N_EXPERTS, 1, 2 * D_FF)
    w["w_down"] = w_down[l]
    w["b_down"] = b_down[l].reshape(N_EXPERTS, 1, D_MODEL)
    return w


def _mixer(x, mod, conv_prev, past, pos0, h2_all, tok0, w, alpha, *, tm_in, tm_out, tq=128, tk=512):
    B, T, _ = x.shape
    sh1, sc1, gt1, sh2, sc2, gt2 = [mod[:, None, i * D_MODEL:(i + 1) * D_MODEL] for i in range(N_MOD)]
    cos_t, sin_t = _rope_tables(pos0 + jnp.arange(T, dtype=jnp.int32))
    a_n, qlat, qrope, kcat, ckv, krope, conv_new, *vt = _inproj(x, sc1, sh1, conv_prev, cos_t, sin_t, w, tm_in,
                                                                 with_vt=past is None)
    if past is None:
        b_n = _attn_prompt(qlat, qrope, kcat, vt[0], w, tq, tk)
    else:
        b_n = _attn_sample(qlat, qrope, kcat, past[0], past[1], w, tk)
    x1, h2_all, meta, gates, cnt = _outproj(x, a_n, b_n, gt1, sc2, sh2, h2_all, tok0, w, tm_out, alpha)
    N = B * T
    route = dict(x1=x1.reshape(N, D_MODEL), idx=meta.reshape(N, LANES)[:, :TOP_K], gates=gates.reshape(N, LANES),
                 counts=cnt[0, :N_EXPERTS].astype(jnp.int32), gt2=gt2, shape=(B, T))
    return h2_all, route, ckv, krope, conv_new


def kernel(x_prompt, x_sample, c_prompt, c_sample, cache_ckv, cache_krope, state_conv, w_ada, b_ada, w_in, conv_w, g_qa, w_qb, g_kva, w_kvb, g_out_conv, g_out_attn, w_out, ln1_g, ln1_b, router_w, router_b, w_gu, b_gu, w_down, b_down, ln2_g, ln2_b):
    depth = w_ada.shape[0]
    Bp, Tp, _ = x_prompt.shape
    Bs, Ts, _ = x_sample.shape
    past_len = cache_ckv.shape[2]
    assert Ts == CHUNK and past_len % CHUNK == 0 and Tp % 512 == 0
    alpha = (2.0 * depth) ** 0.25
    xp, xs = x_prompt, x_sample
    outs = [[] for _ in range(6)]
    c_all = jnp.concatenate([c_prompt, c_sample, jnp.zeros((16 - Bp - Bs, D_MODEL), F32)], axis=0)
    for l in range(depth):
        w = _prep_weights(l, w_in, conv_w, g_qa, w_qb, g_kva, w_kvb, g_out_conv, g_out_attn, w_out,
                          ln1_g, ln1_b, router_w, router_b, w_gu, b_gu, w_down, b_down, ln2_g, ln2_b)
        mod = _ada(c_all, w_ada[l], b_ada[l])
        n_p, n_s = Bp * Tp, Bs * Ts
        n_tok = n_p + n_s
        h2_all = jnp.zeros((n_tok * ROW_TILE, LANES), F32)
        h2_all, rp, ckv_p, kr_p, cv_p = _mixer(xp, mod[:Bp], jnp.zeros((Bp, CONV_K - 1, CONV_WIDTH), F32), None, 0,
                                               h2_all, 0, w, alpha, tm_in=512, tm_out=256)
        h2_all, rs, ckv_s, kr_s, cv_s = _mixer(xs, mod[Bp:Bp + Bs], state_conv[l], (cache_ckv[l], cache_krope[l]),
                                               past_len, h2_all, n_p, w, alpha, tm_in=Ts, tm_out=Ts)
        y_rows = _moe_experts(h2_all.reshape(n_tok, ROW_TILE, LANES), jnp.concatenate([rp["idx"], rs["idx"]]),
                              rp["counts"] + rs["counts"], w, bm=256)
        xp = _combine(y_rows, rp["x1"], rp["gates"], rp["gt2"], w, 256, Tp, alpha, n_tok, 0).reshape(rp["shape"] + (D_MODEL,))
        xs = _combine(y_rows, rs["x1"], rs["gates"], rs["gt2"], w, Ts, Ts, alpha, n_tok, n_p).reshape(rs["shape"] + (D_MODEL,))
        for o, v in zip(outs, (ckv_p, kr_p, cv_p, ckv_s, kr_s, cv_s)):
            o.append(v)
    return (xp, xs) + tuple(jnp.stack(o) for o in outs)
```

```python
import functools
import math

import jax
import jax.numpy as jnp
from jax import lax
from jax.experimental import pallas as pl
from jax.experimental.pallas import tpu as pltpu

F32 = jnp.float32
BF16 = jnp.bfloat16

D_MODEL = 1024
CHUNK = 64
CONV_WIDTH = 512
CONV_K = 3
N_HEADS = 8
QK_NOPE = 64
QK_ROPE = 32
V_HEAD = 64
Q_LORA = 256
KV_LORA = 128
ATTN_WIDTH = N_HEADS * V_HEAD
ROPE_THETA = 10000.0
ATTN_SCALE = 1.0 / math.sqrt(QK_NOPE + QK_ROPE)
Q_SCALE = ATTN_SCALE * math.log2(math.e)
N_EXPERTS = 32
TOP_K = 4
D_FF = 1024
SWIGLU_LIMIT = 7.0
SWIGLU_ALPHA = 1.702
N_MOD = 6
LN_EPS = 1e-5
RMS_EPS = 1e-6

LANES = 128
ATTN_GROUPS = 1
FFN_PARTS = 6
NEG = -1e30
VMEM_LIMIT = 56 * 1024 * 1024

_O_XB, _O_XC, _O_XV = 0, CONV_WIDTH, 2 * CONV_WIDTH
_O_QA = 3 * CONV_WIDTH
_O_KVA = _O_QA + Q_LORA
_O_KR = _O_KVA + KV_LORA
_O_KRS = _O_KR + LANES
IN_COLS_EXT = _O_KRS + LANES


def _cparams(sem):
    return pltpu.CompilerParams(dimension_semantics=sem, vmem_limit_bytes=VMEM_LIMIT)


def _layernorm(x):
    mu = jnp.mean(x, axis=-1, keepdims=True)
    xc = x - mu
    var = jnp.mean(xc * xc, axis=-1, keepdims=True)
    return xc * lax.rsqrt(var + LN_EPS)


def _rms(x):
    return x * lax.rsqrt(jnp.mean(x * x, axis=-1, keepdims=True) + RMS_EPS)


def _sigmoid(x):
    return 1.0 / (1.0 + jnp.exp(-x))


ROW_TILE = D_MODEL // LANES


def _rows_from_tiles(ref, n):
    return jnp.concatenate([ref[pl.ds(c, n, stride=ROW_TILE), :] for c in range(ROW_TILE)], axis=1)


def _rows_to_tiles(ref, x):
    n = x.shape[0]
    for c in range(ROW_TILE):
        ref[pl.ds(c, n, stride=ROW_TILE), :] = x[:, c * LANES:(c + 1) * LANES]


def _ada_kernel(c_ref, w_ref, b_ref, o_ref):
    c = c_ref[...]
    s = (c * _sigmoid(c)).astype(BF16)
    o_ref[...] = jnp.dot(s, w_ref[...].astype(BF16), preferred_element_type=F32) + b_ref[...]


def _ada(c_all, w_ada, b_ada):
    rows = c_all.shape[0]
    ncol = w_ada.shape[1]
    tn = 1024
    return pl.pallas_call(
        _ada_kernel,
        grid=(ncol // tn,),
        in_specs=[pl.BlockSpec((rows, D_MODEL), lambda j: (0, 0)),
                  pl.BlockSpec((D_MODEL, tn), lambda j: (0, j)),
                  pl.BlockSpec((1, tn), lambda j: (0, j))],
        out_specs=pl.BlockSpec((rows, tn), lambda j: (0, j)),
        out_shape=jax.ShapeDtypeStruct((rows, ncol), F32),
        compiler_params=_cparams(("arbitrary",)),
        name="ada",
    )(c_all, w_ada, b_ada.reshape(1, ncol))


def _inproj_kernel(x_ref, sc_ref, sh_ref, win_ref, cw_ref, cprev_ref, gqa_ref, gkva_ref, goc_ref,
                   wqn_ref, wqr_ref, wqrs_ref, wuk_ref, cos_ref, sin_ref,
                   an_ref, qlat_ref, qrope_ref, kcat_ref, ckv_ref, krope_ref, cnew_ref,
                   *rest, tm, with_vt):
    vt_ref, ubuf = rest if with_vt else (None, rest[0])
    j = pl.program_id(1)
    x = x_ref[0]
    h = _layernorm(x) * (1.0 + sc_ref[0]) + sh_ref[0]
    proj = jnp.dot(h.astype(BF16), win_ref[...], preferred_element_type=F32)
    xb = proj[:, _O_XB:_O_XB + CONV_WIDTH]
    xc = proj[:, _O_XC:_O_XC + CONV_WIDTH]
    xv = proj[:, _O_XV:_O_XV + CONV_WIDTH]
    q_a = proj[:, _O_QA:_O_QA + Q_LORA]
    kv_a = proj[:, _O_KVA:_O_KVA + KV_LORA]
    kr4 = proj[:, _O_KR:_O_KR + LANES]
    kr4s = proj[:, _O_KRS:_O_KRS + LANES]

    u = xc * xv

    @pl.when(j == 0)
    def _():
        ubuf[6:8, :] = cprev_ref[0]

    ubuf[8:8 + tm, :] = u
    conv = (cw_ref[0:1, :] * ubuf[6:6 + tm, :] + cw_ref[1:2, :] * ubuf[7:7 + tm, :]
            + cw_ref[2:3, :] * u)
    ubuf[0:8, :] = ubuf[tm:tm + 8, :]
    cnew_ref[0] = u[tm - (CONV_K - 1):tm, :]
    an_ref[0] = (_rms(xb * conv) * goc_ref[...]).astype(BF16)

    cos = cos_ref[...]
    sin = sin_ref[...]

    ckv = _rms(kv_a) * gkva_ref[...]
    kro4 = kr4 * cos + kr4s * sin
    ckv_ref[0] = ckv
    krope_ref[0] = kro4[:, :QK_ROPE]
    kcat_ref[0] = jnp.concatenate([ckv, kro4], axis=1).astype(BF16)
    if with_vt:
        vt_ref[0] = ckv.T.astype(BF16)

    qn = (_rms(q_a) * gqa_ref[...]).astype(BF16)
    q_nope = jnp.dot(qn, wqn_ref[...], preferred_element_type=F32)
    xr = jnp.dot(qn, wqr_ref[...], preferred_element_type=F32)
    xrs = jnp.dot(qn, wqrs_ref[...], preferred_element_type=F32)
    for g in range(2):
        sl = slice(g * LANES, (g + 1) * LANES)
        qrope_ref[0, :, sl] = ((xr[:, sl] * cos + xrs[:, sl] * sin) * Q_SCALE).astype(BF16)
    for p in range(N_HEADS // 2):
        qp = q_nope[:, p * LANES:(p + 1) * LANES].astype(BF16)
        ql = jnp.dot(qp, wuk_ref[p], preferred_element_type=F32)
        qlat_ref[0, :, p * 2 * KV_LORA:(p + 1) * 2 * KV_LORA] = (ql * Q_SCALE).astype(BF16)


def _inproj(x, sc1, sh1, conv_prev, cos_t, sin_t, w, tm, with_vt):
    B, T, _ = x.shape
    nt = T // tm
    full = lambda shp: pl.BlockSpec(shp, lambda b, j: (0,) * len(shp))
    vt_shape = (jax.ShapeDtypeStruct((B, KV_LORA, T), BF16),) if with_vt else ()
    vt_spec = (pl.BlockSpec((1, KV_LORA, tm), lambda b, j: (b, 0, j)),) if with_vt else ()
    out_shapes = (
        jax.ShapeDtypeStruct((B, T, CONV_WIDTH), BF16),
        jax.ShapeDtypeStruct((B, T, N_HEADS * KV_LORA), BF16),
        jax.ShapeDtypeStruct((B, T, 2 * LANES), BF16),
        jax.ShapeDtypeStruct((B, T, 2 * LANES), BF16),
        jax.ShapeDtypeStruct((B, T, KV_LORA), F32),
        jax.ShapeDtypeStruct((B, T, QK_ROPE), F32),
        jax.ShapeDtypeStruct((B, CONV_K - 1, CONV_WIDTH), F32),
    ) + vt_shape
    tile = lambda c: pl.BlockSpec((1, tm, c), lambda b, j: (b, j, 0))
    return pl.pallas_call(
        functools.partial(_inproj_kernel, tm=tm, with_vt=with_vt),
        grid=(B, nt),
        in_specs=[
            tile(D_MODEL),
            pl.BlockSpec((1, 1, D_MODEL), lambda b, j: (b, 0, 0)),
            pl.BlockSpec((1, 1, D_MODEL), lambda b, j: (b, 0, 0)),
            full((D_MODEL, IN_COLS_EXT)),
            full((CONV_K, CONV_WIDTH)),
            pl.BlockSpec((1, CONV_K - 1, CONV_WIDTH), lambda b, j: (b, 0, 0)),
            full((1, Q_LORA)), full((1, KV_LORA)), full((1, CONV_WIDTH)),
            full((Q_LORA, N_HEADS * QK_NOPE)), full((Q_LORA, 2 * LANES)), full((Q_LORA, 2 * LANES)),
            full((N_HEADS // 2, LANES, 2 * KV_LORA)),
            pl.BlockSpec((tm, LANES), lambda b, j: (j, 0)),
            pl.BlockSpec((tm, LANES), lambda b, j: (j, 0)),
        ],
        out_specs=(tile(CONV_WIDTH), tile(N_HEADS * KV_LORA), tile(2 * LANES), tile(2 * LANES),
                   tile(KV_LORA), tile(QK_ROPE),
                   pl.BlockSpec((1, CONV_K - 1, CONV_WIDTH), lambda b, j: (b, 0, 0))) + vt_spec,
        out_shape=out_shapes,
        scratch_shapes=[pltpu.VMEM((tm + 8, CONV_WIDTH), F32)],
        compiler_params=_cparams(("arbitrary", "arbitrary")),
        name="inproj",
    )(x, sc1, sh1, w["w_in"], w["conv_w"], conv_prev, w["g_qa"], w["g_kva"], w["g_oc"],
      w["wq_nope"], w["wq_rope"], w["wq_rope_sw"], w["w_uk_pair"], cos_t, sin_t)


def _stack_queries(qlat_ref, qrope_ref, qs, tq):
    lane = lax.broadcasted_iota(jnp.int32, (tq, LANES), 1)
    for h in range(N_HEADS):
        g, i = divmod(h, 4)
        rope = qrope_ref[0, :, g * LANES:(g + 1) * LANES]
        keep = (lane >= i * QK_ROPE) & (lane < (i + 1) * QK_ROPE)
        qs[h * tq:(h + 1) * tq, 0:KV_LORA] = qlat_ref[0, :, h * KV_LORA:(h + 1) * KV_LORA]
        qs[h * tq:(h + 1) * tq, KV_LORA:KV_LORA + LANES] = jnp.where(keep, rope, jnp.zeros_like(rope))


def _softmax_step(qs, k, v, m_s, l_s, acc_s, mask=None, groups=1):
    tk = k.shape[0]
    rows = qs.shape[0] // groups
    for g in range(groups):
        r = slice(g * rows, (g + 1) * rows)
        s = lax.dot_general(qs[r, :], k, (((1,), (1,)), ((), ())), preferred_element_type=F32)
        if mask is not None:
            col, limit = mask
            s = jnp.where(col < limit[r], s, NEG)
        m_prev = m_s[r, :]
        m_new = jnp.maximum(m_prev, jnp.max(s, axis=1, keepdims=True))
        alpha = jnp.exp2(m_prev - m_new)
        if tk % LANES == 0:
            p = jnp.exp2(s - jnp.tile(m_new, (1, tk // LANES)))
        else:
            p = jnp.exp2(s - m_new[:, :tk])
        l_s[r, :] = alpha * l_s[r, :] + jnp.sum(p, axis=1, keepdims=True)
        acc_s[r, :] = alpha * acc_s[r, :] + jnp.dot(p.astype(BF16), v, preferred_element_type=F32)
        m_s[r, :] = m_new


def _attn_epilogue(acc_s, l_s, wuv_ref, g_ref, o_ref, tq):
    o = acc_s[...] / l_s[...]
    parts = []
    for p in range(N_HEADS // 2):
        op = jnp.concatenate([o[(2 * p) * tq:(2 * p + 1) * tq], o[(2 * p + 1) * tq:(2 * p + 2) * tq]], axis=1)
        parts.append(jnp.dot(op.astype(BF16), wuv_ref[p], preferred_element_type=F32))
    b = jnp.concatenate(parts, axis=1)
    o_ref[0] = (_rms(b) * g_ref[...]).astype(BF16)


def _attn_prompt_kernel(qlat_ref, qrope_ref, k_ref, vt_ref, wuvt_ref, g_ref, o_ref, qs, m_s, l_s, acc_s, sa, sb,
                        *, tq, tk):
    i = pl.program_id(1)
    M = N_HEADS * tq
    _stack_queries(qlat_ref, qrope_ref, qs, tq)
    m_s[...] = jnp.full(m_s.shape, NEG, F32)
    l_s[...] = jnp.zeros(l_s.shape, F32)
    acc_s[...] = jnp.zeros(acc_s.shape, F32)
    q0 = i * tq
    n_full = (q0 + CHUNK) // tk

    def scores(t, dst):
        k = k_ref[0, pl.ds(pl.multiple_of(t * tk, tk), tk), :]
        dst[...] = lax.dot_general(k, qs[...], (((1,), (1,)), ((), ())), preferred_element_type=F32)

    def update(t, src, limit=None):
        start = pl.multiple_of(t * tk, tk)
        vt = vt_ref[0, :, pl.ds(start, tk)]
        s = src[...]
        if limit is not None:
            kpos = start + lax.broadcasted_iota(jnp.int32, (tk, 1), 0)
            s = jnp.where(kpos < limit, s, NEG)
        m_prev = m_s[...]
        m_new = jnp.maximum(m_prev, jnp.max(s, axis=0, keepdims=True))
        alpha = jnp.exp2(m_prev - m_new)
        p = jnp.exp2(s - m_new)
        l_s[...] = alpha * l_s[...] + jnp.sum(p, axis=0, keepdims=True)
        acc_s[...] = alpha * acc_s[...] + jnp.dot(vt, p.astype(BF16), preferred_element_type=F32)
        m_s[...] = m_new

    scores(0, sa)

    def body(j, carry):
        t = 2 * j
        scores(t + 1, sb)
        update(t, sa)
        scores(t + 2, sa)
        update(t + 1, sb)
        return carry

    lax.fori_loop(0, n_full // 2, body, 0)

    col_t = jnp.bitwise_and(lax.broadcasted_iota(jnp.int32, (1, M), 1), tq - 1)
    limit = q0 + (jnp.right_shift(col_t, CHUNK.bit_length() - 1) + 1) * CHUNK
    odd = n_full % 2

    @pl.when(odd == 0)
    def _():
        update(n_full, sa, limit)

    @pl.when(odd == 1)
    def _():
        scores(n_full, sb)
        update(n_full - 1, sa)
        update(n_full, sb, limit)

    o_t = acc_s[...] / l_s[...]
    parts = []
    for p in range(N_HEADS // 2):
        pair = jnp.concatenate([o_t[:, (2 * p) * tq:(2 * p + 1) * tq], o_t[:, (2 * p + 1) * tq:(2 * p + 2) * tq]],
                               axis=0)
        parts.append(jnp.dot(wuvt_ref[p], pair.astype(BF16), preferred_element_type=F32))
    b = jnp.concatenate(parts, axis=0).T
    o_ref[0] = (_rms(b) * g_ref[...]).astype(BF16)


def _attn_prompt(qlat, qrope, kcat, vt, w, tq, tk):
    B, T, _ = qlat.shape
    M = N_HEADS * tq
    return pl.pallas_call(
        functools.partial(_attn_prompt_kernel, tq=tq, tk=tk),
        grid=(B, T // tq),
        in_specs=[
            pl.BlockSpec((1, tq, N_HEADS * KV_LORA), lambda b, i: (b, i, 0)),
            pl.BlockSpec((1, tq, 2 * LANES), lambda b, i: (b, i, 0)),
            pl.BlockSpec((1, T, 2 * LANES), lambda b, i: (b, 0, 0)),
            pl.BlockSpec((1, KV_LORA, T), lambda b, i: (b, 0, 0)),
            pl.BlockSpec((N_HEADS // 2, LANES, 2 * KV_LORA), lambda b, i: (0, 0, 0)),
            pl.BlockSpec((1, ATTN_WIDTH), lambda b, i: (0, 0)),
        ],
        out_specs=pl.BlockSpec((1, tq, ATTN_WIDTH), lambda b, i: (b, i, 0)),
        out_shape=jax.ShapeDtypeStruct((B, T, ATTN_WIDTH), BF16),
        scratch_shapes=[pltpu.VMEM((M, 2 * LANES), BF16), pltpu.VMEM((1, M), F32),
                        pltpu.VMEM((1, M), F32), pltpu.VMEM((KV_LORA, M), F32),
                        pltpu.VMEM((tk, M), F32), pltpu.VMEM((tk, M), F32)],
        compiler_params=_cparams(("arbitrary", "arbitrary")),
        name="attn_prompt",
    )(qlat, qrope, kcat, vt, jnp.swapaxes(w["w_uv_pair"], 1, 2), w["g_oa"])


def _attn_sample_kernel(qlat_ref, qrope_ref, knew_ref, pckv_ref, pkr_ref, wuv_ref, g_ref, o_ref,
                        qs, m_s, l_s, acc_s, *, tq, tk, n_past):
    _stack_queries(qlat_ref, qrope_ref, qs, tq)
    m_s[...] = jnp.full(m_s.shape, NEG, F32)
    l_s[...] = jnp.zeros(l_s.shape, F32)
    acc_s[...] = jnp.zeros(acc_s.shape, F32)

    def body(t, carry):
        start = pl.multiple_of(t * tk, tk)
        ck = pckv_ref[0, pl.ds(start, tk), :]
        kr = pkr_ref[0, pl.ds(start, tk), :]
        k = jnp.concatenate([ck, kr, kr, kr, kr], axis=1).astype(BF16)
        _softmax_step(qs, k, k[:, :KV_LORA], m_s, l_s, acc_s)
        return carry

    lax.fori_loop(0, n_past // tk, body, 0)
    k = knew_ref[0]
    _softmax_step(qs, k, k[:, :KV_LORA], m_s, l_s, acc_s)
    _attn_epilogue(acc_s, l_s, wuv_ref, g_ref, o_ref, tq)


def _attn_sample(qlat, qrope, kcat, past_ckv, past_krope, w, tk):
    B, T, _ = qlat.shape
    n_past = past_ckv.shape[1]
    M = N_HEADS * T
    per_b = lambda r, c: pl.BlockSpec((1, r, c), lambda b: (b, 0, 0))
    return pl.pallas_call(
        functools.partial(_attn_sample_kernel, tq=T, tk=tk, n_past=n_past),
        grid=(B,),
        in_specs=[per_b(T, N_HEADS * KV_LORA), per_b(T, 2 * LANES), per_b(T, 2 * LANES),
                  per_b(n_past, KV_LORA), per_b(n_past, QK_ROPE),
                  pl.BlockSpec((N_HEADS // 2, 2 * KV_LORA, LANES), lambda b: (0, 0, 0)),
                  pl.BlockSpec((1, ATTN_WIDTH), lambda b: (0, 0))],
        out_specs=per_b(T, ATTN_WIDTH),
        out_shape=jax.ShapeDtypeStruct((B, T, ATTN_WIDTH), BF16),
        scratch_shapes=[pltpu.VMEM((M, 2 * LANES), BF16), pltpu.VMEM((M, LANES), F32),
                        pltpu.VMEM((M, LANES), F32), pltpu.VMEM((M, KV_LORA), F32)],
        compiler_params=_cparams(("arbitrary",)),
        name="attn_sample",
    )(qlat, qrope, kcat, past_ckv, past_krope, w["w_uv_pair"], w["g_oa"])


def _outproj_kernel(x_ref, an_ref, bn_ref, gt1_ref, sc2_ref, sh2_ref, wo_ref, l1g_ref, l1b_ref,
                    rw_ref, rb_ref, h2_all_ref, x1_ref, h2_ref, meta_ref, gates_ref, cnt_ref, run_s,
                    *, tm, alpha):
    del h2_all_ref
    first = (pl.program_id(0) == 0) & (pl.program_id(1) == 0)

    @pl.when(first)
    def _():
        run_s[...] = jnp.zeros(run_s.shape, F32)

    m = (jnp.dot(an_ref[0], wo_ref[0:CONV_WIDTH, :], preferred_element_type=F32)
         + jnp.dot(bn_ref[0], wo_ref[CONV_WIDTH:, :], preferred_element_type=F32))
    x1 = _layernorm(alpha * x_ref[0] + gt1_ref[0] * m) * l1g_ref[...] + l1b_ref[...]
    x1_ref[0] = x1
    h2 = _layernorm(x1) * (1.0 + sc2_ref[0]) + sh2_ref[0]
    _rows_to_tiles(h2_ref, h2)

    logits = jnp.dot(h2.astype(BF16), rw_ref[...], preferred_element_type=F32) + rb_ref[...]
    lane = lax.broadcasted_iota(jnp.int32, (tm, LANES), 1)
    lane_f = lane.astype(F32)
    lg = logits
    vals, sels = [], []
    chosen = jnp.zeros((tm, LANES), F32)
    for _ in range(TOP_K):
        mx = jnp.max(lg, axis=1, keepdims=True)
        idx = jnp.min(jnp.where(lg == mx, lane_f, float(LANES)), axis=1, keepdims=True)
        sel = lane_f == idx
        vals.append(mx)
        sels.append(idx)
        chosen = jnp.where(sel, 1.0, chosen)
        lg = jnp.where(sel, NEG, lg)

    es = [jnp.exp(v - vals[0]) for v in vals]
    denom = es[0] + es[1] + es[2] + es[3]

    run_s[0:1, :] = run_s[0:1, :] + jnp.sum(chosen, axis=0, keepdims=True)
    cnt_ref[...] = jnp.broadcast_to(run_s[0:1, :], cnt_ref.shape)

    meta = jnp.zeros((tm, LANES), jnp.int32)
    gates = jnp.zeros((tm, LANES), F32)
    for k in range(TOP_K):
        meta = jnp.where(lane == k, sels[k].astype(jnp.int32), meta)
        gates = jnp.where(lane == k, es[k] / denom, gates)
    meta_ref[0] = meta
    gates_ref[0] = gates


def _outproj(x, a_n, b_n, gt1, sc2, sh2, h2_all, tok0, w, tm, alpha):
    B, T, _ = x.shape
    blk0 = tok0 // tm
    tile = lambda c: pl.BlockSpec((1, tm, c), lambda b, j: (b, j, 0))
    modv = pl.BlockSpec((1, 1, D_MODEL), lambda b, j: (b, 0, 0))
    full = lambda shp: pl.BlockSpec(shp, lambda b, j: (0,) * len(shp))
    return pl.pallas_call(
        functools.partial(_outproj_kernel, tm=tm, alpha=alpha),
        grid=(B, T // tm),
        in_specs=[tile(D_MODEL), tile(CONV_WIDTH), tile(ATTN_WIDTH), modv, modv, modv,
                  full((CONV_WIDTH + ATTN_WIDTH, D_MODEL)), full((1, D_MODEL)), full((1, D_MODEL)),
                  full((D_MODEL, LANES)), full((1, LANES)), pl.BlockSpec(memory_space=pl.ANY)],
        out_specs=(tile(D_MODEL),
                   pl.BlockSpec((tm * ROW_TILE, LANES), lambda b, j: (blk0 + b * (T // tm) + j, 0)),
                   tile(LANES), tile(LANES), pl.BlockSpec((8, LANES), lambda b, j: (0, 0))),
        out_shape=(jax.ShapeDtypeStruct((B, T, D_MODEL), F32),
                   jax.ShapeDtypeStruct(h2_all.shape, F32),
                   jax.ShapeDtypeStruct((B, T, LANES), jnp.int32), jax.ShapeDtypeStruct((B, T, LANES), F32),
                   jax.ShapeDtypeStruct((8, LANES), F32)),
        scratch_shapes=[pltpu.VMEM((8, LANES), F32)],
        input_output_aliases={11: 1},
        compiler_params=_cparams(("arbitrary", "arbitrary")),
        name="outproj",
    )(x, a_n, b_n, gt1, sc2, sh2, w["w_out"], w["ln1_g"], w["ln1_b"], w["router_w"], w["router_b"], h2_all)


def _expert_kernel(order_ref, blk_e_ref, blk_j0_ref, blk_nv_ref, nxt_e_ref, used_ref,
                   h_ref, wgu_hbm, bgu_ref, wd_hbm, bd_ref, y_ref,
                   x0, x1, y0, y1, xb, act_s, wgu_st, wd_st, wgu_bf, wd_bf, gsem, ssem, wsem, *, bm, n_tok):
    s = pl.program_id(0)
    used = used_ref[0]
    dump0 = TOP_K * n_tok

    def weight_copies(e):
        return (pltpu.make_async_copy(wgu_hbm.at[e], wgu_st, wsem.at[0]),
                pltpu.make_async_copy(wd_hbm.at[e], wd_st, wsem.at[1]))

    tok_bits = (n_tok - 1).bit_length()

    def gather_start(j0, xbuf, sem, r, prio=0):
        tok = jnp.bitwise_and(order_ref[j0 + r], (1 << tok_bits) - 1)
        pltpu.make_async_copy(h_ref.at[tok], xbuf.at[pl.ds(r * ROW_TILE, ROW_TILE)], sem).start(priority=prio)

    def scatter_start(j0, nv, ybuf, sem, r, odd, prio=0):
        real = lax.shift_right_logical(order_ref[j0 + r], tok_bits)
        dump = dump0 + odd * bm + r
        valid = jnp.right_shift(r - nv, 31)
        dst = dump + jnp.bitwise_and(valid, real - dump)
        pltpu.make_async_copy(ybuf.at[pl.ds(r * ROW_TILE, ROW_TILE)], y_ref.at[dst], sem).start(priority=prio)

    def block_wait(buf, sem):
        pltpu.make_async_copy(buf, buf, sem).wait()

    def switch_weights(b):
        prev = blk_e_ref[jnp.maximum(b - 1, 0)]
        e = blk_e_ref[b]

        @pl.when((b == 0) | (e != prev))
        def _():
            for c in weight_copies(e):
                c.wait()
            wgu_bf[...] = wgu_st[...].astype(BF16)
            wd_bf[...] = wd_st[...].astype(BF16)

            @pl.when(nxt_e_ref[e] >= 0)
            def _():
                for c in weight_copies(nxt_e_ref[e]):
                    c.start()

    def block(b, x_in, y_out, g_next, x_next, gsem_next, s_prev, y_prev, ssem_prev, prev_odd, wait_y_free):
        gj0 = blk_j0_ref[g_next + 1]
        sj0 = blk_j0_ref[s_prev + 1]
        snv = blk_nv_ref[s_prev + 1]
        def issue(part, after=None):
            zero = 0
            if after is not None:
                bits = pltpu.bitcast(jnp.abs(after[0:8, 0:LANES]), jnp.int32)
                zero = jnp.minimum(bits[0, 0], 0)
            half = FFN_PARTS // 2
            q = part % half
            for r in range(q * bm // half, (q + 1) * bm // half):
                if part < half:
                    gather_start(gj0 + zero, x_next, gsem_next, r, prio=r % 2)
                else:
                    scatter_start(sj0 + zero, snv, y_prev, ssem_prev, r, prev_odd, prio=r % 2)

        xb[...] = _rows_from_tiles(x_in, bm).astype(BF16)
        e = blk_e_ref[b]
        cw = D_FF // 4
        prev = None
        for c in range(4):
            issue(c, prev)
            gs, ls = slice(c * cw, (c + 1) * cw), slice(D_FF + c * cw, D_FF + (c + 1) * cw)
            g = jnp.dot(xb[...], wgu_bf[:, gs], preferred_element_type=F32) + bgu_ref[e][:, gs]
            lin = jnp.dot(xb[...], wgu_bf[:, ls], preferred_element_type=F32) + bgu_ref[e][:, ls]
            prev = g
            g = jnp.minimum(g, SWIGLU_LIMIT)
            lin = jnp.clip(lin, -SWIGLU_LIMIT, SWIGLU_LIMIT)
            act_s[:, gs] = (g * _sigmoid(SWIGLU_ALPHA * g) * (lin + 1.0)).astype(BF16)
        hw = D_MODEL // 2
        wait_y_free()
        for h in range(2):
            issue(4 + h, prev)
            y = jnp.dot(act_s[...], wd_bf[:, h * hw:(h + 1) * hw], preferred_element_type=F32)
            y = y + bd_ref[e][:, h * hw:(h + 1) * hw]
            prev = y
            for c in range(hw // LANES):
                y_out[pl.ds(h * (hw // LANES) + c, bm, stride=ROW_TILE), :] = y[:, c * LANES:(c + 1) * LANES]

    b0 = 2 * s
    b1 = b0 + 1

    @pl.when(b0 < used)
    def _():
        @pl.when(s == 0)
        def _():
            for c in weight_copies(blk_e_ref[0]):
                c.start()
            y1[...] = jnp.zeros(y1.shape, F32)

            def first(r, c):
                scatter_start(0, 0, y1, ssem.at[0], r, 0)
                gather_start(blk_j0_ref[1], x0, gsem.at[0], r)
                return c

            lax.fori_loop(0, bm, first, 0)
            block_wait(y1, ssem.at[0])

        def y0_free():
            @pl.when(s > 0)
            def _():
                block_wait(y0, ssem.at[0])

        def y1_free():
            block_wait(y1, ssem.at[1])

        switch_weights(b0)
        block_wait(x0, gsem.at[0])
        block(b0, x0, y0, b1, x1, gsem.at[1], b0 - 1, y1, ssem.at[1], 1, y0_free)

        switch_weights(b1)
        block_wait(x1, gsem.at[1])
        block(b1, x1, y1, b0 + 2, x0, gsem.at[0], b0, y0, ssem.at[0], 0, y1_free)

        @pl.when(b0 + 2 >= used)
        def _():
            def last(r, c):
                scatter_start(blk_j0_ref[b1 + 1], blk_nv_ref[b1 + 1], y1, ssem.at[1], r, 1)
                return c

            lax.fori_loop(0, bm, last, 0)
            block_wait(y0, ssem.at[0])
            block_wait(y1, ssem.at[1])
            block_wait(x0, gsem.at[0])


def _expert(order, blk_e, blk_j0, blk_nv, nxt_e, used, h2, w, bm):
    n_tok = h2.shape[0]
    nblk = blk_e.shape[0]
    f32buf = lambda shp: pltpu.VMEM(shp, F32)
    return pl.pallas_call(
        functools.partial(_expert_kernel, bm=bm, n_tok=n_tok),
        grid_spec=pltpu.PrefetchScalarGridSpec(
            num_scalar_prefetch=6,
            grid=(nblk // 2,),
            in_specs=[
                pl.BlockSpec(memory_space=pl.ANY),
                pl.BlockSpec(memory_space=pl.ANY),
                pl.BlockSpec((N_EXPERTS, 1, 2 * D_FF), lambda s, *_: (0, 0, 0)),
                pl.BlockSpec(memory_space=pl.ANY),
                pl.BlockSpec((N_EXPERTS, 1, D_MODEL), lambda s, *_: (0, 0, 0)),
            ],
            out_specs=pl.BlockSpec(memory_space=pl.ANY),
            scratch_shapes=[f32buf((bm * ROW_TILE, LANES)), f32buf((bm * ROW_TILE, LANES)),
                            f32buf((bm * ROW_TILE, LANES)), f32buf((bm * ROW_TILE, LANES)),
                            pltpu.VMEM((bm, D_MODEL), BF16), pltpu.VMEM((bm, D_FF), BF16),
                            f32buf((D_MODEL, 2 * D_FF)), f32buf((D_FF, D_MODEL)),
                            pltpu.VMEM((D_MODEL, 2 * D_FF), BF16), pltpu.VMEM((D_FF, D_MODEL), BF16),
                            pltpu.SemaphoreType.DMA((2,)), pltpu.SemaphoreType.DMA((2,)),
                            pltpu.SemaphoreType.DMA((2,))],
        ),
        out_shape=jax.ShapeDtypeStruct((TOP_K * n_tok + 2 * bm, ROW_TILE, LANES), F32),
        compiler_params=_cparams(("arbitrary",)),
        name="moe_expert",
    )(order, blk_e, blk_j0, blk_nv, nxt_e, used, h2, w["w_gu"], w["b_gu"], w["w_down"], w["b_down"])


def _combine_kernel(y0_ref, y1_ref, y2_ref, y3_ref, x1_ref, gates_ref, gt2_ref, l2g_ref, l2b_ref, o_ref, *, alpha):
    gates = gates_ref[...]
    tm = gates.shape[0]
    f = gates[:, 0:1] * _rows_from_tiles(y0_ref, tm)
    for k, y_ref in enumerate((y1_ref, y2_ref, y3_ref), start=1):
        f = f + gates[:, k:k + 1] * _rows_from_tiles(y_ref, tm)
    o_ref[...] = _layernorm(alpha * x1_ref[...] + gt2_ref[0] * f) * l2g_ref[...] + l2b_ref[...]


def _combine(y_rows, x1, gates, gt2, w, tm, tokens_per_batch, alpha, n_tok, tok0):
    N = x1.shape[0]
    per_b = tokens_per_batch // tm
    nt = N // tm
    assert n_tok % tm == 0 and tok0 % tm == 0
    slot = lambda k: pl.BlockSpec((tm * ROW_TILE, LANES), lambda i: ((k * n_tok + tok0) // tm + i, 0))
    return pl.pallas_call(
        functools.partial(_combine_kernel, alpha=alpha),
        grid=(nt,),
        in_specs=[slot(0), slot(1), slot(2), slot(3),
                  pl.BlockSpec((tm, D_MODEL), lambda i: (i, 0)),
                  pl.BlockSpec((tm, LANES), lambda i: (i, 0)),
                  pl.BlockSpec((1, 1, D_MODEL), lambda i: (i // per_b, 0, 0)),
                  pl.BlockSpec((1, D_MODEL), lambda i: (0, 0)),
                  pl.BlockSpec((1, D_MODEL), lambda i: (0, 0))],
        out_specs=pl.BlockSpec((tm, D_MODEL), lambda i: (i, 0)),
        out_shape=jax.ShapeDtypeStruct((N, D_MODEL), F32),
        compiler_params=_cparams(("arbitrary",)),
        name="moe_combine",
    )(y_rows, y_rows, y_rows, y_rows, x1, gates, gt2, w["ln2_g"], w["ln2_b"])


def _moe_experts(h2, idx, counts, w, bm):
    N = h2.shape[0]
    n_rows = N * TOP_K
    nblk = (n_rows + N_EXPERTS * (bm - 1)) // bm
    nblk += nblk % 2
    experts = jnp.arange(N_EXPERTS, dtype=jnp.int32)
    id_bits = (n_rows - 1).bit_length()
    tok_bits = (N - 1).bit_length()
    assert tok_bits + (TOP_K * N - 1).bit_length() <= 32
    keys = jnp.left_shift(idx.reshape(-1), id_bits) + jnp.arange(n_rows, dtype=jnp.int32)
    keys = jnp.concatenate([keys, jnp.full(((1 << id_bits) - n_rows,), jnp.iinfo(jnp.int32).max, jnp.int32)])
    flat = jnp.bitwise_and(jnp.sort(keys)[:n_rows], (1 << id_bits) - 1).astype(jnp.uint32)
    tok = jnp.right_shift(flat, 2)
    row = jnp.bitwise_and(flat, TOP_K - 1) * N + tok
    order = lax.bitcast_convert_type(jnp.left_shift(row, tok_bits) | tok, jnp.int32)
    order = jnp.concatenate([order, jnp.zeros((bm,), jnp.int32)])
    nb_e = (counts + bm - 1) // bm
    blk_end = jnp.cumsum(nb_e)
    first_blk = blk_end - nb_e
    start_sorted = jnp.cumsum(counts) - counts
    used = blk_end[-1].astype(jnp.int32)
    b = jnp.arange(-1, nblk + 1, dtype=jnp.int32)
    bc = jnp.clip(b, 0, used - 1)
    e = jnp.minimum(jnp.sum(blk_end[None, :] <= bc[:, None], axis=1), N_EXPERTS - 1).astype(jnp.int32)
    pick = lambda table: jnp.sum(jnp.where(e[:, None] == experts, table, 0), axis=1)
    local = bc - pick(first_blk)
    blk_j0 = (pick(start_sorted) + local * bm).astype(jnp.int32)
    blk_nv = jnp.where((b >= 0) & (b < used), jnp.minimum(bm, pick(counts) - local * bm), 0).astype(jnp.int32)
    blk_e = e[1:nblk + 1]
    later = (experts[None, :] > experts[:, None]) & (counts[None, :] > 0)
    nxt = jnp.min(jnp.where(later, experts[None, :], N_EXPERTS), axis=1)
    nxt_e = jnp.where(nxt < N_EXPERTS, nxt, -1).astype(jnp.int32)
    y_rows = _expert(order, blk_e, blk_j0, blk_nv, nxt_e, used.reshape(1), h2, w, bm)
    return y_rows.reshape(-1, LANES)


def _rope_tables(pos):
    half = QK_ROPE // 2
    inv = ROPE_THETA ** (-jnp.arange(half, dtype=F32) / half)
    ang = pos.astype(F32)[:, None] * inv[None, :]
    cos, sin = jnp.cos(ang), jnp.sin(ang)
    cos32 = jnp.concatenate([cos, cos], axis=1)
    sin32 = jnp.concatenate([-sin, sin], axis=1)
    return jnp.tile(cos32, (1, LANES // QK_ROPE)), jnp.tile(sin32, (1, LANES // QK_ROPE))


def _swap_halves(w32):
    shp = w32.shape
    w = w32.reshape(shp[:-1] + (shp[-1] // QK_ROPE, 2, QK_ROPE // 2))
    return w[..., ::-1, :].reshape(shp)


def _prep_weights(l, w_in, conv_w, g_qa, w_qb, g_kva, w_kvb, g_out_conv, g_out_attn, w_out,
                  ln1_g, ln1_b, router_w, router_b, w_gu, b_gu, w_down, b_down, ln2_g, ln2_b):
    w = {}
    wi = w_in[l]
    k_r = wi[:, _O_KR:_O_KR + QK_ROPE]
    rep = LANES // QK_ROPE
    w["w_in"] = jnp.concatenate([wi[:, :_O_KR], jnp.tile(k_r, (1, rep)), jnp.tile(_swap_halves(k_r), (1, rep))],
                                axis=1).astype(BF16)
    w["conv_w"] = conv_w[l]
    w["g_qa"] = g_qa[l].reshape(1, Q_LORA)
    w["g_kva"] = g_kva[l].reshape(1, KV_LORA)
    w["g_oc"] = g_out_conv[l].reshape(1, CONV_WIDTH)
    w["g_oa"] = g_out_attn[l].reshape(1, ATTN_WIDTH)
    wq = w_qb[l].reshape(Q_LORA, N_HEADS, QK_NOPE + QK_ROPE)
    w["wq_nope"] = wq[:, :, :QK_NOPE].reshape(Q_LORA, N_HEADS * QK_NOPE).astype(BF16)
    wq_rope = wq[:, :, QK_NOPE:].reshape(Q_LORA, N_HEADS * QK_ROPE)
    w["wq_rope"] = wq_rope.astype(BF16)
    w["wq_rope_sw"] = _swap_halves(wq_rope).astype(BF16)
    w_uk = jnp.transpose(w_kvb[l][:, :, :QK_NOPE], (1, 2, 0))
    w_uv = jnp.transpose(w_kvb[l][:, :, QK_NOPE:], (1, 0, 2))
    zk = jnp.zeros((QK_NOPE, KV_LORA), F32)
    zv = jnp.zeros((KV_LORA, V_HEAD), F32)
    w["w_uk_pair"] = jnp.stack([
        jnp.concatenate([jnp.concatenate([w_uk[2 * p], zk], axis=1),
                         jnp.concatenate([zk, w_uk[2 * p + 1]], axis=1)], axis=0)
        for p in range(N_HEADS // 2)]).astype(BF16)
    w["w_uv_pair"] = jnp.stack([
        jnp.concatenate([jnp.concatenate([w_uv[2 * p], zv], axis=1),
                         jnp.concatenate([zv, w_uv[2 * p + 1]], axis=1)], axis=0)
        for p in range(N_HEADS // 2)]).astype(BF16)
    w["w_out"] = w_out[l].astype(BF16)
    w["ln1_g"] = ln1_g[l].reshape(1, D_MODEL)
    w["ln1_b"] = ln1_b[l].reshape(1, D_MODEL)
    w["ln2_g"] = ln2_g[l].reshape(1, D_MODEL)
    w["ln2_b"] = ln2_b[l].reshape(1, D_MODEL)
    w["router_w"] = jnp.pad(router_w[l], ((0, 0), (0, LANES - N_EXPERTS))).astype(BF16)
    w["router_b"] = jnp.concatenate([router_b[l], jnp.full((LANES - N_EXPERTS,), NEG, F32)]).reshape(1, LANES)
    w["w_gu"] = w_gu[l]
    w["b_gu"] = b_gu[l].reshape(N_EXPERTS, 1, 2 * D_FF)
    w["w_down"] = w_down[l]
    w["b_down"] = b_down[l].reshape(N_EXPERTS, 1, D_MODEL)
    return w


def _mixer(x, mod, conv_prev, past, pos0, h2_all, tok0, w, alpha, *, tm_in, tm_out, tq=128, tk=512):
    B, T, _ = x.shape
    sh1, sc1, gt1, sh2, sc2, gt2 = [mod[:, None, i * D_MODEL:(i + 1) * D_MODEL] for i in range(N_MOD)]
    cos_t, sin_t = _rope_tables(pos0 + jnp.arange(T, dtype=jnp.int32))
    a_n, qlat, qrope, kcat, ckv, krope, conv_new, *vt = _inproj(x, sc1, sh1, conv_prev, cos_t, sin_t, w, tm_in,
                                                                 with_vt=past is None)
    if past is None:
        b_n = _attn_prompt(qlat, qrope, kcat, vt[0], w, tq, tk)
    else:
        b_n = _attn_sample(qlat, qrope, kcat, past[0], past[1], w, tk)
    x1, h2_all, meta, gates, cnt = _outproj(x, a_n, b_n, gt1, sc2, sh2, h2_all, tok0, w, tm_out, alpha)
    N = B * T
    route = dict(x1=x1.reshape(N, D_MODEL), idx=meta.reshape(N, LANES)[:, :TOP_K], gates=gates.reshape(N, LANES),
                 counts=cnt[0, :N_EXPERTS].astype(jnp.int32), gt2=gt2, shape=(B, T))
    return h2_all, route, ckv, krope, conv_new


def kernel(x_prompt, x_sample, c_prompt, c_sample, cache_ckv, cache_krope, state_conv, w_ada, b_ada, w_in, conv_w, g_qa, w_qb, g_kva, w_kvb, g_out_conv, g_out_attn, w_out, ln1_g, ln1_b, router_w, router_b, w_gu, b_gu, w_down, b_down, ln2_g, ln2_b):
    depth = w_ada.shape[0]
    Bp, Tp, _ = x_prompt.shape
    Bs, Ts, _ = x_sample.shape
    past_len = cache_ckv.shape[2]
    assert Ts == CHUNK and past_len % CHUNK == 0 and Tp % 512 == 0
    alpha = (2.0 * depth) ** 0.25
    xp, xs = x_prompt, x_sample
    outs = [[] for _ in range(6)]
    c_all = jnp.concatenate([c_prompt, c_sample, jnp.zeros((16 - Bp - Bs, D_MODEL), F32)], axis=0)
    for l in range(depth):
        w = _prep_weights(l, w_in, conv_w, g_qa, w_qb, g_kva, w_kvb, g_out_conv, g_out_attn, w_out,
                          ln1_g, ln1_b, router_w, router_b, w_gu, b_gu, w_down, b_down, ln2_g, ln2_b)
        mod = _ada(c_all, w_ada[l], b_ada[l])
        n_p, n_s = Bp * Tp, Bs * Ts
        n_tok = n_p + n_s
        h2_all = jnp.zeros((n_tok * ROW_TILE, LANES), F32)
        h2_all, rp, ckv_p, kr_p, cv_p = _mixer(xp, mod[:Bp], jnp.zeros((Bp, CONV_K - 1, CONV_WIDTH), F32), None, 0,
                                               h2_all, 0, w, alpha, tm_in=512, tm_out=256)
        h2_all, rs, ckv_s, kr_s, cv_s = _mixer(xs, mod[Bp:Bp + Bs], state_conv[l], (cache_ckv[l], cache_krope[l]),
                                               past_len, h2_all, n_p, w, alpha, tm_in=Ts, tm_out=Ts)
        y_rows = _moe_experts(h2_all.reshape(n_tok, ROW_TILE, LANES), jnp.concatenate([rp["idx"], rs["idx"]]),
                              rp["counts"] + rs["counts"], w, bm=256)
        xp = _combine(y_rows, rp["x1"], rp["gates"], rp["gt2"], w, 256, Tp, alpha, n_tok, 0).reshape(rp["shape"] + (D_MODEL,))
        xs = _combine(y_rows, rs["x1"], rs["gates"], rs["gt2"], w, Ts, Ts, alpha, n_tok, n_p).reshape(rs["shape"] + (D_MODEL,))
        for o, v in zip(outs, (ckv_p, kr_p, cv_p, ckv_s, kr_s, cv_s)):
            o.append(v)
    return (xp, xs) + tuple(jnp.stack(o) for o in outs)
```

```python
import functools
import math

import jax
import jax.numpy as jnp
from jax import lax
from jax.experimental import pallas as pl
from jax.experimental.pallas import tpu as pltpu

F32 = jnp.float32
BF16 = jnp.bfloat16

D_MODEL = 1024
CHUNK = 64
CONV_WIDTH = 512
CONV_K = 3
N_HEADS = 8
QK_NOPE = 64
QK_ROPE = 32
V_HEAD = 64
Q_LORA = 256
KV_LORA = 128
ATTN_WIDTH = N_HEADS * V_HEAD
ROPE_THETA = 10000.0
ATTN_SCALE = 1.0 / math.sqrt(QK_NOPE + QK_ROPE)
Q_SCALE = ATTN_SCALE * math.log2(math.e)
N_EXPERTS = 32
TOP_K = 4
D_FF = 1024
SWIGLU_LIMIT = 7.0
SWIGLU_ALPHA = 1.702
N_MOD = 6
LN_EPS = 1e-5
RMS_EPS = 1e-6

LANES = 128
ATTN_GROUPS = 1
FFN_PARTS = 6
NEG = -1e30
VMEM_LIMIT = 56 * 1024 * 1024

_O_XB, _O_XC, _O_XV = 0, CONV_WIDTH, 2 * CONV_WIDTH
_O_QA = 3 * CONV_WIDTH
_O_KVA = _O_QA + Q_LORA
_O_KR = _O_KVA + KV_LORA
_O_KRS = _O_KR + LANES
IN_COLS_EXT = _O_KRS + LANES


def _cparams(sem):
    return pltpu.CompilerParams(dimension_semantics=sem, vmem_limit_bytes=VMEM_LIMIT)


def _layernorm(x):
    mu = jnp.mean(x, axis=-1, keepdims=True)
    xc = x - mu
    var = jnp.mean(xc * xc, axis=-1, keepdims=True)
    return xc * lax.rsqrt(var + LN_EPS)


def _rms(x):
    return x * lax.rsqrt(jnp.mean(x * x, axis=-1, keepdims=True) + RMS_EPS)


def _sigmoid(x):
    return 1.0 / (1.0 + jnp.exp(-x))


ROW_TILE = D_MODEL // LANES


def _rows_from_tiles(ref, n):
    return jnp.concatenate([ref[pl.ds(c, n, stride=ROW_TILE), :] for c in range(ROW_TILE)], axis=1)


def _rows_to_tiles(ref, x):
    n = x.shape[0]
    for c in range(ROW_TILE):
        ref[pl.ds(c, n, stride=ROW_TILE), :] = x[:, c * LANES:(c + 1) * LANES]


def _ada_kernel(c_ref, w_ref, b_ref, o_ref):
    c = c_ref[...]
    s = (c * _sigmoid(c)).astype(BF16)
    o_ref[...] = jnp.dot(s, w_ref[...].astype(BF16), preferred_element_type=F32) + b_ref[...]


def _ada(c_all, w_ada, b_ada):
    rows = c_all.shape[0]
    ncol = w_ada.shape[1]
    tn = 1024
    return pl.pallas_call(
        _ada_kernel,
        grid=(ncol // tn,),
        in_specs=[pl.BlockSpec((rows, D_MODEL), lambda j: (0, 0)),
                  pl.BlockSpec((D_MODEL, tn), lambda j: (0, j)),
                  pl.BlockSpec((1, tn), lambda j: (0, j))],
        out_specs=pl.BlockSpec((rows, tn), lambda j: (0, j)),
        out_shape=jax.ShapeDtypeStruct((rows, ncol), F32),
        compiler_params=_cparams(("arbitrary",)),
        name="ada",
    )(c_all, w_ada, b_ada.reshape(1, ncol))


def _inproj_kernel(x_ref, sc_ref, sh_ref, win_ref, cw_ref, cprev_ref, gqa_ref, gkva_ref, goc_ref,
                   wqn_ref, wqr_ref, wqrs_ref, wuk_ref, cos_ref, sin_ref,
                   an_ref, qlat_ref, qrope_ref, kcat_ref, ckv_ref, krope_ref, cnew_ref,
                   *rest, tm, with_vt):
    vt_ref, ubuf = rest if with_vt else (None, rest[0])
    j = pl.program_id(1)
    x = x_ref[0]
    h = _layernorm(x) * (1.0 + sc_ref[0]) + sh_ref[0]
    proj = jnp.dot(h.astype(BF16), win_ref[...], preferred_element_type=F32)
    xb = proj[:, _O_XB:_O_XB + CONV_WIDTH]
    xc = proj[:, _O_XC:_O_XC + CONV_WIDTH]
    xv = proj[:, _O_XV:_O_XV + CONV_WIDTH]
    q_a = proj[:, _O_QA:_O_QA + Q_LORA]
    kv_a = proj[:, _O_KVA:_O_KVA + KV_LORA]
    kr4 = proj[:, _O_KR:_O_KR + LANES]
    kr4s = proj[:, _O_KRS:_O_KRS + LANES]

    u = xc * xv

    @pl.when(j == 0)
    def _():
        ubuf[6:8, :] = cprev_ref[0]

    ubuf[8:8 + tm, :] = u
    conv = (cw_ref[0:1, :] * ubuf[6:6 + tm, :] + cw_ref[1:2, :] * ubuf[7:7 + tm, :]
            + cw_ref[2:3, :] * u)
    ubuf[0:8, :] = ubuf[tm:tm + 8, :]
    cnew_ref[0] = u[tm - (CONV_K - 1):tm, :]
    an_ref[0] = (_rms(xb * conv) * goc_ref[...]).astype(BF16)

    cos = cos_ref[...]
    sin = sin_ref[...]

    ckv = _rms(kv_a) * gkva_ref[...]
    kro4 = kr4 * cos + kr4s * sin
    ckv_ref[0] = ckv
    krope_ref[0] = kro4[:, :QK_ROPE]
    kcat_ref[0] = jnp.concatenate([ckv, kro4], axis=1).astype(BF16)
    if with_vt:
        vt_ref[0] = ckv.T.astype(BF16)

    qn = (_rms(q_a) * gqa_ref[...]).astype(BF16)
    q_nope = jnp.dot(qn, wqn_ref[...], preferred_element_type=F32)
    xr = jnp.dot(qn, wqr_ref[...], preferred_element_type=F32)
    xrs = jnp.dot(qn, wqrs_ref[...], preferred_element_type=F32)
    for g in range(2):
        sl = slice(g * LANES, (g + 1) * LANES)
        qrope_ref[0, :, sl] = ((xr[:, sl] * cos + xrs[:, sl] * sin) * Q_SCALE).astype(BF16)
    for p in range(N_HEADS // 2):
        qp = q_nope[:, p * LANES:(p + 1) * LANES].astype(BF16)
        ql = jnp.dot(qp, wuk_ref[p], preferred_element_type=F32)
        qlat_ref[0, :, p * 2 * KV_LORA:(p + 1) * 2 * KV_LORA] = (ql * Q_SCALE).astype(BF16)


def _inproj(x, sc1, sh1, conv_prev, cos_t, sin_t, w, tm, with_vt):
    B, T, _ = x.shape
    nt = T // tm
    full = lambda shp: pl.BlockSpec(shp, lambda b, j: (0,) * len(shp))
    vt_shape = (jax.ShapeDtypeStruct((B, KV_LORA, T), BF16),) if with_vt else ()
    vt_spec = (pl.BlockSpec((1, KV_LORA, tm), lambda b, j: (b, 0, j)),) if with_vt else ()
    out_shapes = (
        jax.ShapeDtypeStruct((B, T, CONV_WIDTH), BF16),
        jax.ShapeDtypeStruct((B, T, N_HEADS * KV_LORA), BF16),
        jax.ShapeDtypeStruct((B, T, 2 * LANES), BF16),
        jax.ShapeDtypeStruct((B, T, 2 * LANES), BF16),
        jax.ShapeDtypeStruct((B, T, KV_LORA), F32),
        jax.ShapeDtypeStruct((B, T, QK_ROPE), F32),
        jax.ShapeDtypeStruct((B, CONV_K - 1, CONV_WIDTH), F32),
    ) + vt_shape
    tile = lambda c: pl.BlockSpec((1, tm, c), lambda b, j: (b, j, 0))
    return pl.pallas_call(
        functools.partial(_inproj_kernel, tm=tm, with_vt=with_vt),
        grid=(B, nt),
        in_specs=[
            tile(D_MODEL),
            pl.BlockSpec((1, 1, D_MODEL), lambda b, j: (b, 0, 0)),
            pl.BlockSpec((1, 1, D_MODEL), lambda b, j: (b, 0, 0)),
            full((D_MODEL, IN_COLS_EXT)),
            full((CONV_K, CONV_WIDTH)),
            pl.BlockSpec((1, CONV_K - 1, CONV_WIDTH), lambda b, j: (b, 0, 0)),
            full((1, Q_LORA)), full((1, KV_LORA)), full((1, CONV_WIDTH)),
            full((Q_LORA, N_HEADS * QK_NOPE)), full((Q_LORA, 2 * LANES)), full((Q_LORA, 2 * LANES)),
            full((N_HEADS // 2, LANES, 2 * KV_LORA)),
            pl.BlockSpec((tm, LANES), lambda b, j: (j, 0)),
            pl.BlockSpec((tm, LANES), lambda b, j: (j, 0)),
        ],
        out_specs=(tile(CONV_WIDTH), tile(N_HEADS * KV_LORA), tile(2 * LANES), tile(2 * LANES),
                   tile(KV_LORA), tile(QK_ROPE),
                   pl.BlockSpec((1, CONV_K - 1, CONV_WIDTH), lambda b, j: (b, 0, 0))) + vt_spec,
        out_shape=out_shapes,
        scratch_shapes=[pltpu.VMEM((tm + 8, CONV_WIDTH), F32)],
        compiler_params=_cparams(("arbitrary", "arbitrary")),
        name="inproj",
    )(x, sc1, sh1, w["w_in"], w["conv_w"], conv_prev, w["g_qa"], w["g_kva"], w["g_oc"],
      w["wq_nope"], w["wq_rope"], w["wq_rope_sw"], w["w_uk_pair"], cos_t, sin_t)


def _stack_queries(qlat_ref, qrope_ref, qs, tq):
    lane = lax.broadcasted_iota(jnp.int32, (tq, LANES), 1)
    for h in range(N_HEADS):
        g, i = divmod(h, 4)
        rope = qrope_ref[0, :, g * LANES:(g + 1) * LANES]
        keep = (lane >= i * QK_ROPE) & (lane < (i + 1) * QK_ROPE)
        qs[h * tq:(h + 1) * tq, 0:KV_LORA] = qlat_ref[0, :, h * KV_LORA:(h + 1) * KV_LORA]
        qs[h * tq:(h + 1) * tq, KV_LORA:KV_LORA + LANES] = jnp.where(keep, rope, jnp.zeros_like(rope))


def _softmax_step(qs, k, v, m_s, l_s, acc_s, mask=None, groups=1):
    tk = k.shape[0]
    rows = qs.shape[0] // groups
    for g in range(groups):
        r = slice(g * rows, (g + 1) * rows)
        s = lax.dot_general(qs[r, :], k, (((1,), (1,)), ((), ())), preferred_element_type=F32)
        if mask is not None:
            col, limit = mask
            s = jnp.where(col < limit[r], s, NEG)
        m_prev = m_s[r, :]
        m_new = jnp.maximum(m_prev, jnp.max(s, axis=1, keepdims=True))
        alpha = jnp.exp2(m_prev - m_new)
        if tk % LANES == 0:
            p = jnp.exp2(s - jnp.tile(m_new, (1, tk // LANES)))
        else:
            p = jnp.exp2(s - m_new[:, :tk])
        l_s[r, :] = alpha * l_s[r, :] + jnp.sum(p, axis=1, keepdims=True)
        acc_s[r, :] = alpha * acc_s[r, :] + jnp.dot(p.astype(BF16), v, preferred_element_type=F32)
        m_s[r, :] = m_new


def _attn_epilogue(acc_s, l_s, wuv_ref, g_ref, o_ref, tq):
    o = acc_s[...] / l_s[...]
    parts = []
    for p in range(N_HEADS // 2):
        op = jnp.concatenate([o[(2 * p) * tq:(2 * p + 1) * tq], o[(2 * p + 1) * tq:(2 * p + 2) * tq]], axis=1)
        parts.append(jnp.dot(op.astype(BF16), wuv_ref[p], preferred_element_type=F32))
    b = jnp.concatenate(parts, axis=1)
    o_ref[0] = (_rms(b) * g_ref[...]).astype(BF16)


def _attn_prompt_kernel(qlat_ref, qrope_ref, k_ref, vt_ref, wuvt_ref, g_ref, o_ref, qs, m_s, l_s, acc_s, sa, sb,
                        *, tq, tk):
    i = pl.program_id(1)
    M = N_HEADS * tq
    _stack_queries(qlat_ref, qrope_ref, qs, tq)
    m_s[...] = jnp.full(m_s.shape, NEG, F32)
    l_s[...] = jnp.zeros(l_s.shape, F32)
    acc_s[...] = jnp.zeros(acc_s.shape, F32)
    q0 = i * tq
    n_full = (q0 + CHUNK) // tk

    def scores(t, dst):
        k = k_ref[0, pl.ds(pl.multiple_of(t * tk, tk), tk), :]
        dst[...] = lax.dot_general(k, qs[...], (((1,), (1,)), ((), ())), preferred_element_type=F32)

    def update(t, src, limit=None):
        start = pl.multiple_of(t * tk, tk)
        vt = vt_ref[0, :, pl.ds(start, tk)]
        s = src[...]
        if limit is not None:
            kpos = start + lax.broadcasted_iota(jnp.int32, (tk, 1), 0)
            s = jnp.where(kpos < limit, s, NEG)
        m_prev = m_s[...]
        m_new = jnp.maximum(m_prev, jnp.max(s, axis=0, keepdims=True))
        alpha = jnp.exp2(m_prev - m_new)
        p = jnp.exp2(s - m_new)
        l_s[...] = alpha * l_s[...] + jnp.sum(p, axis=0, keepdims=True)
        acc_s[...] = alpha * acc_s[...] + jnp.dot(vt, p.astype(BF16), preferred_element_type=F32)
        m_s[...] = m_new

    scores(0, sa)

    def body(j, carry):
        t = 2 * j
        scores(t + 1, sb)
        update(t, sa)
        scores(t + 2, sa)
        update(t + 1, sb)
        return carry

    lax.fori_loop(0, n_full // 2, body, 0)

    col_t = jnp.bitwise_and(lax.broadcasted_iota(jnp.int32, (1, M), 1), tq - 1)
    limit = q0 + (jnp.right_shift(col_t, CHUNK.bit_length() - 1) + 1) * CHUNK
    odd = n_full % 2

    @pl.when(odd == 0)
    def _():
        update(n_full, sa, limit)

    @pl.when(odd == 1)
    def _():
        scores(n_full, sb)
        update(n_full - 1, sa)
        update(n_full, sb, limit)

    o_t = acc_s[...] / l_s[...]
    parts = []
    for p in range(N_HEADS // 2):
        pair = jnp.concatenate([o_t[:, (2 * p) * tq:(2 * p + 1) * tq], o_t[:, (2 * p + 1) * tq:(2 * p + 2) * tq]],
                               axis=0)
        parts.append(jnp.dot(wuvt_ref[p], pair.astype(BF16), preferred_element_type=F32))
    b = jnp.concatenate(parts, axis=0).T
    o_ref[0] = (_rms(b) * g_ref[...]).astype(BF16)


def _attn_prompt(qlat, qrope, kcat, vt, w, tq, tk):
    B, T, _ = qlat.shape
    M = N_HEADS * tq
    return pl.pallas_call(
        functools.partial(_attn_prompt_kernel, tq=tq, tk=tk),
        grid=(B, T // tq),
        in_specs=[
            pl.BlockSpec((1, tq, N_HEADS * KV_LORA), lambda b, i: (b, i, 0)),
            pl.BlockSpec((1, tq, 2 * LANES), lambda b, i: (b, i, 0)),
            pl.BlockSpec((1, T, 2 * LANES), lambda b, i: (b, 0, 0)),
            pl.BlockSpec((1, KV_LORA, T), lambda b, i: (b, 0, 0)),
            pl.BlockSpec((N_HEADS // 2, LANES, 2 * KV_LORA), lambda b, i: (0, 0, 0)),
            pl.BlockSpec((1, ATTN_WIDTH), lambda b, i: (0, 0)),
        ],
        out_specs=pl.BlockSpec((1, tq, ATTN_WIDTH), lambda b, i: (b, i, 0)),
        out_shape=jax.ShapeDtypeStruct((B, T, ATTN_WIDTH), BF16),
        scratch_shapes=[pltpu.VMEM((M, 2 * LANES), BF16), pltpu.VMEM((1, M), F32),
                        pltpu.VMEM((1, M), F32), pltpu.VMEM((KV_LORA, M), F32),
                        pltpu.VMEM((tk, M), F32), pltpu.VMEM((tk, M), F32)],
        compiler_params=_cparams(("arbitrary", "arbitrary")),
        name="attn_prompt",
    )(qlat, qrope, kcat, vt, jnp.swapaxes(w["w_uv_pair"], 1, 2), w["g_oa"])


def _attn_sample_kernel(qlat_ref, qrope_ref, knew_ref, pckv_ref, pkr_ref, wuv_ref, g_ref, o_ref,
                        qs, m_s, l_s, acc_s, *, tq, tk, n_past):
    _stack_queries(qlat_ref, qrope_ref, qs, tq)
    m_s[...] = jnp.full(m_s.shape, NEG, F32)
    l_s[...] = jnp.zeros(l_s.shape, F32)
    acc_s[...] = jnp.zeros(acc_s.shape, F32)

    def body(t, carry):
        start = pl.multiple_of(t * tk, tk)
        ck = pckv_ref[0, pl.ds(start, tk), :]
        kr = pkr_ref[0, pl.ds(start, tk), :]
        k = jnp.concatenate([ck, kr, kr, kr, kr], axis=1).astype(BF16)
        _softmax_step(qs, k, k[:, :KV_LORA], m_s, l_s, acc_s)
        return carry

    lax.fori_loop(0, n_past // tk, body, 0)
    k = knew_ref[0]
    _softmax_step(qs, k, k[:, :KV_LORA], m_s, l_s, acc_s)
    _attn_epilogue(acc_s, l_s, wuv_ref, g_ref, o_ref, tq)


def _attn_sample(qlat, qrope, kcat, past_ckv, past_krope, w, tk):
    B, T, _ = qlat.shape
    n_past = past_ckv.shape[1]
    M = N_HEADS * T
    per_b = lambda r, c: pl.BlockSpec((1, r, c), lambda b: (b, 0, 0))
    return pl.pallas_call(
        functools.partial(_attn_sample_kernel, tq=T, tk=tk, n_past=n_past),
        grid=(B,),
        in_specs=[per_b(T, N_HEADS * KV_LORA), per_b(T, 2 * LANES), per_b(T, 2 * LANES),
                  per_b(n_past, KV_LORA), per_b(n_past, QK_ROPE),
                  pl.BlockSpec((N_HEADS // 2, 2 * KV_LORA, LANES), lambda b: (0, 0, 0)),
                  pl.BlockSpec((1, ATTN_WIDTH), lambda b: (0, 0))],
        out_specs=per_b(T, ATTN_WIDTH),
        out_shape=jax.ShapeDtypeStruct((B, T, ATTN_WIDTH), BF16),
        scratch_shapes=[pltpu.VMEM((M, 2 * LANES), BF16), pltpu.VMEM((M, LANES), F32),
                        pltpu.VMEM((M, LANES), F32), pltpu.VMEM((M, KV_LORA), F32)],
        compiler_params=_cparams(("arbitrary",)),
        name="attn_sample",
    )(qlat, qrope, kcat, past_ckv, past_krope, w["w_uv_pair"], w["g_oa"])


def _outproj_kernel(x_ref, an_ref, bn_ref, gt1_ref, sc2_ref, sh2_ref, wo_ref, l1g_ref, l1b_ref,
                    rw_ref, rb_ref, h2_all_ref, x1_ref, h2_ref, meta_ref, gates_ref, cnt_ref, run_s,
                    *, tm, alpha):
    del h2_all_ref
    first = (pl.program_id(0) == 0) & (pl.program_id(1) == 0)

    @pl.when(first)
    def _():
        run_s[...] = jnp.zeros(run_s.shape, F32)

    m = (jnp.dot(an_ref[0], wo_ref[0:CONV_WIDTH, :], preferred_element_type=F32)
         + jnp.dot(bn_ref[0], wo_ref[CONV_WIDTH:, :], preferred_element_type=F32))
    x1 = _layernorm(alpha * x_ref[0] + gt1_ref[0] * m) * l1g_ref[...] + l1b_ref[...]
    x1_ref[0] = x1
    h2 = _layernorm(x1) * (1.0 + sc2_ref[0]) + sh2_ref[0]
    _rows_to_tiles(h2_ref, h2)

    logits = jnp.dot(h2.astype(BF16), rw_ref[...], preferred_element_type=F32) + rb_ref[...]
    lane = lax.broadcasted_iota(jnp.int32, (tm, LANES), 1)
    lane_f = lane.astype(F32)
    lg = logits
    vals, sels = [], []
    chosen = jnp.zeros((tm, LANES), F32)
    for _ in range(TOP_K):
        mx = jnp.max(lg, axis=1, keepdims=True)
        idx = jnp.min(jnp.where(lg == mx, lane_f, float(LANES)), axis=1, keepdims=True)
        sel = lane_f == idx
        vals.append(mx)
        sels.append(idx)
        chosen = jnp.where(sel, 1.0, chosen)
        lg = jnp.where(sel, NEG, lg)

    es = [jnp.exp(v - vals[0]) for v in vals]
    denom = es[0] + es[1] + es[2] + es[3]

    run_s[0:1, :] = run_s[0:1, :] + jnp.sum(chosen, axis=0, keepdims=True)
    cnt_ref[...] = jnp.broadcast_to(run_s[0:1, :], cnt_ref.shape)

    meta = jnp.zeros((tm, LANES), jnp.int32)
    gates = jnp.zeros((tm, LANES), F32)
    for k in range(TOP_K):
        meta = jnp.where(lane == k, sels[k].astype(jnp.int32), meta)
        gates = jnp.where(lane == k, es[k] / denom, gates)
    meta_ref[0] = meta
    gates_ref[0] = gates


def _outproj(x, a_n, b_n, gt1, sc2, sh2, h2_all, tok0, w, tm, alpha):
    B, T, _ = x.shape
    blk0 = tok0 // tm
    tile = lambda c: pl.BlockSpec((1, tm, c), lambda b, j: (b, j, 0))
    modv = pl.BlockSpec((1, 1, D_MODEL), lambda b, j: (b, 0, 0))
    full = lambda shp: pl.BlockSpec(shp, lambda b, j: (0,) * len(shp))
    return pl.pallas_call(
        functools.partial(_outproj_kernel, tm=tm, alpha=alpha),
        grid=(B, T // tm),
        in_specs=[tile(D_MODEL), tile(CONV_WIDTH), tile(ATTN_WIDTH), modv, modv, modv,
                  full((CONV_WIDTH + ATTN_WIDTH, D_MODEL)), full((1, D_MODEL)), full((1, D_MODEL)),
                  full((D_MODEL, LANES)), full((1, LANES)), pl.BlockSpec(memory_space=pl.ANY)],
        out_specs=(tile(D_MODEL),
                   pl.BlockSpec((tm * ROW_TILE, LANES), lambda b, j: (blk0 + b * (T // tm) + j, 0)),
                   tile(LANES), tile(LANES), pl.BlockSpec((8, LANES), lambda b, j: (0, 0))),
        out_shape=(jax.ShapeDtypeStruct((B, T, D_MODEL), F32),
                   jax.ShapeDtypeStruct(h2_all.shape, F32),
                   jax.ShapeDtypeStruct((B, T, LANES), jnp.int32), jax.ShapeDtypeStruct((B, T, LANES), F32),
                   jax.ShapeDtypeStruct((8, LANES), F32)),
        scratch_shapes=[pltpu.VMEM((8, LANES), F32)],
        input_output_aliases={11: 1},
        compiler_params=_cparams(("arbitrary", "arbitrary")),
        name="outproj",
    )(x, a_n, b_n, gt1, sc2, sh2, w["w_out"], w["ln1_g"], w["ln1_b"], w["router_w"], w["router_b"], h2_all)


def _expert_kernel(order_ref, blk_e_ref, blk_j0_ref, blk_nv_ref, nxt_e_ref, used_ref,
                   h_ref, wgu_hbm, bgu_ref, wd_hbm, bd_ref, y_ref,
                   x0, x1, y0, y1, xb, act_s, wgu_st, wd_st, wgu_bf, wd_bf, gsem, ssem, wsem, *, bm, n_tok):
    s = pl.program_id(0)
    used = used_ref[0]
    dump0 = TOP_K * n_tok

    def weight_copies(e):
        return (pltpu.make_async_copy(wgu_hbm.at[e], wgu_st, wsem.at[0]),
                pltpu.make_async_copy(wd_hbm.at[e], wd_st, wsem.at[1]))

    tok_bits = (n_tok - 1).bit_length()

    def gather_start(j0, xbuf, sem, r, prio=0):
        tok = jnp.bitwise_and(order_ref[j0 + r], (1 << tok_bits) - 1)
        pltpu.make_async_copy(h_ref.at[tok], xbuf.at[pl.ds(r * ROW_TILE, ROW_TILE)], sem).start(priority=prio)

    def scatter_start(j0, nv, ybuf, sem, r, odd, prio=0):
        real = lax.shift_right_logical(order_ref[j0 + r], tok_bits)
        dump = dump0 + odd * bm + r
        valid = jnp.right_shift(r - nv, 31)
        dst = dump + jnp.bitwise_and(valid, real - dump)
        pltpu.make_async_copy(ybuf.at[pl.ds(r * ROW_TILE, ROW_TILE)], y_ref.at[dst], sem).start(priority=prio)

    def block_wait(buf, sem):
        pltpu.make_async_copy(buf, buf, sem).wait()

    def switch_weights(b):
        prev = blk_e_ref[jnp.maximum(b - 1, 0)]
        e = blk_e_ref[b]

        @pl.when((b == 0) | (e != prev))
        def _():
            for c in weight_copies(e):
                c.wait()
            wgu_bf[...] = wgu_st[...].astype(BF16)
            wd_bf[...] = wd_st[...].astype(BF16)

            @pl.when(nxt_e_ref[e] >= 0)
            def _():
                for c in weight_copies(nxt_e_ref[e]):
                    c.start()

    def block(b, x_in, y_out, g_next, x_next, gsem_next, s_prev, y_prev, ssem_prev, prev_odd, wait_y_free):
        gj0 = blk_j0_ref[g_next + 1]
        sj0 = blk_j0_ref[s_prev + 1]
        snv = blk_nv_ref[s_prev + 1]
        def issue(part, after=None):
            zero = 0
            if after is not None:
                bits = pltpu.bitcast(jnp.abs(after[0:8, 0:LANES]), jnp.int32)
                zero = jnp.minimum(bits[0, 0], 0)
            half = FFN_PARTS // 2
            q = part % half
            for r in range(q * bm // half, (q + 1) * bm // half):
                if part < half:
                    gather_start(gj0 + zero, x_next, gsem_next, r, prio=r % 2)
                else:
                    scatter_start(sj0 + zero, snv, y_prev, ssem_prev, r, prev_odd, prio=r % 2)

        xb[...] = _rows_from_tiles(x_in, bm).astype(BF16)
        e = blk_e_ref[b]
        cw = D_FF // 4
        prev = None
        for c in range(4):
            issue(c, prev)
            gs, ls = slice(c * cw, (c + 1) * cw), slice(D_FF + c * cw, D_FF + (c + 1) * cw)
            g = jnp.dot(xb[...], wgu_bf[:, gs], preferred_element_type=F32) + bgu_ref[e][:, gs]
            lin = jnp.dot(xb[...], wgu_bf[:, ls], preferred_element_type=F32) + bgu_ref[e][:, ls]
            prev = g
            g = jnp.minimum(g, SWIGLU_LIMIT)
            lin = jnp.clip(lin, -SWIGLU_LIMIT, SWIGLU_LIMIT)
            act_s[:, gs] = (g * _sigmoid(SWIGLU_ALPHA * g) * (lin + 1.0)).astype(BF16)
        hw = D_MODEL // 2
        wait_y_free()
        for h in range(2):
            issue(4 + h, prev)
            y = jnp.dot(act_s[...], wd_bf[:, h * hw:(h + 1) * hw], preferred_element_type=F32)
            y = y + bd_ref[e][:, h * hw:(h + 1) * hw]
            prev = y
            for c in range(hw // LANES):
                y_out[pl.ds(h * (hw // LANES) + c, bm, stride=ROW_TILE), :] = y[:, c * LANES:(c + 1) * LANES]

    b0 = 2 * s
    b1 = b0 + 1

    @pl.when(b0 < used)
    def _():
        @pl.when(s == 0)
        def _():
            for c in weight_copies(blk_e_ref[0]):
                c.start()
            y1[...] = jnp.zeros(y1.shape, F32)

            def first(r, c):
                scatter_start(0, 0, y1, ssem.at[0], r, 0)
                gather_start(blk_j0_ref[1], x0, gsem.at[0], r)
                return c

            lax.fori_loop(0, bm, first, 0)
            block_wait(y1, ssem.at[0])

        def y0_free():
            @pl.when(s > 0)
            def _():
                block_wait(y0, ssem.at[0])

        def y1_free():
            block_wait(y1, ssem.at[1])

        switch_weights(b0)
        block_wait(x0, gsem.at[0])
        block(b0, x0, y0, b1, x1, gsem.at[1], b0 - 1, y1, ssem.at[1], 1, y0_free)

        switch_weights(b1)
        block_wait(x1, gsem.at[1])
        block(b1, x1, y1, b0 + 2, x0, gsem.at[0], b0, y0, ssem.at[0], 0, y1_free)

        @pl.when(b0 + 2 >= used)
        def _():
            def last(r, c):
                scatter_start(blk_j0_ref[b1 + 1], blk_nv_ref[b1 + 1], y1, ssem.at[1], r, 1)
                return c

            lax.fori_loop(0, bm, last, 0)
            block_wait(y0, ssem.at[0])
            block_wait(y1, ssem.at[1])
            block_wait(x0, gsem.at[0])


def _expert(order, blk_e, blk_j0, blk_nv, nxt_e, used, h2, w, bm):
    n_tok = h2.shape[0]
    nblk = blk_e.shape[0]
    f32buf = lambda shp: pltpu.VMEM(shp, F32)
    return pl.pallas_call(
        functools.partial(_expert_kernel, bm=bm, n_tok=n_tok),
        grid_spec=pltpu.PrefetchScalarGridSpec(
            num_scalar_prefetch=6,
            grid=(nblk // 2,),
            in_specs=[
                pl.BlockSpec(memory_space=pl.ANY),
                pl.BlockSpec(memory_space=pl.ANY),
                pl.BlockSpec((N_EXPERTS, 1, 2 * D_FF), lambda s, *_: (0, 0, 0)),
                pl.BlockSpec(memory_space=pl.ANY),
                pl.BlockSpec((N_EXPERTS, 1, D_MODEL), lambda s, *_: (0, 0, 0)),
            ],
            out_specs=pl.BlockSpec(memory_space=pl.ANY),
            scratch_shapes=[f32buf((bm * ROW_TILE, LANES)), f32buf((bm * ROW_TILE, LANES)),
                            f32buf((bm * ROW_TILE, LANES)), f32buf((bm * ROW_TILE, LANES)),
                            pltpu.VMEM((bm, D_MODEL), BF16), pltpu.VMEM((bm, D_FF), BF16),
                            f32buf((D_MODEL, 2 * D_FF)), f32buf((D_FF, D_MODEL)),
                            pltpu.VMEM((D_MODEL, 2 * D_FF), BF16), pltpu.VMEM((D_FF, D_MODEL), BF16),
                            pltpu.SemaphoreType.DMA((2,)), pltpu.SemaphoreType.DMA((2,)),
                            pltpu.SemaphoreType.DMA((2,))],
        ),
        out_shape=jax.ShapeDtypeStruct((TOP_K * n_tok + 2 * bm, ROW_TILE, LANES), F32),
        compiler_params=_cparams(("arbitrary",)),
        name="moe_expert",
    )(order, blk_e, blk_j0, blk_nv, nxt_e, used, h2, w["w_gu"], w["b_gu"], w["w_down"], w["b_down"])


def _combine_kernel(y0_ref, y1_ref, y2_ref, y3_ref, x1_ref, gates_ref, gt2_ref, l2g_ref, l2b_ref, o_ref, *, alpha):
    gates = gates_ref[...]
    tm = gates.shape[0]
    f = gates[:, 0:1] * _rows_from_tiles(y0_ref, tm)
    for k, y_ref in enumerate((y1_ref, y2_ref, y3_ref), start=1):
        f = f + gates[:, k:k + 1] * _rows_from_tiles(y_ref, tm)
    o_ref[...] = _layernorm(alpha * x1_ref[...] + gt2_ref[0] * f) * l2g_ref[...] + l2b_ref[...]


def _combine(y_rows, x1, gates, gt2, w, tm, tokens_per_batch, alpha, n_tok, tok0):
    N = x1.shape[0]
    per_b = tokens_per_batch // tm
    nt = N // tm
    assert n_tok % tm == 0 and tok0 % tm == 0
    slot = lambda k: pl.BlockSpec((tm * ROW_TILE, LANES), lambda i: ((k * n_tok + tok0) // tm + i, 0))
    return pl.pallas_call(
        functools.partial(_combine_kernel, alpha=alpha),
        grid=(nt,),
        in_specs=[slot(0), slot(1), slot(2), slot(3),
                  pl.BlockSpec((tm, D_MODEL), lambda i: (i, 0)),
                  pl.BlockSpec((tm, LANES), lambda i: (i, 0)),
                  pl.BlockSpec((1, 1, D_MODEL), lambda i: (i // per_b, 0, 0)),
                  pl.BlockSpec((1, D_MODEL), lambda i: (0, 0)),
                  pl.BlockSpec((1, D_MODEL), lambda i: (0, 0))],
        out_specs=pl.BlockSpec((tm, D_MODEL), lambda i: (i, 0)),
        out_shape=jax.ShapeDtypeStruct((N, D_MODEL), F32),
        compiler_params=_cparams(("arbitrary",)),
        name="moe_combine",
    )(y_rows, y_rows, y_rows, y_rows, x1, gates, gt2, w["ln2_g"], w["ln2_b"])


def _moe_experts(h2, idx, counts, w, bm):
    N = h2.shape[0]
    n_rows = N * TOP_K
    nblk = (n_rows + N_EXPERTS * (bm - 1)) // bm
    nblk += nblk % 2
    experts = jnp.arange(N_EXPERTS, dtype=jnp.int32)
    id_bits = (n_rows - 1).bit_length()
    tok_bits = (N - 1).bit_length()
    assert tok_bits + (TOP_K * N - 1).bit_length() <= 32
    keys = jnp.left_shift(idx.reshape(-1), id_bits) + jnp.arange(n_rows, dtype=jnp.int32)
    keys = jnp.concatenate([keys, jnp.full(((1 << id_bits) - n_rows,), jnp.iinfo(jnp.int32).max, jnp.int32)])
    flat = jnp.bitwise_and(jnp.sort(keys)[:n_rows], (1 << id_bits) - 1).astype(jnp.uint32)
    tok = jnp.right_shift(flat, 2)
    row = jnp.bitwise_and(flat, TOP_K - 1) * N + tok
    order = lax.bitcast_convert_type(jnp.left_shift(row, tok_bits) | tok, jnp.int32)
    order = jnp.concatenate([order, jnp.zeros((bm,), jnp.int32)])
    nb_e = (counts + bm - 1) // bm
    blk_end = jnp.cumsum(nb_e)
    first_blk = blk_end - nb_e
    start_sorted = jnp.cumsum(counts) - counts
    used = blk_end[-1].astype(jnp.int32)
    b = jnp.arange(-1, nblk + 1, dtype=jnp.int32)
    bc = jnp.clip(b, 0, used - 1)
    e = jnp.minimum(jnp.sum(blk_end[None, :] <= bc[:, None], axis=1), N_EXPERTS - 1).astype(jnp.int32)
    pick = lambda table: jnp.sum(jnp.where(e[:, None] == experts, table, 0), axis=1)
    local = bc - pick(first_blk)
    blk_j0 = (pick(start_sorted) + local * bm).astype(jnp.int32)
    blk_nv = jnp.where((b >= 0) & (b < used), jnp.minimum(bm, pick(counts) - local * bm), 0).astype(jnp.int32)
    blk_e = e[1:nblk + 1]
    later = (experts[None, :] > experts[:, None]) & (counts[None, :] > 0)
    nxt = jnp.min(jnp.where(later, experts[None, :], N_EXPERTS), axis=1)
    nxt_e = jnp.where(nxt < N_EXPERTS, nxt, -1).astype(jnp.int32)
    y_rows = _expert(order, blk_e, blk_j0, blk_nv, nxt_e, used.reshape(1), h2, w, bm)
    return y_rows.reshape(-1, LANES)


def _rope_tables(pos):
    half = QK_ROPE // 2
    inv = ROPE_THETA ** (-jnp.arange(half, dtype=F32) / half)
    ang = pos.astype(F32)[:, None] * inv[None, :]
    cos, sin = jnp.cos(ang), jnp.sin(ang)
    cos32 = jnp.concatenate([cos, cos], axis=1)
    sin32 = jnp.concatenate([-sin, sin], axis=1)
    return jnp.tile(cos32, (1, LANES // QK_ROPE)), jnp.tile(sin32, (1, LANES // QK_ROPE))


def _swap_halves(w32):
    shp = w32.shape
    w = w32.reshape(shp[:-1] + (shp[-1] // QK_ROPE, 2, QK_ROPE // 2))
    return w[..., ::-1, :].reshape(shp)


def _prep_weights(l, w_in, conv_w, g_qa, w_qb, g_kva, w_kvb, g_out_conv, g_out_attn, w_out,
                  ln1_g, ln1_b, router_w, router_b, w_gu, b_gu, w_down, b_down, ln2_g, ln2_b):
    w = {}
    wi = w_in[l]
    k_r = wi[:, _O_KR:_O_KR + QK_ROPE]
    rep = LANES // QK_ROPE
    w["w_in"] = jnp.concatenate([wi[:, :_O_KR], jnp.tile(k_r, (1, rep)), jnp.tile(_swap_halves(k_r), (1, rep))],
                                axis=1).astype(BF16)
    w["conv_w"] = conv_w[l]
    w["g_qa"] = g_qa[l].reshape(1, Q_LORA)
    w["g_kva"] = g_kva[l].reshape(1, KV_LORA)
    w["g_oc"] = g_out_conv[l].reshape(1, CONV_WIDTH)
    w["g_oa"] = g_out_attn[l].reshape(1, ATTN_WIDTH)
    wq = w_qb[l].reshape(Q_LORA, N_HEADS, QK_NOPE + QK_ROPE)
    w["wq_nope"] = wq[:, :, :QK_NOPE].reshape(Q_LORA, N_HEADS * QK_NOPE).astype(BF16)
    wq_rope = wq[:, :, QK_NOPE:].reshape(Q_LORA, N_HEADS * QK_ROPE)
    w["wq_rope"] = wq_rope.astype(BF16)
    w["wq_rope_sw"] = _swap_halves(wq_rope).astype(BF16)
    w_uk = jnp.transpose(w_kvb[l][:, :, :QK_NOPE], (1, 2, 0))
    w_uv = jnp.transpose(w_kvb[l][:, :, QK_NOPE:], (1, 0, 2))
    zk = jnp.zeros((QK_NOPE, KV_LORA), F32)
    zv = jnp.zeros((KV_LORA, V_HEAD), F32)
    w["w_uk_pair"] = jnp.stack([
        jnp.concatenate([jnp.concatenate([w_uk[2 * p], zk], axis=1),
                         jnp.concatenate([zk, w_uk[2 * p + 1]], axis=1)], axis=0)
        for p in range(N_HEADS // 2)]).astype(BF16)
    w["w_uv_pair"] = jnp.stack([
        jnp.concatenate([jnp.concatenate([w_uv[2 * p], zv], axis=1),
                         jnp.concatenate([zv, w_uv[2 * p + 1]], axis=1)], axis=0)
        for p in range(N_HEADS // 2)]).astype(BF16)
    w["w_out"] = w_out[l].astype(BF16)
    w["ln1_g"] = ln1_g[l].reshape(1, D_MODEL)
    w["ln1_b"] = ln1_b[l].reshape(1, D_MODEL)
    w["ln2_g"] = ln2_g[l].reshape(1, D_MODEL)
    w["ln2_b"] = ln2_b[l].reshape(1, D_MODEL)
    w["router_w"] = jnp.pad(router_w[l], ((0, 0), (0, LANES - N_EXPERTS))).astype(BF16)
    w["router_b"] = jnp.concatenate([router_b[l], jnp.full((LANES - N_EXPERTS,), NEG, F32)]).reshape(1, LANES)
    w["w_gu"] = w_gu[l]
    w["b_gu"] = b_gu[l].reshape(N_EXPERTS, 1, 2 * D_FF)
    w["w_down"] = w_down[l]
    w["b_down"] = b_down[l].reshape(N_EXPERTS, 1, D_MODEL)
    return w


def _mixer(x, mod, conv_prev, past, pos0, h2_all, tok0, w, alpha, *, tm_in, tm_out, tq=128, tk=512):
    B, T, _ = x.shape
    sh1, sc1, gt1, sh2, sc2, gt2 = [mod[:, None, i * D_MODEL:(i + 1) * D_MODEL] for i in range(N_MOD)]
    cos_t, sin_t = _rope_tables(pos0 + jnp.arange(T, dtype=jnp.int32))
    a_n, qlat, qrope, kcat, ckv, krope, conv_new, *vt = _inproj(x, sc1, sh1, conv_prev, cos_t, sin_t, w, tm_in,
                                                                 with_vt=past is None)
    if past is None:
        b_n = _attn_prompt(qlat, qrope, kcat, vt[0], w, tq, tk)
    else:
        b_n = _attn_sample(qlat, qrope, kcat, past[0], past[1], w, tk)
    x1, h2_all, meta, gates, cnt = _outproj(x, a_n, b_n, gt1, sc2, sh2, h2_all, tok0, w, tm_out, alpha)
    N = B * T
    route = dict(x1=x1.reshape(N, D_MODEL), idx=meta.reshape(N, LANES)[:, :TOP_K], gates=gates.reshape(N, LANES),
                 counts=cnt[0, :N_EXPERTS].astype(jnp.int32), gt2=gt2, shape=(B, T))
    return h2_all, route, ckv, krope, conv_new


def kernel(x_prompt, x_sample, c_prompt, c_sample, cache_ckv, cache_krope, state_conv, w_ada, b_ada, w_in, conv_w, g_qa, w_qb, g_kva, w_kvb, g_out_conv, g_out_attn, w_out, ln1_g, ln1_b, router_w, router_b, w_gu, b_gu, w_down, b_down, ln2_g, ln2_b):
    depth = w_ada.shape[0]
    Bp, Tp, _ = x_prompt.shape
    Bs, Ts, _ = x_sample.shape
    past_len = cache_ckv.shape[2]
    assert Ts == CHUNK and past_len % CHUNK == 0 and Tp % 512 == 0
    alpha = (2.0 * depth) ** 0.25
    xp, xs = x_prompt, x_sample
    outs = [[] for _ in range(6)]
    c_all = jnp.concatenate([c_prompt, c_sample, jnp.zeros((16 - Bp - Bs, D_MODEL), F32)], axis=0)
    for l in range(depth):
        w = _prep_weights(l, w_in, conv_w, g_qa, w_qb, g_kva, w_kvb, g_out_conv, g_out_attn, w_out,
                          ln1_g, ln1_b, router_w, router_b, w_gu, b_gu, w_down, b_down, ln2_g, ln2_b)
        mod = _ada(c_all, w_ada[l], b_ada[l])
        n_p, n_s = Bp * Tp, Bs * Ts
        n_tok = n_p + n_s
        h2_all = jnp.zeros((n_tok * ROW_TILE, LANES), F32)
        h2_all, rp, ckv_p, kr_p, cv_p = _mixer(xp, mod[:Bp], jnp.zeros((Bp, CONV_K - 1, CONV_WIDTH), F32), None, 0,
                                               h2_all, 0, w, alpha, tm_in=1024, tm_out=512)
        h2_all, rs, ckv_s, kr_s, cv_s = _mixer(xs, mod[Bp:Bp + Bs], state_conv[l], (cache_ckv[l], cache_krope[l]),
                                               past_len, h2_all, n_p, w, alpha, tm_in=Ts, tm_out=Ts)
        y_rows = _moe_experts(h2_all.reshape(n_tok, ROW_TILE, LANES), jnp.concatenate([rp["idx"], rs["idx"]]),
                              rp["counts"] + rs["counts"], w, bm=256)
        xp = _combine(y_rows, rp["x1"], rp["gates"], rp["gt2"], w, 256, Tp, alpha, n_tok, 0).reshape(rp["shape"] + (D_MODEL,))
        xs = _combine(y_rows, rs["x1"], rs["gates"], rs["gt2"], w, Ts, Ts, alpha, n_tok, n_p).reshape(rs["shape"] + (D_MODEL,))
        for o, v in zip(outs, (ckv_p, kr_p, cv_p, ckv_s, kr_s, cv_s)):
            o.append(v)
    return (xp, xs) + tuple(jnp.stack(o) for o in outs)
```

```python
import functools
import math

import jax
import jax.numpy as jnp
from jax import lax
from jax.experimental import pallas as pl
from jax.experimental.pallas import tpu as pltpu

F32 = jnp.float32
BF16 = jnp.bfloat16

D_MODEL = 1024
CHUNK = 64
CONV_WIDTH = 512
CONV_K = 3
N_HEADS = 8
QK_NOPE = 64
QK_ROPE = 32
V_HEAD = 64
Q_LORA = 256
KV_LORA = 128
ATTN_WIDTH = N_HEADS * V_HEAD
ROPE_THETA = 10000.0
ATTN_SCALE = 1.0 / math.sqrt(QK_NOPE + QK_ROPE)
Q_SCALE = ATTN_SCALE * math.log2(math.e)
N_EXPERTS = 32
TOP_K = 4
D_FF = 1024
SWIGLU_LIMIT = 7.0
SWIGLU_ALPHA = 1.702
N_MOD = 6
LN_EPS = 1e-5
RMS_EPS = 1e-6

LANES = 128
ATTN_GROUPS = 1
DMA_BATCHES = 10
NEG = -1e30
VMEM_LIMIT = 56 * 1024 * 1024

_O_XB, _O_XC, _O_XV = 0, CONV_WIDTH, 2 * CONV_WIDTH
_O_QA = 3 * CONV_WIDTH
_O_KVA = _O_QA + Q_LORA
_O_KR = _O_KVA + KV_LORA
_O_KRS = _O_KR + LANES
IN_COLS_EXT = _O_KRS + LANES


def _cparams(sem):
    return pltpu.CompilerParams(dimension_semantics=sem, vmem_limit_bytes=VMEM_LIMIT)


def _layernorm(x):
    mu = jnp.mean(x, axis=-1, keepdims=True)
    xc = x - mu
    var = jnp.mean(xc * xc, axis=-1, keepdims=True)
    return xc * lax.rsqrt(var + LN_EPS)


def _rms(x):
    return x * lax.rsqrt(jnp.mean(x * x, axis=-1, keepdims=True) + RMS_EPS)


def _sigmoid(x):
    return 1.0 / (1.0 + jnp.exp(-x))


ROW_TILE = D_MODEL // LANES


def _rows_from_tiles(ref, n):
    return jnp.concatenate([ref[pl.ds(c, n, stride=ROW_TILE), :] for c in range(ROW_TILE)], axis=1)


def _rows_to_tiles(ref, x):
    n = x.shape[0]
    for c in range(ROW_TILE):
        ref[pl.ds(c, n, stride=ROW_TILE), :] = x[:, c * LANES:(c + 1) * LANES]


def _ada_kernel(c_ref, w_ref, b_ref, o_ref):
    c = c_ref[...]
    s = (c * _sigmoid(c)).astype(BF16)
    o_ref[...] = jnp.dot(s, w_ref[...].astype(BF16), preferred_element_type=F32) + b_ref[...]


def _ada(c_all, w_ada, b_ada):
    rows = c_all.shape[0]
    ncol = w_ada.shape[1]
    tn = 1024
    return pl.pallas_call(
        _ada_kernel,
        grid=(ncol // tn,),
        in_specs=[pl.BlockSpec((rows, D_MODEL), lambda j: (0, 0)),
                  pl.BlockSpec((D_MODEL, tn), lambda j: (0, j)),
                  pl.BlockSpec((1, tn), lambda j: (0, j))],
        out_specs=pl.BlockSpec((rows, tn), lambda j: (0, j)),
        out_shape=jax.ShapeDtypeStruct((rows, ncol), F32),
        compiler_params=_cparams(("arbitrary",)),
        name="ada",
    )(c_all, w_ada, b_ada.reshape(1, ncol))


def _inproj_kernel(x_ref, sc_ref, sh_ref, win_ref, cw_ref, cprev_ref, gqa_ref, gkva_ref, goc_ref,
                   wqn_ref, wqr_ref, wqrs_ref, wuk_ref, cos_ref, sin_ref,
                   an_ref, qlat_ref, qrope_ref, kcat_ref, ckv_ref, krope_ref, cnew_ref,
                   *rest, tm, with_vt):
    vt_ref, ubuf = rest if with_vt else (None, rest[0])
    j = pl.program_id(1)
    x = x_ref[0]
    h = _layernorm(x) * (1.0 + sc_ref[0]) + sh_ref[0]
    proj = jnp.dot(h.astype(BF16), win_ref[...], preferred_element_type=F32)
    xb = proj[:, _O_XB:_O_XB + CONV_WIDTH]
    xc = proj[:, _O_XC:_O_XC + CONV_WIDTH]
    xv = proj[:, _O_XV:_O_XV + CONV_WIDTH]
    q_a = proj[:, _O_QA:_O_QA + Q_LORA]
    kv_a = proj[:, _O_KVA:_O_KVA + KV_LORA]
    kr4 = proj[:, _O_KR:_O_KR + LANES]
    kr4s = proj[:, _O_KRS:_O_KRS + LANES]

    u = xc * xv

    @pl.when(j == 0)
    def _():
        ubuf[6:8, :] = cprev_ref[0]

    ubuf[8:8 + tm, :] = u
    conv = (cw_ref[0:1, :] * ubuf[6:6 + tm, :] + cw_ref[1:2, :] * ubuf[7:7 + tm, :]
            + cw_ref[2:3, :] * u)
    ubuf[0:8, :] = ubuf[tm:tm + 8, :]
    cnew_ref[0] = u[tm - (CONV_K - 1):tm, :]
    an_ref[0] = (_rms(xb * conv) * goc_ref[...]).astype(BF16)

    cos = cos_ref[...]
    sin = sin_ref[...]

    ckv = _rms(kv_a) * gkva_ref[...]
    kro4 = kr4 * cos + kr4s * sin
    ckv_ref[0] = ckv
    krope_ref[0] = kro4[:, :QK_ROPE]
    kcat_ref[0] = jnp.concatenate([ckv, kro4], axis=1).astype(BF16)
    if with_vt:
        vt_ref[0] = ckv.T.astype(BF16)

    qn = (_rms(q_a) * gqa_ref[...]).astype(BF16)
    q_nope = jnp.dot(qn, wqn_ref[...], preferred_element_type=F32)
    xr = jnp.dot(qn, wqr_ref[...], preferred_element_type=F32)
    xrs = jnp.dot(qn, wqrs_ref[...], preferred_element_type=F32)
    for g in range(2):
        sl = slice(g * LANES, (g + 1) * LANES)
        qrope_ref[0, :, sl] = ((xr[:, sl] * cos + xrs[:, sl] * sin) * Q_SCALE).astype(BF16)
    for p in range(N_HEADS // 2):
        qp = q_nope[:, p * LANES:(p + 1) * LANES].astype(BF16)
        ql = jnp.dot(qp, wuk_ref[p], preferred_element_type=F32)
        qlat_ref[0, :, p * 2 * KV_LORA:(p + 1) * 2 * KV_LORA] = (ql * Q_SCALE).astype(BF16)


def _inproj(x, sc1, sh1, conv_prev, cos_t, sin_t, w, tm, with_vt):
    B, T, _ = x.shape
    nt = T // tm
    full = lambda shp: pl.BlockSpec(shp, lambda b, j: (0,) * len(shp))
    vt_shape = (jax.ShapeDtypeStruct((B, KV_LORA, T), BF16),) if with_vt else ()
    vt_spec = (pl.BlockSpec((1, KV_LORA, tm), lambda b, j: (b, 0, j)),) if with_vt else ()
    out_shapes = (
        jax.ShapeDtypeStruct((B, T, CONV_WIDTH), BF16),
        jax.ShapeDtypeStruct((B, T, N_HEADS * KV_LORA), BF16),
        jax.ShapeDtypeStruct((B, T, 2 * LANES), BF16),
        jax.ShapeDtypeStruct((B, T, 2 * LANES), BF16),
        jax.ShapeDtypeStruct((B, T, KV_LORA), F32),
        jax.ShapeDtypeStruct((B, T, QK_ROPE), F32),
        jax.ShapeDtypeStruct((B, CONV_K - 1, CONV_WIDTH), F32),
    ) + vt_shape
    tile = lambda c: pl.BlockSpec((1, tm, c), lambda b, j: (b, j, 0))
    return pl.pallas_call(
        functools.partial(_inproj_kernel, tm=tm, with_vt=with_vt),
        grid=(B, nt),
        in_specs=[
            tile(D_MODEL),
            pl.BlockSpec((1, 1, D_MODEL), lambda b, j: (b, 0, 0)),
            pl.BlockSpec((1, 1, D_MODEL), lambda b, j: (b, 0, 0)),
            full((D_MODEL, IN_COLS_EXT)),
            full((CONV_K, CONV_WIDTH)),
            pl.BlockSpec((1, CONV_K - 1, CONV_WIDTH), lambda b, j: (b, 0, 0)),
            full((1, Q_LORA)), full((1, KV_LORA)), full((1, CONV_WIDTH)),
            full((Q_LORA, N_HEADS * QK_NOPE)), full((Q_LORA, 2 * LANES)), full((Q_LORA, 2 * LANES)),
            full((N_HEADS // 2, LANES, 2 * KV_LORA)),
            pl.BlockSpec((tm, LANES), lambda b, j: (j, 0)),
            pl.BlockSpec((tm, LANES), lambda b, j: (j, 0)),
        ],
        out_specs=(tile(CONV_WIDTH), tile(N_HEADS * KV_LORA), tile(2 * LANES), tile(2 * LANES),
                   tile(KV_LORA), tile(QK_ROPE),
                   pl.BlockSpec((1, CONV_K - 1, CONV_WIDTH), lambda b, j: (b, 0, 0))) + vt_spec,
        out_shape=out_shapes,
        scratch_shapes=[pltpu.VMEM((tm + 8, CONV_WIDTH), F32)],
        compiler_params=_cparams(("arbitrary", "arbitrary")),
        name="inproj",
    )(x, sc1, sh1, w["w_in"], w["conv_w"], conv_prev, w["g_qa"], w["g_kva"], w["g_oc"],
      w["wq_nope"], w["wq_rope"], w["wq_rope_sw"], w["w_uk_pair"], cos_t, sin_t)


def _stack_queries(qlat_ref, qrope_ref, qs, tq):
    lane = lax.broadcasted_iota(jnp.int32, (tq, LANES), 1)
    for h in range(N_HEADS):
        g, i = divmod(h, 4)
        rope = qrope_ref[0, :, g * LANES:(g + 1) * LANES]
        keep = (lane >= i * QK_ROPE) & (lane < (i + 1) * QK_ROPE)
        qs[h * tq:(h + 1) * tq, 0:KV_LORA] = qlat_ref[0, :, h * KV_LORA:(h + 1) * KV_LORA]
        qs[h * tq:(h + 1) * tq, KV_LORA:KV_LORA + LANES] = jnp.where(keep, rope, jnp.zeros_like(rope))


def _softmax_step(qs, k, v, m_s, l_s, acc_s, mask=None, groups=1):
    tk = k.shape[0]
    rows = qs.shape[0] // groups
    for g in range(groups):
        r = slice(g * rows, (g + 1) * rows)
        s = lax.dot_general(qs[r, :], k, (((1,), (1,)), ((), ())), preferred_element_type=F32)
        if mask is not None:
            col, limit = mask
            s = jnp.where(col < limit[r], s, NEG)
        m_prev = m_s[r, :]
        m_new = jnp.maximum(m_prev, jnp.max(s, axis=1, keepdims=True))
        alpha = jnp.exp2(m_prev - m_new)
        if tk % LANES == 0:
            p = jnp.exp2(s - jnp.tile(m_new, (1, tk // LANES)))
        else:
            p = jnp.exp2(s - m_new[:, :tk])
        l_s[r, :] = alpha * l_s[r, :] + jnp.sum(p, axis=1, keepdims=True)
        acc_s[r, :] = alpha * acc_s[r, :] + jnp.dot(p.astype(BF16), v, preferred_element_type=F32)
        m_s[r, :] = m_new


def _attn_epilogue(acc_s, l_s, wuv_ref, g_ref, o_ref, tq):
    o = acc_s[...] / l_s[...]
    parts = []
    for p in range(N_HEADS // 2):
        op = jnp.concatenate([o[(2 * p) * tq:(2 * p + 1) * tq], o[(2 * p + 1) * tq:(2 * p + 2) * tq]], axis=1)
        parts.append(jnp.dot(op.astype(BF16), wuv_ref[p], preferred_element_type=F32))
    b = jnp.concatenate(parts, axis=1)
    o_ref[0] = (_rms(b) * g_ref[...]).astype(BF16)


def _attn_prompt_kernel(qlat_ref, qrope_ref, k_ref, vt_ref, wuvt_ref, g_ref, o_ref, qs, m_s, l_s, acc_s, sa, sb,
                        *, tq, tk):
    i = pl.program_id(1)
    M = N_HEADS * tq
    _stack_queries(qlat_ref, qrope_ref, qs, tq)
    m_s[...] = jnp.full(m_s.shape, NEG, F32)
    l_s[...] = jnp.zeros(l_s.shape, F32)
    acc_s[...] = jnp.zeros(acc_s.shape, F32)
    q0 = i * tq
    n_full = (q0 + CHUNK) // tk

    def scores(t, dst):
        k = k_ref[0, pl.ds(pl.multiple_of(t * tk, tk), tk), :]
        dst[...] = lax.dot_general(k, qs[...], (((1,), (1,)), ((), ())), preferred_element_type=F32)

    def update(t, src, limit=None):
        start = pl.multiple_of(t * tk, tk)
        vt = vt_ref[0, :, pl.ds(start, tk)]
        s = src[...]
        if limit is not None:
            kpos = start + lax.broadcasted_iota(jnp.int32, (tk, 1), 0)
            s = jnp.where(kpos < limit, s, NEG)
        m_prev = m_s[...]
        m_new = jnp.maximum(m_prev, jnp.max(s, axis=0, keepdims=True))
        alpha = jnp.exp2(m_prev - m_new)
        p = jnp.exp2(s - m_new)
        l_s[...] = alpha * l_s[...] + jnp.sum(p, axis=0, keepdims=True)
        acc_s[...] = alpha * acc_s[...] + jnp.dot(vt, p.astype(BF16), preferred_element_type=F32)
        m_s[...] = m_new

    scores(0, sa)

    def body(j, carry):
        t = 2 * j
        scores(t + 1, sb)
        update(t, sa)
        scores(t + 2, sa)
        update(t + 1, sb)
        return carry

    lax.fori_loop(0, n_full // 2, body, 0)

    col_t = jnp.bitwise_and(lax.broadcasted_iota(jnp.int32, (1, M), 1), tq - 1)
    limit = q0 + (jnp.right_shift(col_t, CHUNK.bit_length() - 1) + 1) * CHUNK
    odd = n_full % 2

    @pl.when(odd == 0)
    def _():
        update(n_full, sa, limit)

    @pl.when(odd == 1)
    def _():
        scores(n_full, sb)
        update(n_full - 1, sa)
        update(n_full, sb, limit)

    o_t = acc_s[...] / l_s[...]
    parts = []
    for p in range(N_HEADS // 2):
        pair = jnp.concatenate([o_t[:, (2 * p) * tq:(2 * p + 1) * tq], o_t[:, (2 * p + 1) * tq:(2 * p + 2) * tq]],
                               axis=0)
        parts.append(jnp.dot(wuvt_ref[p], pair.astype(BF16), preferred_element_type=F32))
    b = jnp.concatenate(parts, axis=0).T
    o_ref[0] = (_rms(b) * g_ref[...]).astype(BF16)


def _attn_prompt(qlat, qrope, kcat, vt, w, tq, tk):
    B, T, _ = qlat.shape
    M = N_HEADS * tq
    return pl.pallas_call(
        functools.partial(_attn_prompt_kernel, tq=tq, tk=tk),
        grid=(B, T // tq),
        in_specs=[
            pl.BlockSpec((1, tq, N_HEADS * KV_LORA), lambda b, i: (b, i, 0)),
            pl.BlockSpec((1, tq, 2 * LANES), lambda b, i: (b, i, 0)),
            pl.BlockSpec((1, T, 2 * LANES), lambda b, i: (b, 0, 0)),
            pl.BlockSpec((1, KV_LORA, T), lambda b, i: (b, 0, 0)),
            pl.BlockSpec((N_HEADS // 2, LANES, 2 * KV_LORA), lambda b, i: (0, 0, 0)),
            pl.BlockSpec((1, ATTN_WIDTH), lambda b, i: (0, 0)),
        ],
        out_specs=pl.BlockSpec((1, tq, ATTN_WIDTH), lambda b, i: (b, i, 0)),
        out_shape=jax.ShapeDtypeStruct((B, T, ATTN_WIDTH), BF16),
        scratch_shapes=[pltpu.VMEM((M, 2 * LANES), BF16), pltpu.VMEM((1, M), F32),
                        pltpu.VMEM((1, M), F32), pltpu.VMEM((KV_LORA, M), F32),
                        pltpu.VMEM((tk, M), F32), pltpu.VMEM((tk, M), F32)],
        compiler_params=_cparams(("arbitrary", "arbitrary")),
        name="attn_prompt",
    )(qlat, qrope, kcat, vt, jnp.swapaxes(w["w_uv_pair"], 1, 2), w["g_oa"])


def _attn_sample_kernel(qlat_ref, qrope_ref, knew_ref, pckv_ref, pkr_ref, wuv_ref, g_ref, o_ref,
                        qs, m_s, l_s, acc_s, *, tq, tk, n_past):
    _stack_queries(qlat_ref, qrope_ref, qs, tq)
    m_s[...] = jnp.full(m_s.shape, NEG, F32)
    l_s[...] = jnp.zeros(l_s.shape, F32)
    acc_s[...] = jnp.zeros(acc_s.shape, F32)

    def body(t, carry):
        start = pl.multiple_of(t * tk, tk)
        ck = pckv_ref[0, pl.ds(start, tk), :]
        kr = pkr_ref[0, pl.ds(start, tk), :]
        k = jnp.concatenate([ck, kr, kr, kr, kr], axis=1).astype(BF16)
        _softmax_step(qs, k, k[:, :KV_LORA], m_s, l_s, acc_s)
        return carry

    lax.fori_loop(0, n_past // tk, body, 0)
    k = knew_ref[0]
    _softmax_step(qs, k, k[:, :KV_LORA], m_s, l_s, acc_s)
    _attn_epilogue(acc_s, l_s, wuv_ref, g_ref, o_ref, tq)


def _attn_sample(qlat, qrope, kcat, past_ckv, past_krope, w, tk):
    B, T, _ = qlat.shape
    n_past = past_ckv.shape[1]
    M = N_HEADS * T
    per_b = lambda r, c: pl.BlockSpec((1, r, c), lambda b: (b, 0, 0))
    return pl.pallas_call(
        functools.partial(_attn_sample_kernel, tq=T, tk=tk, n_past=n_past),
        grid=(B,),
        in_specs=[per_b(T, N_HEADS * KV_LORA), per_b(T, 2 * LANES), per_b(T, 2 * LANES),
                  per_b(n_past, KV_LORA), per_b(n_past, QK_ROPE),
                  pl.BlockSpec((N_HEADS // 2, 2 * KV_LORA, LANES), lambda b: (0, 0, 0)),
                  pl.BlockSpec((1, ATTN_WIDTH), lambda b: (0, 0))],
        out_specs=per_b(T, ATTN_WIDTH),
        out_shape=jax.ShapeDtypeStruct((B, T, ATTN_WIDTH), BF16),
        scratch_shapes=[pltpu.VMEM((M, 2 * LANES), BF16), pltpu.VMEM((M, LANES), F32),
                        pltpu.VMEM((M, LANES), F32), pltpu.VMEM((M, KV_LORA), F32)],
        compiler_params=_cparams(("arbitrary",)),
        name="attn_sample",
    )(qlat, qrope, kcat, past_ckv, past_krope, w["w_uv_pair"], w["g_oa"])


def _outproj_kernel(x_ref, an_ref, bn_ref, gt1_ref, sc2_ref, sh2_ref, wo_ref, l1g_ref, l1b_ref,
                    rw_ref, rb_ref, h2_all_ref, x1_ref, h2_ref, meta_ref, gates_ref, cnt_ref, run_s,
                    *, tm, alpha):
    del h2_all_ref
    first = (pl.program_id(0) == 0) & (pl.program_id(1) == 0)

    @pl.when(first)
    def _():
        run_s[...] = jnp.zeros(run_s.shape, F32)

    m = (jnp.dot(an_ref[0], wo_ref[0:CONV_WIDTH, :], preferred_element_type=F32)
         + jnp.dot(bn_ref[0], wo_ref[CONV_WIDTH:, :], preferred_element_type=F32))
    x1 = _layernorm(alpha * x_ref[0] + gt1_ref[0] * m) * l1g_ref[...] + l1b_ref[...]
    x1_ref[0] = x1
    h2 = _layernorm(x1) * (1.0 + sc2_ref[0]) + sh2_ref[0]
    _rows_to_tiles(h2_ref, h2)

    logits = jnp.dot(h2.astype(BF16), rw_ref[...], preferred_element_type=F32) + rb_ref[...]
    lane = lax.broadcasted_iota(jnp.int32, (tm, LANES), 1)
    lane_f = lane.astype(F32)
    lg = logits
    vals, sels = [], []
    chosen = jnp.zeros((tm, LANES), F32)
    for _ in range(TOP_K):
        mx = jnp.max(lg, axis=1, keepdims=True)
        idx = jnp.min(jnp.where(lg == mx, lane_f, float(LANES)), axis=1, keepdims=True)
        sel = lane_f == idx
        vals.append(mx)
        sels.append(idx)
        chosen = jnp.where(sel, 1.0, chosen)
        lg = jnp.where(sel, NEG, lg)

    es = [jnp.exp(v - vals[0]) for v in vals]
    denom = es[0] + es[1] + es[2] + es[3]

    run_s[0:1, :] = run_s[0:1, :] + jnp.sum(chosen, axis=0, keepdims=True)
    cnt_ref[...] = jnp.broadcast_to(run_s[0:1, :], cnt_ref.shape)

    meta = jnp.zeros((tm, LANES), jnp.int32)
    gates = jnp.zeros((tm, LANES), F32)
    for k in range(TOP_K):
        meta = jnp.where(lane == k, sels[k].astype(jnp.int32), meta)
        gates = jnp.where(lane == k, es[k] / denom, gates)
    meta_ref[0] = meta
    gates_ref[0] = gates


def _outproj(x, a_n, b_n, gt1, sc2, sh2, h2_all, tok0, w, tm, alpha):
    B, T, _ = x.shape
    blk0 = tok0 // tm
    tile = lambda c: pl.BlockSpec((1, tm, c), lambda b, j: (b, j, 0))
    modv = pl.BlockSpec((1, 1, D_MODEL), lambda b, j: (b, 0, 0))
    full = lambda shp: pl.BlockSpec(shp, lambda b, j: (0,) * len(shp))
    return pl.pallas_call(
        functools.partial(_outproj_kernel, tm=tm, alpha=alpha),
        grid=(B, T // tm),
        in_specs=[tile(D_MODEL), tile(CONV_WIDTH), tile(ATTN_WIDTH), modv, modv, modv,
                  full((CONV_WIDTH + ATTN_WIDTH, D_MODEL)), full((1, D_MODEL)), full((1, D_MODEL)),
                  full((D_MODEL, LANES)), full((1, LANES)), pl.BlockSpec(memory_space=pl.ANY)],
        out_specs=(tile(D_MODEL),
                   pl.BlockSpec((tm * ROW_TILE, LANES), lambda b, j: (blk0 + b * (T // tm) + j, 0)),
                   tile(LANES), tile(LANES), pl.BlockSpec((8, LANES), lambda b, j: (0, 0))),
        out_shape=(jax.ShapeDtypeStruct((B, T, D_MODEL), F32),
                   jax.ShapeDtypeStruct(h2_all.shape, F32),
                   jax.ShapeDtypeStruct((B, T, LANES), jnp.int32), jax.ShapeDtypeStruct((B, T, LANES), F32),
                   jax.ShapeDtypeStruct((8, LANES), F32)),
        scratch_shapes=[pltpu.VMEM((8, LANES), F32)],
        input_output_aliases={11: 1},
        compiler_params=_cparams(("arbitrary", "arbitrary")),
        name="outproj",
    )(x, a_n, b_n, gt1, sc2, sh2, w["w_out"], w["ln1_g"], w["ln1_b"], w["router_w"], w["router_b"], h2_all)


def _expert_kernel(order_ref, blk_e_ref, blk_j0_ref, blk_nv_ref, nxt_e_ref, used_ref,
                   h_ref, wgu_hbm, bgu_ref, wd_hbm, bd_ref, y_ref,
                   x0, x1, y0, y1, xb, act_s, wgu_st, wd_st, wgu_bf, wd_bf, gsem, ssem, wsem, *, bm, n_tok):
    s = pl.program_id(0)
    used = used_ref[0]
    dump0 = TOP_K * n_tok

    def weight_copies(e):
        return (pltpu.make_async_copy(wgu_hbm.at[e], wgu_st, wsem.at[0]),
                pltpu.make_async_copy(wd_hbm.at[e], wd_st, wsem.at[1]))

    tok_bits = (n_tok - 1).bit_length()

    def gather_start(j0, xbuf, sem, r):
        tok = jnp.bitwise_and(order_ref[j0 + r], (1 << tok_bits) - 1)
        pltpu.make_async_copy(h_ref.at[tok], xbuf.at[pl.ds(r * ROW_TILE, ROW_TILE)], sem).start()

    def scatter_start(j0, nv, ybuf, sem, r, odd):
        real = lax.shift_right_logical(order_ref[j0 + r], tok_bits)
        dump = dump0 + odd * bm + r
        valid = jnp.right_shift(r - nv, 31)
        dst = dump + jnp.bitwise_and(valid, real - dump)
        pltpu.make_async_copy(ybuf.at[pl.ds(r * ROW_TILE, ROW_TILE)], y_ref.at[dst], sem).start()

    def block_wait(buf, sem):
        pltpu.make_async_copy(buf, buf, sem).wait()

    def switch_weights(b):
        prev = blk_e_ref[jnp.maximum(b - 1, 0)]
        e = blk_e_ref[b]

        @pl.when((b == 0) | (e != prev))
        def _():
            for c in weight_copies(e):
                c.wait()
            wgu_bf[...] = wgu_st[...].astype(BF16)
            wd_bf[...] = wd_st[...].astype(BF16)

            @pl.when(nxt_e_ref[e] >= 0)
            def _():
                for c in weight_copies(nxt_e_ref[e]):
                    c.start()

    def block(b, x_in, y_out, g_next, x_next, gsem_next, s_prev, y_prev, ssem_prev, prev_odd, wait_y_free):
        gj0 = blk_j0_ref[g_next + 1]
        sj0 = blk_j0_ref[s_prev + 1]
        snv = blk_nv_ref[s_prev + 1]
        def issue(k, after=None):
            zero = 0
            if after is not None:
                bits = pltpu.bitcast(jnp.abs(after[0:8, 0:LANES]), jnp.int32)
                zero = jnp.minimum(bits[0, 0], 0)
            half = DMA_BATCHES // 2
            q = k % half
            for r in range(q * bm // half, (q + 1) * bm // half):
                if k < half:
                    gather_start(gj0 + zero, x_next, gsem_next, r)
                else:
                    scatter_start(sj0 + zero, snv, y_prev, ssem_prev, r, prev_odd)

        xb[...] = _rows_from_tiles(x_in, bm).astype(BF16)
        e = blk_e_ref[b]
        cw = D_FF // 4
        prev = None
        for c in range(4):
            issue(2 * c, prev)
            gs, ls = slice(c * cw, (c + 1) * cw), slice(D_FF + c * cw, D_FF + (c + 1) * cw)
            g = jnp.dot(xb[...], wgu_bf[:, gs], preferred_element_type=F32) + bgu_ref[e][:, gs]
            issue(2 * c + 1, g)
            lin = jnp.dot(xb[...], wgu_bf[:, ls], preferred_element_type=F32) + bgu_ref[e][:, ls]
            prev = lin
            g = jnp.minimum(g, SWIGLU_LIMIT)
            lin = jnp.clip(lin, -SWIGLU_LIMIT, SWIGLU_LIMIT)
            act_s[:, gs] = (g * _sigmoid(SWIGLU_ALPHA * g) * (lin + 1.0)).astype(BF16)
        hw = D_MODEL // 2
        wait_y_free()
        for h in range(2):
            issue(8 + h, prev)
            y = jnp.dot(act_s[...], wd_bf[:, h * hw:(h + 1) * hw], preferred_element_type=F32)
            y = y + bd_ref[e][:, h * hw:(h + 1) * hw]
            prev = y
            for c in range(hw // LANES):
                y_out[pl.ds(h * (hw // LANES) + c, bm, stride=ROW_TILE), :] = y[:, c * LANES:(c + 1) * LANES]

    b0 = 2 * s
    b1 = b0 + 1

    @pl.when(b0 < used)
    def _():
        @pl.when(s == 0)
        def _():
            for c in weight_copies(blk_e_ref[0]):
                c.start()
            y1[...] = jnp.zeros(y1.shape, F32)

            def first(r, c):
                scatter_start(0, 0, y1, ssem.at[0], r, 0)
                gather_start(blk_j0_ref[1], x0, gsem.at[0], r)
                return c

            lax.fori_loop(0, bm, first, 0)
            block_wait(y1, ssem.at[0])

        def y0_free():
            @pl.when(s > 0)
            def _():
                block_wait(y0, ssem.at[0])

        def y1_free():
            block_wait(y1, ssem.at[1])

        switch_weights(b0)
        block_wait(x0, gsem.at[0])
        block(b0, x0, y0, b1, x1, gsem.at[1], b0 - 1, y1, ssem.at[1], 1, y0_free)

        switch_weights(b1)
        block_wait(x1, gsem.at[1])
        block(b1, x1, y1, b0 + 2, x0, gsem.at[0], b0, y0, ssem.at[0], 0, y1_free)

        @pl.when(b0 + 2 >= used)
        def _():
            def last(r, c):
                scatter_start(blk_j0_ref[b1 + 1], blk_nv_ref[b1 + 1], y1, ssem.at[1], r, 1)
                return c

            lax.fori_loop(0, bm, last, 0)
            block_wait(y0, ssem.at[0])
            block_wait(y1, ssem.at[1])
            block_wait(x0, gsem.at[0])


def _expert(order, blk_e, blk_j0, blk_nv, nxt_e, used, h2, w, bm):
    n_tok = h2.shape[0]
    nblk = blk_e.shape[0]
    f32buf = lambda shp: pltpu.VMEM(shp, F32)
    return pl.pallas_call(
        functools.partial(_expert_kernel, bm=bm, n_tok=n_tok),
        grid_spec=pltpu.PrefetchScalarGridSpec(
            num_scalar_prefetch=6,
            grid=(nblk // 2,),
            in_specs=[
                pl.BlockSpec(memory_space=pl.ANY),
                pl.BlockSpec(memory_space=pl.ANY),
                pl.BlockSpec((N_EXPERTS, 1, 2 * D_FF), lambda s, *_: (0, 0, 0)),
                pl.BlockSpec(memory_space=pl.ANY),
                pl.BlockSpec((N_EXPERTS, 1, D_MODEL), lambda s, *_: (0, 0, 0)),
            ],
            out_specs=pl.BlockSpec(memory_space=pl.ANY),
            scratch_shapes=[f32buf((bm * ROW_TILE, LANES)), f32buf((bm * ROW_TILE, LANES)),
                            f32buf((bm * ROW_TILE, LANES)), f32buf((bm * ROW_TILE, LANES)),
                            pltpu.VMEM((bm, D_MODEL), BF16), pltpu.VMEM((bm, D_FF), BF16),
                            f32buf((D_MODEL, 2 * D_FF)), f32buf((D_FF, D_MODEL)),
                            pltpu.VMEM((D_MODEL, 2 * D_FF), BF16), pltpu.VMEM((D_FF, D_MODEL), BF16),
                            pltpu.SemaphoreType.DMA((2,)), pltpu.SemaphoreType.DMA((2,)),
                            pltpu.SemaphoreType.DMA((2,))],
        ),
        out_shape=jax.ShapeDtypeStruct((TOP_K * n_tok + 2 * bm, ROW_TILE, LANES), F32),
        compiler_params=_cparams(("arbitrary",)),
        name="moe_expert",
    )(order, blk_e, blk_j0, blk_nv, nxt_e, used, h2, w["w_gu"], w["b_gu"], w["w_down"], w["b_down"])


def _combine_kernel(y0_ref, y1_ref, y2_ref, y3_ref, x1_ref, gates_ref, gt2_ref, l2g_ref, l2b_ref, o_ref, *, alpha):
    gates = gates_ref[...]
    tm = gates.shape[0]
    f = gates[:, 0:1] * _rows_from_tiles(y0_ref, tm)
    for k, y_ref in enumerate((y1_ref, y2_ref, y3_ref), start=1):
        f = f + gates[:, k:k + 1] * _rows_from_tiles(y_ref, tm)
    o_ref[...] = _layernorm(alpha * x1_ref[...] + gt2_ref[0] * f) * l2g_ref[...] + l2b_ref[...]


def _combine(y_rows, x1, gates, gt2, w, tm, tokens_per_batch, alpha, n_tok, tok0):
    N = x1.shape[0]
    per_b = tokens_per_batch // tm
    nt = N // tm
    assert n_tok % tm == 0 and tok0 % tm == 0
    slot = lambda k: pl.BlockSpec((tm * ROW_TILE, LANES), lambda i: ((k * n_tok + tok0) // tm + i, 0))
    return pl.pallas_call(
        functools.partial(_combine_kernel, alpha=alpha),
        grid=(nt,),
        in_specs=[slot(0), slot(1), slot(2), slot(3),
                  pl.BlockSpec((tm, D_MODEL), lambda i: (i, 0)),
                  pl.BlockSpec((tm, LANES), lambda i: (i, 0)),
                  pl.BlockSpec((1, 1, D_MODEL), lambda i: (i // per_b, 0, 0)),
                  pl.BlockSpec((1, D_MODEL), lambda i: (0, 0)),
                  pl.BlockSpec((1, D_MODEL), lambda i: (0, 0))],
        out_specs=pl.BlockSpec((tm, D_MODEL), lambda i: (i, 0)),
        out_shape=jax.ShapeDtypeStruct((N, D_MODEL), F32),
        compiler_params=_cparams(("arbitrary",)),
        name="moe_combine",
    )(y_rows, y_rows, y_rows, y_rows, x1, gates, gt2, w["ln2_g"], w["ln2_b"])


def _moe_experts(h2, idx, counts, w, bm):
    N = h2.shape[0]
    n_rows = N * TOP_K
    nblk = (n_rows + N_EXPERTS * (bm - 1)) // bm
    nblk += nblk % 2
    experts = jnp.arange(N_EXPERTS, dtype=jnp.int32)
    id_bits = (n_rows - 1).bit_length()
    tok_bits = (N - 1).bit_length()
    assert tok_bits + (TOP_K * N - 1).bit_length() <= 32
    keys = jnp.left_shift(idx.reshape(-1), id_bits) + jnp.arange(n_rows, dtype=jnp.int32)
    keys = jnp.concatenate([keys, jnp.full(((1 << id_bits) - n_rows,), jnp.iinfo(jnp.int32).max, jnp.int32)])
    flat = jnp.bitwise_and(jnp.sort(keys)[:n_rows], (1 << id_bits) - 1).astype(jnp.uint32)
    tok = jnp.right_shift(flat, 2)
    row = jnp.bitwise_and(flat, TOP_K - 1) * N + tok
    order = lax.bitcast_convert_type(jnp.left_shift(row, tok_bits) | tok, jnp.int32)
    order = jnp.concatenate([order, jnp.zeros((bm,), jnp.int32)])
    nb_e = (counts + bm - 1) // bm
    blk_end = jnp.cumsum(nb_e)
    first_blk = blk_end - nb_e
    start_sorted = jnp.cumsum(counts) - counts
    used = blk_end[-1].astype(jnp.int32)
    b = jnp.arange(-1, nblk + 1, dtype=jnp.int32)
    bc = jnp.clip(b, 0, used - 1)
    e = jnp.minimum(jnp.sum(blk_end[None, :] <= bc[:, None], axis=1), N_EXPERTS - 1).astype(jnp.int32)
    pick = lambda table: jnp.sum(jnp.where(e[:, None] == experts, table, 0), axis=1)
    local = bc - pick(first_blk)
    blk_j0 = (pick(start_sorted) + local * bm).astype(jnp.int32)
    blk_nv = jnp.where((b >= 0) & (b < used), jnp.minimum(bm, pick(counts) - local * bm), 0).astype(jnp.int32)
    blk_e = e[1:nblk + 1]
    later = (experts[None, :] > experts[:, None]) & (counts[None, :] > 0)
    nxt = jnp.min(jnp.where(later, experts[None, :], N_EXPERTS), axis=1)
    nxt_e = jnp.where(nxt < N_EXPERTS, nxt, -1).astype(jnp.int32)
    y_rows = _expert(order, blk_e, blk_j0, blk_nv, nxt_e, used.reshape(1), h2, w, bm)
    return y_rows.reshape(-1, LANES)


def _rope_tables(pos):
    half = QK_ROPE // 2
    inv = ROPE_THETA ** (-jnp.arange(half, dtype=F32) / half)
    ang = pos.astype(F32)[:, None] * inv[None, :]
    cos, sin = jnp.cos(ang), jnp.sin(ang)
    cos32 = jnp.concatenate([cos, cos], axis=1)
    sin32 = jnp.concatenate([-sin, sin], axis=1)
    return jnp.tile(cos32, (1, LANES // QK_ROPE)), jnp.tile(sin32, (1, LANES // QK_ROPE))


def _swap_halves(w32):
    shp = w32.shape
    w = w32.reshape(shp[:-1] + (shp[-1] // QK_ROPE, 2, QK_ROPE // 2))
    return w[..., ::-1, :].reshape(shp)


def _prep_weights(l, w_in, conv_w, g_qa, w_qb, g_kva, w_kvb, g_out_conv, g_out_attn, w_out,
                  ln1_g, ln1_b, router_w, router_b, w_gu, b_gu, w_down, b_down, ln2_g, ln2_b):
    w = {}
    wi = w_in[l]
    k_r = wi[:, _O_KR:_O_KR + QK_ROPE]
    rep = LANES // QK_ROPE
    w["w_in"] = jnp.concatenate([wi[:, :_O_KR], jnp.tile(k_r, (1, rep)), jnp.tile(_swap_halves(k_r), (1, rep))],
                                axis=1).astype(BF16)
    w["conv_w"] = conv_w[l]
    w["g_qa"] = g_qa[l].reshape(1, Q_LORA)
    w["g_kva"] = g_kva[l].reshape(1, KV_LORA)
    w["g_oc"] = g_out_conv[l].reshape(1, CONV_WIDTH)
    w["g_oa"] = g_out_attn[l].reshape(1, ATTN_WIDTH)
    wq = w_qb[l].reshape(Q_LORA, N_HEADS, QK_NOPE + QK_ROPE)
    w["wq_nope"] = wq[:, :, :QK_NOPE].reshape(Q_LORA, N_HEADS * QK_NOPE).astype(BF16)
    wq_rope = wq[:, :, QK_NOPE:].reshape(Q_LORA, N_HEADS * QK_ROPE)
    w["wq_rope"] = wq_rope.astype(BF16)
    w["wq_rope_sw"] = _swap_halves(wq_rope).astype(BF16)
    w_uk = jnp.transpose(w_kvb[l][:, :, :QK_NOPE], (1, 2, 0))
    w_uv = jnp.transpose(w_kvb[l][:, :, QK_NOPE:], (1, 0, 2))
    zk = jnp.zeros((QK_NOPE, KV_LORA), F32)
    zv = jnp.zeros((KV_LORA, V_HEAD), F32)
    w["w_uk_pair"] = jnp.stack([
        jnp.concatenate([jnp.concatenate([w_uk[2 * p], zk], axis=1),
                         jnp.concatenate([zk, w_uk[2 * p + 1]], axis=1)], axis=0)
        for p in range(N_HEADS // 2)]).astype(BF16)
    w["w_uv_pair"] = jnp.stack([
        jnp.concatenate([jnp.concatenate([w_uv[2 * p], zv], axis=1),
                         jnp.concatenate([zv, w_uv[2 * p + 1]], axis=1)], axis=0)
        for p in range(N_HEADS // 2)]).astype(BF16)
    w["w_out"] = w_out[l].astype(BF16)
    w["ln1_g"] = ln1_g[l].reshape(1, D_MODEL)
    w["ln1_b"] = ln1_b[l].reshape(1, D_MODEL)
    w["ln2_g"] = ln2_g[l].reshape(1, D_MODEL)
    w["ln2_b"] = ln2_b[l].reshape(1, D_MODEL)
    w["router_w"] = jnp.pad(router_w[l], ((0, 0), (0, LANES - N_EXPERTS))).astype(BF16)
    w["router_b"] = jnp.concatenate([router_b[l], jnp.full((LANES - N_EXPERTS,), NEG, F32)]).reshape(1, LANES)
    w["w_gu"] = w_gu[l]
    w["b_gu"] = b_gu[l].reshape(N_EXPERTS, 1, 2 * D_FF)
    w["w_down"] = w_down[l]
    w["b_down"] = b_down[l].reshape(N_EXPERTS, 1, D_MODEL)
    return w


def _mixer(x, mod, conv_prev, past, pos0, h2_all, tok0, w, alpha, *, tm_in, tm_out, tq=128, tk=512):
    B, T, _ = x.shape
    sh1, sc1, gt1, sh2, sc2, gt2 = [mod[:, None, i * D_MODEL:(i + 1) * D_MODEL] for i in range(N_MOD)]
    cos_t, sin_t = _rope_tables(pos0 + jnp.arange(T, dtype=jnp.int32))
    a_n, qlat, qrope, kcat, ckv, krope, conv_new, *vt = _inproj(x, sc1, sh1, conv_prev, cos_t, sin_t, w, tm_in,
                                                                 with_vt=past is None)
    if past is None:
        b_n = _attn_prompt(qlat, qrope, kcat, vt[0], w, tq, tk)
    else:
        b_n = _attn_sample(qlat, qrope, kcat, past[0], past[1], w, tk)
    x1, h2_all, meta, gates, cnt = _outproj(x, a_n, b_n, gt1, sc2, sh2, h2_all, tok0, w, tm_out, alpha)
    N = B * T
    route = dict(x1=x1.reshape(N, D_MODEL), idx=meta.reshape(N, LANES)[:, :TOP_K], gates=gates.reshape(N, LANES),
                 counts=cnt[0, :N_EXPERTS].astype(jnp.int32), gt2=gt2, shape=(B, T))
    return h2_all, route, ckv, krope, conv_new


def kernel(x_prompt, x_sample, c_prompt, c_sample, cache_ckv, cache_krope, state_conv, w_ada, b_ada, w_in, conv_w, g_qa, w_qb, g_kva, w_kvb, g_out_conv, g_out_attn, w_out, ln1_g, ln1_b, router_w, router_b, w_gu, b_gu, w_down, b_down, ln2_g, ln2_b):
    depth = w_ada.shape[0]
    Bp, Tp, _ = x_prompt.shape
    Bs, Ts, _ = x_sample.shape
    past_len = cache_ckv.shape[2]
    assert Ts == CHUNK and past_len % CHUNK == 0 and Tp % 512 == 0
    alpha = (2.0 * depth) ** 0.25
    xp, xs = x_prompt, x_sample
    outs = [[] for _ in range(6)]
    c_all = jnp.concatenate([c_prompt, c_sample, jnp.zeros((16 - Bp - Bs, D_MODEL), F32)], axis=0)
    for l in range(depth):
        w = _prep_weights(l, w_in, conv_w, g_qa, w_qb, g_kva, w_kvb, g_out_conv, g_out_attn, w_out,
                          ln1_g, ln1_b, router_w, router_b, w_gu, b_gu, w_down, b_down, ln2_g, ln2_b)
        mod = _ada(c_all, w_ada[l], b_ada[l])
        n_p, n_s = Bp * Tp, Bs * Ts
        n_tok = n_p + n_s
        h2_all = jnp.zeros((n_tok * ROW_TILE, LANES), F32)
        h2_all, rp, ckv_p, kr_p, cv_p = _mixer(xp, mod[:Bp], jnp.zeros((Bp, CONV_K - 1, CONV_WIDTH), F32), None, 0,
                                               h2_all, 0, w, alpha, tm_in=1024, tm_out=512)
        h2_all, rs, ckv_s, kr_s, cv_s = _mixer(xs, mod[Bp:Bp + Bs], state_conv[l], (cache_ckv[l], cache_krope[l]),
                                               past_len, h2_all, n_p, w, alpha, tm_in=Ts, tm_out=Ts)
        y_rows = _moe_experts(h2_all.reshape(n_tok, ROW_TILE, LANES), jnp.concatenate([rp["idx"], rs["idx"]]),
                              rp["counts"] + rs["counts"], w, bm=256)
        xp = _combine(y_rows, rp["x1"], rp["gates"], rp["gt2"], w, 256, Tp, alpha, n_tok, 0).reshape(rp["shape"] + (D_MODEL,))
        xs = _combine(y_rows, rs["x1"], rs["gates"], rs["gt2"], w, Ts, Ts, alpha, n_tok, n_p).reshape(rs["shape"] + (D_MODEL,))
        for o, v in zip(outs, (ckv_p, kr_p, cv_p, ckv_s, kr_s, cv_s)):
            o.append(v)
    return (xp, xs) + tuple(jnp.stack(o) for o in outs)
```

```python
import functools
import math

import jax
import jax.numpy as jnp
from jax import lax
from jax.experimental import pallas as pl
from jax.experimental.pallas import tpu as pltpu

F32 = jnp.float32
BF16 = jnp.bfloat16

D_MODEL = 1024
CHUNK = 64
CONV_WIDTH = 512
CONV_K = 3
N_HEADS = 8
QK_NOPE = 64
QK_ROPE = 32
V_HEAD = 64
Q_LORA = 256
KV_LORA = 128
ATTN_WIDTH = N_HEADS * V_HEAD
ROPE_THETA = 10000.0
ATTN_SCALE = 1.0 / math.sqrt(QK_NOPE + QK_ROPE)
Q_SCALE = ATTN_SCALE * math.log2(math.e)
N_EXPERTS = 32
TOP_K = 4
D_FF = 1024
SWIGLU_LIMIT = 7.0
SWIGLU_ALPHA = 1.702
N_MOD = 6
LN_EPS = 1e-5
RMS_EPS = 1e-6

LANES = 128
ATTN_GROUPS = 1
DMA_BATCHES = 20
NEG = -1e30
VMEM_LIMIT = 56 * 1024 * 1024

_O_XB, _O_XC, _O_XV = 0, CONV_WIDTH, 2 * CONV_WIDTH
_O_QA = 3 * CONV_WIDTH
_O_KVA = _O_QA + Q_LORA
_O_KR = _O_KVA + KV_LORA
_O_KRS = _O_KR + LANES
IN_COLS_EXT = _O_KRS + LANES


def _cparams(sem):
    return pltpu.CompilerParams(dimension_semantics=sem, vmem_limit_bytes=VMEM_LIMIT)


def _layernorm(x):
    mu = jnp.mean(x, axis=-1, keepdims=True)
    xc = x - mu
    var = jnp.mean(xc * xc, axis=-1, keepdims=True)
    return xc * lax.rsqrt(var + LN_EPS)


def _rms(x):
    return x * lax.rsqrt(jnp.mean(x * x, axis=-1, keepdims=True) + RMS_EPS)


def _sigmoid(x):
    return 1.0 / (1.0 + jnp.exp(-x))


ROW_TILE = D_MODEL // LANES


def _rows_from_tiles(ref, n):
    return jnp.concatenate([ref[pl.ds(c, n, stride=ROW_TILE), :] for c in range(ROW_TILE)], axis=1)


def _rows_to_tiles(ref, x):
    n = x.shape[0]
    for c in range(ROW_TILE):
        ref[pl.ds(c, n, stride=ROW_TILE), :] = x[:, c * LANES:(c + 1) * LANES]


def _ada_kernel(c_ref, w_ref, b_ref, o_ref):
    c = c_ref[...]
    s = (c * _sigmoid(c)).astype(BF16)
    o_ref[...] = jnp.dot(s, w_ref[...].astype(BF16), preferred_element_type=F32) + b_ref[...]


def _ada(c_all, w_ada, b_ada):
    rows = c_all.shape[0]
    ncol = w_ada.shape[1]
    tn = 1024
    return pl.pallas_call(
        _ada_kernel,
        grid=(ncol // tn,),
        in_specs=[pl.BlockSpec((rows, D_MODEL), lambda j: (0, 0)),
                  pl.BlockSpec((D_MODEL, tn), lambda j: (0, j)),
                  pl.BlockSpec((1, tn), lambda j: (0, j))],
        out_specs=pl.BlockSpec((rows, tn), lambda j: (0, j)),
        out_shape=jax.ShapeDtypeStruct((rows, ncol), F32),
        compiler_params=_cparams(("arbitrary",)),
        name="ada",
    )(c_all, w_ada, b_ada.reshape(1, ncol))


def _inproj_kernel(x_ref, sc_ref, sh_ref, win_ref, cw_ref, cprev_ref, gqa_ref, gkva_ref, goc_ref,
                   wqn_ref, wqr_ref, wqrs_ref, wuk_ref, cos_ref, sin_ref,
                   an_ref, qlat_ref, qrope_ref, kcat_ref, ckv_ref, krope_ref, cnew_ref,
                   *rest, tm, with_vt):
    vt_ref, ubuf = rest if with_vt else (None, rest[0])
    j = pl.program_id(1)
    x = x_ref[0]
    h = _layernorm(x) * (1.0 + sc_ref[0]) + sh_ref[0]
    proj = jnp.dot(h.astype(BF16), win_ref[...], preferred_element_type=F32)
    xb = proj[:, _O_XB:_O_XB + CONV_WIDTH]
    xc = proj[:, _O_XC:_O_XC + CONV_WIDTH]
    xv = proj[:, _O_XV:_O_XV + CONV_WIDTH]
    q_a = proj[:, _O_QA:_O_QA + Q_LORA]
    kv_a = proj[:, _O_KVA:_O_KVA + KV_LORA]
    kr4 = proj[:, _O_KR:_O_KR + LANES]
    kr4s = proj[:, _O_KRS:_O_KRS + LANES]

    u = xc * xv

    @pl.when(j == 0)
    def _():
        ubuf[6:8, :] = cprev_ref[0]

    ubuf[8:8 + tm, :] = u
    conv = (cw_ref[0:1, :] * ubuf[6:6 + tm, :] + cw_ref[1:2, :] * ubuf[7:7 + tm, :]
            + cw_ref[2:3, :] * u)
    ubuf[0:8, :] = ubuf[tm:tm + 8, :]
    cnew_ref[0] = u[tm - (CONV_K - 1):tm, :]
    an_ref[0] = (_rms(xb * conv) * goc_ref[...]).astype(BF16)

    cos = cos_ref[...]
    sin = sin_ref[...]

    ckv = _rms(kv_a) * gkva_ref[...]
    kro4 = kr4 * cos + kr4s * sin
    ckv_ref[0] = ckv
    krope_ref[0] = kro4[:, :QK_ROPE]
    kcat_ref[0] = jnp.concatenate([ckv, kro4], axis=1).astype(BF16)
    if with_vt:
        vt_ref[0] = ckv.T.astype(BF16)

    qn = (_rms(q_a) * gqa_ref[...]).astype(BF16)
    q_nope = jnp.dot(qn, wqn_ref[...], preferred_element_type=F32)
    xr = jnp.dot(qn, wqr_ref[...], preferred_element_type=F32)
    xrs = jnp.dot(qn, wqrs_ref[...], preferred_element_type=F32)
    for g in range(2):
        sl = slice(g * LANES, (g + 1) * LANES)
        qrope_ref[0, :, sl] = ((xr[:, sl] * cos + xrs[:, sl] * sin) * Q_SCALE).astype(BF16)
    for p in range(N_HEADS // 2):
        qp = q_nope[:, p * LANES:(p + 1) * LANES].astype(BF16)
        ql = jnp.dot(qp, wuk_ref[p], preferred_element_type=F32)
        qlat_ref[0, :, p * 2 * KV_LORA:(p + 1) * 2 * KV_LORA] = (ql * Q_SCALE).astype(BF16)


def _inproj(x, sc1, sh1, conv_prev, cos_t, sin_t, w, tm, with_vt):
    B, T, _ = x.shape
    nt = T // tm
    full = lambda shp: pl.BlockSpec(shp, lambda b, j: (0,) * len(shp))
    vt_shape = (jax.ShapeDtypeStruct((B, KV_LORA, T), BF16),) if with_vt else ()
    vt_spec = (pl.BlockSpec((1, KV_LORA, tm), lambda b, j: (b, 0, j)),) if with_vt else ()
    out_shapes = (
        jax.ShapeDtypeStruct((B, T, CONV_WIDTH), BF16),
        jax.ShapeDtypeStruct((B, T, N_HEADS * KV_LORA), BF16),
        jax.ShapeDtypeStruct((B, T, 2 * LANES), BF16),
        jax.ShapeDtypeStruct((B, T, 2 * LANES), BF16),
        jax.ShapeDtypeStruct((B, T, KV_LORA), F32),
        jax.ShapeDtypeStruct((B, T, QK_ROPE), F32),
        jax.ShapeDtypeStruct((B, CONV_K - 1, CONV_WIDTH), F32),
    ) + vt_shape
    tile = lambda c: pl.BlockSpec((1, tm, c), lambda b, j: (b, j, 0))
    return pl.pallas_call(
        functools.partial(_inproj_kernel, tm=tm, with_vt=with_vt),
        grid=(B, nt),
        in_specs=[
            tile(D_MODEL),
            pl.BlockSpec((1, 1, D_MODEL), lambda b, j: (b, 0, 0)),
            pl.BlockSpec((1, 1, D_MODEL), lambda b, j: (b, 0, 0)),
            full((D_MODEL, IN_COLS_EXT)),
            full((CONV_K, CONV_WIDTH)),
            pl.BlockSpec((1, CONV_K - 1, CONV_WIDTH), lambda b, j: (b, 0, 0)),
            full((1, Q_LORA)), full((1, KV_LORA)), full((1, CONV_WIDTH)),
            full((Q_LORA, N_HEADS * QK_NOPE)), full((Q_LORA, 2 * LANES)), full((Q_LORA, 2 * LANES)),
            full((N_HEADS // 2, LANES, 2 * KV_LORA)),
            pl.BlockSpec((tm, LANES), lambda b, j: (j, 0)),
            pl.BlockSpec((tm, LANES), lambda b, j: (j, 0)),
        ],
        out_specs=(tile(CONV_WIDTH), tile(N_HEADS * KV_LORA), tile(2 * LANES), tile(2 * LANES),
                   tile(KV_LORA), tile(QK_ROPE),
                   pl.BlockSpec((1, CONV_K - 1, CONV_WIDTH), lambda b, j: (b, 0, 0))) + vt_spec,
        out_shape=out_shapes,
        scratch_shapes=[pltpu.VMEM((tm + 8, CONV_WIDTH), F32)],
        compiler_params=_cparams(("arbitrary", "arbitrary")),
        name="inproj",
    )(x, sc1, sh1, w["w_in"], w["conv_w"], conv_prev, w["g_qa"], w["g_kva"], w["g_oc"],
      w["wq_nope"], w["wq_rope"], w["wq_rope_sw"], w["w_uk_pair"], cos_t, sin_t)


def _stack_queries(qlat_ref, qrope_ref, qs, tq):
    lane = lax.broadcasted_iota(jnp.int32, (tq, LANES), 1)
    for h in range(N_HEADS):
        g, i = divmod(h, 4)
        rope = qrope_ref[0, :, g * LANES:(g + 1) * LANES]
        keep = (lane >= i * QK_ROPE) & (lane < (i + 1) * QK_ROPE)
        qs[h * tq:(h + 1) * tq, 0:KV_LORA] = qlat_ref[0, :, h * KV_LORA:(h + 1) * KV_LORA]
        qs[h * tq:(h + 1) * tq, KV_LORA:KV_LORA + LANES] = jnp.where(keep, rope, jnp.zeros_like(rope))


def _softmax_step(qs, k, v, m_s, l_s, acc_s, mask=None, groups=1):
    tk = k.shape[0]
    rows = qs.shape[0] // groups
    for g in range(groups):
        r = slice(g * rows, (g + 1) * rows)
        s = lax.dot_general(qs[r, :], k, (((1,), (1,)), ((), ())), preferred_element_type=F32)
        if mask is not None:
            col, limit = mask
            s = jnp.where(col < limit[r], s, NEG)
        m_prev = m_s[r, :]
        m_new = jnp.maximum(m_prev, jnp.max(s, axis=1, keepdims=True))
        alpha = jnp.exp2(m_prev - m_new)
        if tk % LANES == 0:
            p = jnp.exp2(s - jnp.tile(m_new, (1, tk // LANES)))
        else:
            p = jnp.exp2(s - m_new[:, :tk])
        l_s[r, :] = alpha * l_s[r, :] + jnp.sum(p, axis=1, keepdims=True)
        acc_s[r, :] = alpha * acc_s[r, :] + jnp.dot(p.astype(BF16), v, preferred_element_type=F32)
        m_s[r, :] = m_new


def _attn_epilogue(acc_s, l_s, wuv_ref, g_ref, o_ref, tq):
    o = acc_s[...] / l_s[...]
    parts = []
    for p in range(N_HEADS // 2):
        op = jnp.concatenate([o[(2 * p) * tq:(2 * p + 1) * tq], o[(2 * p + 1) * tq:(2 * p + 2) * tq]], axis=1)
        parts.append(jnp.dot(op.astype(BF16), wuv_ref[p], preferred_element_type=F32))
    b = jnp.concatenate(parts, axis=1)
    o_ref[0] = (_rms(b) * g_ref[...]).astype(BF16)


def _attn_prompt_kernel(qlat_ref, qrope_ref, k_ref, vt_ref, wuvt_ref, g_ref, o_ref, qs, m_s, l_s, acc_s, sa, sb,
                        *, tq, tk):
    i = pl.program_id(1)
    M = N_HEADS * tq
    _stack_queries(qlat_ref, qrope_ref, qs, tq)
    m_s[...] = jnp.full(m_s.shape, NEG, F32)
    l_s[...] = jnp.zeros(l_s.shape, F32)
    acc_s[...] = jnp.zeros(acc_s.shape, F32)
    q0 = i * tq
    n_full = (q0 + CHUNK) // tk

    def scores(t, dst):
        k = k_ref[0, pl.ds(pl.multiple_of(t * tk, tk), tk), :]
        dst[...] = lax.dot_general(k, qs[...], (((1,), (1,)), ((), ())), preferred_element_type=F32)

    def update(t, src, limit=None):
        start = pl.multiple_of(t * tk, tk)
        vt = vt_ref[0, :, pl.ds(start, tk)]
        s = src[...]
        if limit is not None:
            kpos = start + lax.broadcasted_iota(jnp.int32, (tk, 1), 0)
            s = jnp.where(kpos < limit, s, NEG)
        m_prev = m_s[...]
        m_new = jnp.maximum(m_prev, jnp.max(s, axis=0, keepdims=True))
        alpha = jnp.exp2(m_prev - m_new)
        p = jnp.exp2(s - m_new)
        l_s[...] = alpha * l_s[...] + jnp.sum(p, axis=0, keepdims=True)
        acc_s[...] = alpha * acc_s[...] + jnp.dot(vt, p.astype(BF16), preferred_element_type=F32)
        m_s[...] = m_new

    scores(0, sa)

    def body(j, carry):
        t = 2 * j
        scores(t + 1, sb)
        update(t, sa)
        scores(t + 2, sa)
        update(t + 1, sb)
        return carry

    lax.fori_loop(0, n_full // 2, body, 0)

    col_t = jnp.bitwise_and(lax.broadcasted_iota(jnp.int32, (1, M), 1), tq - 1)
    limit = q0 + (jnp.right_shift(col_t, CHUNK.bit_length() - 1) + 1) * CHUNK
    odd = n_full % 2

    @pl.when(odd == 0)
    def _():
        update(n_full, sa, limit)

    @pl.when(odd == 1)
    def _():
        scores(n_full, sb)
        update(n_full - 1, sa)
        update(n_full, sb, limit)

    o_t = acc_s[...] / l_s[...]
    parts = []
    for p in range(N_HEADS // 2):
        pair = jnp.concatenate([o_t[:, (2 * p) * tq:(2 * p + 1) * tq], o_t[:, (2 * p + 1) * tq:(2 * p + 2) * tq]],
                               axis=0)
        parts.append(jnp.dot(wuvt_ref[p], pair.astype(BF16), preferred_element_type=F32))
    b = jnp.concatenate(parts, axis=0).T
    o_ref[0] = (_rms(b) * g_ref[...]).astype(BF16)


def _attn_prompt(qlat, qrope, kcat, vt, w, tq, tk):
    B, T, _ = qlat.shape
    M = N_HEADS * tq
    return pl.pallas_call(
        functools.partial(_attn_prompt_kernel, tq=tq, tk=tk),
        grid=(B, T // tq),
        in_specs=[
            pl.BlockSpec((1, tq, N_HEADS * KV_LORA), lambda b, i: (b, i, 0)),
            pl.BlockSpec((1, tq, 2 * LANES), lambda b, i: (b, i, 0)),
            pl.BlockSpec((1, T, 2 * LANES), lambda b, i: (b, 0, 0)),
            pl.BlockSpec((1, KV_LORA, T), lambda b, i: (b, 0, 0)),
            pl.BlockSpec((N_HEADS // 2, LANES, 2 * KV_LORA), lambda b, i: (0, 0, 0)),
            pl.BlockSpec((1, ATTN_WIDTH), lambda b, i: (0, 0)),
        ],
        out_specs=pl.BlockSpec((1, tq, ATTN_WIDTH), lambda b, i: (b, i, 0)),
        out_shape=jax.ShapeDtypeStruct((B, T, ATTN_WIDTH), BF16),
        scratch_shapes=[pltpu.VMEM((M, 2 * LANES), BF16), pltpu.VMEM((1, M), F32),
                        pltpu.VMEM((1, M), F32), pltpu.VMEM((KV_LORA, M), F32),
                        pltpu.VMEM((tk, M), F32), pltpu.VMEM((tk, M), F32)],
        compiler_params=_cparams(("arbitrary", "arbitrary")),
        name="attn_prompt",
    )(qlat, qrope, kcat, vt, jnp.swapaxes(w["w_uv_pair"], 1, 2), w["g_oa"])


def _attn_sample_kernel(qlat_ref, qrope_ref, knew_ref, pckv_ref, pkr_ref, wuv_ref, g_ref, o_ref,
                        qs, m_s, l_s, acc_s, *, tq, tk, n_past):
    _stack_queries(qlat_ref, qrope_ref, qs, tq)
    m_s[...] = jnp.full(m_s.shape, NEG, F32)
    l_s[...] = jnp.zeros(l_s.shape, F32)
    acc_s[...] = jnp.zeros(acc_s.shape, F32)

    def body(t, carry):
        start = pl.multiple_of(t * tk, tk)
        ck = pckv_ref[0, pl.ds(start, tk), :]
        kr = pkr_ref[0, pl.ds(start, tk), :]
        k = jnp.concatenate([ck, kr, kr, kr, kr], axis=1).astype(BF16)
        _softmax_step(qs, k, k[:, :KV_LORA], m_s, l_s, acc_s)
        return carry

    lax.fori_loop(0, n_past // tk, body, 0)
    k = knew_ref[0]
    _softmax_step(qs, k, k[:, :KV_LORA], m_s, l_s, acc_s)
    _attn_epilogue(acc_s, l_s, wuv_ref, g_ref, o_ref, tq)


def _attn_sample(qlat, qrope, kcat, past_ckv, past_krope, w, tk):
    B, T, _ = qlat.shape
    n_past = past_ckv.shape[1]
    M = N_HEADS * T
    per_b = lambda r, c: pl.BlockSpec((1, r, c), lambda b: (b, 0, 0))
    return pl.pallas_call(
        functools.partial(_attn_sample_kernel, tq=T, tk=tk, n_past=n_past),
        grid=(B,),
        in_specs=[per_b(T, N_HEADS * KV_LORA), per_b(T, 2 * LANES), per_b(T, 2 * LANES),
                  per_b(n_past, KV_LORA), per_b(n_past, QK_ROPE),
                  pl.BlockSpec((N_HEADS // 2, 2 * KV_LORA, LANES), lambda b: (0, 0, 0)),
                  pl.BlockSpec((1, ATTN_WIDTH), lambda b: (0, 0))],
        out_specs=per_b(T, ATTN_WIDTH),
        out_shape=jax.ShapeDtypeStruct((B, T, ATTN_WIDTH), BF16),
        scratch_shapes=[pltpu.VMEM((M, 2 * LANES), BF16), pltpu.VMEM((M, LANES), F32),
                        pltpu.VMEM((M, LANES), F32), pltpu.VMEM((M, KV_LORA), F32)],
        compiler_params=_cparams(("arbitrary",)),
        name="attn_sample",
    )(qlat, qrope, kcat, past_ckv, past_krope, w["w_uv_pair"], w["g_oa"])


def _outproj_kernel(x_ref, an_ref, bn_ref, gt1_ref, sc2_ref, sh2_ref, wo_ref, l1g_ref, l1b_ref,
                    rw_ref, rb_ref, h2_all_ref, x1_ref, h2_ref, meta_ref, gates_ref, cnt_ref, run_s,
                    *, tm, alpha):
    del h2_all_ref
    first = (pl.program_id(0) == 0) & (pl.program_id(1) == 0)

    @pl.when(first)
    def _():
        run_s[...] = jnp.zeros(run_s.shape, F32)

    m = (jnp.dot(an_ref[0], wo_ref[0:CONV_WIDTH, :], preferred_element_type=F32)
         + jnp.dot(bn_ref[0], wo_ref[CONV_WIDTH:, :], preferred_element_type=F32))
    x1 = _layernorm(alpha * x_ref[0] + gt1_ref[0] * m) * l1g_ref[...] + l1b_ref[...]
    x1_ref[0] = x1
    h2 = _layernorm(x1) * (1.0 + sc2_ref[0]) + sh2_ref[0]
    _rows_to_tiles(h2_ref, h2)

    logits = jnp.dot(h2.astype(BF16), rw_ref[...], preferred_element_type=F32) + rb_ref[...]
    lane = lax.broadcasted_iota(jnp.int32, (tm, LANES), 1)
    lane_f = lane.astype(F32)
    lg = logits
    vals, sels = [], []
    chosen = jnp.zeros((tm, LANES), F32)
    for _ in range(TOP_K):
        mx = jnp.max(lg, axis=1, keepdims=True)
        idx = jnp.min(jnp.where(lg == mx, lane_f, float(LANES)), axis=1, keepdims=True)
        sel = lane_f == idx
        vals.append(mx)
        sels.append(idx)
        chosen = jnp.where(sel, 1.0, chosen)
        lg = jnp.where(sel, NEG, lg)

    es = [jnp.exp(v - vals[0]) for v in vals]
    denom = es[0] + es[1] + es[2] + es[3]

    run_s[0:1, :] = run_s[0:1, :] + jnp.sum(chosen, axis=0, keepdims=True)
    cnt_ref[...] = jnp.broadcast_to(run_s[0:1, :], cnt_ref.shape)

    meta = jnp.zeros((tm, LANES), jnp.int32)
    gates = jnp.zeros((tm, LANES), F32)
    for k in range(TOP_K):
        meta = jnp.where(lane == k, sels[k].astype(jnp.int32), meta)
        gates = jnp.where(lane == k, es[k] / denom, gates)
    meta_ref[0] = meta
    gates_ref[0] = gates


def _outproj(x, a_n, b_n, gt1, sc2, sh2, h2_all, tok0, w, tm, alpha):
    B, T, _ = x.shape
    blk0 = tok0 // tm
    tile = lambda c: pl.BlockSpec((1, tm, c), lambda b, j: (b, j, 0))
    modv = pl.BlockSpec((1, 1, D_MODEL), lambda b, j: (b, 0, 0))
    full = lambda shp: pl.BlockSpec(shp, lambda b, j: (0,) * len(shp))
    return pl.pallas_call(
        functools.partial(_outproj_kernel, tm=tm, alpha=alpha),
        grid=(B, T // tm),
        in_specs=[tile(D_MODEL), tile(CONV_WIDTH), tile(ATTN_WIDTH), modv, modv, modv,
                  full((CONV_WIDTH + ATTN_WIDTH, D_MODEL)), full((1, D_MODEL)), full((1, D_MODEL)),
                  full((D_MODEL, LANES)), full((1, LANES)), pl.BlockSpec(memory_space=pl.ANY)],
        out_specs=(tile(D_MODEL),
                   pl.BlockSpec((tm * ROW_TILE, LANES), lambda b, j: (blk0 + b * (T // tm) + j, 0)),
                   tile(LANES), tile(LANES), pl.BlockSpec((8, LANES), lambda b, j: (0, 0))),
        out_shape=(jax.ShapeDtypeStruct((B, T, D_MODEL), F32),
                   jax.ShapeDtypeStruct(h2_all.shape, F32),
                   jax.ShapeDtypeStruct((B, T, LANES), jnp.int32), jax.ShapeDtypeStruct((B, T, LANES), F32),
                   jax.ShapeDtypeStruct((8, LANES), F32)),
        scratch_shapes=[pltpu.VMEM((8, LANES), F32)],
        input_output_aliases={11: 1},
        compiler_params=_cparams(("arbitrary", "arbitrary")),
        name="outproj",
    )(x, a_n, b_n, gt1, sc2, sh2, w["w_out"], w["ln1_g"], w["ln1_b"], w["router_w"], w["router_b"], h2_all)


def _expert_kernel(order_ref, blk_e_ref, blk_j0_ref, blk_nv_ref, nxt_e_ref, used_ref,
                   h_ref, wgu_hbm, bgu_ref, wd_hbm, bd_ref, y_ref,
                   x0, x1, y0, y1, xb, act_s, wgu_st, wd_st, wgu_bf, wd_bf, gsem, ssem, wsem, *, bm, n_tok):
    s = pl.program_id(0)
    used = used_ref[0]
    dump0 = TOP_K * n_tok

    def weight_copies(e):
        return (pltpu.make_async_copy(wgu_hbm.at[e], wgu_st, wsem.at[0]),
                pltpu.make_async_copy(wd_hbm.at[e], wd_st, wsem.at[1]))

    tok_bits = (n_tok - 1).bit_length()

    def gather_start(j0, xbuf, sem, r):
        tok = jnp.bitwise_and(order_ref[j0 + r], (1 << tok_bits) - 1)
        pltpu.make_async_copy(h_ref.at[tok], xbuf.at[pl.ds(r * ROW_TILE, ROW_TILE)], sem).start()

    def scatter_start(j0, nv, ybuf, sem, r, odd):
        real = lax.shift_right_logical(order_ref[j0 + r], tok_bits)
        dump = dump0 + odd * bm + r
        valid = jnp.right_shift(r - nv, 31)
        dst = dump + jnp.bitwise_and(valid, real - dump)
        pltpu.make_async_copy(ybuf.at[pl.ds(r * ROW_TILE, ROW_TILE)], y_ref.at[dst], sem).start()

    def block_wait(buf, sem):
        pltpu.make_async_copy(buf, buf, sem).wait()

    def switch_weights(b):
        prev = blk_e_ref[jnp.maximum(b - 1, 0)]
        e = blk_e_ref[b]

        @pl.when((b == 0) | (e != prev))
        def _():
            for c in weight_copies(e):
                c.wait()
            wgu_bf[...] = wgu_st[...].astype(BF16)
            wd_bf[...] = wd_st[...].astype(BF16)

            @pl.when(nxt_e_ref[e] >= 0)
            def _():
                for c in weight_copies(nxt_e_ref[e]):
                    c.start()

    def block(b, x_in, y_out, g_next, x_next, gsem_next, s_prev, y_prev, ssem_prev, prev_odd, wait_y_free):
        gj0 = blk_j0_ref[g_next + 1]
        sj0 = blk_j0_ref[s_prev + 1]
        snv = blk_nv_ref[s_prev + 1]
        def issue(k, after=None, row=0):
            zero = 0
            if after is not None:
                bits = pltpu.bitcast(jnp.abs(after[row:row + 8, 0:LANES]), jnp.int32)
                zero = jnp.minimum(bits[0, 0], 0)
            half = DMA_BATCHES // 2
            q = k % half
            for r in range(q * bm // half, (q + 1) * bm // half):
                if k < half:
                    gather_start(gj0 + zero, x_next, gsem_next, r)
                else:
                    scatter_start(sj0 + zero, snv, y_prev, ssem_prev, r, prev_odd)

        xb[...] = _rows_from_tiles(x_in, bm).astype(BF16)
        e = blk_e_ref[b]
        cw = D_FF // 4
        prev = None
        for c in range(4):
            issue(4 * c, prev, 0)
            issue(4 * c + 1, prev, bm // 2)
            gs, ls = slice(c * cw, (c + 1) * cw), slice(D_FF + c * cw, D_FF + (c + 1) * cw)
            g = jnp.dot(xb[...], wgu_bf[:, gs], preferred_element_type=F32) + bgu_ref[e][:, gs]
            issue(4 * c + 2, g, 0)
            issue(4 * c + 3, g, bm // 2)
            lin = jnp.dot(xb[...], wgu_bf[:, ls], preferred_element_type=F32) + bgu_ref[e][:, ls]
            prev = lin
            g = jnp.minimum(g, SWIGLU_LIMIT)
            lin = jnp.clip(lin, -SWIGLU_LIMIT, SWIGLU_LIMIT)
            act_s[:, gs] = (g * _sigmoid(SWIGLU_ALPHA * g) * (lin + 1.0)).astype(BF16)
        hw = D_MODEL // 2
        wait_y_free()
        for h in range(2):
            issue(16 + 2 * h, prev, 0)
            issue(17 + 2 * h, prev, bm // 2)
            y = jnp.dot(act_s[...], wd_bf[:, h * hw:(h + 1) * hw], preferred_element_type=F32)
            y = y + bd_ref[e][:, h * hw:(h + 1) * hw]
            prev = y
            for c in range(hw // LANES):
                y_out[pl.ds(h * (hw // LANES) + c, bm, stride=ROW_TILE), :] = y[:, c * LANES:(c + 1) * LANES]

    b0 = 2 * s
    b1 = b0 + 1

    @pl.when(b0 < used)
    def _():
        @pl.when(s == 0)
        def _():
            for c in weight_copies(blk_e_ref[0]):
                c.start()
            y1[...] = jnp.zeros(y1.shape, F32)

            def first(r, c):
                scatter_start(0, 0, y1, ssem.at[0], r, 0)
                gather_start(blk_j0_ref[1], x0, gsem.at[0], r)
                return c

            lax.fori_loop(0, bm, first, 0)
            block_wait(y1, ssem.at[0])

        def y0_free():
            @pl.when(s > 0)
            def _():
                block_wait(y0, ssem.at[0])

        def y1_free():
            block_wait(y1, ssem.at[1])

        switch_weights(b0)
        block_wait(x0, gsem.at[0])
        block(b0, x0, y0, b1, x1, gsem.at[1], b0 - 1, y1, ssem.at[1], 1, y0_free)

        switch_weights(b1)
        block_wait(x1, gsem.at[1])
        block(b1, x1, y1, b0 + 2, x0, gsem.at[0], b0, y0, ssem.at[0], 0, y1_free)

        @pl.when(b0 + 2 >= used)
        def _():
            def last(r, c):
                scatter_start(blk_j0_ref[b1 + 1], blk_nv_ref[b1 + 1], y1, ssem.at[1], r, 1)
                return c

            lax.fori_loop(0, bm, last, 0)
            block_wait(y0, ssem.at[0])
            block_wait(y1, ssem.at[1])
            block_wait(x0, gsem.at[0])


def _expert(order, blk_e, blk_j0, blk_nv, nxt_e, used, h2, w, bm):
    n_tok = h2.shape[0]
    nblk = blk_e.shape[0]
    f32buf = lambda shp: pltpu.VMEM(shp, F32)
    return pl.pallas_call(
        functools.partial(_expert_kernel, bm=bm, n_tok=n_tok),
        grid_spec=pltpu.PrefetchScalarGridSpec(
            num_scalar_prefetch=6,
            grid=(nblk // 2,),
            in_specs=[
                pl.BlockSpec(memory_space=pl.ANY),
                pl.BlockSpec(memory_space=pl.ANY),
                pl.BlockSpec((N_EXPERTS, 1, 2 * D_FF), lambda s, *_: (0, 0, 0)),
                pl.BlockSpec(memory_space=pl.ANY),
                pl.BlockSpec((N_EXPERTS, 1, D_MODEL), lambda s, *_: (0, 0, 0)),
            ],
            out_specs=pl.BlockSpec(memory_space=pl.ANY),
            scratch_shapes=[f32buf((bm * ROW_TILE, LANES)), f32buf((bm * ROW_TILE, LANES)),
                            f32buf((bm * ROW_TILE, LANES)), f32buf((bm * ROW_TILE, LANES)),
                            pltpu.VMEM((bm, D_MODEL), BF16), pltpu.VMEM((bm, D_FF), BF16),
                            f32buf((D_MODEL, 2 * D_FF)), f32buf((D_FF, D_MODEL)),
                            pltpu.VMEM((D_MODEL, 2 * D_FF), BF16), pltpu.VMEM((D_FF, D_MODEL), BF16),
                            pltpu.SemaphoreType.DMA((2,)), pltpu.SemaphoreType.DMA((2,)),
                            pltpu.SemaphoreType.DMA((2,))],
        ),
        out_shape=jax.ShapeDtypeStruct((TOP_K * n_tok + 2 * bm, ROW_TILE, LANES), F32),
        compiler_params=_cparams(("arbitrary",)),
        name="moe_expert",
    )(order, blk_e, blk_j0, blk_nv, nxt_e, used, h2, w["w_gu"], w["b_gu"], w["w_down"], w["b_down"])


def _combine_kernel(y0_ref, y1_ref, y2_ref, y3_ref, x1_ref, gates_ref, gt2_ref, l2g_ref, l2b_ref, o_ref, *, alpha):
    gates = gates_ref[...]
    tm = gates.shape[0]
    f = gates[:, 0:1] * _rows_from_tiles(y0_ref, tm)
    for k, y_ref in enumerate((y1_ref, y2_ref, y3_ref), start=1):
        f = f + gates[:, k:k + 1] * _rows_from_tiles(y_ref, tm)
    o_ref[...] = _layernorm(alpha * x1_ref[...] + gt2_ref[0] * f) * l2g_ref[...] + l2b_ref[...]


def _combine(y_rows, x1, gates, gt2, w, tm, tokens_per_batch, alpha, n_tok, tok0):
    N = x1.shape[0]
    per_b = tokens_per_batch // tm
    nt = N // tm
    assert n_tok % tm == 0 and tok0 % tm == 0
    slot = lambda k: pl.BlockSpec((tm * ROW_TILE, LANES), lambda i: ((k * n_tok + tok0) // tm + i, 0))
    return pl.pallas_call(
        functools.partial(_combine_kernel, alpha=alpha),
        grid=(nt,),
        in_specs=[slot(0), slot(1), slot(2), slot(3),
                  pl.BlockSpec((tm, D_MODEL), lambda i: (i, 0)),
                  pl.BlockSpec((tm, LANES), lambda i: (i, 0)),
                  pl.BlockSpec((1, 1, D_MODEL), lambda i: (i // per_b, 0, 0)),
                  pl.BlockSpec((1, D_MODEL), lambda i: (0, 0)),
                  pl.BlockSpec((1, D_MODEL), lambda i: (0, 0))],
        out_specs=pl.BlockSpec((tm, D_MODEL), lambda i: (i, 0)),
        out_shape=jax.ShapeDtypeStruct((N, D_MODEL), F32),
        compiler_params=_cparams(("arbitrary",)),
        name="moe_combine",
    )(y_rows, y_rows, y_rows, y_rows, x1, gates, gt2, w["ln2_g"], w["ln2_b"])


def _moe_experts(h2, idx, counts, w, bm):
    N = h2.shape[0]
    n_rows = N * TOP_K
    nblk = (n_rows + N_EXPERTS * (bm - 1)) // bm
    nblk += nblk % 2
    experts = jnp.arange(N_EXPERTS, dtype=jnp.int32)
    id_bits = (n_rows - 1).bit_length()
    tok_bits = (N - 1).bit_length()
    assert tok_bits + (TOP_K * N - 1).bit_length() <= 32
    keys = jnp.left_shift(idx.reshape(-1), id_bits) + jnp.arange(n_rows, dtype=jnp.int32)
    keys = jnp.concatenate([keys, jnp.full(((1 << id_bits) - n_rows,), jnp.iinfo(jnp.int32).max, jnp.int32)])
    flat = jnp.bitwise_and(jnp.sort(keys)[:n_rows], (1 << id_bits) - 1).astype(jnp.uint32)
    tok = jnp.right_shift(flat, 2)
    row = jnp.bitwise_and(flat, TOP_K - 1) * N + tok
    order = lax.bitcast_convert_type(jnp.left_shift(row, tok_bits) | tok, jnp.int32)
    order = jnp.concatenate([order, jnp.zeros((bm,), jnp.int32)])
    nb_e = (counts + bm - 1) // bm
    blk_end = jnp.cumsum(nb_e)
    first_blk = blk_end - nb_e
    start_sorted = jnp.cumsum(counts) - counts
    used = blk_end[-1].astype(jnp.int32)
    b = jnp.arange(-1, nblk + 1, dtype=jnp.int32)
    bc = jnp.clip(b, 0, used - 1)
    e = jnp.minimum(jnp.sum(blk_end[None, :] <= bc[:, None], axis=1), N_EXPERTS - 1).astype(jnp.int32)
    pick = lambda table: jnp.sum(jnp.where(e[:, None] == experts, table, 0), axis=1)
    local = bc - pick(first_blk)
    blk_j0 = (pick(start_sorted) + local * bm).astype(jnp.int32)
    blk_nv = jnp.where((b >= 0) & (b < used), jnp.minimum(bm, pick(counts) - local * bm), 0).astype(jnp.int32)
    blk_e = e[1:nblk + 1]
    later = (experts[None, :] > experts[:, None]) & (counts[None, :] > 0)
    nxt = jnp.min(jnp.where(later, experts[None, :], N_EXPERTS), axis=1)
    nxt_e = jnp.where(nxt < N_EXPERTS, nxt, -1).astype(jnp.int32)
    y_rows = _expert(order, blk_e, blk_j0, blk_nv, nxt_e, used.reshape(1), h2, w, bm)
    return y_rows.reshape(-1, LANES)


def _rope_tables(pos):
    half = QK_ROPE // 2
    inv = ROPE_THETA ** (-jnp.arange(half, dtype=F32) / half)
    ang = pos.astype(F32)[:, None] * inv[None, :]
    cos, sin = jnp.cos(ang), jnp.sin(ang)
    cos32 = jnp.concatenate([cos, cos], axis=1)
    sin32 = jnp.concatenate([-sin, sin], axis=1)
    return jnp.tile(cos32, (1, LANES // QK_ROPE)), jnp.tile(sin32, (1, LANES // QK_ROPE))


def _swap_halves(w32):
    shp = w32.shape
    w = w32.reshape(shp[:-1] + (shp[-1] // QK_ROPE, 2, QK_ROPE // 2))
    return w[..., ::-1, :].reshape(shp)


def _prep_weights(l, w_in, conv_w, g_qa, w_qb, g_kva, w_kvb, g_out_conv, g_out_attn, w_out,
                  ln1_g, ln1_b, router_w, router_b, w_gu, b_gu, w_down, b_down, ln2_g, ln2_b):
    w = {}
    wi = w_in[l]
    k_r = wi[:, _O_KR:_O_KR + QK_ROPE]
    rep = LANES // QK_ROPE
    w["w_in"] = jnp.concatenate([wi[:, :_O_KR], jnp.tile(k_r, (1, rep)), jnp.tile(_swap_halves(k_r), (1, rep))],
                                axis=1).astype(BF16)
    w["conv_w"] = conv_w[l]
    w["g_qa"] = g_qa[l].reshape(1, Q_LORA)
    w["g_kva"] = g_kva[l].reshape(1, KV_LORA)
    w["g_oc"] = g_out_conv[l].reshape(1, CONV_WIDTH)
    w["g_oa"] = g_out_attn[l].reshape(1, ATTN_WIDTH)
    wq = w_qb[l].reshape(Q_LORA, N_HEADS, QK_NOPE + QK_ROPE)
    w["wq_nope"] = wq[:, :, :QK_NOPE].reshape(Q_LORA, N_HEADS * QK_NOPE).astype(BF16)
    wq_rope = wq[:, :, QK_NOPE:].reshape(Q_LORA, N_HEADS * QK_ROPE)
    w["wq_rope"] = wq_rope.astype(BF16)
    w["wq_rope_sw"] = _swap_halves(wq_rope).astype(BF16)
    w_uk = jnp.transpose(w_kvb[l][:, :, :QK_NOPE], (1, 2, 0))
    w_uv = jnp.transpose(w_kvb[l][:, :, QK_NOPE:], (1, 0, 2))
    zk = jnp.zeros((QK_NOPE, KV_LORA), F32)
    zv = jnp.zeros((KV_LORA, V_HEAD), F32)
    w["w_uk_pair"] = jnp.stack([
        jnp.concatenate([jnp.concatenate([w_uk[2 * p], zk], axis=1),
                         jnp.concatenate([zk, w_uk[2 * p + 1]], axis=1)], axis=0)
        for p in range(N_HEADS // 2)]).astype(BF16)
    w["w_uv_pair"] = jnp.stack([
        jnp.concatenate([jnp.concatenate([w_uv[2 * p], zv], axis=1),
                         jnp.concatenate([zv, w_uv[2 * p + 1]], axis=1)], axis=0)
        for p in range(N_HEADS // 2)]).astype(BF16)
    w["w_out"] = w_out[l].astype(BF16)
    w["ln1_g"] = ln1_g[l].reshape(1, D_MODEL)
    w["ln1_b"] = ln1_b[l].reshape(1, D_MODEL)
    w["ln2_g"] = ln2_g[l].reshape(1, D_MODEL)
    w["ln2_b"] = ln2_b[l].reshape(1, D_MODEL)
    w["router_w"] = jnp.pad(router_w[l], ((0, 0), (0, LANES - N_EXPERTS))).astype(BF16)
    w["router_b"] = jnp.concatenate([router_b[l], jnp.full((LANES - N_EXPERTS,), NEG, F32)]).reshape(1, LANES)
    w["w_gu"] = w_gu[l]
    w["b_gu"] = b_gu[l].reshape(N_EXPERTS, 1, 2 * D_FF)
    w["w_down"] = w_down[l]
    w["b_down"] = b_down[l].reshape(N_EXPERTS, 1, D_MODEL)
    return w


def _mixer(x, mod, conv_prev, past, pos0, h2_all, tok0, w, alpha, *, tm_in, tm_out, tq=128, tk=512):
    B, T, _ = x.shape
    sh1, sc1, gt1, sh2, sc2, gt2 = [mod[:, None, i * D_MODEL:(i + 1) * D_MODEL] for i in range(N_MOD)]
    cos_t, sin_t = _rope_tables(pos0 + jnp.arange(T, dtype=jnp.int32))
    a_n, qlat, qrope, kcat, ckv, krope, conv_new, *vt = _inproj(x, sc1, sh1, conv_prev, cos_t, sin_t, w, tm_in,
                                                                 with_vt=past is None)
    if past is None:
        b_n = _attn_prompt(qlat, qrope, kcat, vt[0], w, tq, tk)
    else:
        b_n = _attn_sample(qlat, qrope, kcat, past[0], past[1], w, tk)
    x1, h2_all, meta, gates, cnt = _outproj(x, a_n, b_n, gt1, sc2, sh2, h2_all, tok0, w, tm_out, alpha)
    N = B * T
    route = dict(x1=x1.reshape(N, D_MODEL), idx=meta.reshape(N, LANES)[:, :TOP_K], gates=gates.reshape(N, LANES),
                 counts=cnt[0, :N_EXPERTS].astype(jnp.int32), gt2=gt2, shape=(B, T))
    return h2_all, route, ckv, krope, conv_new


def kernel(x_prompt, x_sample, c_prompt, c_sample, cache_ckv, cache_krope, state_conv, w_ada, b_ada, w_in, conv_w, g_qa, w_qb, g_kva, w_kvb, g_out_conv, g_out_attn, w_out, ln1_g, ln1_b, router_w, router_b, w_gu, b_gu, w_down, b_down, ln2_g, ln2_b):
    depth = w_ada.shape[0]
    Bp, Tp, _ = x_prompt.shape
    Bs, Ts, _ = x_sample.shape
    past_len = cache_ckv.shape[2]
    assert Ts == CHUNK and past_len % CHUNK == 0 and Tp % 512 == 0
    alpha = (2.0 * depth) ** 0.25
    xp, xs = x_prompt, x_sample
    outs = [[] for _ in range(6)]
    c_all = jnp.concatenate([c_prompt, c_sample, jnp.zeros((16 - Bp - Bs, D_MODEL), F32)], axis=0)
    for l in range(depth):
        w = _prep_weights(l, w_in, conv_w, g_qa, w_qb, g_kva, w_kvb, g_out_conv, g_out_attn, w_out,
                          ln1_g, ln1_b, router_w, router_b, w_gu, b_gu, w_down, b_down, ln2_g, ln2_b)
        mod = _ada(c_all, w_ada[l], b_ada[l])
        n_p, n_s = Bp * Tp, Bs * Ts
        n_tok = n_p + n_s
        h2_all = jnp.zeros((n_tok * ROW_TILE, LANES), F32)
        h2_all, rp, ckv_p, kr_p, cv_p = _mixer(xp, mod[:Bp], jnp.zeros((Bp, CONV_K - 1, CONV_WIDTH), F32), None, 0,
                                               h2_all, 0, w, alpha, tm_in=1024, tm_out=512)
        h2_all, rs, ckv_s, kr_s, cv_s = _mixer(xs, mod[Bp:Bp + Bs], state_conv[l], (cache_ckv[l], cache_krope[l]),
                                               past_len, h2_all, n_p, w, alpha, tm_in=Ts, tm_out=Ts)
        y_rows = _moe_experts(h2_all.reshape(n_tok, ROW_TILE, LANES), jnp.concatenate([rp["idx"], rs["idx"]]),
                              rp["counts"] + rs["counts"], w, bm=256)
        xp = _combine(y_rows, rp["x1"], rp["gates"], rp["gt2"], w, 256, Tp, alpha, n_tok, 0).reshape(rp["shape"] + (D_MODEL,))
        xs = _combine(y_rows, rs["x1"], rs["gates"], rs["gt2"], w, Ts, Ts, alpha, n_tok, n_p).reshape(rs["shape"] + (D_MODEL,))
        for o, v in zip(outs, (ckv_p, kr_p, cv_p, ckv_s, kr_s, cv_s)):
            o.append(v)
    return (xp, xs) + tuple(jnp.stack(o) for o in outs)
```

```python
import functools
import math

import jax
import jax.numpy as jnp
from jax import lax
from jax.experimental import pallas as pl
from jax.experimental.pallas import tpu as pltpu

F32 = jnp.float32
BF16 = jnp.bfloat16

D_MODEL = 1024
CHUNK = 64
CONV_WIDTH = 512
CONV_K = 3
N_HEADS = 8
QK_NOPE = 64
QK_ROPE = 32
V_HEAD = 64
Q_LORA = 256
KV_LORA = 128
ATTN_WIDTH = N_HEADS * V_HEAD
ROPE_THETA = 10000.0
ATTN_SCALE = 1.0 / math.sqrt(QK_NOPE + QK_ROPE)
Q_SCALE = ATTN_SCALE * math.log2(math.e)
N_EXPERTS = 32
TOP_K = 4
D_FF = 1024
SWIGLU_LIMIT = 7.0
SWIGLU_ALPHA = 1.702
N_MOD = 6
LN_EPS = 1e-5
RMS_EPS = 1e-6

LANES = 128
SUBLANES = 8
DMA_BATCHES = 10
NEG = -1e30
VMEM_LIMIT = 56 * 1024 * 1024

_O_XB, _O_XC, _O_XV = 0, CONV_WIDTH, 2 * CONV_WIDTH
_O_QA = 3 * CONV_WIDTH
_O_KVA = _O_QA + Q_LORA
_O_KR = _O_KVA + KV_LORA
_O_KRS = _O_KR + LANES
IN_COLS_EXT = _O_KRS + LANES


def _cparams(sem):
    return pltpu.CompilerParams(dimension_semantics=sem, vmem_limit_bytes=VMEM_LIMIT)


def _layernorm(x):
    mu = jnp.mean(x, axis=-1, keepdims=True)
    xc = x - mu
    var = jnp.mean(xc * xc, axis=-1, keepdims=True)
    return xc * lax.rsqrt(var + LN_EPS)


def _rms(x):
    return x * lax.rsqrt(jnp.mean(x * x, axis=-1, keepdims=True) + RMS_EPS)


def _sigmoid(x):
    return 1.0 / (1.0 + jnp.exp(-x))


ROW_TILE = D_MODEL // LANES


def _rows_from_tiles(ref, n):
    return jnp.concatenate([ref[pl.ds(c, n, stride=ROW_TILE), :] for c in range(ROW_TILE)], axis=1)


def _rows_to_tiles(ref, x):
    n = x.shape[0]
    for c in range(ROW_TILE):
        ref[pl.ds(c, n, stride=ROW_TILE), :] = x[:, c * LANES:(c + 1) * LANES]


def _ada_kernel(c_ref, w_ref, b_ref, o_ref):
    c = c_ref[...]
    s = (c * _sigmoid(c)).astype(BF16)
    o_ref[...] = jnp.dot(s, w_ref[...].astype(BF16), preferred_element_type=F32) + b_ref[...]


def _ada(c_all, w_ada, b_ada):
    rows = c_all.shape[0]
    ncol = w_ada.shape[1]
    tn = 1024
    return pl.pallas_call(
        _ada_kernel,
        grid=(ncol // tn,),
        in_specs=[pl.BlockSpec((rows, D_MODEL), lambda j: (0, 0)),
                  pl.BlockSpec((D_MODEL, tn), lambda j: (0, j)),
                  pl.BlockSpec((1, tn), lambda j: (0, j))],
        out_specs=pl.BlockSpec((rows, tn), lambda j: (0, j)),
        out_shape=jax.ShapeDtypeStruct((rows, ncol), F32),
        compiler_params=_cparams(("arbitrary",)),
        name="ada",
    )(c_all, w_ada, b_ada.reshape(1, ncol))


def _inproj_kernel(x_ref, sc_ref, sh_ref, win_ref, cw_ref, cprev_ref, gqa_ref, gkva_ref, goc_ref,
                   wqn_ref, wqr_ref, wqrs_ref, wuk_ref, cos_ref, sin_ref,
                   an_ref, qlat_ref, qrope_ref, kcat_ref, ckv_ref, krope_ref, cnew_ref,
                   *rest, tm, with_vt):
    vt_ref, ubuf = rest if with_vt else (None, rest[0])
    j = pl.program_id(1)
    x = x_ref[0]
    h = _layernorm(x) * (1.0 + sc_ref[0]) + sh_ref[0]
    proj = jnp.dot(h.astype(BF16), win_ref[...], preferred_element_type=F32)
    xb = proj[:, _O_XB:_O_XB + CONV_WIDTH]
    xc = proj[:, _O_XC:_O_XC + CONV_WIDTH]
    xv = proj[:, _O_XV:_O_XV + CONV_WIDTH]
    q_a = proj[:, _O_QA:_O_QA + Q_LORA]
    kv_a = proj[:, _O_KVA:_O_KVA + KV_LORA]
    kr4 = proj[:, _O_KR:_O_KR + LANES]
    kr4s = proj[:, _O_KRS:_O_KRS + LANES]

    u = xc * xv

    @pl.when(j == 0)
    def _():
        ubuf[6:8, :] = cprev_ref[0]

    ubuf[8:8 + tm, :] = u
    conv = (cw_ref[0:1, :] * ubuf[6:6 + tm, :] + cw_ref[1:2, :] * ubuf[7:7 + tm, :]
            + cw_ref[2:3, :] * u)
    ubuf[0:8, :] = ubuf[tm:tm + 8, :]
    cnew_ref[0] = u[tm - (CONV_K - 1):tm, :]
    an_ref[0] = (_rms(xb * conv) * goc_ref[...]).astype(BF16)

    cos = cos_ref[...]
    sin = sin_ref[...]

    ckv = _rms(kv_a) * gkva_ref[...]
    kro4 = kr4 * cos + kr4s * sin
    ckv_ref[0] = ckv
    krope_ref[0] = kro4[:, :QK_ROPE]
    kcat_ref[0] = jnp.concatenate([ckv, kro4], axis=1).astype(BF16)
    if with_vt:
        vt_ref[0] = ckv.T.astype(BF16)

    qn = (_rms(q_a) * gqa_ref[...]).astype(BF16)
    q_nope = jnp.dot(qn, wqn_ref[...], preferred_element_type=F32)
    xr = jnp.dot(qn, wqr_ref[...], preferred_element_type=F32)
    xrs = jnp.dot(qn, wqrs_ref[...], preferred_element_type=F32)
    for g in range(2):
        sl = slice(g * LANES, (g + 1) * LANES)
        qrope_ref[0, :, sl] = ((xr[:, sl] * cos + xrs[:, sl] * sin) * Q_SCALE).astype(BF16)
    for p in range(N_HEADS // 2):
        qp = q_nope[:, p * LANES:(p + 1) * LANES].astype(BF16)
        ql = jnp.dot(qp, wuk_ref[p], preferred_element_type=F32)
        qlat_ref[0, :, p * 2 * KV_LORA:(p + 1) * 2 * KV_LORA] = (ql * Q_SCALE).astype(BF16)


def _inproj(x, sc1, sh1, conv_prev, cos_t, sin_t, w, tm, with_vt):
    B, T, _ = x.shape
    nt = T // tm
    full = lambda shp: pl.BlockSpec(shp, lambda b, j: (0,) * len(shp))
    vt_shape = (jax.ShapeDtypeStruct((B, KV_LORA, T), BF16),) if with_vt else ()
    vt_spec = (pl.BlockSpec((1, KV_LORA, tm), lambda b, j: (b, 0, j)),) if with_vt else ()
    out_shapes = (
        jax.ShapeDtypeStruct((B, T, CONV_WIDTH), BF16),
        jax.ShapeDtypeStruct((B, T, N_HEADS * KV_LORA), BF16),
        jax.ShapeDtypeStruct((B, T, 2 * LANES), BF16),
        jax.ShapeDtypeStruct((B, T, 2 * LANES), BF16),
        jax.ShapeDtypeStruct((B, T, KV_LORA), F32),
        jax.ShapeDtypeStruct((B, T, QK_ROPE), F32),
        jax.ShapeDtypeStruct((B, CONV_K - 1, CONV_WIDTH), F32),
    ) + vt_shape
    tile = lambda c: pl.BlockSpec((1, tm, c), lambda b, j: (b, j, 0))
    return pl.pallas_call(
        functools.partial(_inproj_kernel, tm=tm, with_vt=with_vt),
        grid=(B, nt),
        in_specs=[
            tile(D_MODEL),
            pl.BlockSpec((1, 1, D_MODEL), lambda b, j: (b, 0, 0)),
            pl.BlockSpec((1, 1, D_MODEL), lambda b, j: (b, 0, 0)),
            full((D_MODEL, IN_COLS_EXT)),
            full((CONV_K, CONV_WIDTH)),
            pl.BlockSpec((1, CONV_K - 1, CONV_WIDTH), lambda b, j: (b, 0, 0)),
            full((1, Q_LORA)), full((1, KV_LORA)), full((1, CONV_WIDTH)),
            full((Q_LORA, N_HEADS * QK_NOPE)), full((Q_LORA, 2 * LANES)), full((Q_LORA, 2 * LANES)),
            full((N_HEADS // 2, LANES, 2 * KV_LORA)),
            pl.BlockSpec((tm, LANES), lambda b, j: (j, 0)),
            pl.BlockSpec((tm, LANES), lambda b, j: (j, 0)),
        ],
        out_specs=(tile(CONV_WIDTH), tile(N_HEADS * KV_LORA), tile(2 * LANES), tile(2 * LANES),
                   tile(KV_LORA), tile(QK_ROPE),
                   pl.BlockSpec((1, CONV_K - 1, CONV_WIDTH), lambda b, j: (b, 0, 0))) + vt_spec,
        out_shape=out_shapes,
        scratch_shapes=[pltpu.VMEM((tm + 8, CONV_WIDTH), F32)],
        compiler_params=_cparams(("arbitrary", "arbitrary")),
        name="inproj",
    )(x, sc1, sh1, w["w_in"], w["conv_w"], conv_prev, w["g_qa"], w["g_kva"], w["g_oc"],
      w["wq_nope"], w["wq_rope"], w["wq_rope_sw"], w["w_uk_pair"], cos_t, sin_t)


def _stack_queries(qlat_ref, qrope_ref, qs, tq):
    lane = lax.broadcasted_iota(jnp.int32, (tq, LANES), 1)
    for h in range(N_HEADS):
        g, i = divmod(h, 4)
        rope = qrope_ref[0, :, g * LANES:(g + 1) * LANES]
        keep = (lane >= i * QK_ROPE) & (lane < (i + 1) * QK_ROPE)
        qs[h * tq:(h + 1) * tq, 0:KV_LORA] = qlat_ref[0, :, h * KV_LORA:(h + 1) * KV_LORA]
        qs[h * tq:(h + 1) * tq, KV_LORA:KV_LORA + LANES] = jnp.where(keep, rope, jnp.zeros_like(rope))


def _softmax_step(qs, k, v, m_s, l_s, acc_s):
    tk = k.shape[0]
    s = lax.dot_general(qs[...], k, (((1,), (1,)), ((), ())), preferred_element_type=F32)
    m_prev = m_s[...]
    m_new = jnp.maximum(m_prev, jnp.max(s, axis=1, keepdims=True))
    alpha = jnp.exp2(m_prev - m_new)
    if tk % LANES == 0:
        p = jnp.exp2(s - jnp.tile(m_new, (1, tk // LANES)))
    else:
        p = jnp.exp2(s - m_new[:, :tk])
    l_s[...] = alpha * l_s[...] + jnp.sum(p, axis=1, keepdims=True)
    acc_s[...] = alpha * acc_s[...] + jnp.dot(p.astype(BF16), v, preferred_element_type=F32)
    m_s[...] = m_new


def _attn_epilogue(acc_s, l_s, wuv_ref, g_ref, o_ref, tq):
    o = acc_s[...] / l_s[...]
    parts = []
    for p in range(N_HEADS // 2):
        op = jnp.concatenate([o[(2 * p) * tq:(2 * p + 1) * tq], o[(2 * p + 1) * tq:(2 * p + 2) * tq]], axis=1)
        parts.append(jnp.dot(op.astype(BF16), wuv_ref[p], preferred_element_type=F32))
    b = jnp.concatenate(parts, axis=1)
    o_ref[0] = (_rms(b) * g_ref[...]).astype(BF16)


def _attn_prompt_kernel(qlat_ref, qrope_ref, k_ref, vt_ref, wuvt_ref, g_ref, o_ref, qs, m_s, l_s, acc_s, sa, sb,
                        *, tq, tk):
    i = pl.program_id(1)
    M = N_HEADS * tq
    _stack_queries(qlat_ref, qrope_ref, qs, tq)
    m_s[...] = jnp.full(m_s.shape, NEG, F32)
    l_s[...] = jnp.zeros(l_s.shape, F32)
    acc_s[...] = jnp.zeros(acc_s.shape, F32)
    q0 = i * tq
    n_full = (q0 + CHUNK) // tk

    def scores(t, dst):
        k = k_ref[0, pl.ds(pl.multiple_of(t * tk, tk), tk), :]
        dst[...] = lax.dot_general(k, qs[...], (((1,), (1,)), ((), ())), preferred_element_type=F32)

    def update(t, src, limit=None):
        start = pl.multiple_of(t * tk, tk)
        vt = vt_ref[0, :, pl.ds(start, tk)]
        s = src[...]
        if limit is not None:
            kpos = start + lax.broadcasted_iota(jnp.int32, (tk, 1), 0)
            s = jnp.where(kpos < limit, s, NEG)
        m_prev = m_s[...]
        m_new = jnp.maximum(m_prev, jnp.max(s, axis=0, keepdims=True))
        alpha = jnp.exp2(m_prev - m_new)
        p = jnp.exp2(s - m_new)
        l_s[...] = alpha * l_s[...] + jnp.sum(p, axis=0, keepdims=True)
        acc_s[...] = alpha * acc_s[...] + jnp.dot(vt, p.astype(BF16), preferred_element_type=F32)
        m_s[...] = m_new

    scores(0, sa)

    def body(j, carry):
        t = 2 * j
        scores(t + 1, sb)
        update(t, sa)
        scores(t + 2, sa)
        update(t + 1, sb)
        return carry

    lax.fori_loop(0, n_full // 2, body, 0)

    col_t = jnp.bitwise_and(lax.broadcasted_iota(jnp.int32, (1, M), 1), tq - 1)
    limit = q0 + (jnp.right_shift(col_t, CHUNK.bit_length() - 1) + 1) * CHUNK
    odd = n_full % 2

    @pl.when(odd == 0)
    def _():
        update(n_full, sa, limit)

    @pl.when(odd == 1)
    def _():
        scores(n_full, sb)
        update(n_full - 1, sa)
        update(n_full, sb, limit)

    o_t = acc_s[...] / l_s[...]
    parts = []
    for p in range(N_HEADS // 2):
        pair = jnp.concatenate([o_t[:, (2 * p) * tq:(2 * p + 1) * tq], o_t[:, (2 * p + 1) * tq:(2 * p + 2) * tq]],
                               axis=0)
        parts.append(jnp.dot(wuvt_ref[p], pair.astype(BF16), preferred_element_type=F32))
    b = jnp.concatenate(parts, axis=0).T
    o_ref[0] = (_rms(b) * g_ref[...]).astype(BF16)


def _attn_prompt(qlat, qrope, kcat, vt, w, tq, tk):
    B, T, _ = qlat.shape
    M = N_HEADS * tq
    return pl.pallas_call(
        functools.partial(_attn_prompt_kernel, tq=tq, tk=tk),
        grid=(B, T // tq),
        in_specs=[
            pl.BlockSpec((1, tq, N_HEADS * KV_LORA), lambda b, i: (b, i, 0)),
            pl.BlockSpec((1, tq, 2 * LANES), lambda b, i: (b, i, 0)),
            pl.BlockSpec((1, T, 2 * LANES), lambda b, i: (b, 0, 0)),
            pl.BlockSpec((1, KV_LORA, T), lambda b, i: (b, 0, 0)),
            pl.BlockSpec((N_HEADS // 2, LANES, 2 * KV_LORA), lambda b, i: (0, 0, 0)),
            pl.BlockSpec((1, ATTN_WIDTH), lambda b, i: (0, 0)),
        ],
        out_specs=pl.BlockSpec((1, tq, ATTN_WIDTH), lambda b, i: (b, i, 0)),
        out_shape=jax.ShapeDtypeStruct((B, T, ATTN_WIDTH), BF16),
        scratch_shapes=[pltpu.VMEM((M, 2 * LANES), BF16), pltpu.VMEM((1, M), F32),
                        pltpu.VMEM((1, M), F32), pltpu.VMEM((KV_LORA, M), F32),
                        pltpu.VMEM((tk, M), F32), pltpu.VMEM((tk, M), F32)],
        compiler_params=_cparams(("arbitrary", "arbitrary")),
        name="attn_prompt",
    )(qlat, qrope, kcat, vt, jnp.swapaxes(w["w_uv_pair"], 1, 2), w["g_oa"])


def _attn_sample_kernel(qlat_ref, qrope_ref, knew_ref, pckv_ref, pkr_ref, wuv_ref, g_ref, o_ref,
                        qs, m_s, l_s, acc_s, *, tq, tk, n_past):
    _stack_queries(qlat_ref, qrope_ref, qs, tq)
    m_s[...] = jnp.full(m_s.shape, NEG, F32)
    l_s[...] = jnp.zeros(l_s.shape, F32)
    acc_s[...] = jnp.zeros(acc_s.shape, F32)

    def body(t, carry):
        start = pl.multiple_of(t * tk, tk)
        ck = pckv_ref[0, pl.ds(start, tk), :]
        kr = pkr_ref[0, pl.ds(start, tk), :]
        k = jnp.concatenate([ck, kr, kr, kr, kr], axis=1).astype(BF16)
        _softmax_step(qs, k, k[:, :KV_LORA], m_s, l_s, acc_s)
        return carry

    lax.fori_loop(0, n_past // tk, body, 0)
    k = knew_ref[0]
    _softmax_step(qs, k, k[:, :KV_LORA], m_s, l_s, acc_s)
    _attn_epilogue(acc_s, l_s, wuv_ref, g_ref, o_ref, tq)


def _attn_sample(qlat, qrope, kcat, past_ckv, past_krope, w, tk):
    B, T, _ = qlat.shape
    n_past = past_ckv.shape[1]
    M = N_HEADS * T
    per_b = lambda r, c: pl.BlockSpec((1, r, c), lambda b: (b, 0, 0))
    return pl.pallas_call(
        functools.partial(_attn_sample_kernel, tq=T, tk=tk, n_past=n_past),
        grid=(B,),
        in_specs=[per_b(T, N_HEADS * KV_LORA), per_b(T, 2 * LANES), per_b(T, 2 * LANES),
                  per_b(n_past, KV_LORA), per_b(n_past, QK_ROPE),
                  pl.BlockSpec((N_HEADS // 2, 2 * KV_LORA, LANES), lambda b: (0, 0, 0)),
                  pl.BlockSpec((1, ATTN_WIDTH), lambda b: (0, 0))],
        out_specs=per_b(T, ATTN_WIDTH),
        out_shape=jax.ShapeDtypeStruct((B, T, ATTN_WIDTH), BF16),
        scratch_shapes=[pltpu.VMEM((M, 2 * LANES), BF16), pltpu.VMEM((M, LANES), F32),
                        pltpu.VMEM((M, LANES), F32), pltpu.VMEM((M, KV_LORA), F32)],
        compiler_params=_cparams(("arbitrary",)),
        name="attn_sample",
    )(qlat, qrope, kcat, past_ckv, past_krope, w["w_uv_pair"], w["g_oa"])


def _outproj_kernel(x_ref, an_ref, bn_ref, gt1_ref, sc2_ref, sh2_ref, wo_ref, l1g_ref, l1b_ref,
                    rw_ref, rb_ref, h2_all_ref, x1_ref, h2_ref, meta_ref, gates_ref, cnt_ref, run_s,
                    *, tm, alpha):
    del h2_all_ref
    first = (pl.program_id(0) == 0) & (pl.program_id(1) == 0)

    @pl.when(first)
    def _():
        run_s[...] = jnp.zeros(run_s.shape, F32)

    m = (jnp.dot(an_ref[0], wo_ref[0:CONV_WIDTH, :], preferred_element_type=F32)
         + jnp.dot(bn_ref[0], wo_ref[CONV_WIDTH:, :], preferred_element_type=F32))
    x1 = _layernorm(alpha * x_ref[0] + gt1_ref[0] * m) * l1g_ref[...] + l1b_ref[...]
    x1_ref[0] = x1
    h2 = _layernorm(x1) * (1.0 + sc2_ref[0]) + sh2_ref[0]
    _rows_to_tiles(h2_ref, h2)

    logits = jnp.dot(h2.astype(BF16), rw_ref[...], preferred_element_type=F32) + rb_ref[...]
    lane = lax.broadcasted_iota(jnp.int32, (tm, LANES), 1)
    lane_f = lane.astype(F32)
    lg = logits
    vals, sels = [], []
    chosen = jnp.zeros((tm, LANES), F32)
    for _ in range(TOP_K):
        mx = jnp.max(lg, axis=1, keepdims=True)
        idx = jnp.min(jnp.where(lg == mx, lane_f, float(LANES)), axis=1, keepdims=True)
        sel = lane_f == idx
        vals.append(mx)
        sels.append(idx)
        chosen = jnp.where(sel, 1.0, chosen)
        lg = jnp.where(sel, NEG, lg)

    es = [jnp.exp(v - vals[0]) for v in vals]
    denom = es[0] + es[1] + es[2] + es[3]

    run_s[0:1, :] = run_s[0:1, :] + jnp.sum(chosen, axis=0, keepdims=True)
    cnt_ref[...] = jnp.broadcast_to(run_s[0:1, :], cnt_ref.shape)

    meta = jnp.zeros((tm, LANES), jnp.int32)
    gates = jnp.zeros((tm, LANES), F32)
    for k in range(TOP_K):
        meta = jnp.where(lane == k, sels[k].astype(jnp.int32), meta)
        gates = jnp.where(lane == k, es[k] / denom, gates)
    meta_ref[0] = meta
    gates_ref[0] = gates


def _outproj(x, a_n, b_n, gt1, sc2, sh2, h2_all, tok0, w, tm, alpha):
    B, T, _ = x.shape
    blk0 = tok0 // tm
    tile = lambda c: pl.BlockSpec((1, tm, c), lambda b, j: (b, j, 0))
    modv = pl.BlockSpec((1, 1, D_MODEL), lambda b, j: (b, 0, 0))
    full = lambda shp: pl.BlockSpec(shp, lambda b, j: (0,) * len(shp))
    return pl.pallas_call(
        functools.partial(_outproj_kernel, tm=tm, alpha=alpha),
        grid=(B, T // tm),
        in_specs=[tile(D_MODEL), tile(CONV_WIDTH), tile(ATTN_WIDTH), modv, modv, modv,
                  full((CONV_WIDTH + ATTN_WIDTH, D_MODEL)), full((1, D_MODEL)), full((1, D_MODEL)),
                  full((D_MODEL, LANES)), full((1, LANES)), pl.BlockSpec(memory_space=pl.ANY)],
        out_specs=(tile(D_MODEL),
                   pl.BlockSpec((tm * ROW_TILE, LANES), lambda b, j: (blk0 + b * (T // tm) + j, 0)),
                   tile(LANES), tile(LANES), pl.BlockSpec((8, LANES), lambda b, j: (0, 0))),
        out_shape=(jax.ShapeDtypeStruct((B, T, D_MODEL), F32),
                   jax.ShapeDtypeStruct(h2_all.shape, F32),
                   jax.ShapeDtypeStruct((B, T, LANES), jnp.int32), jax.ShapeDtypeStruct((B, T, LANES), F32),
                   jax.ShapeDtypeStruct((8, LANES), F32)),
        scratch_shapes=[pltpu.VMEM((8, LANES), F32)],
        input_output_aliases={11: 1},
        compiler_params=_cparams(("arbitrary", "arbitrary")),
        name="outproj",
    )(x, a_n, b_n, gt1, sc2, sh2, w["w_out"], w["ln1_g"], w["ln1_b"], w["router_w"], w["router_b"], h2_all)


def _expert_kernel(order_ref, blk_e_ref, blk_j0_ref, blk_nv_ref, nxt_e_ref, used_ref,
                   h_ref, wgu_hbm, bgu_ref, wd_hbm, bd_ref, y_ref,
                   x0, x1, y0, y1, xb, act_s, wgu_st, wd_st, wgu_bf, wd_bf, gsem, ssem, wsem, *, bm, n_tok):
    s = pl.program_id(0)
    used = used_ref[0]
    dump0 = TOP_K * n_tok

    def weight_copies(e):
        return (pltpu.make_async_copy(wgu_hbm.at[e], wgu_st, wsem.at[0]),
                pltpu.make_async_copy(wd_hbm.at[e], wd_st, wsem.at[1]))

    tok_bits = (n_tok - 1).bit_length()

    def gather_start(j0, xbuf, sem, r):
        tok = jnp.bitwise_and(order_ref[j0 + r], (1 << tok_bits) - 1)
        pltpu.make_async_copy(h_ref.at[tok], xbuf.at[pl.ds(r * ROW_TILE, ROW_TILE)], sem).start()

    def scatter_start(j0, nv, ybuf, sem, r, odd):
        real = lax.shift_right_logical(order_ref[j0 + r], tok_bits)
        dump = dump0 + odd * bm + r
        valid = jnp.right_shift(r - nv, 31)
        dst = dump + jnp.bitwise_and(valid, real - dump)
        pltpu.make_async_copy(ybuf.at[pl.ds(r * ROW_TILE, ROW_TILE)], y_ref.at[dst], sem).start()

    def block_wait(buf, sem):
        pltpu.make_async_copy(buf, buf, sem).wait()

    def switch_weights(b):
        prev = blk_e_ref[jnp.maximum(b - 1, 0)]
        e = blk_e_ref[b]

        @pl.when((b == 0) | (e != prev))
        def _():
            for c in weight_copies(e):
                c.wait()
            wgu_bf[...] = wgu_st[...].astype(BF16)
            wd_bf[...] = wd_st[...].astype(BF16)

            @pl.when(nxt_e_ref[e] >= 0)
            def _():
                for c in weight_copies(nxt_e_ref[e]):
                    c.start()

    def block(b, x_in, y_out, g_next, x_next, gsem_next, s_prev, y_prev, ssem_prev, prev_odd, wait_y_free):
        gj0 = blk_j0_ref[g_next + 1]
        sj0 = blk_j0_ref[s_prev + 1]
        snv = blk_nv_ref[s_prev + 1]
        def issue(k, after=None):
            zero = 0
            if after is not None:
                bits = pltpu.bitcast(jnp.abs(after[0:SUBLANES, 0:LANES]), jnp.int32)
                zero = jnp.minimum(bits[0, 0], 0)
            half = DMA_BATCHES // 2
            q = k % half
            for r in range(q * bm // half, (q + 1) * bm // half):
                if k < half:
                    gather_start(gj0 + zero, x_next, gsem_next, r)
                else:
                    scatter_start(sj0 + zero, snv, y_prev, ssem_prev, r, prev_odd)

        xb[...] = _rows_from_tiles(x_in, bm).astype(BF16)
        e = blk_e_ref[b]
        cw = D_FF // 4
        prev = None
        for c in range(4):
            issue(2 * c, prev)
            gs, ls = slice(c * cw, (c + 1) * cw), slice(D_FF + c * cw, D_FF + (c + 1) * cw)
            g = jnp.dot(xb[...], wgu_bf[:, gs], preferred_element_type=F32) + bgu_ref[e][:, gs]
            issue(2 * c + 1, g)
            lin = jnp.dot(xb[...], wgu_bf[:, ls], preferred_element_type=F32) + bgu_ref[e][:, ls]
            prev = lin
            g = jnp.minimum(g, SWIGLU_LIMIT)
            lin = jnp.clip(lin, -SWIGLU_LIMIT, SWIGLU_LIMIT)
            act_s[:, gs] = (g * _sigmoid(SWIGLU_ALPHA * g) * (lin + 1.0)).astype(BF16)
        hw = D_MODEL // 2
        wait_y_free()
        for h in range(2):
            issue(8 + h, prev)
            y = jnp.dot(act_s[...], wd_bf[:, h * hw:(h + 1) * hw], preferred_element_type=F32)
            y = y + bd_ref[e][:, h * hw:(h + 1) * hw]
            prev = y
            for c in range(hw // LANES):
                y_out[pl.ds(h * (hw // LANES) + c, bm, stride=ROW_TILE), :] = y[:, c * LANES:(c + 1) * LANES]

    b0 = 2 * s
    b1 = b0 + 1

    @pl.when(b0 < used)
    def _():
        @pl.when(s == 0)
        def _():
            for c in weight_copies(blk_e_ref[0]):
                c.start()
            y1[...] = jnp.zeros(y1.shape, F32)

            def first(r, c):
                scatter_start(0, 0, y1, ssem.at[0], r, 0)
                gather_start(blk_j0_ref[1], x0, gsem.at[0], r)
                return c

            lax.fori_loop(0, bm, first, 0)
            block_wait(y1, ssem.at[0])

        def y0_free():
            @pl.when(s > 0)
            def _():
                block_wait(y0, ssem.at[0])

        def y1_free():
            block_wait(y1, ssem.at[1])

        switch_weights(b0)
        block_wait(x0, gsem.at[0])
        block(b0, x0, y0, b1, x1, gsem.at[1], b0 - 1, y1, ssem.at[1], 1, y0_free)

        switch_weights(b1)
        block_wait(x1, gsem.at[1])
        block(b1, x1, y1, b0 + 2, x0, gsem.at[0], b0, y0, ssem.at[0], 0, y1_free)

        @pl.when(b0 + 2 >= used)
        def _():
            def last(r, c):
                scatter_start(blk_j0_ref[b1 + 1], blk_nv_ref[b1 + 1], y1, ssem.at[1], r, 1)
                return c

            lax.fori_loop(0, bm, last, 0)
            block_wait(y0, ssem.at[0])
            block_wait(y1, ssem.at[1])
            block_wait(x0, gsem.at[0])


def _expert(order, blk_e, blk_j0, blk_nv, nxt_e, used, h2, w, bm):
    n_tok = h2.shape[0]
    nblk = blk_e.shape[0]
    f32buf = lambda shp: pltpu.VMEM(shp, F32)
    return pl.pallas_call(
        functools.partial(_expert_kernel, bm=bm, n_tok=n_tok),
        grid_spec=pltpu.PrefetchScalarGridSpec(
            num_scalar_prefetch=6,
            grid=(nblk // 2,),
            in_specs=[
                pl.BlockSpec(memory_space=pl.ANY),
                pl.BlockSpec(memory_space=pl.ANY),
                pl.BlockSpec((N_EXPERTS, 1, 2 * D_FF), lambda s, *_: (0, 0, 0)),
                pl.BlockSpec(memory_space=pl.ANY),
                pl.BlockSpec((N_EXPERTS, 1, D_MODEL), lambda s, *_: (0, 0, 0)),
            ],
            out_specs=pl.BlockSpec(memory_space=pl.ANY),
            scratch_shapes=[f32buf((bm * ROW_TILE, LANES)), f32buf((bm * ROW_TILE, LANES)),
                            f32buf((bm * ROW_TILE, LANES)), f32buf((bm * ROW_TILE, LANES)),
                            pltpu.VMEM((bm, D_MODEL), BF16), pltpu.VMEM((bm, D_FF), BF16),
                            f32buf((D_MODEL, 2 * D_FF)), f32buf((D_FF, D_MODEL)),
                            pltpu.VMEM((D_MODEL, 2 * D_FF), BF16), pltpu.VMEM((D_FF, D_MODEL), BF16),
                            pltpu.SemaphoreType.DMA((2,)), pltpu.SemaphoreType.DMA((2,)),
                            pltpu.SemaphoreType.DMA((2,))],
        ),
        out_shape=jax.ShapeDtypeStruct((TOP_K * n_tok + 2 * bm, ROW_TILE, LANES), F32),
        compiler_params=_cparams(("arbitrary",)),
        name="moe_expert",
    )(order, blk_e, blk_j0, blk_nv, nxt_e, used, h2, w["w_gu"], w["b_gu"], w["w_down"], w["b_down"])


def _combine_kernel(y0_ref, y1_ref, y2_ref, y3_ref, x1_ref, gates_ref, gt2_ref, l2g_ref, l2b_ref, o_ref, *, alpha):
    gates = gates_ref[...]
    tm = gates.shape[0]
    f = gates[:, 0:1] * _rows_from_tiles(y0_ref, tm)
    for k, y_ref in enumerate((y1_ref, y2_ref, y3_ref), start=1):
        f = f + gates[:, k:k + 1] * _rows_from_tiles(y_ref, tm)
    o_ref[...] = _layernorm(alpha * x1_ref[...] + gt2_ref[0] * f) * l2g_ref[...] + l2b_ref[...]


def _combine(y_rows, x1, gates, gt2, w, tm, tokens_per_batch, alpha, n_tok, tok0):
    N = x1.shape[0]
    per_b = tokens_per_batch // tm
    nt = N // tm
    assert n_tok % tm == 0 and tok0 % tm == 0
    slot = lambda k: pl.BlockSpec((tm * ROW_TILE, LANES), lambda i: ((k * n_tok + tok0) // tm + i, 0))
    return pl.pallas_call(
        functools.partial(_combine_kernel, alpha=alpha),
        grid=(nt,),
        in_specs=[slot(0), slot(1), slot(2), slot(3),
                  pl.BlockSpec((tm, D_MODEL), lambda i: (i, 0)),
                  pl.BlockSpec((tm, LANES), lambda i: (i, 0)),
                  pl.BlockSpec((1, 1, D_MODEL), lambda i: (i // per_b, 0, 0)),
                  pl.BlockSpec((1, D_MODEL), lambda i: (0, 0)),
                  pl.BlockSpec((1, D_MODEL), lambda i: (0, 0))],
        out_specs=pl.BlockSpec((tm, D_MODEL), lambda i: (i, 0)),
        out_shape=jax.ShapeDtypeStruct((N, D_MODEL), F32),
        compiler_params=_cparams(("arbitrary",)),
        name="moe_combine",
    )(y_rows, y_rows, y_rows, y_rows, x1, gates, gt2, w["ln2_g"], w["ln2_b"])


def _moe_experts(h2, idx, counts, w, bm):
    N = h2.shape[0]
    n_rows = N * TOP_K
    nblk = (n_rows + N_EXPERTS * (bm - 1)) // bm
    nblk += nblk % 2
    experts = jnp.arange(N_EXPERTS, dtype=jnp.int32)
    id_bits = (n_rows - 1).bit_length()
    tok_bits = (N - 1).bit_length()
    assert tok_bits + (TOP_K * N - 1).bit_length() <= 32
    keys = jnp.left_shift(idx.reshape(-1), id_bits) + jnp.arange(n_rows, dtype=jnp.int32)
    keys = jnp.concatenate([keys, jnp.full(((1 << id_bits) - n_rows,), jnp.iinfo(jnp.int32).max, jnp.int32)])
    flat = jnp.bitwise_and(jnp.sort(keys)[:n_rows], (1 << id_bits) - 1).astype(jnp.uint32)
    tok = jnp.right_shift(flat, 2)
    row = jnp.bitwise_and(flat, TOP_K - 1) * N + tok
    order = lax.bitcast_convert_type(jnp.left_shift(row, tok_bits) | tok, jnp.int32)
    order = jnp.concatenate([order, jnp.zeros((bm,), jnp.int32)])
    nb_e = (counts + bm - 1) // bm
    blk_end = jnp.cumsum(nb_e)
    first_blk = blk_end - nb_e
    start_sorted = jnp.cumsum(counts) - counts
    used = blk_end[-1].astype(jnp.int32)
    b = jnp.arange(-1, nblk + 1, dtype=jnp.int32)
    bc = jnp.clip(b, 0, used - 1)
    e = jnp.minimum(jnp.sum(blk_end[None, :] <= bc[:, None], axis=1), N_EXPERTS - 1).astype(jnp.int32)
    pick = lambda table: jnp.sum(jnp.where(e[:, None] == experts, table, 0), axis=1)
    local = bc - pick(first_blk)
    blk_j0 = (pick(start_sorted) + local * bm).astype(jnp.int32)
    blk_nv = jnp.where((b >= 0) & (b < used), jnp.minimum(bm, pick(counts) - local * bm), 0).astype(jnp.int32)
    blk_e = e[1:nblk + 1]
    later = (experts[None, :] > experts[:, None]) & (counts[None, :] > 0)
    nxt = jnp.min(jnp.where(later, experts[None, :], N_EXPERTS), axis=1)
    nxt_e = jnp.where(nxt < N_EXPERTS, nxt, -1).astype(jnp.int32)
    y_rows = _expert(order, blk_e, blk_j0, blk_nv, nxt_e, used.reshape(1), h2, w, bm)
    return y_rows.reshape(-1, LANES)


def _rope_tables(pos):
    half = QK_ROPE // 2
    inv = ROPE_THETA ** (-jnp.arange(half, dtype=F32) / half)
    ang = pos.astype(F32)[:, None] * inv[None, :]
    cos, sin = jnp.cos(ang), jnp.sin(ang)
    cos32 = jnp.concatenate([cos, cos], axis=1)
    sin32 = jnp.concatenate([-sin, sin], axis=1)
    return jnp.tile(cos32, (1, LANES // QK_ROPE)), jnp.tile(sin32, (1, LANES // QK_ROPE))


def _swap_halves(w32):
    shp = w32.shape
    w = w32.reshape(shp[:-1] + (shp[-1] // QK_ROPE, 2, QK_ROPE // 2))
    return w[..., ::-1, :].reshape(shp)


def _prep_weights(l, w_in, conv_w, g_qa, w_qb, g_kva, w_kvb, g_out_conv, g_out_attn, w_out,
                  ln1_g, ln1_b, router_w, router_b, w_gu, b_gu, w_down, b_down, ln2_g, ln2_b):
    w = {}
    wi = w_in[l]
    k_r = wi[:, _O_KR:_O_KR + QK_ROPE]
    rep = LANES // QK_ROPE
    w["w_in"] = jnp.concatenate([wi[:, :_O_KR], jnp.tile(k_r, (1, rep)), jnp.tile(_swap_halves(k_r), (1, rep))],
                                axis=1).astype(BF16)
    w["conv_w"] = conv_w[l]
    w["g_qa"] = g_qa[l].reshape(1, Q_LORA)
    w["g_kva"] = g_kva[l].reshape(1, KV_LORA)
    w["g_oc"] = g_out_conv[l].reshape(1, CONV_WIDTH)
    w["g_oa"] = g_out_attn[l].reshape(1, ATTN_WIDTH)
    wq = w_qb[l].reshape(Q_LORA, N_HEADS, QK_NOPE + QK_ROPE)
    w["wq_nope"] = wq[:, :, :QK_NOPE].reshape(Q_LORA, N_HEADS * QK_NOPE).astype(BF16)
    wq_rope = wq[:, :, QK_NOPE:].reshape(Q_LORA, N_HEADS * QK_ROPE)
    w["wq_rope"] = wq_rope.astype(BF16)
    w["wq_rope_sw"] = _swap_halves(wq_rope).astype(BF16)
    w_uk = jnp.transpose(w_kvb[l][:, :, :QK_NOPE], (1, 2, 0))
    w_uv = jnp.transpose(w_kvb[l][:, :, QK_NOPE:], (1, 0, 2))
    zk = jnp.zeros((QK_NOPE, KV_LORA), F32)
    zv = jnp.zeros((KV_LORA, V_HEAD), F32)
    w["w_uk_pair"] = jnp.stack([
        jnp.concatenate([jnp.concatenate([w_uk[2 * p], zk], axis=1),
                         jnp.concatenate([zk, w_uk[2 * p + 1]], axis=1)], axis=0)
        for p in range(N_HEADS // 2)]).astype(BF16)
    w["w_uv_pair"] = jnp.stack([
        jnp.concatenate([jnp.concatenate([w_uv[2 * p], zv], axis=1),
                         jnp.concatenate([zv, w_uv[2 * p + 1]], axis=1)], axis=0)
        for p in range(N_HEADS // 2)]).astype(BF16)
    w["w_out"] = w_out[l].astype(BF16)
    w["ln1_g"] = ln1_g[l].reshape(1, D_MODEL)
    w["ln1_b"] = ln1_b[l].reshape(1, D_MODEL)
    w["ln2_g"] = ln2_g[l].reshape(1, D_MODEL)
    w["ln2_b"] = ln2_b[l].reshape(1, D_MODEL)
    w["router_w"] = jnp.pad(router_w[l], ((0, 0), (0, LANES - N_EXPERTS))).astype(BF16)
    w["router_b"] = jnp.concatenate([router_b[l], jnp.full((LANES - N_EXPERTS,), NEG, F32)]).reshape(1, LANES)
    w["w_gu"] = w_gu[l]
    w["b_gu"] = b_gu[l].reshape(N_EXPERTS, 1, 2 * D_FF)
    w["w_down"] = w_down[l]
    w["b_down"] = b_down[l].reshape(N_EXPERTS, 1, D_MODEL)
    return w


def _mixer(x, mod, conv_prev, past, pos0, h2_all, tok0, w, alpha, *, tm_in, tm_out, tq=128, tk=512):
    B, T, _ = x.shape
    sh1, sc1, gt1, sh2, sc2, gt2 = [mod[:, None, i * D_MODEL:(i + 1) * D_MODEL] for i in range(N_MOD)]
    cos_t, sin_t = _rope_tables(pos0 + jnp.arange(T, dtype=jnp.int32))
    a_n, qlat, qrope, kcat, ckv, krope, conv_new, *vt = _inproj(x, sc1, sh1, conv_prev, cos_t, sin_t, w, tm_in,
                                                                 with_vt=past is None)
    if past is None:
        b_n = _attn_prompt(qlat, qrope, kcat, vt[0], w, tq, tk)
    else:
        b_n = _attn_sample(qlat, qrope, kcat, past[0], past[1], w, tk)
    x1, h2_all, meta, gates, cnt = _outproj(x, a_n, b_n, gt1, sc2, sh2, h2_all, tok0, w, tm_out, alpha)
    N = B * T
    route = dict(x1=x1.reshape(N, D_MODEL), idx=meta.reshape(N, LANES)[:, :TOP_K], gates=gates.reshape(N, LANES),
                 counts=cnt[0, :N_EXPERTS].astype(jnp.int32), gt2=gt2, shape=(B, T))
    return h2_all, route, ckv, krope, conv_new


def kernel(x_prompt, x_sample, c_prompt, c_sample, cache_ckv, cache_krope, state_conv, w_ada, b_ada, w_in, conv_w, g_qa, w_qb, g_kva, w_kvb, g_out_conv, g_out_attn, w_out, ln1_g, ln1_b, router_w, router_b, w_gu, b_gu, w_down, b_down, ln2_g, ln2_b):
    depth = w_ada.shape[0]
    Bp, Tp, _ = x_prompt.shape
    Bs, Ts, _ = x_sample.shape
    past_len = cache_ckv.shape[2]
    assert Ts == CHUNK and past_len % 512 == 0 and Tp % 1024 == 0
    alpha = (2.0 * depth) ** 0.25
    xp, xs = x_prompt, x_sample
    outs = [[] for _ in range(6)]
    c_all = jnp.concatenate([c_prompt, c_sample, jnp.zeros((16 - Bp - Bs, D_MODEL), F32)], axis=0)
    for l in range(depth):
        w = _prep_weights(l, w_in, conv_w, g_qa, w_qb, g_kva, w_kvb, g_out_conv, g_out_attn, w_out,
                          ln1_g, ln1_b, router_w, router_b, w_gu, b_gu, w_down, b_down, ln2_g, ln2_b)
        mod = _ada(c_all, w_ada[l], b_ada[l])
        n_p, n_s = Bp * Tp, Bs * Ts
        n_tok = n_p + n_s
        h2_all = jnp.zeros((n_tok * ROW_TILE, LANES), F32)
        h2_all, rp, ckv_p, kr_p, cv_p = _mixer(xp, mod[:Bp], jnp.zeros((Bp, CONV_K - 1, CONV_WIDTH), F32), None, 0,
                                               h2_all, 0, w, alpha, tm_in=1024, tm_out=512)
        h2_all, rs, ckv_s, kr_s, cv_s = _mixer(xs, mod[Bp:Bp + Bs], state_conv[l], (cache_ckv[l], cache_krope[l]),
                                               past_len, h2_all, n_p, w, alpha, tm_in=Ts, tm_out=Ts)
        y_rows = _moe_experts(h2_all.reshape(n_tok, ROW_TILE, LANES), jnp.concatenate([rp["idx"], rs["idx"]]),
                              rp["counts"] + rs["counts"], w, bm=256)
        xp = _combine(y_rows, rp["x1"], rp["gates"], rp["gt2"], w, 256, Tp, alpha, n_tok, 0).reshape(rp["shape"] + (D_MODEL,))
        xs = _combine(y_rows, rs["x1"], rs["gates"], rs["gt2"], w, Ts, Ts, alpha, n_tok, n_p).reshape(rs["shape"] + (D_MODEL,))
        for o, v in zip(outs, (ckv_p, kr_p, cv_p, ckv_s, kr_s, cv_s)):
            o.append(v)
    return (xp, xs) + tuple(jnp.stack(o) for o in outs)
```

```python
import functools
import math

import jax
import jax.numpy as jnp
from jax import lax
from jax.experimental import pallas as pl
from jax.experimental.pallas import tpu as pltpu

F32 = jnp.float32
BF16 = jnp.bfloat16

D_MODEL = 1024
CHUNK = 64
CONV_WIDTH = 512
CONV_K = 3
N_HEADS = 8
QK_NOPE = 64
QK_ROPE = 32
V_HEAD = 64
Q_LORA = 256
KV_LORA = 128
ATTN_WIDTH = N_HEADS * V_HEAD
ROPE_THETA = 10000.0
ATTN_SCALE = 1.0 / math.sqrt(QK_NOPE + QK_ROPE)
Q_SCALE = ATTN_SCALE * math.log2(math.e)
N_EXPERTS = 32
TOP_K = 4
D_FF = 1024
SWIGLU_LIMIT = 7.0
SWIGLU_ALPHA = 1.702
N_MOD = 6
LN_EPS = 1e-5
RMS_EPS = 1e-6

LANES = 128
SUBLANES = 8
DMA_BATCHES = 10
NEG = -1e30
VMEM_LIMIT = 56 * 1024 * 1024

_O_XB, _O_XC, _O_XV = 0, CONV_WIDTH, 2 * CONV_WIDTH
_O_QA = 3 * CONV_WIDTH
_O_KVA = _O_QA + Q_LORA
_O_KR = _O_KVA + KV_LORA
_O_KRS = _O_KR + LANES
IN_COLS_EXT = _O_KRS + LANES


def _cparams(sem):
    return pltpu.CompilerParams(dimension_semantics=sem, vmem_limit_bytes=VMEM_LIMIT)


def _layernorm(x):
    mu = jnp.mean(x, axis=-1, keepdims=True)
    xc = x - mu
    var = jnp.mean(xc * xc, axis=-1, keepdims=True)
    return xc * lax.rsqrt(var + LN_EPS)


def _rms(x):
    return x * lax.rsqrt(jnp.mean(x * x, axis=-1, keepdims=True) + RMS_EPS)


def _sigmoid(x):
    return 1.0 / (1.0 + jnp.exp(-x))


ROW_TILE = D_MODEL // LANES


def _rows_from_tiles(ref, n):
    return jnp.concatenate([ref[pl.ds(c, n, stride=ROW_TILE), :] for c in range(ROW_TILE)], axis=1)


def _rows_to_tiles(ref, x):
    n = x.shape[0]
    for c in range(ROW_TILE):
        ref[pl.ds(c, n, stride=ROW_TILE), :] = x[:, c * LANES:(c + 1) * LANES]


def _ada_kernel(c_ref, w_ref, b_ref, o_ref):
    c = c_ref[...]
    s = (c * _sigmoid(c)).astype(BF16)
    o_ref[...] = jnp.dot(s, w_ref[...].astype(BF16), preferred_element_type=F32) + b_ref[...]


def _ada(c_all, w_ada, b_ada):
    rows = c_all.shape[0]
    ncol = w_ada.shape[1]
    tn = 1024
    return pl.pallas_call(
        _ada_kernel,
        grid=(ncol // tn,),
        in_specs=[pl.BlockSpec((rows, D_MODEL), lambda j: (0, 0)),
                  pl.BlockSpec((D_MODEL, tn), lambda j: (0, j)),
                  pl.BlockSpec((1, tn), lambda j: (0, j))],
        out_specs=pl.BlockSpec((rows, tn), lambda j: (0, j)),
        out_shape=jax.ShapeDtypeStruct((rows, ncol), F32),
        compiler_params=_cparams(("arbitrary",)),
        name="ada",
    )(c_all, w_ada, b_ada.reshape(1, ncol))


def _inproj_kernel(x_ref, sc_ref, sh_ref, win_ref, cw_ref, cprev_ref, gqa_ref, gkva_ref, goc_ref,
                   wqn_ref, wqr_ref, wqrs_ref, wuk_ref, cos_ref, sin_ref,
                   an_ref, qlat_ref, qrope_ref, kcat_ref, ckv_ref, krope_ref, cnew_ref,
                   *rest, tm, with_vt):
    vt_ref, ubuf = rest if with_vt else (None, rest[0])
    j = pl.program_id(1)
    x = x_ref[0]
    h = _layernorm(x) * (1.0 + sc_ref[0]) + sh_ref[0]
    proj = jnp.dot(h.astype(BF16), win_ref[...], preferred_element_type=F32)
    xb = proj[:, _O_XB:_O_XB + CONV_WIDTH]
    xc = proj[:, _O_XC:_O_XC + CONV_WIDTH]
    xv = proj[:, _O_XV:_O_XV + CONV_WIDTH]
    q_a = proj[:, _O_QA:_O_QA + Q_LORA]
    kv_a = proj[:, _O_KVA:_O_KVA + KV_LORA]
    kr4 = proj[:, _O_KR:_O_KR + LANES]
    kr4s = proj[:, _O_KRS:_O_KRS + LANES]

    u = xc * xv

    @pl.when(j == 0)
    def _():
        ubuf[6:8, :] = cprev_ref[0]

    ubuf[8:8 + tm, :] = u
    conv = (cw_ref[0:1, :] * ubuf[6:6 + tm, :] + cw_ref[1:2, :] * ubuf[7:7 + tm, :]
            + cw_ref[2:3, :] * u)
    ubuf[0:8, :] = ubuf[tm:tm + 8, :]
    cnew_ref[0] = u[tm - (CONV_K - 1):tm, :]
    an_ref[0] = (_rms(xb * conv) * goc_ref[...]).astype(BF16)

    cos = cos_ref[...]
    sin = sin_ref[...]

    ckv = _rms(kv_a) * gkva_ref[...]
    kro4 = kr4 * cos + kr4s * sin
    ckv_ref[0] = ckv
    krope_ref[0] = kro4[:, :QK_ROPE]
    kcat_ref[0] = jnp.concatenate([ckv, kro4], axis=1).astype(BF16)
    if with_vt:
        vt_ref[0] = ckv.T.astype(BF16)

    qn = (_rms(q_a) * gqa_ref[...]).astype(BF16)
    q_nope = jnp.dot(qn, wqn_ref[...], preferred_element_type=F32)
    xr = jnp.dot(qn, wqr_ref[...], preferred_element_type=F32)
    xrs = jnp.dot(qn, wqrs_ref[...], preferred_element_type=F32)
    for g in range(2):
        sl = slice(g * LANES, (g + 1) * LANES)
        qrope_ref[0, :, sl] = ((xr[:, sl] * cos + xrs[:, sl] * sin) * Q_SCALE).astype(BF16)
    for p in range(N_HEADS // 2):
        qp = q_nope[:, p * LANES:(p + 1) * LANES].astype(BF16)
        ql = jnp.dot(qp, wuk_ref[p], preferred_element_type=F32)
        qlat_ref[0, :, p * 2 * KV_LORA:(p + 1) * 2 * KV_LORA] = (ql * Q_SCALE).astype(BF16)


def _inproj(x, sc1, sh1, conv_prev, cos_t, sin_t, w, tm, with_vt):
    B, T, _ = x.shape
    nt = T // tm
    full = lambda shp: pl.BlockSpec(shp, lambda b, j: (0,) * len(shp))
    vt_shape = (jax.ShapeDtypeStruct((B, KV_LORA, T), BF16),) if with_vt else ()
    vt_spec = (pl.BlockSpec((1, KV_LORA, tm), lambda b, j: (b, 0, j)),) if with_vt else ()
    out_shapes = (
        jax.ShapeDtypeStruct((B, T, CONV_WIDTH), BF16),
        jax.ShapeDtypeStruct((B, T, N_HEADS * KV_LORA), BF16),
        jax.ShapeDtypeStruct((B, T, 2 * LANES), BF16),
        jax.ShapeDtypeStruct((B, T, 2 * LANES), BF16),
        jax.ShapeDtypeStruct((B, T, KV_LORA), F32),
        jax.ShapeDtypeStruct((B, T, QK_ROPE), F32),
        jax.ShapeDtypeStruct((B, CONV_K - 1, CONV_WIDTH), F32),
    ) + vt_shape
    tile = lambda c: pl.BlockSpec((1, tm, c), lambda b, j: (b, j, 0))
    return pl.pallas_call(
        functools.partial(_inproj_kernel, tm=tm, with_vt=with_vt),
        grid=(B, nt),
        in_specs=[
            tile(D_MODEL),
            pl.BlockSpec((1, 1, D_MODEL), lambda b, j: (b, 0, 0)),
            pl.BlockSpec((1, 1, D_MODEL), lambda b, j: (b, 0, 0)),
            full((D_MODEL, IN_COLS_EXT)),
            full((CONV_K, CONV_WIDTH)),
            pl.BlockSpec((1, CONV_K - 1, CONV_WIDTH), lambda b, j: (b, 0, 0)),
            full((1, Q_LORA)), full((1, KV_LORA)), full((1, CONV_WIDTH)),
            full((Q_LORA, N_HEADS * QK_NOPE)), full((Q_LORA, 2 * LANES)), full((Q_LORA, 2 * LANES)),
            full((N_HEADS // 2, LANES, 2 * KV_LORA)),
            pl.BlockSpec((tm, LANES), lambda b, j: (j, 0)),
            pl.BlockSpec((tm, LANES), lambda b, j: (j, 0)),
        ],
        out_specs=(tile(CONV_WIDTH), tile(N_HEADS * KV_LORA), tile(2 * LANES), tile(2 * LANES),
                   tile(KV_LORA), tile(QK_ROPE),
                   pl.BlockSpec((1, CONV_K - 1, CONV_WIDTH), lambda b, j: (b, 0, 0))) + vt_spec,
        out_shape=out_shapes,
        scratch_shapes=[pltpu.VMEM((tm + 8, CONV_WIDTH), F32)],
        compiler_params=_cparams(("arbitrary", "arbitrary")),
        name="inproj",
    )(x, sc1, sh1, w["w_in"], w["conv_w"], conv_prev, w["g_qa"], w["g_kva"], w["g_oc"],
      w["wq_nope"], w["wq_rope"], w["wq_rope_sw"], w["w_uk_pair"], cos_t, sin_t)


def _stack_queries(qlat_ref, qrope_ref, qs, tq):
    lane = lax.broadcasted_iota(jnp.int32, (tq, LANES), 1)
    for h in range(N_HEADS):
        g, i = divmod(h, 4)
        rope = qrope_ref[0, :, g * LANES:(g + 1) * LANES]
        keep = (lane >= i * QK_ROPE) & (lane < (i + 1) * QK_ROPE)
        qs[h * tq:(h + 1) * tq, 0:KV_LORA] = qlat_ref[0, :, h * KV_LORA:(h + 1) * KV_LORA]
        qs[h * tq:(h + 1) * tq, KV_LORA:KV_LORA + LANES] = jnp.where(keep, rope, jnp.zeros_like(rope))


def _softmax_step(qs, k, v, m_s, l_s, acc_s):
    tk = k.shape[0]
    s = lax.dot_general(qs[...], k, (((1,), (1,)), ((), ())), preferred_element_type=F32)
    m_prev = m_s[...]
    m_new = jnp.maximum(m_prev, jnp.max(s, axis=1, keepdims=True))
    alpha = jnp.exp2(m_prev - m_new)
    if tk % LANES == 0:
        p = jnp.exp2(s - jnp.tile(m_new, (1, tk // LANES)))
    else:
        p = jnp.exp2(s - m_new[:, :tk])
    l_s[...] = alpha * l_s[...] + jnp.sum(p, axis=1, keepdims=True)
    acc_s[...] = alpha * acc_s[...] + jnp.dot(p.astype(BF16), v, preferred_element_type=F32)
    m_s[...] = m_new


def _attn_epilogue(acc_s, l_s, wuv_ref, g_ref, o_ref, tq):
    o = acc_s[...] / l_s[...]
    parts = []
    for p in range(N_HEADS // 2):
        op = jnp.concatenate([o[(2 * p) * tq:(2 * p + 1) * tq], o[(2 * p + 1) * tq:(2 * p + 2) * tq]], axis=1)
        parts.append(jnp.dot(op.astype(BF16), wuv_ref[p], preferred_element_type=F32))
    b = jnp.concatenate(parts, axis=1)
    o_ref[0] = (_rms(b) * g_ref[...]).astype(BF16)


def _attn_prompt_kernel(qlat_ref, qrope_ref, k_ref, vt_ref, wuvt_ref, g_ref, o_ref, qs, m_s, l_s, acc_s, sa, sb,
                        *, tq, tk):
    i = pl.program_id(1)
    M = N_HEADS * tq
    _stack_queries(qlat_ref, qrope_ref, qs, tq)
    m_s[...] = jnp.full(m_s.shape, NEG, F32)
    l_s[...] = jnp.zeros(l_s.shape, F32)
    acc_s[...] = jnp.zeros(acc_s.shape, F32)
    q0 = i * tq
    n_full = (q0 + CHUNK) // tk

    def scores(t, dst):
        k = k_ref[0, pl.ds(pl.multiple_of(t * tk, tk), tk), :]
        dst[...] = lax.dot_general(k, qs[...], (((1,), (1,)), ((), ())), preferred_element_type=F32)

    def update(t, src, limit=None):
        start = pl.multiple_of(t * tk, tk)
        vt = vt_ref[0, :, pl.ds(start, tk)]
        s = src[...]
        if limit is not None:
            kpos = start + lax.broadcasted_iota(jnp.int32, (tk, 1), 0)
            s = jnp.where(kpos < limit, s, NEG)
        m_prev = m_s[...]
        m_new = jnp.maximum(m_prev, jnp.max(s, axis=0, keepdims=True))
        alpha = jnp.exp2(m_prev - m_new)
        p = jnp.exp2(s - m_new)
        l_s[...] = alpha * l_s[...] + jnp.sum(p, axis=0, keepdims=True)
        acc_s[...] = alpha * acc_s[...] + jnp.dot(vt, p.astype(BF16), preferred_element_type=F32)
        m_s[...] = m_new

    scores(0, sa)

    def body(j, carry):
        t = 2 * j
        scores(t + 1, sb)
        update(t, sa)
        scores(t + 2, sa)
        update(t + 1, sb)
        return carry

    lax.fori_loop(0, n_full // 2, body, 0)

    col_t = jnp.bitwise_and(lax.broadcasted_iota(jnp.int32, (1, M), 1), tq - 1)
    limit = q0 + (jnp.right_shift(col_t, CHUNK.bit_length() - 1) + 1) * CHUNK
    odd = n_full % 2

    @pl.when(odd == 0)
    def _():
        update(n_full, sa, limit)

    @pl.when(odd == 1)
    def _():
        scores(n_full, sb)
        update(n_full - 1, sa)
        update(n_full, sb, limit)

    o_t = acc_s[...] / l_s[...]
    parts = []
    for p in range(N_HEADS // 2):
        pair = jnp.concatenate([o_t[:, (2 * p) * tq:(2 * p + 1) * tq], o_t[:, (2 * p + 1) * tq:(2 * p + 2) * tq]],
                               axis=0)
        parts.append(jnp.dot(wuvt_ref[p], pair.astype(BF16), preferred_element_type=F32))
    b = jnp.concatenate(parts, axis=0).T
    o_ref[0] = (_rms(b) * g_ref[...]).astype(BF16)


def _attn_prompt(qlat, qrope, kcat, vt, w, tq, tk):
    B, T, _ = qlat.shape
    M = N_HEADS * tq
    return pl.pallas_call(
        functools.partial(_attn_prompt_kernel, tq=tq, tk=tk),
        grid=(B, T // tq),
        in_specs=[
            pl.BlockSpec((1, tq, N_HEADS * KV_LORA), lambda b, i: (b, i, 0)),
            pl.BlockSpec((1, tq, 2 * LANES), lambda b, i: (b, i, 0)),
            pl.BlockSpec((1, T, 2 * LANES), lambda b, i: (b, 0, 0)),
            pl.BlockSpec((1, KV_LORA, T), lambda b, i: (b, 0, 0)),
            pl.BlockSpec((N_HEADS // 2, LANES, 2 * KV_LORA), lambda b, i: (0, 0, 0)),
            pl.BlockSpec((1, ATTN_WIDTH), lambda b, i: (0, 0)),
        ],
        out_specs=pl.BlockSpec((1, tq, ATTN_WIDTH), lambda b, i: (b, i, 0)),
        out_shape=jax.ShapeDtypeStruct((B, T, ATTN_WIDTH), BF16),
        scratch_shapes=[pltpu.VMEM((M, 2 * LANES), BF16), pltpu.VMEM((1, M), F32),
                        pltpu.VMEM((1, M), F32), pltpu.VMEM((KV_LORA, M), F32),
                        pltpu.VMEM((tk, M), F32), pltpu.VMEM((tk, M), F32)],
        compiler_params=_cparams(("arbitrary", "arbitrary")),
        name="attn_prompt",
    )(qlat, qrope, kcat, vt, jnp.swapaxes(w["w_uv_pair"], 1, 2), w["g_oa"])


def _attn_sample_kernel(qlat_ref, qrope_ref, knew_ref, pckv_ref, pkr_ref, wuv_ref, g_ref, o_ref,
                        qs, m_s, l_s, acc_s, *, tq, tk, n_past):
    _stack_queries(qlat_ref, qrope_ref, qs, tq)
    m_s[...] = jnp.full(m_s.shape, NEG, F32)
    l_s[...] = jnp.zeros(l_s.shape, F32)
    acc_s[...] = jnp.zeros(acc_s.shape, F32)

    def body(t, carry):
        start = pl.multiple_of(t * tk, tk)
        ck = pckv_ref[0, pl.ds(start, tk), :]
        kr = pkr_ref[0, pl.ds(start, tk), :]
        k = jnp.concatenate([ck, kr, kr, kr, kr], axis=1).astype(BF16)
        _softmax_step(qs, k, k[:, :KV_LORA], m_s, l_s, acc_s)
        return carry

    lax.fori_loop(0, n_past // tk, body, 0)
    k = knew_ref[0]
    _softmax_step(qs, k, k[:, :KV_LORA], m_s, l_s, acc_s)
    _attn_epilogue(acc_s, l_s, wuv_ref, g_ref, o_ref, tq)


def _attn_sample(qlat, qrope, kcat, past_ckv, past_krope, w, tk):
    B, T, _ = qlat.shape
    n_past = past_ckv.shape[1]
    M = N_HEADS * T
    per_b = lambda r, c: pl.BlockSpec((1, r, c), lambda b: (b, 0, 0))
    return pl.pallas_call(
        functools.partial(_attn_sample_kernel, tq=T, tk=tk, n_past=n_past),
        grid=(B,),
        in_specs=[per_b(T, N_HEADS * KV_LORA), per_b(T, 2 * LANES), per_b(T, 2 * LANES),
                  per_b(n_past, KV_LORA), per_b(n_past, QK_ROPE),
                  pl.BlockSpec((N_HEADS // 2, 2 * KV_LORA, LANES), lambda b: (0, 0, 0)),
                  pl.BlockSpec((1, ATTN_WIDTH), lambda b: (0, 0))],
        out_specs=per_b(T, ATTN_WIDTH),
        out_shape=jax.ShapeDtypeStruct((B, T, ATTN_WIDTH), BF16),
        scratch_shapes=[pltpu.VMEM((M, 2 * LANES), BF16), pltpu.VMEM((M, LANES), F32),
                        pltpu.VMEM((M, LANES), F32), pltpu.VMEM((M, KV_LORA), F32)],
        compiler_params=_cparams(("arbitrary",)),
        name="attn_sample",
    )(qlat, qrope, kcat, past_ckv, past_krope, w["w_uv_pair"], w["g_oa"])


def _outproj_kernel(x_ref, an_ref, bn_ref, gt1_ref, sc2_ref, sh2_ref, wo_ref, l1g_ref, l1b_ref,
                    rw_ref, rb_ref, h2_all_ref, x1_ref, h2_ref, meta_ref, gates_ref, cnt_ref, run_s,
                    *, tm, alpha):
    del h2_all_ref
    first = (pl.program_id(0) == 0) & (pl.program_id(1) == 0)

    @pl.when(first)
    def _():
        run_s[...] = jnp.zeros(run_s.shape, F32)

    m = (jnp.dot(an_ref[0], wo_ref[0:CONV_WIDTH, :], preferred_element_type=F32)
         + jnp.dot(bn_ref[0], wo_ref[CONV_WIDTH:, :], preferred_element_type=F32))
    x1 = _layernorm(alpha * x_ref[0] + gt1_ref[0] * m) * l1g_ref[...] + l1b_ref[...]
    x1_ref[0] = x1
    h2 = _layernorm(x1) * (1.0 + sc2_ref[0]) + sh2_ref[0]
    _rows_to_tiles(h2_ref, h2)

    logits = jnp.dot(h2.astype(BF16), rw_ref[...], preferred_element_type=F32) + rb_ref[...]
    lane = lax.broadcasted_iota(jnp.int32, (tm, LANES), 1)
    lane_f = lane.astype(F32)
    lg = logits
    vals, sels = [], []
    chosen = jnp.zeros((tm, LANES), F32)
    for _ in range(TOP_K):
        mx = jnp.max(lg, axis=1, keepdims=True)
        idx = jnp.min(jnp.where(lg == mx, lane_f, float(LANES)), axis=1, keepdims=True)
        sel = lane_f == idx
        vals.append(mx)
        sels.append(idx)
        chosen = jnp.where(sel, 1.0, chosen)
        lg = jnp.where(sel, NEG, lg)

    es = [jnp.exp(v - vals[0]) for v in vals]
    denom = es[0] + es[1] + es[2] + es[3]

    run_s[0:1, :] = run_s[0:1, :] + jnp.sum(chosen, axis=0, keepdims=True)
    cnt_ref[...] = jnp.broadcast_to(run_s[0:1, :], cnt_ref.shape)

    meta = jnp.zeros((tm, LANES), jnp.int32)
    gates = jnp.zeros((tm, LANES), F32)
    for k in range(TOP_K):
        meta = jnp.where(lane == k, sels[k].astype(jnp.int32), meta)
        gates = jnp.where(lane == k, es[k] / denom, gates)
    meta_ref[0] = meta
    gates_ref[0] = gates


def _outproj(x, a_n, b_n, gt1, sc2, sh2, h2_all, tok0, w, tm, alpha):
    B, T, _ = x.shape
    blk0 = tok0 // tm
    tile = lambda c: pl.BlockSpec((1, tm, c), lambda b, j: (b, j, 0))
    modv = pl.BlockSpec((1, 1, D_MODEL), lambda b, j: (b, 0, 0))
    full = lambda shp: pl.BlockSpec(shp, lambda b, j: (0,) * len(shp))
    return pl.pallas_call(
        functools.partial(_outproj_kernel, tm=tm, alpha=alpha),
        grid=(B, T // tm),
        in_specs=[tile(D_MODEL), tile(CONV_WIDTH), tile(ATTN_WIDTH), modv, modv, modv,
                  full((CONV_WIDTH + ATTN_WIDTH, D_MODEL)), full((1, D_MODEL)), full((1, D_MODEL)),
                  full((D_MODEL, LANES)), full((1, LANES)), pl.BlockSpec(memory_space=pl.ANY)],
        out_specs=(tile(D_MODEL),
                   pl.BlockSpec((tm * ROW_TILE, LANES), lambda b, j: (blk0 + b * (T // tm) + j, 0)),
                   tile(LANES), tile(LANES), pl.BlockSpec((8, LANES), lambda b, j: (0, 0))),
        out_shape=(jax.ShapeDtypeStruct((B, T, D_MODEL), F32),
                   jax.ShapeDtypeStruct(h2_all.shape, F32),
                   jax.ShapeDtypeStruct((B, T, LANES), jnp.int32), jax.ShapeDtypeStruct((B, T, LANES), F32),
                   jax.ShapeDtypeStruct((8, LANES), F32)),
        scratch_shapes=[pltpu.VMEM((8, LANES), F32)],
        input_output_aliases={11: 1},
        compiler_params=_cparams(("arbitrary", "arbitrary")),
        name="outproj",
    )(x, a_n, b_n, gt1, sc2, sh2, w["w_out"], w["ln1_g"], w["ln1_b"], w["router_w"], w["router_b"], h2_all)


def _expert_kernel(order_ref, blk_e_ref, blk_j0_ref, blk_nv_ref, nxt_e_ref, used_ref,
                   h_ref, wgu_hbm, bgu_ref, wd_hbm, bd_ref, y_ref,
                   x0, x1, y0, y1, xb, act_s, wgu_st, wd_st, wgu_bf, wd_bf, gsem, ssem, wsem, *, bm, n_tok):
    s = pl.program_id(0)
    used = used_ref[0]
    dump0 = TOP_K * n_tok

    def weight_copies(e):
        return (pltpu.make_async_copy(wgu_hbm.at[e], wgu_st, wsem.at[0]),
                pltpu.make_async_copy(wd_hbm.at[e], wd_st, wsem.at[1]))

    tok_bits = (n_tok - 1).bit_length()

    def gather_start(j0, xbuf, sem, r):
        tok = jnp.bitwise_and(order_ref[j0 + r], (1 << tok_bits) - 1)
        pltpu.make_async_copy(h_ref.at[tok], xbuf.at[pl.ds(r * ROW_TILE, ROW_TILE)], sem).start()

    def scatter_start(j0, nv, ybuf, sem, r, odd):
        real = lax.shift_right_logical(order_ref[j0 + r], tok_bits)
        dump = dump0 + odd * bm + r
        valid = jnp.right_shift(r - nv, 31)
        dst = dump + jnp.bitwise_and(valid, real - dump)
        pltpu.make_async_copy(ybuf.at[pl.ds(r * ROW_TILE, ROW_TILE)], y_ref.at[dst], sem).start()

    def block_wait(buf, sem):
        pltpu.make_async_copy(buf, buf, sem).wait()

    def switch_weights(b):
        prev = blk_e_ref[jnp.maximum(b - 1, 0)]
        e = blk_e_ref[b]

        @pl.when((b == 0) | (e != prev))
        def _():
            for c in weight_copies(e):
                c.wait()
            wgu_bf[...] = wgu_st[...].astype(BF16)
            wd_bf[...] = wd_st[...].astype(BF16)

            @pl.when(nxt_e_ref[e] >= 0)
            def _():
                for c in weight_copies(nxt_e_ref[e]):
                    c.start()

    def block(b, x_in, y_out, g_next, x_next, gsem_next, s_prev, y_prev, ssem_prev, prev_odd, wait_y_free):
        gj0 = blk_j0_ref[g_next + 1]
        sj0 = blk_j0_ref[s_prev + 1]
        snv = blk_nv_ref[s_prev + 1]
        def issue(k, after=None):
            zero = 0
            if after is not None:
                bits = pltpu.bitcast(jnp.abs(after[0:SUBLANES, 0:LANES]), jnp.int32)
                zero = jnp.minimum(bits[0, 0], 0)
            half = DMA_BATCHES // 2
            q = k % half
            for r in range(q * bm // half, (q + 1) * bm // half):
                if k < half:
                    gather_start(gj0 + zero, x_next, gsem_next, r)
                else:
                    scatter_start(sj0 + zero, snv, y_prev, ssem_prev, r, prev_odd)

        xb[...] = _rows_from_tiles(x_in, bm).astype(BF16)
        e = blk_e_ref[b]
        cw = D_FF // 4
        prev = None
        for c in range(4):
            issue(2 * c, prev)
            gs, ls = slice(c * cw, (c + 1) * cw), slice(D_FF + c * cw, D_FF + (c + 1) * cw)
            g = jnp.dot(xb[...], wgu_bf[:, gs], preferred_element_type=F32) + bgu_ref[e][:, gs]
            issue(2 * c + 1, g)
            lin = jnp.dot(xb[...], wgu_bf[:, ls], preferred_element_type=F32) + bgu_ref[e][:, ls]
            prev = lin
            g = jnp.minimum(g, SWIGLU_LIMIT)
            lin = jnp.clip(lin, -SWIGLU_LIMIT, SWIGLU_LIMIT)
            act_s[:, gs] = (g * _sigmoid(SWIGLU_ALPHA * g) * (lin + 1.0)).astype(BF16)
        hw = D_MODEL // 2
        wait_y_free()
        for h in range(2):
            issue(8 + h, prev)
            y = jnp.dot(act_s[...], wd_bf[:, h * hw:(h + 1) * hw], preferred_element_type=F32)
            y = y + bd_ref[e][:, h * hw:(h + 1) * hw]
            prev = y
            for c in range(hw // LANES):
                y_out[pl.ds(h * (hw // LANES) + c, bm, stride=ROW_TILE), :] = y[:, c * LANES:(c + 1) * LANES]

    b0 = 2 * s
    b1 = b0 + 1

    @pl.when(b0 < used)
    def _():
        @pl.when(s == 0)
        def _():
            for c in weight_copies(blk_e_ref[0]):
                c.start()
            y1[...] = jnp.zeros(y1.shape, F32)

            def first(r, c):
                scatter_start(0, 0, y1, ssem.at[0], r, 0)
                gather_start(blk_j0_ref[1], x0, gsem.at[0], r)
                return c

            lax.fori_loop(0, bm, first, 0)
            block_wait(y1, ssem.at[0])

        def y0_free():
            @pl.when(s > 0)
            def _():
                block_wait(y0, ssem.at[0])

        def y1_free():
            block_wait(y1, ssem.at[1])

        switch_weights(b0)
        block_wait(x0, gsem.at[0])
        block(b0, x0, y0, b1, x1, gsem.at[1], b0 - 1, y1, ssem.at[1], 1, y0_free)

        switch_weights(b1)
        block_wait(x1, gsem.at[1])
        block(b1, x1, y1, b0 + 2, x0, gsem.at[0], b0, y0, ssem.at[0], 0, y1_free)

        @pl.when(b0 + 2 >= used)
        def _():
            def last(r, c):
                scatter_start(blk_j0_ref[b1 + 1], blk_nv_ref[b1 + 1], y1, ssem.at[1], r, 1)
                return c

            lax.fori_loop(0, bm, last, 0)
            block_wait(y0, ssem.at[0])
            block_wait(y1, ssem.at[1])
            block_wait(x0, gsem.at[0])


def _expert(order, blk_e, blk_j0, blk_nv, nxt_e, used, h2, w, bm):
    n_tok = h2.shape[0]
    nblk = blk_e.shape[0]
    f32buf = lambda shp: pltpu.VMEM(shp, F32)
    return pl.pallas_call(
        functools.partial(_expert_kernel, bm=bm, n_tok=n_tok),
        grid_spec=pltpu.PrefetchScalarGridSpec(
            num_scalar_prefetch=6,
            grid=(nblk // 2,),
            in_specs=[
                pl.BlockSpec(memory_space=pl.ANY),
                pl.BlockSpec(memory_space=pl.ANY),
                pl.BlockSpec((N_EXPERTS, 1, 2 * D_FF), lambda s, *_: (0, 0, 0)),
                pl.BlockSpec(memory_space=pl.ANY),
                pl.BlockSpec((N_EXPERTS, 1, D_MODEL), lambda s, *_: (0, 0, 0)),
            ],
            out_specs=pl.BlockSpec(memory_space=pl.ANY),
            scratch_shapes=[f32buf((bm * ROW_TILE, LANES)), f32buf((bm * ROW_TILE, LANES)),
                            f32buf((bm * ROW_TILE, LANES)), f32buf((bm * ROW_TILE, LANES)),
                            pltpu.VMEM((bm, D_MODEL), BF16), pltpu.VMEM((bm, D_FF), BF16),
                            f32buf((D_MODEL, 2 * D_FF)), f32buf((D_FF, D_MODEL)),
                            pltpu.VMEM((D_MODEL, 2 * D_FF), BF16), pltpu.VMEM((D_FF, D_MODEL), BF16),
                            pltpu.SemaphoreType.DMA((2,)), pltpu.SemaphoreType.DMA((2,)),
                            pltpu.SemaphoreType.DMA((2,))],
        ),
        out_shape=jax.ShapeDtypeStruct((TOP_K * n_tok + 2 * bm, ROW_TILE, LANES), F32),
        compiler_params=_cparams(("arbitrary",)),
        name="moe_expert",
    )(order, blk_e, blk_j0, blk_nv, nxt_e, used, h2, w["w_gu"], w["b_gu"], w["w_down"], w["b_down"])


def _combine_kernel(y0_ref, y1_ref, y2_ref, y3_ref, x1_ref, gates_ref, gt2_ref, l2g_ref, l2b_ref, o_ref, *, alpha):
    gates = gates_ref[...]
    tm = gates.shape[0]
    f = gates[:, 0:1] * _rows_from_tiles(y0_ref, tm)
    for k, y_ref in enumerate((y1_ref, y2_ref, y3_ref), start=1):
        f = f + gates[:, k:k + 1] * _rows_from_tiles(y_ref, tm)
    o_ref[...] = _layernorm(alpha * x1_ref[...] + gt2_ref[0] * f) * l2g_ref[...] + l2b_ref[...]


def _combine(y_rows, x1, gates, gt2, w, tm, tokens_per_batch, alpha, n_tok, tok0):
    N = x1.shape[0]
    per_b = tokens_per_batch // tm
    nt = N // tm
    assert n_tok % tm == 0 and tok0 % tm == 0
    slot = lambda k: pl.BlockSpec((tm * ROW_TILE, LANES), lambda i: ((k * n_tok + tok0) // tm + i, 0))
    return pl.pallas_call(
        functools.partial(_combine_kernel, alpha=alpha),
        grid=(nt,),
        in_specs=[slot(0), slot(1), slot(2), slot(3),
                  pl.BlockSpec((tm, D_MODEL), lambda i: (i, 0)),
                  pl.BlockSpec((tm, LANES), lambda i: (i, 0)),
                  pl.BlockSpec((1, 1, D_MODEL), lambda i: (i // per_b, 0, 0)),
                  pl.BlockSpec((1, D_MODEL), lambda i: (0, 0)),
                  pl.BlockSpec((1, D_MODEL), lambda i: (0, 0))],
        out_specs=pl.BlockSpec((tm, D_MODEL), lambda i: (i, 0)),
        out_shape=jax.ShapeDtypeStruct((N, D_MODEL), F32),
        compiler_params=_cparams(("arbitrary",)),
        name="moe_combine",
    )(y_rows, y_rows, y_rows, y_rows, x1, gates, gt2, w["ln2_g"], w["ln2_b"])


def _moe_experts(h2, idx, counts, w, bm):
    N = h2.shape[0]
    n_rows = N * TOP_K
    nblk = (n_rows + N_EXPERTS * (bm - 1)) // bm
    nblk += nblk % 2
    experts = jnp.arange(N_EXPERTS, dtype=jnp.int32)
    id_bits = (n_rows - 1).bit_length()
    tok_bits = (N - 1).bit_length()
    assert tok_bits + (TOP_K * N - 1).bit_length() <= 32
    keys = jnp.left_shift(idx.reshape(-1), id_bits) + jnp.arange(n_rows, dtype=jnp.int32)
    keys = jnp.concatenate([keys, jnp.full(((1 << id_bits) - n_rows,), jnp.iinfo(jnp.int32).max, jnp.int32)])
    flat = jnp.bitwise_and(jnp.sort(keys, stable=False)[:n_rows], (1 << id_bits) - 1).astype(jnp.uint32)
    tok = jnp.right_shift(flat, 2)
    row = jnp.bitwise_and(flat, TOP_K - 1) * N + tok
    order = lax.bitcast_convert_type(jnp.left_shift(row, tok_bits) | tok, jnp.int32)
    order = jnp.concatenate([order, jnp.zeros((bm,), jnp.int32)])
    nb_e = (counts + bm - 1) // bm
    blk_end = jnp.cumsum(nb_e)
    first_blk = blk_end - nb_e
    start_sorted = jnp.cumsum(counts) - counts
    used = blk_end[-1].astype(jnp.int32)
    b = jnp.arange(-1, nblk + 1, dtype=jnp.int32)
    bc = jnp.clip(b, 0, used - 1)
    e = jnp.minimum(jnp.sum(blk_end[None, :] <= bc[:, None], axis=1), N_EXPERTS - 1).astype(jnp.int32)
    pick = lambda table: jnp.sum(jnp.where(e[:, None] == experts, table, 0), axis=1)
    local = bc - pick(first_blk)
    blk_j0 = (pick(start_sorted) + local * bm).astype(jnp.int32)
    blk_nv = jnp.where((b >= 0) & (b < used), jnp.minimum(bm, pick(counts) - local * bm), 0).astype(jnp.int32)
    blk_e = e[1:nblk + 1]
    later = (experts[None, :] > experts[:, None]) & (counts[None, :] > 0)
    nxt = jnp.min(jnp.where(later, experts[None, :], N_EXPERTS), axis=1)
    nxt_e = jnp.where(nxt < N_EXPERTS, nxt, -1).astype(jnp.int32)
    y_rows = _expert(order, blk_e, blk_j0, blk_nv, nxt_e, used.reshape(1), h2, w, bm)
    return y_rows.reshape(-1, LANES)


def _rope_tables(pos):
    half = QK_ROPE // 2
    inv = ROPE_THETA ** (-jnp.arange(half, dtype=F32) / half)
    ang = pos.astype(F32)[:, None] * inv[None, :]
    cos, sin = jnp.cos(ang), jnp.sin(ang)
    cos32 = jnp.concatenate([cos, cos], axis=1)
    sin32 = jnp.concatenate([-sin, sin], axis=1)
    return jnp.tile(cos32, (1, LANES // QK_ROPE)), jnp.tile(sin32, (1, LANES // QK_ROPE))


def _swap_halves(w32):
    shp = w32.shape
    w = w32.reshape(shp[:-1] + (shp[-1] // QK_ROPE, 2, QK_ROPE // 2))
    return w[..., ::-1, :].reshape(shp)


def _prep_weights(l, w_in, conv_w, g_qa, w_qb, g_kva, w_kvb, g_out_conv, g_out_attn, w_out,
                  ln1_g, ln1_b, router_w, router_b, w_gu, b_gu, w_down, b_down, ln2_g, ln2_b):
    w = {}
    wi = w_in[l]
    k_r = wi[:, _O_KR:_O_KR + QK_ROPE]
    rep = LANES // QK_ROPE
    w["w_in"] = jnp.concatenate([wi[:, :_O_KR], jnp.tile(k_r, (1, rep)), jnp.tile(_swap_halves(k_r), (1, rep))],
                                axis=1).astype(BF16)
    w["conv_w"] = conv_w[l]
    w["g_qa"] = g_qa[l].reshape(1, Q_LORA)
    w["g_kva"] = g_kva[l].reshape(1, KV_LORA)
    w["g_oc"] = g_out_conv[l].reshape(1, CONV_WIDTH)
    w["g_oa"] = g_out_attn[l].reshape(1, ATTN_WIDTH)
    wq = w_qb[l].reshape(Q_LORA, N_HEADS, QK_NOPE + QK_ROPE)
    w["wq_nope"] = wq[:, :, :QK_NOPE].reshape(Q_LORA, N_HEADS * QK_NOPE).astype(BF16)
    wq_rope = wq[:, :, QK_NOPE:].reshape(Q_LORA, N_HEADS * QK_ROPE)
    w["wq_rope"] = wq_rope.astype(BF16)
    w["wq_rope_sw"] = _swap_halves(wq_rope).astype(BF16)
    w_uk = jnp.transpose(w_kvb[l][:, :, :QK_NOPE], (1, 2, 0))
    w_uv = jnp.transpose(w_kvb[l][:, :, QK_NOPE:], (1, 0, 2))
    zk = jnp.zeros((QK_NOPE, KV_LORA), F32)
    zv = jnp.zeros((KV_LORA, V_HEAD), F32)
    w["w_uk_pair"] = jnp.stack([
        jnp.concatenate([jnp.concatenate([w_uk[2 * p], zk], axis=1),
                         jnp.concatenate([zk, w_uk[2 * p + 1]], axis=1)], axis=0)
        for p in range(N_HEADS // 2)]).astype(BF16)
    w["w_uv_pair"] = jnp.stack([
        jnp.concatenate([jnp.concatenate([w_uv[2 * p], zv], axis=1),
                         jnp.concatenate([zv, w_uv[2 * p + 1]], axis=1)], axis=0)
        for p in range(N_HEADS // 2)]).astype(BF16)
    w["w_out"] = w_out[l].astype(BF16)
    w["ln1_g"] = ln1_g[l].reshape(1, D_MODEL)
    w["ln1_b"] = ln1_b[l].reshape(1, D_MODEL)
    w["ln2_g"] = ln2_g[l].reshape(1, D_MODEL)
    w["ln2_b"] = ln2_b[l].reshape(1, D_MODEL)
    w["router_w"] = jnp.pad(router_w[l], ((0, 0), (0, LANES - N_EXPERTS))).astype(BF16)
    w["router_b"] = jnp.concatenate([router_b[l], jnp.full((LANES - N_EXPERTS,), NEG, F32)]).reshape(1, LANES)
    w["w_gu"] = w_gu[l]
    w["b_gu"] = b_gu[l].reshape(N_EXPERTS, 1, 2 * D_FF)
    w["w_down"] = w_down[l]
    w["b_down"] = b_down[l].reshape(N_EXPERTS, 1, D_MODEL)
    return w


def _mixer(x, mod, conv_prev, past, pos0, h2_all, tok0, w, alpha, *, tm_in, tm_out, tq=128, tk=512):
    B, T, _ = x.shape
    sh1, sc1, gt1, sh2, sc2, gt2 = [mod[:, None, i * D_MODEL:(i + 1) * D_MODEL] for i in range(N_MOD)]
    cos_t, sin_t = _rope_tables(pos0 + jnp.arange(T, dtype=jnp.int32))
    a_n, qlat, qrope, kcat, ckv, krope, conv_new, *vt = _inproj(x, sc1, sh1, conv_prev, cos_t, sin_t, w, tm_in,
                                                                 with_vt=past is None)
    if past is None:
        b_n = _attn_prompt(qlat, qrope, kcat, vt[0], w, tq, tk)
    else:
        b_n = _attn_sample(qlat, qrope, kcat, past[0], past[1], w, tk)
    x1, h2_all, meta, gates, cnt = _outproj(x, a_n, b_n, gt1, sc2, sh2, h2_all, tok0, w, tm_out, alpha)
    N = B * T
    route = dict(x1=x1.reshape(N, D_MODEL), idx=meta.reshape(N, LANES)[:, :TOP_K], gates=gates.reshape(N, LANES),
                 counts=cnt[0, :N_EXPERTS].astype(jnp.int32), gt2=gt2, shape=(B, T))
    return h2_all, route, ckv, krope, conv_new


def kernel(x_prompt, x_sample, c_prompt, c_sample, cache_ckv, cache_krope, state_conv, w_ada, b_ada, w_in, conv_w, g_qa, w_qb, g_kva, w_kvb, g_out_conv, g_out_attn, w_out, ln1_g, ln1_b, router_w, router_b, w_gu, b_gu, w_down, b_down, ln2_g, ln2_b):
    depth = w_ada.shape[0]
    Bp, Tp, _ = x_prompt.shape
    Bs, Ts, _ = x_sample.shape
    past_len = cache_ckv.shape[2]
    assert Ts == CHUNK and past_len % 512 == 0 and Tp % 1024 == 0
    alpha = (2.0 * depth) ** 0.25
    xp, xs = x_prompt, x_sample
    outs = [[] for _ in range(6)]
    c_all = jnp.concatenate([c_prompt, c_sample, jnp.zeros((16 - Bp - Bs, D_MODEL), F32)], axis=0)
    for l in range(depth):
        w = _prep_weights(l, w_in, conv_w, g_qa, w_qb, g_kva, w_kvb, g_out_conv, g_out_attn, w_out,
                          ln1_g, ln1_b, router_w, router_b, w_gu, b_gu, w_down, b_down, ln2_g, ln2_b)
        mod = _ada(c_all, w_ada[l], b_ada[l])
        n_p, n_s = Bp * Tp, Bs * Ts
        n_tok = n_p + n_s
        h2_all = jnp.zeros((n_tok * ROW_TILE, LANES), F32)
        h2_all, rp, ckv_p, kr_p, cv_p = _mixer(xp, mod[:Bp], jnp.zeros((Bp, CONV_K - 1, CONV_WIDTH), F32), None, 0,
                                               h2_all, 0, w, alpha, tm_in=1024, tm_out=512)
        h2_all, rs, ckv_s, kr_s, cv_s = _mixer(xs, mod[Bp:Bp + Bs], state_conv[l], (cache_ckv[l], cache_krope[l]),
                                               past_len, h2_all, n_p, w, alpha, tm_in=Ts, tm_out=Ts)
        y_rows = _moe_experts(h2_all.reshape(n_tok, ROW_TILE, LANES), jnp.concatenate([rp["idx"], rs["idx"]]),
                              rp["counts"] + rs["counts"], w, bm=256)
        xp = _combine(y_rows, rp["x1"], rp["gates"], rp["gt2"], w, 256, Tp, alpha, n_tok, 0).reshape(rp["shape"] + (D_MODEL,))
        xs = _combine(y_rows, rs["x1"], rs["gates"], rs["gt2"], w, Ts, Ts, alpha, n_tok, n_p).reshape(rs["shape"] + (D_MODEL,))
        for o, v in zip(outs, (ckv_p, kr_p, cv_p, ckv_s, kr_s, cv_s)):
            o.append(v)
    return (xp, xs) + tuple(jnp.stack(o) for o in outs)
```

```python
import functools
import math

import jax
import jax.numpy as jnp
from jax import lax
from jax.experimental import pallas as pl
from jax.experimental.pallas import tpu as pltpu

F32 = jnp.float32
BF16 = jnp.bfloat16

D_MODEL = 1024
CHUNK = 64
CONV_WIDTH = 512
CONV_K = 3
N_HEADS = 8
QK_NOPE = 64
QK_ROPE = 32
V_HEAD = 64
Q_LORA = 256
KV_LORA = 128
ATTN_WIDTH = N_HEADS * V_HEAD
ROPE_THETA = 10000.0
ATTN_SCALE = 1.0 / math.sqrt(QK_NOPE + QK_ROPE)
Q_SCALE = ATTN_SCALE * math.log2(math.e)
N_EXPERTS = 32
TOP_K = 4
D_FF = 1024
SWIGLU_LIMIT = 7.0
SWIGLU_ALPHA = 1.702
N_MOD = 6
LN_EPS = 1e-5
RMS_EPS = 1e-6

LANES = 128
SUBLANES = 8
DMA_BATCHES = 10
GATHER_BATCHES = 4
NEG = -1e30
VMEM_LIMIT = 56 * 1024 * 1024

_O_XB, _O_XC, _O_XV = 0, CONV_WIDTH, 2 * CONV_WIDTH
_O_QA = 3 * CONV_WIDTH
_O_KVA = _O_QA + Q_LORA
_O_KR = _O_KVA + KV_LORA
_O_KRS = _O_KR + LANES
IN_COLS_EXT = _O_KRS + LANES


def _cparams(sem):
    return pltpu.CompilerParams(dimension_semantics=sem, vmem_limit_bytes=VMEM_LIMIT)


def _layernorm(x):
    mu = jnp.mean(x, axis=-1, keepdims=True)
    xc = x - mu
    var = jnp.mean(xc * xc, axis=-1, keepdims=True)
    return xc * lax.rsqrt(var + LN_EPS)


def _rms(x):
    return x * lax.rsqrt(jnp.mean(x * x, axis=-1, keepdims=True) + RMS_EPS)


def _sigmoid(x):
    return 1.0 / (1.0 + jnp.exp(-x))


ROW_TILE = D_MODEL // LANES


def _rows_from_tiles(ref, n):
    return jnp.concatenate([ref[pl.ds(c, n, stride=ROW_TILE), :] for c in range(ROW_TILE)], axis=1)


def _rows_to_tiles(ref, x):
    n = x.shape[0]
    for c in range(ROW_TILE):
        ref[pl.ds(c, n, stride=ROW_TILE), :] = x[:, c * LANES:(c + 1) * LANES]


def _ada_kernel(c_ref, w_ref, b_ref, o_ref):
    c = c_ref[...]
    s = (c * _sigmoid(c)).astype(BF16)
    o_ref[...] = jnp.dot(s, w_ref[...].astype(BF16), preferred_element_type=F32) + b_ref[...]


def _ada(c_all, w_ada, b_ada):
    rows = c_all.shape[0]
    ncol = w_ada.shape[1]
    tn = 1024
    return pl.pallas_call(
        _ada_kernel,
        grid=(ncol // tn,),
        in_specs=[pl.BlockSpec((rows, D_MODEL), lambda j: (0, 0)),
                  pl.BlockSpec((D_MODEL, tn), lambda j: (0, j)),
                  pl.BlockSpec((1, tn), lambda j: (0, j))],
        out_specs=pl.BlockSpec((rows, tn), lambda j: (0, j)),
        out_shape=jax.ShapeDtypeStruct((rows, ncol), F32),
        compiler_params=_cparams(("arbitrary",)),
        name="ada",
    )(c_all, w_ada, b_ada.reshape(1, ncol))


def _inproj_kernel(x_ref, sc_ref, sh_ref, win_ref, cw_ref, cprev_ref, gqa_ref, gkva_ref, goc_ref,
                   wqn_ref, wqr_ref, wqrs_ref, wuk_ref, cos_ref, sin_ref,
                   an_ref, qlat_ref, qrope_ref, kcat_ref, ckv_ref, krope_ref, cnew_ref,
                   *rest, tm, with_vt):
    vt_ref, ubuf = rest if with_vt else (None, rest[0])
    j = pl.program_id(1)
    x = x_ref[0]
    h = _layernorm(x) * (1.0 + sc_ref[0]) + sh_ref[0]
    proj = jnp.dot(h.astype(BF16), win_ref[...], preferred_element_type=F32)
    xb = proj[:, _O_XB:_O_XB + CONV_WIDTH]
    xc = proj[:, _O_XC:_O_XC + CONV_WIDTH]
    xv = proj[:, _O_XV:_O_XV + CONV_WIDTH]
    q_a = proj[:, _O_QA:_O_QA + Q_LORA]
    kv_a = proj[:, _O_KVA:_O_KVA + KV_LORA]
    kr4 = proj[:, _O_KR:_O_KR + LANES]
    kr4s = proj[:, _O_KRS:_O_KRS + LANES]

    u = xc * xv

    @pl.when(j == 0)
    def _():
        ubuf[6:8, :] = cprev_ref[0]

    ubuf[8:8 + tm, :] = u
    conv = (cw_ref[0:1, :] * ubuf[6:6 + tm, :] + cw_ref[1:2, :] * ubuf[7:7 + tm, :]
            + cw_ref[2:3, :] * u)
    ubuf[0:8, :] = ubuf[tm:tm + 8, :]
    cnew_ref[0] = u[tm - (CONV_K - 1):tm, :]
    an_ref[0] = (_rms(xb * conv) * goc_ref[...]).astype(BF16)

    cos = cos_ref[...]
    sin = sin_ref[...]

    ckv = _rms(kv_a) * gkva_ref[...]
    kro4 = kr4 * cos + kr4s * sin
    ckv_ref[0] = ckv
    krope_ref[0] = kro4[:, :QK_ROPE]
    kcat_ref[0] = jnp.concatenate([ckv, kro4], axis=1).astype(BF16)
    if with_vt:
        vt_ref[0] = ckv.T.astype(BF16)

    qn = (_rms(q_a) * gqa_ref[...]).astype(BF16)
    q_nope = jnp.dot(qn, wqn_ref[...], preferred_element_type=F32)
    xr = jnp.dot(qn, wqr_ref[...], preferred_element_type=F32)
    xrs = jnp.dot(qn, wqrs_ref[...], preferred_element_type=F32)
    for g in range(2):
        sl = slice(g * LANES, (g + 1) * LANES)
        qrope_ref[0, :, sl] = ((xr[:, sl] * cos + xrs[:, sl] * sin) * Q_SCALE).astype(BF16)
    for p in range(N_HEADS // 2):
        qp = q_nope[:, p * LANES:(p + 1) * LANES].astype(BF16)
        ql = jnp.dot(qp, wuk_ref[p], preferred_element_type=F32)
        qlat_ref[0, :, p * 2 * KV_LORA:(p + 1) * 2 * KV_LORA] = (ql * Q_SCALE).astype(BF16)


def _inproj(x, sc1, sh1, conv_prev, cos_t, sin_t, w, tm, with_vt):
    B, T, _ = x.shape
    nt = T // tm
    full = lambda shp: pl.BlockSpec(shp, lambda b, j: (0,) * len(shp))
    vt_shape = (jax.ShapeDtypeStruct((B, KV_LORA, T), BF16),) if with_vt else ()
    vt_spec = (pl.BlockSpec((1, KV_LORA, tm), lambda b, j: (b, 0, j)),) if with_vt else ()
    out_shapes = (
        jax.ShapeDtypeStruct((B, T, CONV_WIDTH), BF16),
        jax.ShapeDtypeStruct((B, T, N_HEADS * KV_LORA), BF16),
        jax.ShapeDtypeStruct((B, T, 2 * LANES), BF16),
        jax.ShapeDtypeStruct((B, T, 2 * LANES), BF16),
        jax.ShapeDtypeStruct((B, T, KV_LORA), F32),
        jax.ShapeDtypeStruct((B, T, QK_ROPE), F32),
        jax.ShapeDtypeStruct((B, CONV_K - 1, CONV_WIDTH), F32),
    ) + vt_shape
    tile = lambda c: pl.BlockSpec((1, tm, c), lambda b, j: (b, j, 0))
    return pl.pallas_call(
        functools.partial(_inproj_kernel, tm=tm, with_vt=with_vt),
        grid=(B, nt),
        in_specs=[
            tile(D_MODEL),
            pl.BlockSpec((1, 1, D_MODEL), lambda b, j: (b, 0, 0)),
            pl.BlockSpec((1, 1, D_MODEL), lambda b, j: (b, 0, 0)),
            full((D_MODEL, IN_COLS_EXT)),
            full((CONV_K, CONV_WIDTH)),
            pl.BlockSpec((1, CONV_K - 1, CONV_WIDTH), lambda b, j: (b, 0, 0)),
            full((1, Q_LORA)), full((1, KV_LORA)), full((1, CONV_WIDTH)),
            full((Q_LORA, N_HEADS * QK_NOPE)), full((Q_LORA, 2 * LANES)), full((Q_LORA, 2 * LANES)),
            full((N_HEADS // 2, LANES, 2 * KV_LORA)),
            pl.BlockSpec((tm, LANES), lambda b, j: (j, 0)),
            pl.BlockSpec((tm, LANES), lambda b, j: (j, 0)),
        ],
        out_specs=(tile(CONV_WIDTH), tile(N_HEADS * KV_LORA), tile(2 * LANES), tile(2 * LANES),
                   tile(KV_LORA), tile(QK_ROPE),
                   pl.BlockSpec((1, CONV_K - 1, CONV_WIDTH), lambda b, j: (b, 0, 0))) + vt_spec,
        out_shape=out_shapes,
        scratch_shapes=[pltpu.VMEM((tm + 8, CONV_WIDTH), F32)],
        compiler_params=_cparams(("arbitrary", "arbitrary")),
        name="inproj",
    )(x, sc1, sh1, w["w_in"], w["conv_w"], conv_prev, w["g_qa"], w["g_kva"], w["g_oc"],
      w["wq_nope"], w["wq_rope"], w["wq_rope_sw"], w["w_uk_pair"], cos_t, sin_t)


def _stack_queries(qlat_ref, qrope_ref, qs, tq):
    lane = lax.broadcasted_iota(jnp.int32, (tq, LANES), 1)
    for h in range(N_HEADS):
        g, i = divmod(h, 4)
        rope = qrope_ref[0, :, g * LANES:(g + 1) * LANES]
        keep = (lane >= i * QK_ROPE) & (lane < (i + 1) * QK_ROPE)
        qs[h * tq:(h + 1) * tq, 0:KV_LORA] = qlat_ref[0, :, h * KV_LORA:(h + 1) * KV_LORA]
        qs[h * tq:(h + 1) * tq, KV_LORA:KV_LORA + LANES] = jnp.where(keep, rope, jnp.zeros_like(rope))


def _softmax_step(qs, k, v, m_s, l_s, acc_s):
    tk = k.shape[0]
    s = lax.dot_general(qs[...], k, (((1,), (1,)), ((), ())), preferred_element_type=F32)
    m_prev = m_s[...]
    m_new = jnp.maximum(m_prev, jnp.max(s, axis=1, keepdims=True))
    alpha = jnp.exp2(m_prev - m_new)
    if tk % LANES == 0:
        p = jnp.exp2(s - jnp.tile(m_new, (1, tk // LANES)))
    else:
        p = jnp.exp2(s - m_new[:, :tk])
    l_s[...] = alpha * l_s[...] + jnp.sum(p, axis=1, keepdims=True)
    acc_s[...] = alpha * acc_s[...] + jnp.dot(p.astype(BF16), v, preferred_element_type=F32)
    m_s[...] = m_new


def _attn_epilogue(acc_s, l_s, wuv_ref, g_ref, o_ref, tq):
    o = acc_s[...] / l_s[...]
    parts = []
    for p in range(N_HEADS // 2):
        op = jnp.concatenate([o[(2 * p) * tq:(2 * p + 1) * tq], o[(2 * p + 1) * tq:(2 * p + 2) * tq]], axis=1)
        parts.append(jnp.dot(op.astype(BF16), wuv_ref[p], preferred_element_type=F32))
    b = jnp.concatenate(parts, axis=1)
    o_ref[0] = (_rms(b) * g_ref[...]).astype(BF16)


def _attn_prompt_kernel(qlat_ref, qrope_ref, k_ref, vt_ref, wuvt_ref, g_ref, o_ref, qs, m_s, l_s, acc_s, sa, sb,
                        *, tq, tk):
    i = pl.program_id(1)
    M = N_HEADS * tq
    _stack_queries(qlat_ref, qrope_ref, qs, tq)
    m_s[...] = jnp.full(m_s.shape, NEG, F32)
    l_s[...] = jnp.zeros(l_s.shape, F32)
    acc_s[...] = jnp.zeros(acc_s.shape, F32)
    q0 = i * tq
    n_full = (q0 + CHUNK) // tk

    def scores(t, dst):
        k = k_ref[0, pl.ds(pl.multiple_of(t * tk, tk), tk), :]
        dst[...] = lax.dot_general(k, qs[...], (((1,), (1,)), ((), ())), preferred_element_type=F32)

    def update(t, src, limit=None):
        start = pl.multiple_of(t * tk, tk)
        vt = vt_ref[0, :, pl.ds(start, tk)]
        s = src[...]
        if limit is not None:
            kpos = start + lax.broadcasted_iota(jnp.int32, (tk, 1), 0)
            s = jnp.where(kpos < limit, s, NEG)
        m_prev = m_s[...]
        m_new = jnp.maximum(m_prev, jnp.max(s, axis=0, keepdims=True))
        alpha = jnp.exp2(m_prev - m_new)
        p = jnp.exp2(s - m_new)
        l_s[...] = alpha * l_s[...] + jnp.sum(p, axis=0, keepdims=True)
        acc_s[...] = alpha * acc_s[...] + jnp.dot(vt, p.astype(BF16), preferred_element_type=F32)
        m_s[...] = m_new

    scores(0, sa)

    def body(j, carry):
        t = 2 * j
        scores(t + 1, sb)
        update(t, sa)
        scores(t + 2, sa)
        update(t + 1, sb)
        return carry

    lax.fori_loop(0, n_full // 2, body, 0)

    col_t = jnp.bitwise_and(lax.broadcasted_iota(jnp.int32, (1, M), 1), tq - 1)
    limit = q0 + (jnp.right_shift(col_t, CHUNK.bit_length() - 1) + 1) * CHUNK
    odd = n_full % 2

    @pl.when(odd == 0)
    def _():
        update(n_full, sa, limit)

    @pl.when(odd == 1)
    def _():
        scores(n_full, sb)
        update(n_full - 1, sa)
        update(n_full, sb, limit)

    o_t = acc_s[...] / l_s[...]
    parts = []
    for p in range(N_HEADS // 2):
        pair = jnp.concatenate([o_t[:, (2 * p) * tq:(2 * p + 1) * tq], o_t[:, (2 * p + 1) * tq:(2 * p + 2) * tq]],
                               axis=0)
        parts.append(jnp.dot(wuvt_ref[p], pair.astype(BF16), preferred_element_type=F32))
    b = jnp.concatenate(parts, axis=0).T
    o_ref[0] = (_rms(b) * g_ref[...]).astype(BF16)


def _attn_prompt(qlat, qrope, kcat, vt, w, tq, tk):
    B, T, _ = qlat.shape
    M = N_HEADS * tq
    return pl.pallas_call(
        functools.partial(_attn_prompt_kernel, tq=tq, tk=tk),
        grid=(B, T // tq),
        in_specs=[
            pl.BlockSpec((1, tq, N_HEADS * KV_LORA), lambda b, i: (b, i, 0)),
            pl.BlockSpec((1, tq, 2 * LANES), lambda b, i: (b, i, 0)),
            pl.BlockSpec((1, T, 2 * LANES), lambda b, i: (b, 0, 0)),
            pl.BlockSpec((1, KV_LORA, T), lambda b, i: (b, 0, 0)),
            pl.BlockSpec((N_HEADS // 2, LANES, 2 * KV_LORA), lambda b, i: (0, 0, 0)),
            pl.BlockSpec((1, ATTN_WIDTH), lambda b, i: (0, 0)),
        ],
        out_specs=pl.BlockSpec((1, tq, ATTN_WIDTH), lambda b, i: (b, i, 0)),
        out_shape=jax.ShapeDtypeStruct((B, T, ATTN_WIDTH), BF16),
        scratch_shapes=[pltpu.VMEM((M, 2 * LANES), BF16), pltpu.VMEM((1, M), F32),
                        pltpu.VMEM((1, M), F32), pltpu.VMEM((KV_LORA, M), F32),
                        pltpu.VMEM((tk, M), F32), pltpu.VMEM((tk, M), F32)],
        compiler_params=_cparams(("arbitrary", "arbitrary")),
        name="attn_prompt",
    )(qlat, qrope, kcat, vt, jnp.swapaxes(w["w_uv_pair"], 1, 2), w["g_oa"])


def _attn_sample_kernel(qlat_ref, qrope_ref, knew_ref, pckv_ref, pkr_ref, wuv_ref, g_ref, o_ref,
                        qs, m_s, l_s, acc_s, *, tq, tk, n_past):
    _stack_queries(qlat_ref, qrope_ref, qs, tq)
    m_s[...] = jnp.full(m_s.shape, NEG, F32)
    l_s[...] = jnp.zeros(l_s.shape, F32)
    acc_s[...] = jnp.zeros(acc_s.shape, F32)

    def body(t, carry):
        start = pl.multiple_of(t * tk, tk)
        ck = pckv_ref[0, pl.ds(start, tk), :]
        kr = pkr_ref[0, pl.ds(start, tk), :]
        k = jnp.concatenate([ck, kr, kr, kr, kr], axis=1).astype(BF16)
        _softmax_step(qs, k, k[:, :KV_LORA], m_s, l_s, acc_s)
        return carry

    lax.fori_loop(0, n_past // tk, body, 0)
    k = knew_ref[0]
    _softmax_step(qs, k, k[:, :KV_LORA], m_s, l_s, acc_s)
    _attn_epilogue(acc_s, l_s, wuv_ref, g_ref, o_ref, tq)


def _attn_sample(qlat, qrope, kcat, past_ckv, past_krope, w, tk):
    B, T, _ = qlat.shape
    n_past = past_ckv.shape[1]
    M = N_HEADS * T
    per_b = lambda r, c: pl.BlockSpec((1, r, c), lambda b: (b, 0, 0))
    return pl.pallas_call(
        functools.partial(_attn_sample_kernel, tq=T, tk=tk, n_past=n_past),
        grid=(B,),
        in_specs=[per_b(T, N_HEADS * KV_LORA), per_b(T, 2 * LANES), per_b(T, 2 * LANES),
                  per_b(n_past, KV_LORA), per_b(n_past, QK_ROPE),
                  pl.BlockSpec((N_HEADS // 2, 2 * KV_LORA, LANES), lambda b: (0, 0, 0)),
                  pl.BlockSpec((1, ATTN_WIDTH), lambda b: (0, 0))],
        out_specs=per_b(T, ATTN_WIDTH),
        out_shape=jax.ShapeDtypeStruct((B, T, ATTN_WIDTH), BF16),
        scratch_shapes=[pltpu.VMEM((M, 2 * LANES), BF16), pltpu.VMEM((M, LANES), F32),
                        pltpu.VMEM((M, LANES), F32), pltpu.VMEM((M, KV_LORA), F32)],
        compiler_params=_cparams(("arbitrary",)),
        name="attn_sample",
    )(qlat, qrope, kcat, past_ckv, past_krope, w["w_uv_pair"], w["g_oa"])


def _outproj_kernel(x_ref, an_ref, bn_ref, gt1_ref, sc2_ref, sh2_ref, wo_ref, l1g_ref, l1b_ref,
                    rw_ref, rb_ref, h2_all_ref, x1_ref, h2_ref, meta_ref, gates_ref, cnt_ref, run_s,
                    *, tm, alpha):
    del h2_all_ref
    first = (pl.program_id(0) == 0) & (pl.program_id(1) == 0)

    @pl.when(first)
    def _():
        run_s[...] = jnp.zeros(run_s.shape, F32)

    m = (jnp.dot(an_ref[0], wo_ref[0:CONV_WIDTH, :], preferred_element_type=F32)
         + jnp.dot(bn_ref[0], wo_ref[CONV_WIDTH:, :], preferred_element_type=F32))
    x1 = _layernorm(alpha * x_ref[0] + gt1_ref[0] * m) * l1g_ref[...] + l1b_ref[...]
    x1_ref[0] = x1
    h2 = _layernorm(x1) * (1.0 + sc2_ref[0]) + sh2_ref[0]
    _rows_to_tiles(h2_ref, h2)

    logits = jnp.dot(h2.astype(BF16), rw_ref[...], preferred_element_type=F32) + rb_ref[...]
    lane = lax.broadcasted_iota(jnp.int32, (tm, LANES), 1)
    lane_f = lane.astype(F32)
    lg = logits
    vals, sels = [], []
    chosen = jnp.zeros((tm, LANES), F32)
    for _ in range(TOP_K):
        mx = jnp.max(lg, axis=1, keepdims=True)
        idx = jnp.min(jnp.where(lg == mx, lane_f, float(LANES)), axis=1, keepdims=True)
        sel = lane_f == idx
        vals.append(mx)
        sels.append(idx)
        chosen = jnp.where(sel, 1.0, chosen)
        lg = jnp.where(sel, NEG, lg)

    es = [jnp.exp(v - vals[0]) for v in vals]
    denom = es[0] + es[1] + es[2] + es[3]

    run_s[0:1, :] = run_s[0:1, :] + jnp.sum(chosen, axis=0, keepdims=True)
    cnt_ref[...] = jnp.broadcast_to(run_s[0:1, :], cnt_ref.shape)

    meta = jnp.zeros((tm, LANES), jnp.int32)
    gates = jnp.zeros((tm, LANES), F32)
    for k in range(TOP_K):
        meta = jnp.where(lane == k, sels[k].astype(jnp.int32), meta)
        gates = jnp.where(lane == k, es[k] / denom, gates)
    meta_ref[0] = meta
    gates_ref[0] = gates


def _outproj(x, a_n, b_n, gt1, sc2, sh2, h2_all, tok0, w, tm, alpha):
    B, T, _ = x.shape
    blk0 = tok0 // tm
    tile = lambda c: pl.BlockSpec((1, tm, c), lambda b, j: (b, j, 0))
    modv = pl.BlockSpec((1, 1, D_MODEL), lambda b, j: (b, 0, 0))
    full = lambda shp: pl.BlockSpec(shp, lambda b, j: (0,) * len(shp))
    return pl.pallas_call(
        functools.partial(_outproj_kernel, tm=tm, alpha=alpha),
        grid=(B, T // tm),
        in_specs=[tile(D_MODEL), tile(CONV_WIDTH), tile(ATTN_WIDTH), modv, modv, modv,
                  full((CONV_WIDTH + ATTN_WIDTH, D_MODEL)), full((1, D_MODEL)), full((1, D_MODEL)),
                  full((D_MODEL, LANES)), full((1, LANES)), pl.BlockSpec(memory_space=pl.ANY)],
        out_specs=(tile(D_MODEL),
                   pl.BlockSpec((tm * ROW_TILE, LANES), lambda b, j: (blk0 + b * (T // tm) + j, 0)),
                   tile(LANES), tile(LANES), pl.BlockSpec((8, LANES), lambda b, j: (0, 0))),
        out_shape=(jax.ShapeDtypeStruct((B, T, D_MODEL), F32),
                   jax.ShapeDtypeStruct(h2_all.shape, F32),
                   jax.ShapeDtypeStruct((B, T, LANES), jnp.int32), jax.ShapeDtypeStruct((B, T, LANES), F32),
                   jax.ShapeDtypeStruct((8, LANES), F32)),
        scratch_shapes=[pltpu.VMEM((8, LANES), F32)],
        input_output_aliases={11: 1},
        compiler_params=_cparams(("arbitrary", "arbitrary")),
        name="outproj",
    )(x, a_n, b_n, gt1, sc2, sh2, w["w_out"], w["ln1_g"], w["ln1_b"], w["router_w"], w["router_b"], h2_all)


def _expert_kernel(order_ref, blk_e_ref, blk_j0_ref, blk_nv_ref, nxt_e_ref, used_ref,
                   h_ref, wgu_hbm, bgu_ref, wd_hbm, bd_ref, y_ref,
                   x0, x1, y0, y1, xb, act_s, wgu_st, wd_st, wgu_bf, wd_bf, gsem, ssem, wsem, *, bm, n_tok):
    s = pl.program_id(0)
    used = used_ref[0]
    dump0 = TOP_K * n_tok

    def weight_copies(e):
        return (pltpu.make_async_copy(wgu_hbm.at[e], wgu_st, wsem.at[0]),
                pltpu.make_async_copy(wd_hbm.at[e], wd_st, wsem.at[1]))

    tok_bits = (n_tok - 1).bit_length()

    def gather_start(j0, xbuf, sem, r):
        tok = jnp.bitwise_and(order_ref[j0 + r], (1 << tok_bits) - 1)
        pltpu.make_async_copy(h_ref.at[tok], xbuf.at[pl.ds(r * ROW_TILE, ROW_TILE)], sem).start()

    def scatter_start(j0, nv, ybuf, sem, r, odd):
        real = lax.shift_right_logical(order_ref[j0 + r], tok_bits)
        dump = dump0 + odd * bm + r
        valid = jnp.right_shift(r - nv, 31)
        dst = dump + jnp.bitwise_and(valid, real - dump)
        pltpu.make_async_copy(ybuf.at[pl.ds(r * ROW_TILE, ROW_TILE)], y_ref.at[dst], sem).start()

    def block_wait(buf, sem):
        pltpu.make_async_copy(buf, buf, sem).wait()

    def switch_weights(b):
        prev = blk_e_ref[jnp.maximum(b - 1, 0)]
        e = blk_e_ref[b]

        @pl.when((b == 0) | (e != prev))
        def _():
            for c in weight_copies(e):
                c.wait()
            wgu_bf[...] = wgu_st[...].astype(BF16)
            wd_bf[...] = wd_st[...].astype(BF16)

            @pl.when(nxt_e_ref[e] >= 0)
            def _():
                for c in weight_copies(nxt_e_ref[e]):
                    c.start()

    def block(b, x_in, y_out, g_next, x_next, gsem_next, s_prev, y_prev, ssem_prev, prev_odd, wait_y_free):
        gj0 = blk_j0_ref[g_next + 1]
        sj0 = blk_j0_ref[s_prev + 1]
        snv = blk_nv_ref[s_prev + 1]
        def issue(k, after=None):
            zero = 0
            if after is not None:
                bits = pltpu.bitcast(jnp.abs(after[0:SUBLANES, 0:LANES]), jnp.int32)
                zero = jnp.minimum(bits[0, 0], 0)
            ng = GATHER_BATCHES
            q, n = (k, ng) if k < ng else (k - ng, DMA_BATCHES - ng)
            for r in range(q * bm // n, (q + 1) * bm // n):
                if k < ng:
                    gather_start(gj0 + zero, x_next, gsem_next, r)
                else:
                    scatter_start(sj0 + zero, snv, y_prev, ssem_prev, r, prev_odd)

        xb[...] = _rows_from_tiles(x_in, bm).astype(BF16)
        e = blk_e_ref[b]
        cw = D_FF // 4
        prev = None
        for c in range(4):
            issue(2 * c, prev)
            gs, ls = slice(c * cw, (c + 1) * cw), slice(D_FF + c * cw, D_FF + (c + 1) * cw)
            g = jnp.dot(xb[...], wgu_bf[:, gs], preferred_element_type=F32) + bgu_ref[e][:, gs]
            issue(2 * c + 1, g)
            lin = jnp.dot(xb[...], wgu_bf[:, ls], preferred_element_type=F32) + bgu_ref[e][:, ls]
            prev = lin
            g = jnp.minimum(g, SWIGLU_LIMIT)
            lin = jnp.clip(lin, -SWIGLU_LIMIT, SWIGLU_LIMIT)
            act_s[:, gs] = (g * _sigmoid(SWIGLU_ALPHA * g) * (lin + 1.0)).astype(BF16)
        hw = D_MODEL // 2
        wait_y_free()
        for h in range(2):
            issue(8 + h, prev)
            y = jnp.dot(act_s[...], wd_bf[:, h * hw:(h + 1) * hw], preferred_element_type=F32)
            y = y + bd_ref[e][:, h * hw:(h + 1) * hw]
            prev = y
            for c in range(hw // LANES):
                y_out[pl.ds(h * (hw // LANES) + c, bm, stride=ROW_TILE), :] = y[:, c * LANES:(c + 1) * LANES]

    b0 = 2 * s
    b1 = b0 + 1

    @pl.when(b0 < used)
    def _():
        @pl.when(s == 0)
        def _():
            for c in weight_copies(blk_e_ref[0]):
                c.start()
            y1[...] = jnp.zeros(y1.shape, F32)

            def first(r, c):
                scatter_start(0, 0, y1, ssem.at[0], r, 0)
                gather_start(blk_j0_ref[1], x0, gsem.at[0], r)
                return c

            lax.fori_loop(0, bm, first, 0)
            block_wait(y1, ssem.at[0])

        def y0_free():
            @pl.when(s > 0)
            def _():
                block_wait(y0, ssem.at[0])

        def y1_free():
            block_wait(y1, ssem.at[1])

        switch_weights(b0)
        block_wait(x0, gsem.at[0])
        block(b0, x0, y0, b1, x1, gsem.at[1], b0 - 1, y1, ssem.at[1], 1, y0_free)

        switch_weights(b1)
        block_wait(x1, gsem.at[1])
        block(b1, x1, y1, b0 + 2, x0, gsem.at[0], b0, y0, ssem.at[0], 0, y1_free)

        @pl.when(b0 + 2 >= used)
        def _():
            def last(r, c):
                scatter_start(blk_j0_ref[b1 + 1], blk_nv_ref[b1 + 1], y1, ssem.at[1], r, 1)
                return c

            lax.fori_loop(0, bm, last, 0)
            block_wait(y0, ssem.at[0])
            block_wait(y1, ssem.at[1])
            block_wait(x0, gsem.at[0])


def _expert(order, blk_e, blk_j0, blk_nv, nxt_e, used, h2, w, bm):
    n_tok = h2.shape[0]
    nblk = blk_e.shape[0]
    f32buf = lambda shp: pltpu.VMEM(shp, F32)
    return pl.pallas_call(
        functools.partial(_expert_kernel, bm=bm, n_tok=n_tok),
        grid_spec=pltpu.PrefetchScalarGridSpec(
            num_scalar_prefetch=6,
            grid=(nblk // 2,),
            in_specs=[
                pl.BlockSpec(memory_space=pl.ANY),
                pl.BlockSpec(memory_space=pl.ANY),
                pl.BlockSpec((N_EXPERTS, 1, 2 * D_FF), lambda s, *_: (0, 0, 0)),
                pl.BlockSpec(memory_space=pl.ANY),
                pl.BlockSpec((N_EXPERTS, 1, D_MODEL), lambda s, *_: (0, 0, 0)),
            ],
            out_specs=pl.BlockSpec(memory_space=pl.ANY),
            scratch_shapes=[f32buf((bm * ROW_TILE, LANES)), f32buf((bm * ROW_TILE, LANES)),
                            f32buf((bm * ROW_TILE, LANES)), f32buf((bm * ROW_TILE, LANES)),
                            pltpu.VMEM((bm, D_MODEL), BF16), pltpu.VMEM((bm, D_FF), BF16),
                            f32buf((D_MODEL, 2 * D_FF)), f32buf((D_FF, D_MODEL)),
                            pltpu.VMEM((D_MODEL, 2 * D_FF), BF16), pltpu.VMEM((D_FF, D_MODEL), BF16),
                            pltpu.SemaphoreType.DMA((2,)), pltpu.SemaphoreType.DMA((2,)),
                            pltpu.SemaphoreType.DMA((2,))],
        ),
        out_shape=jax.ShapeDtypeStruct((TOP_K * n_tok + 2 * bm, ROW_TILE, LANES), F32),
        compiler_params=_cparams(("arbitrary",)),
        name="moe_expert",
    )(order, blk_e, blk_j0, blk_nv, nxt_e, used, h2, w["w_gu"], w["b_gu"], w["w_down"], w["b_down"])


def _combine_kernel(y0_ref, y1_ref, y2_ref, y3_ref, x1_ref, gates_ref, gt2_ref, l2g_ref, l2b_ref, o_ref, *, alpha):
    gates = gates_ref[...]
    tm = gates.shape[0]
    f = gates[:, 0:1] * _rows_from_tiles(y0_ref, tm)
    for k, y_ref in enumerate((y1_ref, y2_ref, y3_ref), start=1):
        f = f + gates[:, k:k + 1] * _rows_from_tiles(y_ref, tm)
    o_ref[...] = _layernorm(alpha * x1_ref[...] + gt2_ref[0] * f) * l2g_ref[...] + l2b_ref[...]


def _combine(y_rows, x1, gates, gt2, w, tm, tokens_per_batch, alpha, n_tok, tok0):
    N = x1.shape[0]
    per_b = tokens_per_batch // tm
    nt = N // tm
    assert n_tok % tm == 0 and tok0 % tm == 0
    slot = lambda k: pl.BlockSpec((tm * ROW_TILE, LANES), lambda i: ((k * n_tok + tok0) // tm + i, 0))
    return pl.pallas_call(
        functools.partial(_combine_kernel, alpha=alpha),
        grid=(nt,),
        in_specs=[slot(0), slot(1), slot(2), slot(3),
                  pl.BlockSpec((tm, D_MODEL), lambda i: (i, 0)),
                  pl.BlockSpec((tm, LANES), lambda i: (i, 0)),
                  pl.BlockSpec((1, 1, D_MODEL), lambda i: (i // per_b, 0, 0)),
                  pl.BlockSpec((1, D_MODEL), lambda i: (0, 0)),
                  pl.BlockSpec((1, D_MODEL), lambda i: (0, 0))],
        out_specs=pl.BlockSpec((tm, D_MODEL), lambda i: (i, 0)),
        out_shape=jax.ShapeDtypeStruct((N, D_MODEL), F32),
        compiler_params=_cparams(("arbitrary",)),
        name="moe_combine",
    )(y_rows, y_rows, y_rows, y_rows, x1, gates, gt2, w["ln2_g"], w["ln2_b"])


def _moe_experts(h2, idx, counts, w, bm):
    N = h2.shape[0]
    n_rows = N * TOP_K
    nblk = (n_rows + N_EXPERTS * (bm - 1)) // bm
    nblk += nblk % 2
    experts = jnp.arange(N_EXPERTS, dtype=jnp.int32)
    id_bits = (n_rows - 1).bit_length()
    tok_bits = (N - 1).bit_length()
    assert tok_bits + (TOP_K * N - 1).bit_length() <= 32
    keys = jnp.left_shift(idx.reshape(-1), id_bits) + jnp.arange(n_rows, dtype=jnp.int32)
    keys = jnp.concatenate([keys, jnp.full(((1 << id_bits) - n_rows,), jnp.iinfo(jnp.int32).max, jnp.int32)])
    flat = jnp.bitwise_and(jnp.sort(keys, stable=False)[:n_rows], (1 << id_bits) - 1).astype(jnp.uint32)
    tok = jnp.right_shift(flat, 2)
    row = jnp.bitwise_and(flat, TOP_K - 1) * N + tok
    order = lax.bitcast_convert_type(jnp.left_shift(row, tok_bits) | tok, jnp.int32)
    order = jnp.concatenate([order, jnp.zeros((bm,), jnp.int32)])
    nb_e = (counts + bm - 1) // bm
    blk_end = jnp.cumsum(nb_e)
    first_blk = blk_end - nb_e
    start_sorted = jnp.cumsum(counts) - counts
    used = blk_end[-1].astype(jnp.int32)
    b = jnp.arange(-1, nblk + 1, dtype=jnp.int32)
    bc = jnp.clip(b, 0, used - 1)
    e = jnp.minimum(jnp.sum(blk_end[None, :] <= bc[:, None], axis=1), N_EXPERTS - 1).astype(jnp.int32)
    pick = lambda table: jnp.sum(jnp.where(e[:, None] == experts, table, 0), axis=1)
    local = bc - pick(first_blk)
    blk_j0 = (pick(start_sorted) + local * bm).astype(jnp.int32)
    blk_nv = jnp.where((b >= 0) & (b < used), jnp.minimum(bm, pick(counts) - local * bm), 0).astype(jnp.int32)
    blk_e = e[1:nblk + 1]
    later = (experts[None, :] > experts[:, None]) & (counts[None, :] > 0)
    nxt = jnp.min(jnp.where(later, experts[None, :], N_EXPERTS), axis=1)
    nxt_e = jnp.where(nxt < N_EXPERTS, nxt, -1).astype(jnp.int32)
    y_rows = _expert(order, blk_e, blk_j0, blk_nv, nxt_e, used.reshape(1), h2, w, bm)
    return y_rows.reshape(-1, LANES)


def _rope_tables(pos):
    half = QK_ROPE // 2
    inv = ROPE_THETA ** (-jnp.arange(half, dtype=F32) / half)
    ang = pos.astype(F32)[:, None] * inv[None, :]
    cos, sin = jnp.cos(ang), jnp.sin(ang)
    cos32 = jnp.concatenate([cos, cos], axis=1)
    sin32 = jnp.concatenate([-sin, sin], axis=1)
    return jnp.tile(cos32, (1, LANES // QK_ROPE)), jnp.tile(sin32, (1, LANES // QK_ROPE))


def _swap_halves(w32):
    shp = w32.shape
    w = w32.reshape(shp[:-1] + (shp[-1] // QK_ROPE, 2, QK_ROPE // 2))
    return w[..., ::-1, :].reshape(shp)


def _prep_weights(l, w_in, conv_w, g_qa, w_qb, g_kva, w_kvb, g_out_conv, g_out_attn, w_out,
                  ln1_g, ln1_b, router_w, router_b, w_gu, b_gu, w_down, b_down, ln2_g, ln2_b):
    w = {}
    wi = w_in[l]
    k_r = wi[:, _O_KR:_O_KR + QK_ROPE]
    rep = LANES // QK_ROPE
    w["w_in"] = jnp.concatenate([wi[:, :_O_KR], jnp.tile(k_r, (1, rep)), jnp.tile(_swap_halves(k_r), (1, rep))],
                                axis=1).astype(BF16)
    w["conv_w"] = conv_w[l]
    w["g_qa"] = g_qa[l].reshape(1, Q_LORA)
    w["g_kva"] = g_kva[l].reshape(1, KV_LORA)
    w["g_oc"] = g_out_conv[l].reshape(1, CONV_WIDTH)
    w["g_oa"] = g_out_attn[l].reshape(1, ATTN_WIDTH)
    wq = w_qb[l].reshape(Q_LORA, N_HEADS, QK_NOPE + QK_ROPE)
    w["wq_nope"] = wq[:, :, :QK_NOPE].reshape(Q_LORA, N_HEADS * QK_NOPE).astype(BF16)
    wq_rope = wq[:, :, QK_NOPE:].reshape(Q_LORA, N_HEADS * QK_ROPE)
    w["wq_rope"] = wq_rope.astype(BF16)
    w["wq_rope_sw"] = _swap_halves(wq_rope).astype(BF16)
    w_uk = jnp.transpose(w_kvb[l][:, :, :QK_NOPE], (1, 2, 0))
    w_uv = jnp.transpose(w_kvb[l][:, :, QK_NOPE:], (1, 0, 2))
    zk = jnp.zeros((QK_NOPE, KV_LORA), F32)
    zv = jnp.zeros((KV_LORA, V_HEAD), F32)
    w["w_uk_pair"] = jnp.stack([
        jnp.concatenate([jnp.concatenate([w_uk[2 * p], zk], axis=1),
                         jnp.concatenate([zk, w_uk[2 * p + 1]], axis=1)], axis=0)
        for p in range(N_HEADS // 2)]).astype(BF16)
    w["w_uv_pair"] = jnp.stack([
        jnp.concatenate([jnp.concatenate([w_uv[2 * p], zv], axis=1),
                         jnp.concatenate([zv, w_uv[2 * p + 1]], axis=1)], axis=0)
        for p in range(N_HEADS // 2)]).astype(BF16)
    w["w_out"] = w_out[l].astype(BF16)
    w["ln1_g"] = ln1_g[l].reshape(1, D_MODEL)
    w["ln1_b"] = ln1_b[l].reshape(1, D_MODEL)
    w["ln2_g"] = ln2_g[l].reshape(1, D_MODEL)
    w["ln2_b"] = ln2_b[l].reshape(1, D_MODEL)
    w["router_w"] = jnp.pad(router_w[l], ((0, 0), (0, LANES - N_EXPERTS))).astype(BF16)
    w["router_b"] = jnp.concatenate([router_b[l], jnp.full((LANES - N_EXPERTS,), NEG, F32)]).reshape(1, LANES)
    w["w_gu"] = w_gu[l]
    w["b_gu"] = b_gu[l].reshape(N_EXPERTS, 1, 2 * D_FF)
    w["w_down"] = w_down[l]
    w["b_down"] = b_down[l].reshape(N_EXPERTS, 1, D_MODEL)
    return w


def _mixer(x, mod, conv_prev, past, pos0, h2_all, tok0, w, alpha, *, tm_in, tm_out, tq=128, tk=512):
    B, T, _ = x.shape
    sh1, sc1, gt1, sh2, sc2, gt2 = [mod[:, None, i * D_MODEL:(i + 1) * D_MODEL] for i in range(N_MOD)]
    cos_t, sin_t = _rope_tables(pos0 + jnp.arange(T, dtype=jnp.int32))
    a_n, qlat, qrope, kcat, ckv, krope, conv_new, *vt = _inproj(x, sc1, sh1, conv_prev, cos_t, sin_t, w, tm_in,
                                                                 with_vt=past is None)
    if past is None:
        b_n = _attn_prompt(qlat, qrope, kcat, vt[0], w, tq, tk)
    else:
        b_n = _attn_sample(qlat, qrope, kcat, past[0], past[1], w, tk)
    x1, h2_all, meta, gates, cnt = _outproj(x, a_n, b_n, gt1, sc2, sh2, h2_all, tok0, w, tm_out, alpha)
    N = B * T
    route = dict(x1=x1.reshape(N, D_MODEL), idx=meta.reshape(N, LANES)[:, :TOP_K], gates=gates.reshape(N, LANES),
                 counts=cnt[0, :N_EXPERTS].astype(jnp.int32), gt2=gt2, shape=(B, T))
    return h2_all, route, ckv, krope, conv_new


def kernel(x_prompt, x_sample, c_prompt, c_sample, cache_ckv, cache_krope, state_conv, w_ada, b_ada, w_in, conv_w, g_qa, w_qb, g_kva, w_kvb, g_out_conv, g_out_attn, w_out, ln1_g, ln1_b, router_w, router_b, w_gu, b_gu, w_down, b_down, ln2_g, ln2_b):
    depth = w_ada.shape[0]
    Bp, Tp, _ = x_prompt.shape
    Bs, Ts, _ = x_sample.shape
    past_len = cache_ckv.shape[2]
    assert Ts == CHUNK and past_len % 512 == 0 and Tp % 1024 == 0
    alpha = (2.0 * depth) ** 0.25
    xp, xs = x_prompt, x_sample
    outs = [[] for _ in range(6)]
    c_all = jnp.concatenate([c_prompt, c_sample, jnp.zeros((16 - Bp - Bs, D_MODEL), F32)], axis=0)
    for l in range(depth):
        w = _prep_weights(l, w_in, conv_w, g_qa, w_qb, g_kva, w_kvb, g_out_conv, g_out_attn, w_out,
                          ln1_g, ln1_b, router_w, router_b, w_gu, b_gu, w_down, b_down, ln2_g, ln2_b)
        mod = _ada(c_all, w_ada[l], b_ada[l])
        n_p, n_s = Bp * Tp, Bs * Ts
        n_tok = n_p + n_s
        h2_all = jnp.zeros((n_tok * ROW_TILE, LANES), F32)
        h2_all, rp, ckv_p, kr_p, cv_p = _mixer(xp, mod[:Bp], jnp.zeros((Bp, CONV_K - 1, CONV_WIDTH), F32), None, 0,
                                               h2_all, 0, w, alpha, tm_in=1024, tm_out=512)
        h2_all, rs, ckv_s, kr_s, cv_s = _mixer(xs, mod[Bp:Bp + Bs], state_conv[l], (cache_ckv[l], cache_krope[l]),
                                               past_len, h2_all, n_p, w, alpha, tm_in=Ts, tm_out=Ts)
        y_rows = _moe_experts(h2_all.reshape(n_tok, ROW_TILE, LANES), jnp.concatenate([rp["idx"], rs["idx"]]),
                              rp["counts"] + rs["counts"], w, bm=256)
        xp = _combine(y_rows, rp["x1"], rp["gates"], rp["gt2"], w, 256, Tp, alpha, n_tok, 0).reshape(rp["shape"] + (D_MODEL,))
        xs = _combine(y_rows, rs["x1"], rs["gates"], rs["gt2"], w, Ts, Ts, alpha, n_tok, n_p).reshape(rs["shape"] + (D_MODEL,))
        for o, v in zip(outs, (ckv_p, kr_p, cv_p, ckv_s, kr_s, cv_s)):
            o.append(v)
    return (xp, xs) + tuple(jnp.stack(o) for o in outs)
```

```python
import functools
import math

import jax
import jax.numpy as jnp
from jax import lax
from jax.experimental import pallas as pl
from jax.experimental.pallas import tpu as pltpu

F32 = jnp.float32
BF16 = jnp.bfloat16

D_MODEL = 1024
CHUNK = 64
CONV_WIDTH = 512
CONV_K = 3
N_HEADS = 8
QK_NOPE = 64
QK_ROPE = 32
V_HEAD = 64
Q_LORA = 256
KV_LORA = 128
ATTN_WIDTH = N_HEADS * V_HEAD
ROPE_THETA = 10000.0
ATTN_SCALE = 1.0 / math.sqrt(QK_NOPE + QK_ROPE)
Q_SCALE = ATTN_SCALE * math.log2(math.e)
N_EXPERTS = 32
TOP_K = 4
D_FF = 1024
SWIGLU_LIMIT = 7.0
SWIGLU_ALPHA = 1.702
N_MOD = 6
LN_EPS = 1e-5
RMS_EPS = 1e-6

LANES = 128
SUBLANES = 8
DMA_BATCHES = 10
GATHER_BATCHES = 6
NEG = -1e30
VMEM_LIMIT = 56 * 1024 * 1024

_O_XB, _O_XC, _O_XV = 0, CONV_WIDTH, 2 * CONV_WIDTH
_O_QA = 3 * CONV_WIDTH
_O_KVA = _O_QA + Q_LORA
_O_KR = _O_KVA + KV_LORA
_O_KRS = _O_KR + LANES
IN_COLS_EXT = _O_KRS + LANES


def _cparams(sem):
    return pltpu.CompilerParams(dimension_semantics=sem, vmem_limit_bytes=VMEM_LIMIT)


def _layernorm(x):
    mu = jnp.mean(x, axis=-1, keepdims=True)
    xc = x - mu
    var = jnp.mean(xc * xc, axis=-1, keepdims=True)
    return xc * lax.rsqrt(var + LN_EPS)


def _rms(x):
    return x * lax.rsqrt(jnp.mean(x * x, axis=-1, keepdims=True) + RMS_EPS)


def _sigmoid(x):
    return 1.0 / (1.0 + jnp.exp(-x))


ROW_TILE = D_MODEL // LANES


def _rows_from_tiles(ref, n):
    return jnp.concatenate([ref[pl.ds(c, n, stride=ROW_TILE), :] for c in range(ROW_TILE)], axis=1)


def _rows_to_tiles(ref, x):
    n = x.shape[0]
    for c in range(ROW_TILE):
        ref[pl.ds(c, n, stride=ROW_TILE), :] = x[:, c * LANES:(c + 1) * LANES]


def _ada_kernel(c_ref, w_ref, b_ref, o_ref):
    c = c_ref[...]
    s = (c * _sigmoid(c)).astype(BF16)
    o_ref[...] = jnp.dot(s, w_ref[...].astype(BF16), preferred_element_type=F32) + b_ref[...]


def _ada(c_all, w_ada, b_ada):
    rows = c_all.shape[0]
    ncol = w_ada.shape[1]
    tn = 1024
    return pl.pallas_call(
        _ada_kernel,
        grid=(ncol // tn,),
        in_specs=[pl.BlockSpec((rows, D_MODEL), lambda j: (0, 0)),
                  pl.BlockSpec((D_MODEL, tn), lambda j: (0, j)),
                  pl.BlockSpec((1, tn), lambda j: (0, j))],
        out_specs=pl.BlockSpec((rows, tn), lambda j: (0, j)),
        out_shape=jax.ShapeDtypeStruct((rows, ncol), F32),
        compiler_params=_cparams(("arbitrary",)),
        name="ada",
    )(c_all, w_ada, b_ada.reshape(1, ncol))


def _inproj_kernel(x_ref, sc_ref, sh_ref, win_ref, cw_ref, cprev_ref, gqa_ref, gkva_ref, goc_ref,
                   wqn_ref, wqr_ref, wqrs_ref, wuk_ref, cos_ref, sin_ref,
                   an_ref, qlat_ref, qrope_ref, kcat_ref, ckv_ref, krope_ref, cnew_ref,
                   *rest, tm, with_vt):
    vt_ref, ubuf = rest if with_vt else (None, rest[0])
    j = pl.program_id(1)
    x = x_ref[0]
    h = _layernorm(x) * (1.0 + sc_ref[0]) + sh_ref[0]
    proj = jnp.dot(h.astype(BF16), win_ref[...], preferred_element_type=F32)
    xb = proj[:, _O_XB:_O_XB + CONV_WIDTH]
    xc = proj[:, _O_XC:_O_XC + CONV_WIDTH]
    xv = proj[:, _O_XV:_O_XV + CONV_WIDTH]
    q_a = proj[:, _O_QA:_O_QA + Q_LORA]
    kv_a = proj[:, _O_KVA:_O_KVA + KV_LORA]
    kr4 = proj[:, _O_KR:_O_KR + LANES]
    kr4s = proj[:, _O_KRS:_O_KRS + LANES]

    u = xc * xv

    @pl.when(j == 0)
    def _():
        ubuf[6:8, :] = cprev_ref[0]

    ubuf[8:8 + tm, :] = u
    conv = (cw_ref[0:1, :] * ubuf[6:6 + tm, :] + cw_ref[1:2, :] * ubuf[7:7 + tm, :]
            + cw_ref[2:3, :] * u)
    ubuf[0:8, :] = ubuf[tm:tm + 8, :]
    cnew_ref[0] = u[tm - (CONV_K - 1):tm, :]
    an_ref[0] = (_rms(xb * conv) * goc_ref[...]).astype(BF16)

    cos = cos_ref[...]
    sin = sin_ref[...]

    ckv = _rms(kv_a) * gkva_ref[...]
    kro4 = kr4 * cos + kr4s * sin
    ckv_ref[0] = ckv
    krope_ref[0] = kro4[:, :QK_ROPE]
    kcat_ref[0] = jnp.concatenate([ckv, kro4], axis=1).astype(BF16)
    if with_vt:
        vt_ref[0] = ckv.T.astype(BF16)

    qn = (_rms(q_a) * gqa_ref[...]).astype(BF16)
    q_nope = jnp.dot(qn, wqn_ref[...], preferred_element_type=F32)
    xr = jnp.dot(qn, wqr_ref[...], preferred_element_type=F32)
    xrs = jnp.dot(qn, wqrs_ref[...], preferred_element_type=F32)
    for g in range(2):
        sl = slice(g * LANES, (g + 1) * LANES)
        qrope_ref[0, :, sl] = ((xr[:, sl] * cos + xrs[:, sl] * sin) * Q_SCALE).astype(BF16)
    for p in range(N_HEADS // 2):
        qp = q_nope[:, p * LANES:(p + 1) * LANES].astype(BF16)
        ql = jnp.dot(qp, wuk_ref[p], preferred_element_type=F32)
        qlat_ref[0, :, p * 2 * KV_LORA:(p + 1) * 2 * KV_LORA] = (ql * Q_SCALE).astype(BF16)


def _inproj(x, sc1, sh1, conv_prev, cos_t, sin_t, w, tm, with_vt):
    B, T, _ = x.shape
    nt = T // tm
    full = lambda shp: pl.BlockSpec(shp, lambda b, j: (0,) * len(shp))
    vt_shape = (jax.ShapeDtypeStruct((B, KV_LORA, T), BF16),) if with_vt else ()
    vt_spec = (pl.BlockSpec((1, KV_LORA, tm), lambda b, j: (b, 0, j)),) if with_vt else ()
    out_shapes = (
        jax.ShapeDtypeStruct((B, T, CONV_WIDTH), BF16),
        jax.ShapeDtypeStruct((B, T, N_HEADS * KV_LORA), BF16),
        jax.ShapeDtypeStruct((B, T, 2 * LANES), BF16),
        jax.ShapeDtypeStruct((B, T, 2 * LANES), BF16),
        jax.ShapeDtypeStruct((B, T, KV_LORA), F32),
        jax.ShapeDtypeStruct((B, T, QK_ROPE), F32),
        jax.ShapeDtypeStruct((B, CONV_K - 1, CONV_WIDTH), F32),
    ) + vt_shape
    tile = lambda c: pl.BlockSpec((1, tm, c), lambda b, j: (b, j, 0))
    return pl.pallas_call(
        functools.partial(_inproj_kernel, tm=tm, with_vt=with_vt),
        grid=(B, nt),
        in_specs=[
            tile(D_MODEL),
            pl.BlockSpec((1, 1, D_MODEL), lambda b, j: (b, 0, 0)),
            pl.BlockSpec((1, 1, D_MODEL), lambda b, j: (b, 0, 0)),
            full((D_MODEL, IN_COLS_EXT)),
            full((CONV_K, CONV_WIDTH)),
            pl.BlockSpec((1, CONV_K - 1, CONV_WIDTH), lambda b, j: (b, 0, 0)),
            full((1, Q_LORA)), full((1, KV_LORA)), full((1, CONV_WIDTH)),
            full((Q_LORA, N_HEADS * QK_NOPE)), full((Q_LORA, 2 * LANES)), full((Q_LORA, 2 * LANES)),
            full((N_HEADS // 2, LANES, 2 * KV_LORA)),
            pl.BlockSpec((tm, LANES), lambda b, j: (j, 0)),
            pl.BlockSpec((tm, LANES), lambda b, j: (j, 0)),
        ],
        out_specs=(tile(CONV_WIDTH), tile(N_HEADS * KV_LORA), tile(2 * LANES), tile(2 * LANES),
                   tile(KV_LORA), tile(QK_ROPE),
                   pl.BlockSpec((1, CONV_K - 1, CONV_WIDTH), lambda b, j: (b, 0, 0))) + vt_spec,
        out_shape=out_shapes,
        scratch_shapes=[pltpu.VMEM((tm + 8, CONV_WIDTH), F32)],
        compiler_params=_cparams(("arbitrary", "arbitrary")),
        name="inproj",
    )(x, sc1, sh1, w["w_in"], w["conv_w"], conv_prev, w["g_qa"], w["g_kva"], w["g_oc"],
      w["wq_nope"], w["wq_rope"], w["wq_rope_sw"], w["w_uk_pair"], cos_t, sin_t)


def _stack_queries(qlat_ref, qrope_ref, qs, tq):
    lane = lax.broadcasted_iota(jnp.int32, (tq, LANES), 1)
    for h in range(N_HEADS):
        g, i = divmod(h, 4)
        rope = qrope_ref[0, :, g * LANES:(g + 1) * LANES]
        keep = (lane >= i * QK_ROPE) & (lane < (i + 1) * QK_ROPE)
        qs[h * tq:(h + 1) * tq, 0:KV_LORA] = qlat_ref[0, :, h * KV_LORA:(h + 1) * KV_LORA]
        qs[h * tq:(h + 1) * tq, KV_LORA:KV_LORA + LANES] = jnp.where(keep, rope, jnp.zeros_like(rope))


def _softmax_step(qs, k, v, m_s, l_s, acc_s):
    tk = k.shape[0]
    s = lax.dot_general(qs[...], k, (((1,), (1,)), ((), ())), preferred_element_type=F32)
    m_prev = m_s[...]
    m_new = jnp.maximum(m_prev, jnp.max(s, axis=1, keepdims=True))
    alpha = jnp.exp2(m_prev - m_new)
    if tk % LANES == 0:
        p = jnp.exp2(s - jnp.tile(m_new, (1, tk // LANES)))
    else:
        p = jnp.exp2(s - m_new[:, :tk])
    l_s[...] = alpha * l_s[...] + jnp.sum(p, axis=1, keepdims=True)
    acc_s[...] = alpha * acc_s[...] + jnp.dot(p.astype(BF16), v, preferred_element_type=F32)
    m_s[...] = m_new


def _attn_epilogue(acc_s, l_s, wuv_ref, g_ref, o_ref, tq):
    o = acc_s[...] / l_s[...]
    parts = []
    for p in range(N_HEADS // 2):
        op = jnp.concatenate([o[(2 * p) * tq:(2 * p + 1) * tq], o[(2 * p + 1) * tq:(2 * p + 2) * tq]], axis=1)
        parts.append(jnp.dot(op.astype(BF16), wuv_ref[p], preferred_element_type=F32))
    b = jnp.concatenate(parts, axis=1)
    o_ref[0] = (_rms(b) * g_ref[...]).astype(BF16)


def _attn_prompt_kernel(qlat_ref, qrope_ref, k_ref, vt_ref, wuvt_ref, g_ref, o_ref, qs, m_s, l_s, acc_s, sa, sb,
                        *, tq, tk):
    i = pl.program_id(1)
    M = N_HEADS * tq
    _stack_queries(qlat_ref, qrope_ref, qs, tq)
    m_s[...] = jnp.full(m_s.shape, NEG, F32)
    l_s[...] = jnp.zeros(l_s.shape, F32)
    acc_s[...] = jnp.zeros(acc_s.shape, F32)
    q0 = i * tq
    n_full = (q0 + CHUNK) // tk

    def scores(t, dst):
        k = k_ref[0, pl.ds(pl.multiple_of(t * tk, tk), tk), :]
        dst[...] = lax.dot_general(k, qs[...], (((1,), (1,)), ((), ())), preferred_element_type=F32)

    def update(t, src, limit=None):
        start = pl.multiple_of(t * tk, tk)
        vt = vt_ref[0, :, pl.ds(start, tk)]
        s = src[...]
        if limit is not None:
            kpos = start + lax.broadcasted_iota(jnp.int32, (tk, 1), 0)
            s = jnp.where(kpos < limit, s, NEG)
        m_prev = m_s[...]
        m_new = jnp.maximum(m_prev, jnp.max(s, axis=0, keepdims=True))
        alpha = jnp.exp2(m_prev - m_new)
        p = jnp.exp2(s - m_new)
        l_s[...] = alpha * l_s[...] + jnp.sum(p, axis=0, keepdims=True)
        acc_s[...] = alpha * acc_s[...] + jnp.dot(vt, p.astype(BF16), preferred_element_type=F32)
        m_s[...] = m_new

    scores(0, sa)

    def body(j, carry):
        t = 2 * j
        scores(t + 1, sb)
        update(t, sa)
        scores(t + 2, sa)
        update(t + 1, sb)
        return carry

    lax.fori_loop(0, n_full // 2, body, 0)

    col_t = jnp.bitwise_and(lax.broadcasted_iota(jnp.int32, (1, M), 1), tq - 1)
    limit = q0 + (jnp.right_shift(col_t, CHUNK.bit_length() - 1) + 1) * CHUNK
    odd = n_full % 2

    @pl.when(odd == 0)
    def _():
        update(n_full, sa, limit)

    @pl.when(odd == 1)
    def _():
        scores(n_full, sb)
        update(n_full - 1, sa)
        update(n_full, sb, limit)

    o_t = acc_s[...] / l_s[...]
    parts = []
    for p in range(N_HEADS // 2):
        pair = jnp.concatenate([o_t[:, (2 * p) * tq:(2 * p + 1) * tq], o_t[:, (2 * p + 1) * tq:(2 * p + 2) * tq]],
                               axis=0)
        parts.append(jnp.dot(wuvt_ref[p], pair.astype(BF16), preferred_element_type=F32))
    b = jnp.concatenate(parts, axis=0).T
    o_ref[0] = (_rms(b) * g_ref[...]).astype(BF16)


def _attn_prompt(qlat, qrope, kcat, vt, w, tq, tk):
    B, T, _ = qlat.shape
    M = N_HEADS * tq
    return pl.pallas_call(
        functools.partial(_attn_prompt_kernel, tq=tq, tk=tk),
        grid=(B, T // tq),
        in_specs=[
            pl.BlockSpec((1, tq, N_HEADS * KV_LORA), lambda b, i: (b, i, 0)),
            pl.BlockSpec((1, tq, 2 * LANES), lambda b, i: (b, i, 0)),
            pl.BlockSpec((1, T, 2 * LANES), lambda b, i: (b, 0, 0)),
            pl.BlockSpec((1, KV_LORA, T), lambda b, i: (b, 0, 0)),
            pl.BlockSpec((N_HEADS // 2, LANES, 2 * KV_LORA), lambda b, i: (0, 0, 0)),
            pl.BlockSpec((1, ATTN_WIDTH), lambda b, i: (0, 0)),
        ],
        out_specs=pl.BlockSpec((1, tq, ATTN_WIDTH), lambda b, i: (b, i, 0)),
        out_shape=jax.ShapeDtypeStruct((B, T, ATTN_WIDTH), BF16),
        scratch_shapes=[pltpu.VMEM((M, 2 * LANES), BF16), pltpu.VMEM((1, M), F32),
                        pltpu.VMEM((1, M), F32), pltpu.VMEM((KV_LORA, M), F32),
                        pltpu.VMEM((tk, M), F32), pltpu.VMEM((tk, M), F32)],
        compiler_params=_cparams(("arbitrary", "arbitrary")),
        name="attn_prompt",
    )(qlat, qrope, kcat, vt, jnp.swapaxes(w["w_uv_pair"], 1, 2), w["g_oa"])


def _attn_sample_kernel(qlat_ref, qrope_ref, knew_ref, pckv_ref, pkr_ref, wuv_ref, g_ref, o_ref,
                        qs, m_s, l_s, acc_s, *, tq, tk, n_past):
    _stack_queries(qlat_ref, qrope_ref, qs, tq)
    m_s[...] = jnp.full(m_s.shape, NEG, F32)
    l_s[...] = jnp.zeros(l_s.shape, F32)
    acc_s[...] = jnp.zeros(acc_s.shape, F32)

    def body(t, carry):
        start = pl.multiple_of(t * tk, tk)
        ck = pckv_ref[0, pl.ds(start, tk), :]
        kr = pkr_ref[0, pl.ds(start, tk), :]
        k = jnp.concatenate([ck, kr, kr, kr, kr], axis=1).astype(BF16)
        _softmax_step(qs, k, k[:, :KV_LORA], m_s, l_s, acc_s)
        return carry

    lax.fori_loop(0, n_past // tk, body, 0)
    k = knew_ref[0]
    _softmax_step(qs, k, k[:, :KV_LORA], m_s, l_s, acc_s)
    _attn_epilogue(acc_s, l_s, wuv_ref, g_ref, o_ref, tq)


def _attn_sample(qlat, qrope, kcat, past_ckv, past_krope, w, tk):
    B, T, _ = qlat.shape
    n_past = past_ckv.shape[1]
    M = N_HEADS * T
    per_b = lambda r, c: pl.BlockSpec((1, r, c), lambda b: (b, 0, 0))
    return pl.pallas_call(
        functools.partial(_attn_sample_kernel, tq=T, tk=tk, n_past=n_past),
        grid=(B,),
        in_specs=[per_b(T, N_HEADS * KV_LORA), per_b(T, 2 * LANES), per_b(T, 2 * LANES),
                  per_b(n_past, KV_LORA), per_b(n_past, QK_ROPE),
                  pl.BlockSpec((N_HEADS // 2, 2 * KV_LORA, LANES), lambda b: (0, 0, 0)),
                  pl.BlockSpec((1, ATTN_WIDTH), lambda b: (0, 0))],
        out_specs=per_b(T, ATTN_WIDTH),
        out_shape=jax.ShapeDtypeStruct((B, T, ATTN_WIDTH), BF16),
        scratch_shapes=[pltpu.VMEM((M, 2 * LANES), BF16), pltpu.VMEM((M, LANES), F32),
                        pltpu.VMEM((M, LANES), F32), pltpu.VMEM((M, KV_LORA), F32)],
        compiler_params=_cparams(("arbitrary",)),
        name="attn_sample",
    )(qlat, qrope, kcat, past_ckv, past_krope, w["w_uv_pair"], w["g_oa"])


def _outproj_kernel(x_ref, an_ref, bn_ref, gt1_ref, sc2_ref, sh2_ref, wo_ref, l1g_ref, l1b_ref,
                    rw_ref, rb_ref, h2_all_ref, x1_ref, h2_ref, meta_ref, gates_ref, cnt_ref, run_s,
                    *, tm, alpha):
    del h2_all_ref
    first = (pl.program_id(0) == 0) & (pl.program_id(1) == 0)

    @pl.when(first)
    def _():
        run_s[...] = jnp.zeros(run_s.shape, F32)

    m = (jnp.dot(an_ref[0], wo_ref[0:CONV_WIDTH, :], preferred_element_type=F32)
         + jnp.dot(bn_ref[0], wo_ref[CONV_WIDTH:, :], preferred_element_type=F32))
    x1 = _layernorm(alpha * x_ref[0] + gt1_ref[0] * m) * l1g_ref[...] + l1b_ref[...]
    x1_ref[0] = x1
    h2 = _layernorm(x1) * (1.0 + sc2_ref[0]) + sh2_ref[0]
    _rows_to_tiles(h2_ref, h2)

    logits = jnp.dot(h2.astype(BF16), rw_ref[...], preferred_element_type=F32) + rb_ref[...]
    lane = lax.broadcasted_iota(jnp.int32, (tm, LANES), 1)
    lane_f = lane.astype(F32)
    lg = logits
    vals, sels = [], []
    chosen = jnp.zeros((tm, LANES), F32)
    for _ in range(TOP_K):
        mx = jnp.max(lg, axis=1, keepdims=True)
        idx = jnp.min(jnp.where(lg == mx, lane_f, float(LANES)), axis=1, keepdims=True)
        sel = lane_f == idx
        vals.append(mx)
        sels.append(idx)
        chosen = jnp.where(sel, 1.0, chosen)
        lg = jnp.where(sel, NEG, lg)

    es = [jnp.exp(v - vals[0]) for v in vals]
    denom = es[0] + es[1] + es[2] + es[3]

    run_s[0:1, :] = run_s[0:1, :] + jnp.sum(chosen, axis=0, keepdims=True)
    cnt_ref[...] = jnp.broadcast_to(run_s[0:1, :], cnt_ref.shape)

    meta = jnp.zeros((tm, LANES), jnp.int32)
    gates = jnp.zeros((tm, LANES), F32)
    for k in range(TOP_K):
        meta = jnp.where(lane == k, sels[k].astype(jnp.int32), meta)
        gates = jnp.where(lane == k, es[k] / denom, gates)
    meta_ref[0] = meta
    gates_ref[0] = gates


def _outproj(x, a_n, b_n, gt1, sc2, sh2, h2_all, tok0, w, tm, alpha):
    B, T, _ = x.shape
    blk0 = tok0 // tm
    tile = lambda c: pl.BlockSpec((1, tm, c), lambda b, j: (b, j, 0))
    modv = pl.BlockSpec((1, 1, D_MODEL), lambda b, j: (b, 0, 0))
    full = lambda shp: pl.BlockSpec(shp, lambda b, j: (0,) * len(shp))
    return pl.pallas_call(
        functools.partial(_outproj_kernel, tm=tm, alpha=alpha),
        grid=(B, T // tm),
        in_specs=[tile(D_MODEL), tile(CONV_WIDTH), tile(ATTN_WIDTH), modv, modv, modv,
                  full((CONV_WIDTH + ATTN_WIDTH, D_MODEL)), full((1, D_MODEL)), full((1, D_MODEL)),
                  full((D_MODEL, LANES)), full((1, LANES)), pl.BlockSpec(memory_space=pl.ANY)],
        out_specs=(tile(D_MODEL),
                   pl.BlockSpec((tm * ROW_TILE, LANES), lambda b, j: (blk0 + b * (T // tm) + j, 0)),
                   tile(LANES), tile(LANES), pl.BlockSpec((8, LANES), lambda b, j: (0, 0))),
        out_shape=(jax.ShapeDtypeStruct((B, T, D_MODEL), F32),
                   jax.ShapeDtypeStruct(h2_all.shape, F32),
                   jax.ShapeDtypeStruct((B, T, LANES), jnp.int32), jax.ShapeDtypeStruct((B, T, LANES), F32),
                   jax.ShapeDtypeStruct((8, LANES), F32)),
        scratch_shapes=[pltpu.VMEM((8, LANES), F32)],
        input_output_aliases={11: 1},
        compiler_params=_cparams(("arbitrary", "arbitrary")),
        name="outproj",
    )(x, a_n, b_n, gt1, sc2, sh2, w["w_out"], w["ln1_g"], w["ln1_b"], w["router_w"], w["router_b"], h2_all)


def _expert_kernel(order_ref, blk_e_ref, blk_j0_ref, blk_nv_ref, nxt_e_ref, used_ref,
                   h_ref, wgu_hbm, bgu_ref, wd_hbm, bd_ref, y_ref,
                   x0, x1, y0, y1, xb, act_s, wgu_st, wd_st, wgu_bf, wd_bf, gsem, ssem, wsem, *, bm, n_tok):
    s = pl.program_id(0)
    used = used_ref[0]
    dump0 = TOP_K * n_tok

    def weight_copies(e):
        return (pltpu.make_async_copy(wgu_hbm.at[e], wgu_st, wsem.at[0]),
                pltpu.make_async_copy(wd_hbm.at[e], wd_st, wsem.at[1]))

    tok_bits = (n_tok - 1).bit_length()

    def gather_start(j0, xbuf, sem, r):
        tok = jnp.bitwise_and(order_ref[j0 + r], (1 << tok_bits) - 1)
        pltpu.make_async_copy(h_ref.at[tok], xbuf.at[pl.ds(r * ROW_TILE, ROW_TILE)], sem).start()

    def scatter_start(j0, nv, ybuf, sem, r, odd):
        real = lax.shift_right_logical(order_ref[j0 + r], tok_bits)
        dump = dump0 + odd * bm + r
        valid = jnp.right_shift(r - nv, 31)
        dst = dump + jnp.bitwise_and(valid, real - dump)
        pltpu.make_async_copy(ybuf.at[pl.ds(r * ROW_TILE, ROW_TILE)], y_ref.at[dst], sem).start()

    def block_wait(buf, sem):
        pltpu.make_async_copy(buf, buf, sem).wait()

    def switch_weights(b):
        prev = blk_e_ref[jnp.maximum(b - 1, 0)]
        e = blk_e_ref[b]

        @pl.when((b == 0) | (e != prev))
        def _():
            for c in weight_copies(e):
                c.wait()
            wgu_bf[...] = wgu_st[...].astype(BF16)
            wd_bf[...] = wd_st[...].astype(BF16)

            @pl.when(nxt_e_ref[e] >= 0)
            def _():
                for c in weight_copies(nxt_e_ref[e]):
                    c.start()

    def block(b, x_in, y_out, g_next, x_next, gsem_next, s_prev, y_prev, ssem_prev, prev_odd, wait_y_free):
        gj0 = blk_j0_ref[g_next + 1]
        sj0 = blk_j0_ref[s_prev + 1]
        snv = blk_nv_ref[s_prev + 1]
        def issue(k, after=None):
            zero = 0
            if after is not None:
                bits = pltpu.bitcast(jnp.abs(after[0:SUBLANES, 0:LANES]), jnp.int32)
                zero = jnp.minimum(bits[0, 0], 0)
            ng = GATHER_BATCHES
            q, n = (k, ng) if k < ng else (k - ng, DMA_BATCHES - ng)
            for r in range(q * bm // n, (q + 1) * bm // n):
                if k < ng:
                    gather_start(gj0 + zero, x_next, gsem_next, r)
                else:
                    scatter_start(sj0 + zero, snv, y_prev, ssem_prev, r, prev_odd)

        xb[...] = _rows_from_tiles(x_in, bm).astype(BF16)
        e = blk_e_ref[b]
        cw = D_FF // 4
        prev = None
        for c in range(4):
            issue(2 * c, prev)
            gs, ls = slice(c * cw, (c + 1) * cw), slice(D_FF + c * cw, D_FF + (c + 1) * cw)
            g = jnp.dot(xb[...], wgu_bf[:, gs], preferred_element_type=F32) + bgu_ref[e][:, gs]
            issue(2 * c + 1, g)
            lin = jnp.dot(xb[...], wgu_bf[:, ls], preferred_element_type=F32) + bgu_ref[e][:, ls]
            prev = lin
            g = jnp.minimum(g, SWIGLU_LIMIT)
            lin = jnp.clip(lin, -SWIGLU_LIMIT, SWIGLU_LIMIT)
            act_s[:, gs] = (g * _sigmoid(SWIGLU_ALPHA * g) * (lin + 1.0)).astype(BF16)
        hw = D_MODEL // 2
        wait_y_free()
        for h in range(2):
            issue(8 + h, prev)
            y = jnp.dot(act_s[...], wd_bf[:, h * hw:(h + 1) * hw], preferred_element_type=F32)
            y = y + bd_ref[e][:, h * hw:(h + 1) * hw]
            prev = y
            for c in range(hw // LANES):
                y_out[pl.ds(h * (hw // LANES) + c, bm, stride=ROW_TILE), :] = y[:, c * LANES:(c + 1) * LANES]

    b0 = 2 * s
    b1 = b0 + 1

    @pl.when(b0 < used)
    def _():
        @pl.when(s == 0)
        def _():
            for c in weight_copies(blk_e_ref[0]):
                c.start()
            y1[...] = jnp.zeros(y1.shape, F32)

            def first(r, c):
                scatter_start(0, 0, y1, ssem.at[0], r, 0)
                gather_start(blk_j0_ref[1], x0, gsem.at[0], r)
                return c

            lax.fori_loop(0, bm, first, 0)
            block_wait(y1, ssem.at[0])

        def y0_free():
            @pl.when(s > 0)
            def _():
                block_wait(y0, ssem.at[0])

        def y1_free():
            block_wait(y1, ssem.at[1])

        switch_weights(b0)
        block_wait(x0, gsem.at[0])
        block(b0, x0, y0, b1, x1, gsem.at[1], b0 - 1, y1, ssem.at[1], 1, y0_free)

        switch_weights(b1)
        block_wait(x1, gsem.at[1])
        block(b1, x1, y1, b0 + 2, x0, gsem.at[0], b0, y0, ssem.at[0], 0, y1_free)

        @pl.when(b0 + 2 >= used)
        def _():
            def last(r, c):
                scatter_start(blk_j0_ref[b1 + 1], blk_nv_ref[b1 + 1], y1, ssem.at[1], r, 1)
                return c

            lax.fori_loop(0, bm, last, 0)
            block_wait(y0, ssem.at[0])
            block_wait(y1, ssem.at[1])
            block_wait(x0, gsem.at[0])


def _expert(order, blk_e, blk_j0, blk_nv, nxt_e, used, h2, w, bm):
    n_tok = h2.shape[0]
    nblk = blk_e.shape[0]
    f32buf = lambda shp: pltpu.VMEM(shp, F32)
    return pl.pallas_call(
        functools.partial(_expert_kernel, bm=bm, n_tok=n_tok),
        grid_spec=pltpu.PrefetchScalarGridSpec(
            num_scalar_prefetch=6,
            grid=(nblk // 2,),
            in_specs=[
                pl.BlockSpec(memory_space=pl.ANY),
                pl.BlockSpec(memory_space=pl.ANY),
                pl.BlockSpec((N_EXPERTS, 1, 2 * D_FF), lambda s, *_: (0, 0, 0)),
                pl.BlockSpec(memory_space=pl.ANY),
                pl.BlockSpec((N_EXPERTS, 1, D_MODEL), lambda s, *_: (0, 0, 0)),
            ],
            out_specs=pl.BlockSpec(memory_space=pl.ANY),
            scratch_shapes=[f32buf((bm * ROW_TILE, LANES)), f32buf((bm * ROW_TILE, LANES)),
                            f32buf((bm * ROW_TILE, LANES)), f32buf((bm * ROW_TILE, LANES)),
                            pltpu.VMEM((bm, D_MODEL), BF16), pltpu.VMEM((bm, D_FF), BF16),
                            f32buf((D_MODEL, 2 * D_FF)), f32buf((D_FF, D_MODEL)),
                            pltpu.VMEM((D_MODEL, 2 * D_FF), BF16), pltpu.VMEM((D_FF, D_MODEL), BF16),
                            pltpu.SemaphoreType.DMA((2,)), pltpu.SemaphoreType.DMA((2,)),
                            pltpu.SemaphoreType.DMA((2,))],
        ),
        out_shape=jax.ShapeDtypeStruct((TOP_K * n_tok + 2 * bm, ROW_TILE, LANES), F32),
        compiler_params=_cparams(("arbitrary",)),
        name="moe_expert",
    )(order, blk_e, blk_j0, blk_nv, nxt_e, used, h2, w["w_gu"], w["b_gu"], w["w_down"], w["b_down"])


def _combine_kernel(y0_ref, y1_ref, y2_ref, y3_ref, x1_ref, gates_ref, gt2_ref, l2g_ref, l2b_ref, o_ref, *, alpha):
    gates = gates_ref[...]
    tm = gates.shape[0]
    f = gates[:, 0:1] * _rows_from_tiles(y0_ref, tm)
    for k, y_ref in enumerate((y1_ref, y2_ref, y3_ref), start=1):
        f = f + gates[:, k:k + 1] * _rows_from_tiles(y_ref, tm)
    o_ref[...] = _layernorm(alpha * x1_ref[...] + gt2_ref[0] * f) * l2g_ref[...] + l2b_ref[...]


def _combine(y_rows, x1, gates, gt2, w, tm, tokens_per_batch, alpha, n_tok, tok0):
    N = x1.shape[0]
    per_b = tokens_per_batch // tm
    nt = N // tm
    assert n_tok % tm == 0 and tok0 % tm == 0
    slot = lambda k: pl.BlockSpec((tm * ROW_TILE, LANES), lambda i: ((k * n_tok + tok0) // tm + i, 0))
    return pl.pallas_call(
        functools.partial(_combine_kernel, alpha=alpha),
        grid=(nt,),
        in_specs=[slot(0), slot(1), slot(2), slot(3),
                  pl.BlockSpec((tm, D_MODEL), lambda i: (i, 0)),
                  pl.BlockSpec((tm, LANES), lambda i: (i, 0)),
                  pl.BlockSpec((1, 1, D_MODEL), lambda i: (i // per_b, 0, 0)),
                  pl.BlockSpec((1, D_MODEL), lambda i: (0, 0)),
                  pl.BlockSpec((1, D_MODEL), lambda i: (0, 0))],
        out_specs=pl.BlockSpec((tm, D_MODEL), lambda i: (i, 0)),
        out_shape=jax.ShapeDtypeStruct((N, D_MODEL), F32),
        compiler_params=_cparams(("arbitrary",)),
        name="moe_combine",
    )(y_rows, y_rows, y_rows, y_rows, x1, gates, gt2, w["ln2_g"], w["ln2_b"])


def _moe_experts(h2, idx, counts, w, bm):
    N = h2.shape[0]
    n_rows = N * TOP_K
    nblk = (n_rows + N_EXPERTS * (bm - 1)) // bm
    nblk += nblk % 2
    experts = jnp.arange(N_EXPERTS, dtype=jnp.int32)
    id_bits = (n_rows - 1).bit_length()
    tok_bits = (N - 1).bit_length()
    assert tok_bits + (TOP_K * N - 1).bit_length() <= 32
    keys = jnp.left_shift(idx.reshape(-1), id_bits) + jnp.arange(n_rows, dtype=jnp.int32)
    keys = jnp.concatenate([keys, jnp.full(((1 << id_bits) - n_rows,), jnp.iinfo(jnp.int32).max, jnp.int32)])
    flat = jnp.bitwise_and(jnp.sort(keys, stable=False)[:n_rows], (1 << id_bits) - 1).astype(jnp.uint32)
    tok = jnp.right_shift(flat, 2)
    row = jnp.bitwise_and(flat, TOP_K - 1) * N + tok
    order = lax.bitcast_convert_type(jnp.left_shift(row, tok_bits) | tok, jnp.int32)
    order = jnp.concatenate([order, jnp.zeros((bm,), jnp.int32)])
    nb_e = (counts + bm - 1) // bm
    blk_end = jnp.cumsum(nb_e)
    first_blk = blk_end - nb_e
    start_sorted = jnp.cumsum(counts) - counts
    used = blk_end[-1].astype(jnp.int32)
    b = jnp.arange(-1, nblk + 1, dtype=jnp.int32)
    bc = jnp.clip(b, 0, used - 1)
    e = jnp.minimum(jnp.sum(blk_end[None, :] <= bc[:, None], axis=1), N_EXPERTS - 1).astype(jnp.int32)
    pick = lambda table: jnp.sum(jnp.where(e[:, None] == experts, table, 0), axis=1)
    local = bc - pick(first_blk)
    blk_j0 = (pick(start_sorted) + local * bm).astype(jnp.int32)
    blk_nv = jnp.where((b >= 0) & (b < used), jnp.minimum(bm, pick(counts) - local * bm), 0).astype(jnp.int32)
    blk_e = e[1:nblk + 1]
    later = (experts[None, :] > experts[:, None]) & (counts[None, :] > 0)
    nxt = jnp.min(jnp.where(later, experts[None, :], N_EXPERTS), axis=1)
    nxt_e = jnp.where(nxt < N_EXPERTS, nxt, -1).astype(jnp.int32)
    y_rows = _expert(order, blk_e, blk_j0, blk_nv, nxt_e, used.reshape(1), h2, w, bm)
    return y_rows.reshape(-1, LANES)


def _rope_tables(pos):
    half = QK_ROPE // 2
    inv = ROPE_THETA ** (-jnp.arange(half, dtype=F32) / half)
    ang = pos.astype(F32)[:, None] * inv[None, :]
    cos, sin = jnp.cos(ang), jnp.sin(ang)
    cos32 = jnp.concatenate([cos, cos], axis=1)
    sin32 = jnp.concatenate([-sin, sin], axis=1)
    return jnp.tile(cos32, (1, LANES // QK_ROPE)), jnp.tile(sin32, (1, LANES // QK_ROPE))


def _swap_halves(w32):
    shp = w32.shape
    w = w32.reshape(shp[:-1] + (shp[-1] // QK_ROPE, 2, QK_ROPE // 2))
    return w[..., ::-1, :].reshape(shp)


def _prep_weights(l, w_in, conv_w, g_qa, w_qb, g_kva, w_kvb, g_out_conv, g_out_attn, w_out,
                  ln1_g, ln1_b, router_w, router_b, w_gu, b_gu, w_down, b_down, ln2_g, ln2_b):
    w = {}
    wi = w_in[l]
    k_r = wi[:, _O_KR:_O_KR + QK_ROPE]
    rep = LANES // QK_ROPE
    w["w_in"] = jnp.concatenate([wi[:, :_O_KR], jnp.tile(k_r, (1, rep)), jnp.tile(_swap_halves(k_r), (1, rep))],
                                axis=1).astype(BF16)
    w["conv_w"] = conv_w[l]
    w["g_qa"] = g_qa[l].reshape(1, Q_LORA)
    w["g_kva"] = g_kva[l].reshape(1, KV_LORA)
    w["g_oc"] = g_out_conv[l].reshape(1, CONV_WIDTH)
    w["g_oa"] = g_out_attn[l].reshape(1, ATTN_WIDTH)
    wq = w_qb[l].reshape(Q_LORA, N_HEADS, QK_NOPE + QK_ROPE)
    w["wq_nope"] = wq[:, :, :QK_NOPE].reshape(Q_LORA, N_HEADS * QK_NOPE).astype(BF16)
    wq_rope = wq[:, :, QK_NOPE:].reshape(Q_LORA, N_HEADS * QK_ROPE)
    w["wq_rope"] = wq_rope.astype(BF16)
    w["wq_rope_sw"] = _swap_halves(wq_rope).astype(BF16)
    w_uk = jnp.transpose(w_kvb[l][:, :, :QK_NOPE], (1, 2, 0))
    w_uv = jnp.transpose(w_kvb[l][:, :, QK_NOPE:], (1, 0, 2))
    zk = jnp.zeros((QK_NOPE, KV_LORA), F32)
    zv = jnp.zeros((KV_LORA, V_HEAD), F32)
    w["w_uk_pair"] = jnp.stack([
        jnp.concatenate([jnp.concatenate([w_uk[2 * p], zk], axis=1),
                         jnp.concatenate([zk, w_uk[2 * p + 1]], axis=1)], axis=0)
        for p in range(N_HEADS // 2)]).astype(BF16)
    w["w_uv_pair"] = jnp.stack([
        jnp.concatenate([jnp.concatenate([w_uv[2 * p], zv], axis=1),
                         jnp.concatenate([zv, w_uv[2 * p + 1]], axis=1)], axis=0)
        for p in range(N_HEADS // 2)]).astype(BF16)
    w["w_out"] = w_out[l].astype(BF16)
    w["ln1_g"] = ln1_g[l].reshape(1, D_MODEL)
    w["ln1_b"] = ln1_b[l].reshape(1, D_MODEL)
    w["ln2_g"] = ln2_g[l].reshape(1, D_MODEL)
    w["ln2_b"] = ln2_b[l].reshape(1, D_MODEL)
    w["router_w"] = jnp.pad(router_w[l], ((0, 0), (0, LANES - N_EXPERTS))).astype(BF16)
    w["router_b"] = jnp.concatenate([router_b[l], jnp.full((LANES - N_EXPERTS,), NEG, F32)]).reshape(1, LANES)
    w["w_gu"] = w_gu[l]
    w["b_gu"] = b_gu[l].reshape(N_EXPERTS, 1, 2 * D_FF)
    w["w_down"] = w_down[l]
    w["b_down"] = b_down[l].reshape(N_EXPERTS, 1, D_MODEL)
    return w


def _mixer(x, mod, conv_prev, past, pos0, h2_all, tok0, w, alpha, *, tm_in, tm_out, tq=128, tk=512):
    B, T, _ = x.shape
    sh1, sc1, gt1, sh2, sc2, gt2 = [mod[:, None, i * D_MODEL:(i + 1) * D_MODEL] for i in range(N_MOD)]
    cos_t, sin_t = _rope_tables(pos0 + jnp.arange(T, dtype=jnp.int32))
    a_n, qlat, qrope, kcat, ckv, krope, conv_new, *vt = _inproj(x, sc1, sh1, conv_prev, cos_t, sin_t, w, tm_in,
                                                                 with_vt=past is None)
    if past is None:
        b_n = _attn_prompt(qlat, qrope, kcat, vt[0], w, tq, tk)
    else:
        b_n = _attn_sample(qlat, qrope, kcat, past[0], past[1], w, tk)
    x1, h2_all, meta, gates, cnt = _outproj(x, a_n, b_n, gt1, sc2, sh2, h2_all, tok0, w, tm_out, alpha)
    N = B * T
    route = dict(x1=x1.reshape(N, D_MODEL), idx=meta.reshape(N, LANES)[:, :TOP_K], gates=gates.reshape(N, LANES),
                 counts=cnt[0, :N_EXPERTS].astype(jnp.int32), gt2=gt2, shape=(B, T))
    return h2_all, route, ckv, krope, conv_new


def kernel(x_prompt, x_sample, c_prompt, c_sample, cache_ckv, cache_krope, state_conv, w_ada, b_ada, w_in, conv_w, g_qa, w_qb, g_kva, w_kvb, g_out_conv, g_out_attn, w_out, ln1_g, ln1_b, router_w, router_b, w_gu, b_gu, w_down, b_down, ln2_g, ln2_b):
    depth = w_ada.shape[0]
    Bp, Tp, _ = x_prompt.shape
    Bs, Ts, _ = x_sample.shape
    past_len = cache_ckv.shape[2]
    assert Ts == CHUNK and past_len % 512 == 0 and Tp % 1024 == 0
    alpha = (2.0 * depth) ** 0.25
    xp, xs = x_prompt, x_sample
    outs = [[] for _ in range(6)]
    c_all = jnp.concatenate([c_prompt, c_sample, jnp.zeros((16 - Bp - Bs, D_MODEL), F32)], axis=0)
    for l in range(depth):
        w = _prep_weights(l, w_in, conv_w, g_qa, w_qb, g_kva, w_kvb, g_out_conv, g_out_attn, w_out,
                          ln1_g, ln1_b, router_w, router_b, w_gu, b_gu, w_down, b_down, ln2_g, ln2_b)
        mod = _ada(c_all, w_ada[l], b_ada[l])
        n_p, n_s = Bp * Tp, Bs * Ts
        n_tok = n_p + n_s
        h2_all = jnp.zeros((n_tok * ROW_TILE, LANES), F32)
        h2_all, rp, ckv_p, kr_p, cv_p = _mixer(xp, mod[:Bp], jnp.zeros((Bp, CONV_K - 1, CONV_WIDTH), F32), None, 0,
                                               h2_all, 0, w, alpha, tm_in=1024, tm_out=512)
        h2_all, rs, ckv_s, kr_s, cv_s = _mixer(xs, mod[Bp:Bp + Bs], state_conv[l], (cache_ckv[l], cache_krope[l]),
                                               past_len, h2_all, n_p, w, alpha, tm_in=Ts, tm_out=Ts)
        y_rows = _moe_experts(h2_all.reshape(n_tok, ROW_TILE, LANES), jnp.concatenate([rp["idx"], rs["idx"]]),
                              rp["counts"] + rs["counts"], w, bm=256)
        xp = _combine(y_rows, rp["x1"], rp["gates"], rp["gt2"], w, 256, Tp, alpha, n_tok, 0).reshape(rp["shape"] + (D_MODEL,))
        xs = _combine(y_rows, rs["x1"], rs["gates"], rs["gt2"], w, Ts, Ts, alpha, n_tok, n_p).reshape(rs["shape"] + (D_MODEL,))
        for o, v in zip(outs, (ckv_p, kr_p, cv_p, ckv_s, kr_s, cv_s)):
            o.append(v)
    return (xp, xs) + tuple(jnp.stack(o) for o in outs)
```

```python
import functools
import math

import jax
import jax.numpy as jnp
from jax import lax
from jax.experimental import pallas as pl
from jax.experimental.pallas import tpu as pltpu

F32 = jnp.float32
BF16 = jnp.bfloat16

D_MODEL = 1024
CHUNK = 64
CONV_WIDTH = 512
CONV_K = 3
N_HEADS = 8
QK_NOPE = 64
QK_ROPE = 32
V_HEAD = 64
Q_LORA = 256
KV_LORA = 128
ATTN_WIDTH = N_HEADS * V_HEAD
ROPE_THETA = 10000.0
ATTN_SCALE = 1.0 / math.sqrt(QK_NOPE + QK_ROPE)
Q_SCALE = ATTN_SCALE * math.log2(math.e)
N_EXPERTS = 32
TOP_K = 4
D_FF = 1024
SWIGLU_LIMIT = 7.0
SWIGLU_ALPHA = 1.702
N_MOD = 6
LN_EPS = 1e-5
RMS_EPS = 1e-6

LANES = 128
SUBLANES = 8
DMA_BATCHES = 10
NEG = -1e30
VMEM_LIMIT = 56 * 1024 * 1024

_O_XB, _O_XC, _O_XV = 0, CONV_WIDTH, 2 * CONV_WIDTH
_O_QA = 3 * CONV_WIDTH
_O_KVA = _O_QA + Q_LORA
_O_KR = _O_KVA + KV_LORA
_O_KRS = _O_KR + LANES
IN_COLS_EXT = _O_KRS + LANES


def _cparams(sem):
    return pltpu.CompilerParams(dimension_semantics=sem, vmem_limit_bytes=VMEM_LIMIT)


def _layernorm(x):
    mu = jnp.mean(x, axis=-1, keepdims=True)
    xc = x - mu
    var = jnp.mean(xc * xc, axis=-1, keepdims=True)
    return xc * lax.rsqrt(var + LN_EPS)


def _rms(x):
    return x * lax.rsqrt(jnp.mean(x * x, axis=-1, keepdims=True) + RMS_EPS)


def _sigmoid(x):
    return 1.0 / (1.0 + jnp.exp(-x))


ROW_TILE = D_MODEL // LANES


def _rows_from_tiles(ref, n):
    return jnp.concatenate([ref[pl.ds(c, n, stride=ROW_TILE), :] for c in range(ROW_TILE)], axis=1)


def _rows_to_tiles(ref, x):
    n = x.shape[0]
    for c in range(ROW_TILE):
        ref[pl.ds(c, n, stride=ROW_TILE), :] = x[:, c * LANES:(c + 1) * LANES]


def _ada_kernel(c_ref, w_ref, b_ref, o_ref):
    c = c_ref[...]
    s = (c * _sigmoid(c)).astype(BF16)
    o_ref[...] = jnp.dot(s, w_ref[...].astype(BF16), preferred_element_type=F32) + b_ref[...]


def _ada(c_all, w_ada, b_ada):
    rows = c_all.shape[0]
    ncol = w_ada.shape[1]
    tn = 1024
    return pl.pallas_call(
        _ada_kernel,
        grid=(ncol // tn,),
        in_specs=[pl.BlockSpec((rows, D_MODEL), lambda j: (0, 0)),
                  pl.BlockSpec((D_MODEL, tn), lambda j: (0, j)),
                  pl.BlockSpec((1, tn), lambda j: (0, j))],
        out_specs=pl.BlockSpec((rows, tn), lambda j: (0, j)),
        out_shape=jax.ShapeDtypeStruct((rows, ncol), F32),
        compiler_params=_cparams(("arbitrary",)),
        name="ada",
    )(c_all, w_ada, b_ada.reshape(1, ncol))


def _inproj_kernel(x_ref, sc_ref, sh_ref, win_ref, cw_ref, cprev_ref, gqa_ref, gkva_ref, goc_ref,
                   wqn_ref, wqr_ref, wqrs_ref, wuk_ref, cos_ref, sin_ref,
                   an_ref, qlat_ref, qrope_ref, kcat_ref, ckv_ref, krope_ref, cnew_ref,
                   *rest, tm, with_vt):
    vt_ref, ubuf = rest if with_vt else (None, rest[0])
    j = pl.program_id(1)
    x = x_ref[0]
    h = _layernorm(x) * (1.0 + sc_ref[0]) + sh_ref[0]
    proj = jnp.dot(h.astype(BF16), win_ref[...], preferred_element_type=F32)
    xb = proj[:, _O_XB:_O_XB + CONV_WIDTH]
    xc = proj[:, _O_XC:_O_XC + CONV_WIDTH]
    xv = proj[:, _O_XV:_O_XV + CONV_WIDTH]
    q_a = proj[:, _O_QA:_O_QA + Q_LORA]
    kv_a = proj[:, _O_KVA:_O_KVA + KV_LORA]
    kr4 = proj[:, _O_KR:_O_KR + LANES]
    kr4s = proj[:, _O_KRS:_O_KRS + LANES]

    u = xc * xv

    @pl.when(j == 0)
    def _():
        ubuf[6:8, :] = cprev_ref[0]

    ubuf[8:8 + tm, :] = u
    conv = (cw_ref[0:1, :] * ubuf[6:6 + tm, :] + cw_ref[1:2, :] * ubuf[7:7 + tm, :]
            + cw_ref[2:3, :] * u)
    ubuf[0:8, :] = ubuf[tm:tm + 8, :]
    cnew_ref[0] = u[tm - (CONV_K - 1):tm, :]
    an_ref[0] = (_rms(xb * conv) * goc_ref[...]).astype(BF16)

    cos = cos_ref[...]
    sin = sin_ref[...]

    ckv = _rms(kv_a) * gkva_ref[...]
    kro4 = kr4 * cos + kr4s * sin
    ckv_ref[0] = ckv
    krope_ref[0] = kro4[:, :QK_ROPE]
    kcat_ref[0] = jnp.concatenate([ckv, kro4], axis=1).astype(BF16)
    if with_vt:
        vt_ref[0] = ckv.T.astype(BF16)

    qn = (_rms(q_a) * gqa_ref[...]).astype(BF16)
    q_nope = jnp.dot(qn, wqn_ref[...], preferred_element_type=F32)
    xr = jnp.dot(qn, wqr_ref[...], preferred_element_type=F32)
    xrs = jnp.dot(qn, wqrs_ref[...], preferred_element_type=F32)
    for g in range(2):
        sl = slice(g * LANES, (g + 1) * LANES)
        qrope_ref[0, :, sl] = ((xr[:, sl] * cos + xrs[:, sl] * sin) * Q_SCALE).astype(BF16)
    for p in range(N_HEADS // 2):
        qp = q_nope[:, p * LANES:(p + 1) * LANES].astype(BF16)
        ql = jnp.dot(qp, wuk_ref[p], preferred_element_type=F32)
        qlat_ref[0, :, p * 2 * KV_LORA:(p + 1) * 2 * KV_LORA] = (ql * Q_SCALE).astype(BF16)


def _inproj(x, sc1, sh1, conv_prev, cos_t, sin_t, w, tm, with_vt):
    B, T, _ = x.shape
    nt = T // tm
    full = lambda shp: pl.BlockSpec(shp, lambda b, j: (0,) * len(shp))
    vt_shape = (jax.ShapeDtypeStruct((B, KV_LORA, T), BF16),) if with_vt else ()
    vt_spec = (pl.BlockSpec((1, KV_LORA, tm), lambda b, j: (b, 0, j)),) if with_vt else ()
    out_shapes = (
        jax.ShapeDtypeStruct((B, T, CONV_WIDTH), BF16),
        jax.ShapeDtypeStruct((B, T, N_HEADS * KV_LORA), BF16),
        jax.ShapeDtypeStruct((B, T, 2 * LANES), BF16),
        jax.ShapeDtypeStruct((B, T, 2 * LANES), BF16),
        jax.ShapeDtypeStruct((B, T, KV_LORA), F32),
        jax.ShapeDtypeStruct((B, T, QK_ROPE), F32),
        jax.ShapeDtypeStruct((B, CONV_K - 1, CONV_WIDTH), F32),
    ) + vt_shape
    tile = lambda c: pl.BlockSpec((1, tm, c), lambda b, j: (b, j, 0))
    return pl.pallas_call(
        functools.partial(_inproj_kernel, tm=tm, with_vt=with_vt),
        grid=(B, nt),
        in_specs=[
            tile(D_MODEL),
            pl.BlockSpec((1, 1, D_MODEL), lambda b, j: (b, 0, 0)),
            pl.BlockSpec((1, 1, D_MODEL), lambda b, j: (b, 0, 0)),
            full((D_MODEL, IN_COLS_EXT)),
            full((CONV_K, CONV_WIDTH)),
            pl.BlockSpec((1, CONV_K - 1, CONV_WIDTH), lambda b, j: (b, 0, 0)),
            full((1, Q_LORA)), full((1, KV_LORA)), full((1, CONV_WIDTH)),
            full((Q_LORA, N_HEADS * QK_NOPE)), full((Q_LORA, 2 * LANES)), full((Q_LORA, 2 * LANES)),
            full((N_HEADS // 2, LANES, 2 * KV_LORA)),
            pl.BlockSpec((tm, LANES), lambda b, j: (j, 0)),
            pl.BlockSpec((tm, LANES), lambda b, j: (j, 0)),
        ],
        out_specs=(tile(CONV_WIDTH), tile(N_HEADS * KV_LORA), tile(2 * LANES), tile(2 * LANES),
                   tile(KV_LORA), tile(QK_ROPE),
                   pl.BlockSpec((1, CONV_K - 1, CONV_WIDTH), lambda b, j: (b, 0, 0))) + vt_spec,
        out_shape=out_shapes,
        scratch_shapes=[pltpu.VMEM((tm + 8, CONV_WIDTH), F32)],
        compiler_params=_cparams(("arbitrary", "arbitrary")),
        name="inproj",
    )(x, sc1, sh1, w["w_in"], w["conv_w"], conv_prev, w["g_qa"], w["g_kva"], w["g_oc"],
      w["wq_nope"], w["wq_rope"], w["wq_rope_sw"], w["w_uk_pair"], cos_t, sin_t)


def _stack_queries(qlat_ref, qrope_ref, qs, tq):
    lane = lax.broadcasted_iota(jnp.int32, (tq, LANES), 1)
    for h in range(N_HEADS):
        g, i = divmod(h, 4)
        rope = qrope_ref[0, :, g * LANES:(g + 1) * LANES]
        keep = (lane >= i * QK_ROPE) & (lane < (i + 1) * QK_ROPE)
        qs[h * tq:(h + 1) * tq, 0:KV_LORA] = qlat_ref[0, :, h * KV_LORA:(h + 1) * KV_LORA]
        qs[h * tq:(h + 1) * tq, KV_LORA:KV_LORA + LANES] = jnp.where(keep, rope, jnp.zeros_like(rope))


def _softmax_step(qs, k, v, m_s, l_s, acc_s):
    tk = k.shape[0]
    s = lax.dot_general(qs[...], k, (((1,), (1,)), ((), ())), preferred_element_type=F32)
    m_prev = m_s[...]
    m_new = jnp.maximum(m_prev, jnp.max(s, axis=1, keepdims=True))
    alpha = jnp.exp2(m_prev - m_new)
    if tk % LANES == 0:
        p = jnp.exp2(s - jnp.tile(m_new, (1, tk // LANES)))
    else:
        p = jnp.exp2(s - m_new[:, :tk])
    l_s[...] = alpha * l_s[...] + jnp.sum(p, axis=1, keepdims=True)
    acc_s[...] = alpha * acc_s[...] + jnp.dot(p.astype(BF16), v, preferred_element_type=F32)
    m_s[...] = m_new


def _attn_epilogue(acc_s, l_s, wuv_ref, g_ref, o_ref, tq):
    o = acc_s[...] / l_s[...]
    parts = []
    for p in range(N_HEADS // 2):
        op = jnp.concatenate([o[(2 * p) * tq:(2 * p + 1) * tq], o[(2 * p + 1) * tq:(2 * p + 2) * tq]], axis=1)
        parts.append(jnp.dot(op.astype(BF16), wuv_ref[p], preferred_element_type=F32))
    b = jnp.concatenate(parts, axis=1)
    o_ref[0] = (_rms(b) * g_ref[...]).astype(BF16)


def _attn_prompt_kernel(qlat_ref, qrope_ref, k_ref, vt_ref, wuvt_ref, g_ref, o_ref, qs, m_s, l_s, acc_s, sa, sb,
                        *, tq, tk):
    i = pl.program_id(1)
    M = N_HEADS * tq
    _stack_queries(qlat_ref, qrope_ref, qs, tq)
    m_s[...] = jnp.full(m_s.shape, NEG, F32)
    l_s[...] = jnp.zeros(l_s.shape, F32)
    acc_s[...] = jnp.zeros(acc_s.shape, F32)
    q0 = i * tq
    n_full = (q0 + CHUNK) // tk

    def scores(t, dst):
        k = k_ref[0, pl.ds(pl.multiple_of(t * tk, tk), tk), :]
        dst[...] = lax.dot_general(k, qs[...], (((1,), (1,)), ((), ())), preferred_element_type=F32)

    def update(t, src, limit=None):
        start = pl.multiple_of(t * tk, tk)
        vt = vt_ref[0, :, pl.ds(start, tk)]
        s = src[...]
        if limit is not None:
            kpos = start + lax.broadcasted_iota(jnp.int32, (tk, 1), 0)
            s = jnp.where(kpos < limit, s, NEG)
        m_prev = m_s[...]
        m_new = jnp.maximum(m_prev, jnp.max(s, axis=0, keepdims=True))
        alpha = jnp.exp2(m_prev - m_new)
        p = jnp.exp2(s - m_new)
        l_s[...] = alpha * l_s[...] + jnp.sum(p, axis=0, keepdims=True)
        acc_s[...] = alpha * acc_s[...] + jnp.dot(vt, p.astype(BF16), preferred_element_type=F32)
        m_s[...] = m_new

    scores(0, sa)

    def body(j, carry):
        t = 2 * j
        scores(t + 1, sb)
        update(t, sa)
        scores(t + 2, sa)
        update(t + 1, sb)
        return carry

    lax.fori_loop(0, n_full // 2, body, 0)

    col_t = jnp.bitwise_and(lax.broadcasted_iota(jnp.int32, (1, M), 1), tq - 1)
    limit = q0 + (jnp.right_shift(col_t, CHUNK.bit_length() - 1) + 1) * CHUNK
    odd = n_full % 2

    @pl.when(odd == 0)
    def _():
        update(n_full, sa, limit)

    @pl.when(odd == 1)
    def _():
        scores(n_full, sb)
        update(n_full - 1, sa)
        update(n_full, sb, limit)

    o_t = acc_s[...] / l_s[...]
    parts = []
    for p in range(N_HEADS // 2):
        pair = jnp.concatenate([o_t[:, (2 * p) * tq:(2 * p + 1) * tq], o_t[:, (2 * p + 1) * tq:(2 * p + 2) * tq]],
                               axis=0)
        parts.append(jnp.dot(wuvt_ref[p], pair.astype(BF16), preferred_element_type=F32))
    b = jnp.concatenate(parts, axis=0).T
    o_ref[0] = (_rms(b) * g_ref[...]).astype(BF16)


def _attn_prompt(qlat, qrope, kcat, vt, w, tq, tk):
    B, T, _ = qlat.shape
    M = N_HEADS * tq
    return pl.pallas_call(
        functools.partial(_attn_prompt_kernel, tq=tq, tk=tk),
        grid=(B, T // tq),
        in_specs=[
            pl.BlockSpec((1, tq, N_HEADS * KV_LORA), lambda b, i: (b, i, 0)),
            pl.BlockSpec((1, tq, 2 * LANES), lambda b, i: (b, i, 0)),
            pl.BlockSpec((1, T, 2 * LANES), lambda b, i: (b, 0, 0)),
            pl.BlockSpec((1, KV_LORA, T), lambda b, i: (b, 0, 0)),
            pl.BlockSpec((N_HEADS // 2, LANES, 2 * KV_LORA), lambda b, i: (0, 0, 0)),
            pl.BlockSpec((1, ATTN_WIDTH), lambda b, i: (0, 0)),
        ],
        out_specs=pl.BlockSpec((1, tq, ATTN_WIDTH), lambda b, i: (b, i, 0)),
        out_shape=jax.ShapeDtypeStruct((B, T, ATTN_WIDTH), BF16),
        scratch_shapes=[pltpu.VMEM((M, 2 * LANES), BF16), pltpu.VMEM((1, M), F32),
                        pltpu.VMEM((1, M), F32), pltpu.VMEM((KV_LORA, M), F32),
                        pltpu.VMEM((tk, M), F32), pltpu.VMEM((tk, M), F32)],
        compiler_params=_cparams(("arbitrary", "arbitrary")),
        name="attn_prompt",
    )(qlat, qrope, kcat, vt, jnp.swapaxes(w["w_uv_pair"], 1, 2), w["g_oa"])


def _attn_sample_kernel(qlat_ref, qrope_ref, knew_ref, pckv_ref, pkr_ref, wuv_ref, g_ref, o_ref,
                        qs, m_s, l_s, acc_s, *, tq, tk, n_past):
    _stack_queries(qlat_ref, qrope_ref, qs, tq)
    m_s[...] = jnp.full(m_s.shape, NEG, F32)
    l_s[...] = jnp.zeros(l_s.shape, F32)
    acc_s[...] = jnp.zeros(acc_s.shape, F32)

    def body(t, carry):
        start = pl.multiple_of(t * tk, tk)
        ck = pckv_ref[0, pl.ds(start, tk), :]
        kr = pkr_ref[0, pl.ds(start, tk), :]
        k = jnp.concatenate([ck, kr, kr, kr, kr], axis=1).astype(BF16)
        _softmax_step(qs, k, k[:, :KV_LORA], m_s, l_s, acc_s)
        return carry

    lax.fori_loop(0, n_past // tk, body, 0)
    k = knew_ref[0]
    _softmax_step(qs, k, k[:, :KV_LORA], m_s, l_s, acc_s)
    _attn_epilogue(acc_s, l_s, wuv_ref, g_ref, o_ref, tq)


def _attn_sample(qlat, qrope, kcat, past_ckv, past_krope, w, tk):
    B, T, _ = qlat.shape
    n_past = past_ckv.shape[1]
    M = N_HEADS * T
    per_b = lambda r, c: pl.BlockSpec((1, r, c), lambda b: (b, 0, 0))
    return pl.pallas_call(
        functools.partial(_attn_sample_kernel, tq=T, tk=tk, n_past=n_past),
        grid=(B,),
        in_specs=[per_b(T, N_HEADS * KV_LORA), per_b(T, 2 * LANES), per_b(T, 2 * LANES),
                  per_b(n_past, KV_LORA), per_b(n_past, QK_ROPE),
                  pl.BlockSpec((N_HEADS // 2, 2 * KV_LORA, LANES), lambda b: (0, 0, 0)),
                  pl.BlockSpec((1, ATTN_WIDTH), lambda b: (0, 0))],
        out_specs=per_b(T, ATTN_WIDTH),
        out_shape=jax.ShapeDtypeStruct((B, T, ATTN_WIDTH), BF16),
        scratch_shapes=[pltpu.VMEM((M, 2 * LANES), BF16), pltpu.VMEM((M, LANES), F32),
                        pltpu.VMEM((M, LANES), F32), pltpu.VMEM((M, KV_LORA), F32)],
        compiler_params=_cparams(("arbitrary",)),
        name="attn_sample",
    )(qlat, qrope, kcat, past_ckv, past_krope, w["w_uv_pair"], w["g_oa"])


def _outproj_kernel(x_ref, an_ref, bn_ref, gt1_ref, sc2_ref, sh2_ref, wo_ref, l1g_ref, l1b_ref,
                    rw_ref, rb_ref, h2_all_ref, x1_ref, h2_ref, meta_ref, gates_ref, cnt_ref, run_s,
                    *, tm, alpha):
    del h2_all_ref
    first = (pl.program_id(0) == 0) & (pl.program_id(1) == 0)

    @pl.when(first)
    def _():
        run_s[...] = jnp.zeros(run_s.shape, F32)

    m = (jnp.dot(an_ref[0], wo_ref[0:CONV_WIDTH, :], preferred_element_type=F32)
         + jnp.dot(bn_ref[0], wo_ref[CONV_WIDTH:, :], preferred_element_type=F32))
    x1 = _layernorm(alpha * x_ref[0] + gt1_ref[0] * m) * l1g_ref[...] + l1b_ref[...]
    x1_ref[0] = x1
    h2 = _layernorm(x1) * (1.0 + sc2_ref[0]) + sh2_ref[0]
    _rows_to_tiles(h2_ref, h2)

    logits = jnp.dot(h2.astype(BF16), rw_ref[...], preferred_element_type=F32) + rb_ref[...]
    lane = lax.broadcasted_iota(jnp.int32, (tm, LANES), 1)
    lane_f = lane.astype(F32)
    lg = logits
    vals, sels = [], []
    chosen = jnp.zeros((tm, LANES), F32)
    for _ in range(TOP_K):
        mx = jnp.max(lg, axis=1, keepdims=True)
        idx = jnp.min(jnp.where(lg == mx, lane_f, float(LANES)), axis=1, keepdims=True)
        sel = lane_f == idx
        vals.append(mx)
        sels.append(idx)
        chosen = jnp.where(sel, 1.0, chosen)
        lg = jnp.where(sel, NEG, lg)

    es = [jnp.exp(v - vals[0]) for v in vals]
    denom = es[0] + es[1] + es[2] + es[3]

    run_s[0:1, :] = run_s[0:1, :] + jnp.sum(chosen, axis=0, keepdims=True)
    cnt_ref[...] = jnp.broadcast_to(run_s[0:1, :], cnt_ref.shape)

    meta = jnp.zeros((tm, LANES), jnp.int32)
    gates = jnp.zeros((tm, LANES), F32)
    for k in range(TOP_K):
        meta = jnp.where(lane == k, sels[k].astype(jnp.int32), meta)
        gates = jnp.where(lane == k, es[k] / denom, gates)
    meta_ref[0] = meta
    gates_ref[0] = gates


def _outproj(x, a_n, b_n, gt1, sc2, sh2, h2_all, tok0, w, tm, alpha):
    B, T, _ = x.shape
    blk0 = tok0 // tm
    tile = lambda c: pl.BlockSpec((1, tm, c), lambda b, j: (b, j, 0))
    modv = pl.BlockSpec((1, 1, D_MODEL), lambda b, j: (b, 0, 0))
    full = lambda shp: pl.BlockSpec(shp, lambda b, j: (0,) * len(shp))
    return pl.pallas_call(
        functools.partial(_outproj_kernel, tm=tm, alpha=alpha),
        grid=(B, T // tm),
        in_specs=[tile(D_MODEL), tile(CONV_WIDTH), tile(ATTN_WIDTH), modv, modv, modv,
                  full((CONV_WIDTH + ATTN_WIDTH, D_MODEL)), full((1, D_MODEL)), full((1, D_MODEL)),
                  full((D_MODEL, LANES)), full((1, LANES)), pl.BlockSpec(memory_space=pl.ANY)],
        out_specs=(tile(D_MODEL),
                   pl.BlockSpec((tm * ROW_TILE, LANES), lambda b, j: (blk0 + b * (T // tm) + j, 0)),
                   tile(LANES), tile(LANES), pl.BlockSpec((8, LANES), lambda b, j: (0, 0))),
        out_shape=(jax.ShapeDtypeStruct((B, T, D_MODEL), F32),
                   jax.ShapeDtypeStruct(h2_all.shape, F32),
                   jax.ShapeDtypeStruct((B, T, LANES), jnp.int32), jax.ShapeDtypeStruct((B, T, LANES), F32),
                   jax.ShapeDtypeStruct((8, LANES), F32)),
        scratch_shapes=[pltpu.VMEM((8, LANES), F32)],
        input_output_aliases={11: 1},
        compiler_params=_cparams(("arbitrary", "arbitrary")),
        name="outproj",
    )(x, a_n, b_n, gt1, sc2, sh2, w["w_out"], w["ln1_g"], w["ln1_b"], w["router_w"], w["router_b"], h2_all)


def _expert_kernel(order_ref, blk_e_ref, blk_j0_ref, blk_nv_ref, nxt_e_ref, used_ref,
                   h_ref, wgu_hbm, bgu_ref, wd_hbm, bd_ref, y_ref,
                   x0, x1, y0, y1, xb, act_s, wgu_st, wd_st, wgu_bf, wd_bf, gsem, ssem, wsem, *, bm, n_tok):
    s = pl.program_id(0)
    used = used_ref[0]
    dump0 = TOP_K * n_tok

    def weight_copies(e):
        return (pltpu.make_async_copy(wgu_hbm.at[e], wgu_st, wsem.at[0]),
                pltpu.make_async_copy(wd_hbm.at[e], wd_st, wsem.at[1]))

    tok_bits = (n_tok - 1).bit_length()

    def gather_start(j0, xbuf, sem, r):
        tok = jnp.bitwise_and(order_ref[j0 + r], (1 << tok_bits) - 1)
        pltpu.make_async_copy(h_ref.at[tok], xbuf.at[pl.ds(r * ROW_TILE, ROW_TILE)], sem).start()

    def scatter_start(j0, nv, ybuf, sem, r, odd):
        real = lax.shift_right_logical(order_ref[j0 + r], tok_bits)
        dump = dump0 + odd * bm + r
        valid = jnp.right_shift(r - nv, 31)
        dst = dump + jnp.bitwise_and(valid, real - dump)
        pltpu.make_async_copy(ybuf.at[pl.ds(r * ROW_TILE, ROW_TILE)], y_ref.at[dst], sem).start()

    def block_wait(buf, sem):
        pltpu.make_async_copy(buf, buf, sem).wait()

    def switch_weights(b):
        prev = blk_e_ref[jnp.maximum(b - 1, 0)]
        e = blk_e_ref[b]

        @pl.when((b == 0) | (e != prev))
        def _():
            for c in weight_copies(e):
                c.wait()
            wgu_bf[...] = wgu_st[...].astype(BF16)
            wd_bf[...] = wd_st[...].astype(BF16)

            @pl.when(nxt_e_ref[e] >= 0)
            def _():
                for c in weight_copies(nxt_e_ref[e]):
                    c.start()

    def block(b, x_in, y_out, g_next, x_next, gsem_next, s_prev, y_prev, ssem_prev, prev_odd, wait_y_free):
        gj0 = blk_j0_ref[g_next + 1]
        sj0 = blk_j0_ref[s_prev + 1]
        snv = blk_nv_ref[s_prev + 1]
        def issue(k, after=None):
            zero = 0
            if after is not None:
                bits = pltpu.bitcast(jnp.abs(after[0:SUBLANES, 0:LANES]), jnp.int32)
                zero = jnp.minimum(bits[0, 0], 0)
            half = DMA_BATCHES // 2
            q = k % half
            for r in range(q * bm // half, (q + 1) * bm // half):
                if k < half:
                    gather_start(gj0 + zero, x_next, gsem_next, r)
                else:
                    scatter_start(sj0 + zero, snv, y_prev, ssem_prev, r, prev_odd)

        xb[...] = _rows_from_tiles(x_in, bm).astype(BF16)
        e = blk_e_ref[b]
        cw = D_FF // 4
        prev = None
        for c in range(4):
            issue(2 * c, prev)
            gs, ls = slice(c * cw, (c + 1) * cw), slice(D_FF + c * cw, D_FF + (c + 1) * cw)
            g = jnp.dot(xb[...], wgu_bf[:, gs], preferred_element_type=F32) + bgu_ref[e][:, gs]
            issue(2 * c + 1, g)
            lin = jnp.dot(xb[...], wgu_bf[:, ls], preferred_element_type=F32) + bgu_ref[e][:, ls]
            prev = lin
            g = jnp.minimum(g, SWIGLU_LIMIT)
            lin = jnp.clip(lin, -SWIGLU_LIMIT, SWIGLU_LIMIT)
            act_s[:, gs] = (g * _sigmoid(SWIGLU_ALPHA * g) * (lin + 1.0)).astype(BF16)
        hw = D_MODEL // 2
        wait_y_free()
        for h in range(2):
            issue(8 + h, prev)
            y = jnp.dot(act_s[...], wd_bf[:, h * hw:(h + 1) * hw], preferred_element_type=F32)
            y = y + bd_ref[e][:, h * hw:(h + 1) * hw]
            prev = y
            for c in range(hw // LANES):
                y_out[pl.ds(h * (hw // LANES) + c, bm, stride=ROW_TILE), :] = y[:, c * LANES:(c + 1) * LANES]

    b0 = 2 * s
    b1 = b0 + 1

    @pl.when(b0 < used)
    def _():
        @pl.when(s == 0)
        def _():
            for c in weight_copies(blk_e_ref[0]):
                c.start()
            y1[...] = jnp.zeros(y1.shape, F32)

            def first(r, c):
                scatter_start(0, 0, y1, ssem.at[0], r, 0)
                gather_start(blk_j0_ref[1], x0, gsem.at[0], r)
                return c

            lax.fori_loop(0, bm, first, 0)
            block_wait(y1, ssem.at[0])

        def y0_free():
            @pl.when(s > 0)
            def _():
                block_wait(y0, ssem.at[0])

        def y1_free():
            block_wait(y1, ssem.at[1])

        switch_weights(b0)
        block_wait(x0, gsem.at[0])
        block(b0, x0, y0, b1, x1, gsem.at[1], b0 - 1, y1, ssem.at[1], 1, y0_free)

        switch_weights(b1)
        block_wait(x1, gsem.at[1])
        block(b1, x1, y1, b0 + 2, x0, gsem.at[0], b0, y0, ssem.at[0], 0, y1_free)

        @pl.when(b0 + 2 >= used)
        def _():
            def last(r, c):
                scatter_start(blk_j0_ref[b1 + 1], blk_nv_ref[b1 + 1], y1, ssem.at[1], r, 1)
                return c

            lax.fori_loop(0, bm, last, 0)
            block_wait(y0, ssem.at[0])
            block_wait(y1, ssem.at[1])
            block_wait(x0, gsem.at[0])


def _expert(order, blk_e, blk_j0, blk_nv, nxt_e, used, h2, w, bm):
    n_tok = h2.shape[0]
    nblk = blk_e.shape[0]
    f32buf = lambda shp: pltpu.VMEM(shp, F32)
    return pl.pallas_call(
        functools.partial(_expert_kernel, bm=bm, n_tok=n_tok),
        grid_spec=pltpu.PrefetchScalarGridSpec(
            num_scalar_prefetch=6,
            grid=(nblk // 2,),
            in_specs=[
                pl.BlockSpec(memory_space=pl.ANY),
                pl.BlockSpec(memory_space=pl.ANY),
                pl.BlockSpec((N_EXPERTS, 1, 2 * D_FF), lambda s, *_: (0, 0, 0)),
                pl.BlockSpec(memory_space=pl.ANY),
                pl.BlockSpec((N_EXPERTS, 1, D_MODEL), lambda s, *_: (0, 0, 0)),
            ],
            out_specs=pl.BlockSpec(memory_space=pl.ANY),
            scratch_shapes=[f32buf((bm * ROW_TILE, LANES)), f32buf((bm * ROW_TILE, LANES)),
                            f32buf((bm * ROW_TILE, LANES)), f32buf((bm * ROW_TILE, LANES)),
                            pltpu.VMEM((bm, D_MODEL), BF16), pltpu.VMEM((bm, D_FF), BF16),
                            f32buf((D_MODEL, 2 * D_FF)), f32buf((D_FF, D_MODEL)),
                            pltpu.VMEM((D_MODEL, 2 * D_FF), BF16), pltpu.VMEM((D_FF, D_MODEL), BF16),
                            pltpu.SemaphoreType.DMA((2,)), pltpu.SemaphoreType.DMA((2,)),
                            pltpu.SemaphoreType.DMA((2,))],
        ),
        out_shape=jax.ShapeDtypeStruct((TOP_K * n_tok + 2 * bm, ROW_TILE, LANES), F32),
        compiler_params=_cparams(("arbitrary",)),
        name="moe_expert",
    )(order, blk_e, blk_j0, blk_nv, nxt_e, used, h2, w["w_gu"], w["b_gu"], w["w_down"], w["b_down"])


def _combine_kernel(y0_ref, y1_ref, y2_ref, y3_ref, x1_ref, gates_ref, gt2_ref, l2g_ref, l2b_ref, o_ref, *, alpha):
    gates = gates_ref[...]
    tm = gates.shape[0]
    f = gates[:, 0:1] * _rows_from_tiles(y0_ref, tm)
    for k, y_ref in enumerate((y1_ref, y2_ref, y3_ref), start=1):
        f = f + gates[:, k:k + 1] * _rows_from_tiles(y_ref, tm)
    o_ref[...] = _layernorm(alpha * x1_ref[...] + gt2_ref[0] * f) * l2g_ref[...] + l2b_ref[...]


def _combine(y_rows, x1, gates, gt2, w, tm, tokens_per_batch, alpha, n_tok, tok0):
    N = x1.shape[0]
    per_b = tokens_per_batch // tm
    nt = N // tm
    assert n_tok % tm == 0 and tok0 % tm == 0
    slot = lambda k: pl.BlockSpec((tm * ROW_TILE, LANES), lambda i: ((k * n_tok + tok0) // tm + i, 0))
    return pl.pallas_call(
        functools.partial(_combine_kernel, alpha=alpha),
        grid=(nt,),
        in_specs=[slot(0), slot(1), slot(2), slot(3),
                  pl.BlockSpec((tm, D_MODEL), lambda i: (i, 0)),
                  pl.BlockSpec((tm, LANES), lambda i: (i, 0)),
                  pl.BlockSpec((1, 1, D_MODEL), lambda i: (i // per_b, 0, 0)),
                  pl.BlockSpec((1, D_MODEL), lambda i: (0, 0)),
                  pl.BlockSpec((1, D_MODEL), lambda i: (0, 0))],
        out_specs=pl.BlockSpec((tm, D_MODEL), lambda i: (i, 0)),
        out_shape=jax.ShapeDtypeStruct((N, D_MODEL), F32),
        compiler_params=_cparams(("arbitrary",)),
        name="moe_combine",
    )(y_rows, y_rows, y_rows, y_rows, x1, gates, gt2, w["ln2_g"], w["ln2_b"])


def _moe_experts(h2, idx, counts, w, bm):
    N = h2.shape[0]
    n_rows = N * TOP_K
    nblk = (n_rows + N_EXPERTS * (bm - 1)) // bm
    nblk += nblk % 2
    experts = jnp.arange(N_EXPERTS, dtype=jnp.int32)
    id_bits = (n_rows - 1).bit_length()
    tok_bits = (N - 1).bit_length()
    assert tok_bits + (TOP_K * N - 1).bit_length() <= 32
    keys = jnp.left_shift(idx.reshape(-1), id_bits) + jnp.arange(n_rows, dtype=jnp.int32)
    keys = jnp.concatenate([keys, jnp.full(((1 << id_bits) - n_rows,), jnp.iinfo(jnp.int32).max, jnp.int32)])
    flat = jnp.bitwise_and(jnp.sort(keys, stable=False)[:n_rows], (1 << id_bits) - 1).astype(jnp.uint32)
    tok = jnp.right_shift(flat, 2)
    row = jnp.bitwise_and(flat, TOP_K - 1) * N + tok
    order = lax.bitcast_convert_type(jnp.left_shift(row, tok_bits) | tok, jnp.int32)
    order = jnp.concatenate([order, jnp.zeros((bm,), jnp.int32)])
    nb_e = (counts + bm - 1) // bm
    blk_end = jnp.cumsum(nb_e)
    first_blk = blk_end - nb_e
    start_sorted = jnp.cumsum(counts) - counts
    used = blk_end[-1].astype(jnp.int32)
    b = jnp.arange(-1, nblk + 1, dtype=jnp.int32)
    bc = jnp.clip(b, 0, used - 1)
    e = jnp.minimum(jnp.sum(blk_end[None, :] <= bc[:, None], axis=1), N_EXPERTS - 1).astype(jnp.int32)
    pick = lambda table: jnp.sum(jnp.where(e[:, None] == experts, table, 0), axis=1)
    local = bc - pick(first_blk)
    blk_j0 = (pick(start_sorted) + local * bm).astype(jnp.int32)
    blk_nv = jnp.where((b >= 0) & (b < used), jnp.minimum(bm, pick(counts) - local * bm), 0).astype(jnp.int32)
    blk_e = e[1:nblk + 1]
    later = (experts[None, :] > experts[:, None]) & (counts[None, :] > 0)
    nxt = jnp.min(jnp.where(later, experts[None, :], N_EXPERTS), axis=1)
    nxt_e = jnp.where(nxt < N_EXPERTS, nxt, -1).astype(jnp.int32)
    y_rows = _expert(order, blk_e, blk_j0, blk_nv, nxt_e, used.reshape(1), h2, w, bm)
    return y_rows.reshape(-1, LANES)


def _rope_tables(pos):
    half = QK_ROPE // 2
    inv = ROPE_THETA ** (-jnp.arange(half, dtype=F32) / half)
    ang = pos.astype(F32)[:, None] * inv[None, :]
    cos, sin = jnp.cos(ang), jnp.sin(ang)
    cos32 = jnp.concatenate([cos, cos], axis=1)
    sin32 = jnp.concatenate([-sin, sin], axis=1)
    return jnp.tile(cos32, (1, LANES // QK_ROPE)), jnp.tile(sin32, (1, LANES // QK_ROPE))


def _swap_halves(w32):
    shp = w32.shape
    w = w32.reshape(shp[:-1] + (shp[-1] // QK_ROPE, 2, QK_ROPE // 2))
    return w[..., ::-1, :].reshape(shp)


def _prep_weights(l, w_in, conv_w, g_qa, w_qb, g_kva, w_kvb, g_out_conv, g_out_attn, w_out,
                  ln1_g, ln1_b, router_w, router_b, w_gu, b_gu, w_down, b_down, ln2_g, ln2_b):
    w = {}
    wi = w_in[l]
    k_r = wi[:, _O_KR:_O_KR + QK_ROPE]
    rep = LANES // QK_ROPE
    w["w_in"] = jnp.concatenate([wi[:, :_O_KR], jnp.tile(k_r, (1, rep)), jnp.tile(_swap_halves(k_r), (1, rep))],
                                axis=1).astype(BF16)
    w["conv_w"] = conv_w[l]
    w["g_qa"] = g_qa[l].reshape(1, Q_LORA)
    w["g_kva"] = g_kva[l].reshape(1, KV_LORA)
    w["g_oc"] = g_out_conv[l].reshape(1, CONV_WIDTH)
    w["g_oa"] = g_out_attn[l].reshape(1, ATTN_WIDTH)
    wq = w_qb[l].reshape(Q_LORA, N_HEADS, QK_NOPE + QK_ROPE)
    w["wq_nope"] = wq[:, :, :QK_NOPE].reshape(Q_LORA, N_HEADS * QK_NOPE).astype(BF16)
    wq_rope = wq[:, :, QK_NOPE:].reshape(Q_LORA, N_HEADS * QK_ROPE)
    w["wq_rope"] = wq_rope.astype(BF16)
    w["wq_rope_sw"] = _swap_halves(wq_rope).astype(BF16)
    w_uk = jnp.transpose(w_kvb[l][:, :, :QK_NOPE], (1, 2, 0))
    w_uv = jnp.transpose(w_kvb[l][:, :, QK_NOPE:], (1, 0, 2))
    zk = jnp.zeros((QK_NOPE, KV_LORA), F32)
    zv = jnp.zeros((KV_LORA, V_HEAD), F32)
    w["w_uk_pair"] = jnp.stack([
        jnp.concatenate([jnp.concatenate([w_uk[2 * p], zk], axis=1),
                         jnp.concatenate([zk, w_uk[2 * p + 1]], axis=1)], axis=0)
        for p in range(N_HEADS // 2)]).astype(BF16)
    w["w_uv_pair"] = jnp.stack([
        jnp.concatenate([jnp.concatenate([w_uv[2 * p], zv], axis=1),
                         jnp.concatenate([zv, w_uv[2 * p + 1]], axis=1)], axis=0)
        for p in range(N_HEADS // 2)]).astype(BF16)
    w["w_out"] = w_out[l].astype(BF16)
    w["ln1_g"] = ln1_g[l].reshape(1, D_MODEL)
    w["ln1_b"] = ln1_b[l].reshape(1, D_MODEL)
    w["ln2_g"] = ln2_g[l].reshape(1, D_MODEL)
    w["ln2_b"] = ln2_b[l].reshape(1, D_MODEL)
    w["router_w"] = jnp.pad(router_w[l], ((0, 0), (0, LANES - N_EXPERTS))).astype(BF16)
    w["router_b"] = jnp.concatenate([router_b[l], jnp.full((LANES - N_EXPERTS,), NEG, F32)]).reshape(1, LANES)
    w["w_gu"] = w_gu[l]
    w["b_gu"] = b_gu[l].reshape(N_EXPERTS, 1, 2 * D_FF)
    w["w_down"] = w_down[l]
    w["b_down"] = b_down[l].reshape(N_EXPERTS, 1, D_MODEL)
    return w


def _mixer(x, mod, conv_prev, past, pos0, h2_all, tok0, w, alpha, *, tm_in, tm_out, tq=128, tk=512):
    B, T, _ = x.shape
    sh1, sc1, gt1, sh2, sc2, gt2 = [mod[:, None, i * D_MODEL:(i + 1) * D_MODEL] for i in range(N_MOD)]
    cos_t, sin_t = _rope_tables(pos0 + jnp.arange(T, dtype=jnp.int32))
    a_n, qlat, qrope, kcat, ckv, krope, conv_new, *vt = _inproj(x, sc1, sh1, conv_prev, cos_t, sin_t, w, tm_in,
                                                                 with_vt=past is None)
    if past is None:
        b_n = _attn_prompt(qlat, qrope, kcat, vt[0], w, tq, tk)
    else:
        b_n = _attn_sample(qlat, qrope, kcat, past[0], past[1], w, tk)
    x1, h2_all, meta, gates, cnt = _outproj(x, a_n, b_n, gt1, sc2, sh2, h2_all, tok0, w, tm_out, alpha)
    N = B * T
    route = dict(x1=x1.reshape(N, D_MODEL), idx=meta.reshape(N, LANES)[:, :TOP_K], gates=gates.reshape(N, LANES),
                 counts=cnt[0, :N_EXPERTS].astype(jnp.int32), gt2=gt2, shape=(B, T))
    return h2_all, route, ckv, krope, conv_new


def kernel(x_prompt, x_sample, c_prompt, c_sample, cache_ckv, cache_krope, state_conv, w_ada, b_ada, w_in, conv_w, g_qa, w_qb, g_kva, w_kvb, g_out_conv, g_out_attn, w_out, ln1_g, ln1_b, router_w, router_b, w_gu, b_gu, w_down, b_down, ln2_g, ln2_b):
    depth = w_ada.shape[0]
    Bp, Tp, _ = x_prompt.shape
    Bs, Ts, _ = x_sample.shape
    past_len = cache_ckv.shape[2]
    assert Ts == CHUNK and past_len % 512 == 0 and Tp % 1024 == 0
    alpha = (2.0 * depth) ** 0.25
    xp, xs = x_prompt, x_sample
    outs = [[] for _ in range(6)]
    c_all = jnp.concatenate([c_prompt, c_sample, jnp.zeros((16 - Bp - Bs, D_MODEL), F32)], axis=0)
    for l in range(depth):
        w = _prep_weights(l, w_in, conv_w, g_qa, w_qb, g_kva, w_kvb, g_out_conv, g_out_attn, w_out,
                          ln1_g, ln1_b, router_w, router_b, w_gu, b_gu, w_down, b_down, ln2_g, ln2_b)
        mod = _ada(c_all, w_ada[l], b_ada[l])
        n_p, n_s = Bp * Tp, Bs * Ts
        n_tok = n_p + n_s
        h2_all = jnp.zeros((n_tok * ROW_TILE, LANES), F32)
        h2_all, rp, ckv_p, kr_p, cv_p = _mixer(xp, mod[:Bp], jnp.zeros((Bp, CONV_K - 1, CONV_WIDTH), F32), None, 0,
                                               h2_all, 0, w, alpha, tm_in=1024, tm_out=512)
        h2_all, rs, ckv_s, kr_s, cv_s = _mixer(xs, mod[Bp:Bp + Bs], state_conv[l], (cache_ckv[l], cache_krope[l]),
                                               past_len, h2_all, n_p, w, alpha, tm_in=Ts, tm_out=Ts)
        y_rows = _moe_experts(h2_all.reshape(n_tok, ROW_TILE, LANES), jnp.concatenate([rp["idx"], rs["idx"]]),
                              rp["counts"] + rs["counts"], w, bm=256)
        xp = _combine(y_rows, rp["x1"], rp["gates"], rp["gt2"], w, 512, Tp, alpha, n_tok, 0).reshape(rp["shape"] + (D_MODEL,))
        xs = _combine(y_rows, rs["x1"], rs["gates"], rs["gt2"], w, Ts, Ts, alpha, n_tok, n_p).reshape(rs["shape"] + (D_MODEL,))
        for o, v in zip(outs, (ckv_p, kr_p, cv_p, ckv_s, kr_s, cv_s)):
            o.append(v)
    return (xp, xs) + tuple(jnp.stack(o) for o in outs)
```

```python
import functools
import math

import jax
import jax.numpy as jnp
from jax import lax
from jax.experimental import pallas as pl
from jax.experimental.pallas import tpu as pltpu

F32 = jnp.float32
BF16 = jnp.bfloat16

D_MODEL = 1024
CHUNK = 64
CONV_WIDTH = 512
CONV_K = 3
N_HEADS = 8
QK_NOPE = 64
QK_ROPE = 32
V_HEAD = 64
Q_LORA = 256
KV_LORA = 128
ATTN_WIDTH = N_HEADS * V_HEAD
ROPE_THETA = 10000.0
ATTN_SCALE = 1.0 / math.sqrt(QK_NOPE + QK_ROPE)
Q_SCALE = ATTN_SCALE * math.log2(math.e)
N_EXPERTS = 32
TOP_K = 4
D_FF = 1024
SWIGLU_LIMIT = 7.0
SWIGLU_ALPHA = 1.702
N_MOD = 6
LN_EPS = 1e-5
RMS_EPS = 1e-6

LANES = 128
SUBLANES = 8
DMA_BATCHES = 10
NEG = -1e30
VMEM_LIMIT = 56 * 1024 * 1024

_O_XB, _O_XC, _O_XV = 0, CONV_WIDTH, 2 * CONV_WIDTH
_O_QA = 3 * CONV_WIDTH
_O_KVA = _O_QA + Q_LORA
_O_KR = _O_KVA + KV_LORA
_O_KRS = _O_KR + LANES
IN_COLS_EXT = _O_KRS + LANES


def _cparams(sem):
    return pltpu.CompilerParams(dimension_semantics=sem, vmem_limit_bytes=VMEM_LIMIT)


def _layernorm(x):
    mu = jnp.mean(x, axis=-1, keepdims=True)
    xc = x - mu
    var = jnp.mean(xc * xc, axis=-1, keepdims=True)
    return xc * lax.rsqrt(var + LN_EPS)


def _rms(x):
    return x * lax.rsqrt(jnp.mean(x * x, axis=-1, keepdims=True) + RMS_EPS)


def _sigmoid(x):
    return 1.0 / (1.0 + jnp.exp(-x))


ROW_TILE = D_MODEL // LANES


def _rows_from_tiles(ref, n):
    return jnp.concatenate([ref[pl.ds(c, n, stride=ROW_TILE), :] for c in range(ROW_TILE)], axis=1)


def _rows_to_tiles(ref, x, row0=0):
    n = x.shape[0]
    for c in range(ROW_TILE):
        ref[pl.ds(row0 * ROW_TILE + c, n, stride=ROW_TILE), :] = x[:, c * LANES:(c + 1) * LANES]


def _ada_kernel(c_ref, w_ref, b_ref, o_ref):
    c = c_ref[...]
    s = (c * _sigmoid(c)).astype(BF16)
    o_ref[...] = jnp.dot(s, w_ref[...].astype(BF16), preferred_element_type=F32) + b_ref[...]


def _ada(c_all, w_ada, b_ada):
    rows = c_all.shape[0]
    ncol = w_ada.shape[1]
    tn = 1024
    return pl.pallas_call(
        _ada_kernel,
        grid=(ncol // tn,),
        in_specs=[pl.BlockSpec((rows, D_MODEL), lambda j: (0, 0)),
                  pl.BlockSpec((D_MODEL, tn), lambda j: (0, j)),
                  pl.BlockSpec((1, tn), lambda j: (0, j))],
        out_specs=pl.BlockSpec((rows, tn), lambda j: (0, j)),
        out_shape=jax.ShapeDtypeStruct((rows, ncol), F32),
        compiler_params=_cparams(("arbitrary",)),
        name="ada",
    )(c_all, w_ada, b_ada.reshape(1, ncol))


def _inproj_kernel(x_ref, sc_ref, sh_ref, win_ref, cw_ref, cprev_ref, gqa_ref, gkva_ref, goc_ref,
                   wqn_ref, wqr_ref, wqrs_ref, wuk_ref, cos_ref, sin_ref,
                   an_ref, qlat_ref, qrope_ref, kcat_ref, ckv_ref, krope_ref, cnew_ref,
                   *rest, tm, with_vt):
    vt_ref, ubuf = rest if with_vt else (None, rest[0])
    j = pl.program_id(1)
    x = x_ref[0]
    h = _layernorm(x) * (1.0 + sc_ref[0]) + sh_ref[0]
    proj = jnp.dot(h.astype(BF16), win_ref[...], preferred_element_type=F32)
    xb = proj[:, _O_XB:_O_XB + CONV_WIDTH]
    xc = proj[:, _O_XC:_O_XC + CONV_WIDTH]
    xv = proj[:, _O_XV:_O_XV + CONV_WIDTH]
    q_a = proj[:, _O_QA:_O_QA + Q_LORA]
    kv_a = proj[:, _O_KVA:_O_KVA + KV_LORA]
    kr4 = proj[:, _O_KR:_O_KR + LANES]
    kr4s = proj[:, _O_KRS:_O_KRS + LANES]

    u = xc * xv

    @pl.when(j == 0)
    def _():
        ubuf[6:8, :] = cprev_ref[0]

    ubuf[8:8 + tm, :] = u
    conv = (cw_ref[0:1, :] * ubuf[6:6 + tm, :] + cw_ref[1:2, :] * ubuf[7:7 + tm, :]
            + cw_ref[2:3, :] * u)
    ubuf[0:8, :] = ubuf[tm:tm + 8, :]
    cnew_ref[0] = u[tm - (CONV_K - 1):tm, :]
    an_ref[0] = (_rms(xb * conv) * goc_ref[...]).astype(BF16)

    cos = cos_ref[...]
    sin = sin_ref[...]

    ckv = _rms(kv_a) * gkva_ref[...]
    kro4 = kr4 * cos + kr4s * sin
    ckv_ref[0] = ckv
    krope_ref[0] = kro4[:, :QK_ROPE]
    kcat_ref[0] = jnp.concatenate([ckv, kro4], axis=1).astype(BF16)
    if with_vt:
        vt_ref[0] = ckv.T.astype(BF16)

    qn = (_rms(q_a) * gqa_ref[...]).astype(BF16)
    q_nope = jnp.dot(qn, wqn_ref[...], preferred_element_type=F32)
    xr = jnp.dot(qn, wqr_ref[...], preferred_element_type=F32)
    xrs = jnp.dot(qn, wqrs_ref[...], preferred_element_type=F32)
    for g in range(2):
        sl = slice(g * LANES, (g + 1) * LANES)
        qrope_ref[0, :, sl] = ((xr[:, sl] * cos + xrs[:, sl] * sin) * Q_SCALE).astype(BF16)
    for p in range(N_HEADS // 2):
        qp = q_nope[:, p * LANES:(p + 1) * LANES].astype(BF16)
        ql = jnp.dot(qp, wuk_ref[p], preferred_element_type=F32)
        qlat_ref[0, :, p * 2 * KV_LORA:(p + 1) * 2 * KV_LORA] = (ql * Q_SCALE).astype(BF16)


def _inproj(x, sc1, sh1, conv_prev, cos_t, sin_t, w, tm, with_vt):
    B, T, _ = x.shape
    nt = T // tm
    full = lambda shp: pl.BlockSpec(shp, lambda b, j: (0,) * len(shp))
    vt_shape = (jax.ShapeDtypeStruct((B, KV_LORA, T), BF16),) if with_vt else ()
    vt_spec = (pl.BlockSpec((1, KV_LORA, tm), lambda b, j: (b, 0, j)),) if with_vt else ()
    out_shapes = (
        jax.ShapeDtypeStruct((B, T, CONV_WIDTH), BF16),
        jax.ShapeDtypeStruct((B, T, N_HEADS * KV_LORA), BF16),
        jax.ShapeDtypeStruct((B, T, 2 * LANES), BF16),
        jax.ShapeDtypeStruct((B, T, 2 * LANES), BF16),
        jax.ShapeDtypeStruct((B, T, KV_LORA), F32),
        jax.ShapeDtypeStruct((B, T, QK_ROPE), F32),
        jax.ShapeDtypeStruct((B, CONV_K - 1, CONV_WIDTH), F32),
    ) + vt_shape
    tile = lambda c: pl.BlockSpec((1, tm, c), lambda b, j: (b, j, 0))
    return pl.pallas_call(
        functools.partial(_inproj_kernel, tm=tm, with_vt=with_vt),
        grid=(B, nt),
        in_specs=[
            tile(D_MODEL),
            pl.BlockSpec((1, 1, D_MODEL), lambda b, j: (b, 0, 0)),
            pl.BlockSpec((1, 1, D_MODEL), lambda b, j: (b, 0, 0)),
            full((D_MODEL, IN_COLS_EXT)),
            full((CONV_K, CONV_WIDTH)),
            pl.BlockSpec((1, CONV_K - 1, CONV_WIDTH), lambda b, j: (b, 0, 0)),
            full((1, Q_LORA)), full((1, KV_LORA)), full((1, CONV_WIDTH)),
            full((Q_LORA, N_HEADS * QK_NOPE)), full((Q_LORA, 2 * LANES)), full((Q_LORA, 2 * LANES)),
            full((N_HEADS // 2, LANES, 2 * KV_LORA)),
            pl.BlockSpec((tm, LANES), lambda b, j: (j, 0)),
            pl.BlockSpec((tm, LANES), lambda b, j: (j, 0)),
        ],
        out_specs=(tile(CONV_WIDTH), tile(N_HEADS * KV_LORA), tile(2 * LANES), tile(2 * LANES),
                   tile(KV_LORA), tile(QK_ROPE),
                   pl.BlockSpec((1, CONV_K - 1, CONV_WIDTH), lambda b, j: (b, 0, 0))) + vt_spec,
        out_shape=out_shapes,
        scratch_shapes=[pltpu.VMEM((tm + 8, CONV_WIDTH), F32)],
        compiler_params=_cparams(("arbitrary", "arbitrary")),
        name="inproj",
    )(x, sc1, sh1, w["w_in"], w["conv_w"], conv_prev, w["g_qa"], w["g_kva"], w["g_oc"],
      w["wq_nope"], w["wq_rope"], w["wq_rope_sw"], w["w_uk_pair"], cos_t, sin_t)


def _stack_queries(qlat_ref, qrope_ref, qs, tq):
    lane = lax.broadcasted_iota(jnp.int32, (tq, LANES), 1)
    for h in range(N_HEADS):
        g, i = divmod(h, 4)
        rope = qrope_ref[0, :, g * LANES:(g + 1) * LANES]
        keep = (lane >= i * QK_ROPE) & (lane < (i + 1) * QK_ROPE)
        qs[h * tq:(h + 1) * tq, 0:KV_LORA] = qlat_ref[0, :, h * KV_LORA:(h + 1) * KV_LORA]
        qs[h * tq:(h + 1) * tq, KV_LORA:KV_LORA + LANES] = jnp.where(keep, rope, jnp.zeros_like(rope))


def _softmax_step(qs, k, v, m_s, l_s, acc_s):
    tk = k.shape[0]
    s = lax.dot_general(qs[...], k, (((1,), (1,)), ((), ())), preferred_element_type=F32)
    m_prev = m_s[...]
    m_new = jnp.maximum(m_prev, jnp.max(s, axis=1, keepdims=True))
    alpha = jnp.exp2(m_prev - m_new)
    if tk % LANES == 0:
        p = jnp.exp2(s - jnp.tile(m_new, (1, tk // LANES)))
    else:
        p = jnp.exp2(s - m_new[:, :tk])
    l_s[...] = alpha * l_s[...] + jnp.sum(p, axis=1, keepdims=True)
    acc_s[...] = alpha * acc_s[...] + jnp.dot(p.astype(BF16), v, preferred_element_type=F32)
    m_s[...] = m_new


def _attn_epilogue(acc_s, l_s, wuv_ref, g_ref, o_ref, tq):
    o = acc_s[...] / l_s[...]
    parts = []
    for p in range(N_HEADS // 2):
        op = jnp.concatenate([o[(2 * p) * tq:(2 * p + 1) * tq], o[(2 * p + 1) * tq:(2 * p + 2) * tq]], axis=1)
        parts.append(jnp.dot(op.astype(BF16), wuv_ref[p], preferred_element_type=F32))
    b = jnp.concatenate(parts, axis=1)
    o_ref[0] = (_rms(b) * g_ref[...]).astype(BF16)


def _attn_prompt_kernel(qlat_ref, qrope_ref, k_ref, vt_ref, wuvt_ref, g_ref, o_ref, qs, m_s, l_s, acc_s, sa, sb,
                        *, tq, tk):
    i = pl.program_id(1)
    M = N_HEADS * tq
    _stack_queries(qlat_ref, qrope_ref, qs, tq)
    m_s[...] = jnp.full(m_s.shape, NEG, F32)
    l_s[...] = jnp.zeros(l_s.shape, F32)
    acc_s[...] = jnp.zeros(acc_s.shape, F32)
    q0 = i * tq
    n_full = (q0 + CHUNK) // tk

    def scores(t, dst):
        k = k_ref[0, pl.ds(pl.multiple_of(t * tk, tk), tk), :]
        dst[...] = lax.dot_general(k, qs[...], (((1,), (1,)), ((), ())), preferred_element_type=F32)

    def update(t, src, limit=None):
        start = pl.multiple_of(t * tk, tk)
        vt = vt_ref[0, :, pl.ds(start, tk)]
        s = src[...]
        if limit is not None:
            kpos = start + lax.broadcasted_iota(jnp.int32, (tk, 1), 0)
            s = jnp.where(kpos < limit, s, NEG)
        m_prev = m_s[...]
        m_new = jnp.maximum(m_prev, jnp.max(s, axis=0, keepdims=True))
        alpha = jnp.exp2(m_prev - m_new)
        p = jnp.exp2(s - m_new)
        l_s[...] = alpha * l_s[...] + jnp.sum(p, axis=0, keepdims=True)
        acc_s[...] = alpha * acc_s[...] + jnp.dot(vt, p.astype(BF16), preferred_element_type=F32)
        m_s[...] = m_new

    scores(0, sa)

    def body(j, carry):
        t = 2 * j
        scores(t + 1, sb)
        update(t, sa)
        scores(t + 2, sa)
        update(t + 1, sb)
        return carry

    lax.fori_loop(0, n_full // 2, body, 0)

    col_t = jnp.bitwise_and(lax.broadcasted_iota(jnp.int32, (1, M), 1), tq - 1)
    limit = q0 + (jnp.right_shift(col_t, CHUNK.bit_length() - 1) + 1) * CHUNK
    odd = n_full % 2

    @pl.when(odd == 0)
    def _():
        update(n_full, sa, limit)

    @pl.when(odd == 1)
    def _():
        scores(n_full, sb)
        update(n_full - 1, sa)
        update(n_full, sb, limit)

    o_t = acc_s[...] / l_s[...]
    parts = []
    for p in range(N_HEADS // 2):
        pair = jnp.concatenate([o_t[:, (2 * p) * tq:(2 * p + 1) * tq], o_t[:, (2 * p + 1) * tq:(2 * p + 2) * tq]],
                               axis=0)
        parts.append(jnp.dot(wuvt_ref[p], pair.astype(BF16), preferred_element_type=F32))
    b = jnp.concatenate(parts, axis=0).T
    o_ref[0] = (_rms(b) * g_ref[...]).astype(BF16)


def _attn_prompt(qlat, qrope, kcat, vt, w, tq, tk):
    B, T, _ = qlat.shape
    M = N_HEADS * tq
    return pl.pallas_call(
        functools.partial(_attn_prompt_kernel, tq=tq, tk=tk),
        grid=(B, T // tq),
        in_specs=[
            pl.BlockSpec((1, tq, N_HEADS * KV_LORA), lambda b, i: (b, i, 0)),
            pl.BlockSpec((1, tq, 2 * LANES), lambda b, i: (b, i, 0)),
            pl.BlockSpec((1, T, 2 * LANES), lambda b, i: (b, 0, 0)),
            pl.BlockSpec((1, KV_LORA, T), lambda b, i: (b, 0, 0)),
            pl.BlockSpec((N_HEADS // 2, LANES, 2 * KV_LORA), lambda b, i: (0, 0, 0)),
            pl.BlockSpec((1, ATTN_WIDTH), lambda b, i: (0, 0)),
        ],
        out_specs=pl.BlockSpec((1, tq, ATTN_WIDTH), lambda b, i: (b, i, 0)),
        out_shape=jax.ShapeDtypeStruct((B, T, ATTN_WIDTH), BF16),
        scratch_shapes=[pltpu.VMEM((M, 2 * LANES), BF16), pltpu.VMEM((1, M), F32),
                        pltpu.VMEM((1, M), F32), pltpu.VMEM((KV_LORA, M), F32),
                        pltpu.VMEM((tk, M), F32), pltpu.VMEM((tk, M), F32)],
        compiler_params=_cparams(("arbitrary", "arbitrary")),
        name="attn_prompt",
    )(qlat, qrope, kcat, vt, jnp.swapaxes(w["w_uv_pair"], 1, 2), w["g_oa"])


def _attn_sample_kernel(qlat_ref, qrope_ref, knew_ref, pckv_ref, pkr_ref, wuv_ref, g_ref, o_ref,
                        qs, m_s, l_s, acc_s, *, tq, tk, n_past):
    _stack_queries(qlat_ref, qrope_ref, qs, tq)
    m_s[...] = jnp.full(m_s.shape, NEG, F32)
    l_s[...] = jnp.zeros(l_s.shape, F32)
    acc_s[...] = jnp.zeros(acc_s.shape, F32)

    def body(t, carry):
        start = pl.multiple_of(t * tk, tk)
        ck = pckv_ref[0, pl.ds(start, tk), :]
        kr = pkr_ref[0, pl.ds(start, tk), :]
        k = jnp.concatenate([ck, kr, kr, kr, kr], axis=1).astype(BF16)
        _softmax_step(qs, k, k[:, :KV_LORA], m_s, l_s, acc_s)
        return carry

    lax.fori_loop(0, n_past // tk, body, 0)
    k = knew_ref[0]
    _softmax_step(qs, k, k[:, :KV_LORA], m_s, l_s, acc_s)
    _attn_epilogue(acc_s, l_s, wuv_ref, g_ref, o_ref, tq)


def _attn_sample(qlat, qrope, kcat, past_ckv, past_krope, w, tk):
    B, T, _ = qlat.shape
    n_past = past_ckv.shape[1]
    M = N_HEADS * T
    per_b = lambda r, c: pl.BlockSpec((1, r, c), lambda b: (b, 0, 0))
    return pl.pallas_call(
        functools.partial(_attn_sample_kernel, tq=T, tk=tk, n_past=n_past),
        grid=(B,),
        in_specs=[per_b(T, N_HEADS * KV_LORA), per_b(T, 2 * LANES), per_b(T, 2 * LANES),
                  per_b(n_past, KV_LORA), per_b(n_past, QK_ROPE),
                  pl.BlockSpec((N_HEADS // 2, 2 * KV_LORA, LANES), lambda b: (0, 0, 0)),
                  pl.BlockSpec((1, ATTN_WIDTH), lambda b: (0, 0))],
        out_specs=per_b(T, ATTN_WIDTH),
        out_shape=jax.ShapeDtypeStruct((B, T, ATTN_WIDTH), BF16),
        scratch_shapes=[pltpu.VMEM((M, 2 * LANES), BF16), pltpu.VMEM((M, LANES), F32),
                        pltpu.VMEM((M, LANES), F32), pltpu.VMEM((M, KV_LORA), F32)],
        compiler_params=_cparams(("arbitrary",)),
        name="attn_sample",
    )(qlat, qrope, kcat, past_ckv, past_krope, w["w_uv_pair"], w["g_oa"])


def _outproj_kernel(x_ref, an_ref, bn_ref, gt1_ref, sc2_ref, sh2_ref, wo_ref, l1g_ref, l1b_ref,
                    rw_ref, rb_ref, h2_all_ref, x1_ref, h2_ref, meta_ref, gates_ref, cnt_ref, run_s,
                    *, tm, alpha):
    del h2_all_ref
    first = (pl.program_id(0) == 0) & (pl.program_id(1) == 0)

    @pl.when(first)
    def _():
        run_s[...] = jnp.zeros(run_s.shape, F32)

    chains = 2 if tm >= 512 else 1
    hm = tm // chains
    lane = lax.broadcasted_iota(jnp.int32, (hm, LANES), 1)
    lane_f = lane.astype(F32)
    for ch in range(chains):
        rs = slice(ch * hm, (ch + 1) * hm)
        m = (jnp.dot(an_ref[0, rs, :], wo_ref[0:CONV_WIDTH, :], preferred_element_type=F32)
             + jnp.dot(bn_ref[0, rs, :], wo_ref[CONV_WIDTH:, :], preferred_element_type=F32))
        x1 = _layernorm(alpha * x_ref[0, rs, :] + gt1_ref[0] * m) * l1g_ref[...] + l1b_ref[...]
        x1_ref[0, rs, :] = x1
        h2 = _layernorm(x1) * (1.0 + sc2_ref[0]) + sh2_ref[0]
        _rows_to_tiles(h2_ref, h2, ch * hm)

        lg = jnp.dot(h2.astype(BF16), rw_ref[...], preferred_element_type=F32) + rb_ref[...]
        vals, sels = [], []
        chosen = jnp.zeros((hm, LANES), F32)
        for _ in range(TOP_K):
            mx = jnp.max(lg, axis=1, keepdims=True)
            idx = jnp.min(jnp.where(lg == mx, lane_f, float(LANES)), axis=1, keepdims=True)
            sel = lane_f == idx
            vals.append(mx)
            sels.append(idx)
            chosen = jnp.where(sel, 1.0, chosen)
            lg = jnp.where(sel, NEG, lg)

        es = [jnp.exp(v - vals[0]) for v in vals]
        denom = es[0] + es[1] + es[2] + es[3]
        run_s[0:1, :] = run_s[0:1, :] + jnp.sum(chosen, axis=0, keepdims=True)

        meta = jnp.zeros((hm, LANES), jnp.int32)
        gates = jnp.zeros((hm, LANES), F32)
        for k in range(TOP_K):
            meta = jnp.where(lane == k, sels[k].astype(jnp.int32), meta)
            gates = jnp.where(lane == k, es[k] / denom, gates)
        meta_ref[0, rs, :] = meta
        gates_ref[0, rs, :] = gates
    cnt_ref[...] = jnp.broadcast_to(run_s[0:1, :], cnt_ref.shape)


def _outproj(x, a_n, b_n, gt1, sc2, sh2, h2_all, tok0, w, tm, alpha):
    B, T, _ = x.shape
    blk0 = tok0 // tm
    tile = lambda c: pl.BlockSpec((1, tm, c), lambda b, j: (b, j, 0))
    modv = pl.BlockSpec((1, 1, D_MODEL), lambda b, j: (b, 0, 0))
    full = lambda shp: pl.BlockSpec(shp, lambda b, j: (0,) * len(shp))
    return pl.pallas_call(
        functools.partial(_outproj_kernel, tm=tm, alpha=alpha),
        grid=(B, T // tm),
        in_specs=[tile(D_MODEL), tile(CONV_WIDTH), tile(ATTN_WIDTH), modv, modv, modv,
                  full((CONV_WIDTH + ATTN_WIDTH, D_MODEL)), full((1, D_MODEL)), full((1, D_MODEL)),
                  full((D_MODEL, LANES)), full((1, LANES)), pl.BlockSpec(memory_space=pl.ANY)],
        out_specs=(tile(D_MODEL),
                   pl.BlockSpec((tm * ROW_TILE, LANES), lambda b, j: (blk0 + b * (T // tm) + j, 0)),
                   tile(LANES), tile(LANES), pl.BlockSpec((8, LANES), lambda b, j: (0, 0))),
        out_shape=(jax.ShapeDtypeStruct((B, T, D_MODEL), F32),
                   jax.ShapeDtypeStruct(h2_all.shape, F32),
                   jax.ShapeDtypeStruct((B, T, LANES), jnp.int32), jax.ShapeDtypeStruct((B, T, LANES), F32),
                   jax.ShapeDtypeStruct((8, LANES), F32)),
        scratch_shapes=[pltpu.VMEM((8, LANES), F32)],
        input_output_aliases={11: 1},
        compiler_params=_cparams(("arbitrary", "arbitrary")),
        name="outproj",
    )(x, a_n, b_n, gt1, sc2, sh2, w["w_out"], w["ln1_g"], w["ln1_b"], w["router_w"], w["router_b"], h2_all)


def _expert_kernel(order_ref, blk_e_ref, blk_j0_ref, blk_nv_ref, nxt_e_ref, used_ref,
                   h_ref, wgu_hbm, bgu_ref, wd_hbm, bd_ref, y_ref,
                   x0, x1, y0, y1, xb, act_s, wgu_st, wd_st, wgu_bf, wd_bf, gsem, ssem, wsem, *, bm, n_tok):
    s = pl.program_id(0)
    used = used_ref[0]
    dump0 = TOP_K * n_tok

    def weight_copies(e):
        return (pltpu.make_async_copy(wgu_hbm.at[e], wgu_st, wsem.at[0]),
                pltpu.make_async_copy(wd_hbm.at[e], wd_st, wsem.at[1]))

    tok_bits = (n_tok - 1).bit_length()

    def gather_start(j0, xbuf, sem, r):
        tok = jnp.bitwise_and(order_ref[j0 + r], (1 << tok_bits) - 1)
        pltpu.make_async_copy(h_ref.at[tok], xbuf.at[pl.ds(r * ROW_TILE, ROW_TILE)], sem).start()

    def scatter_start(j0, nv, ybuf, sem, r, odd):
        real = lax.shift_right_logical(order_ref[j0 + r], tok_bits)
        dump = dump0 + odd * bm + r
        valid = jnp.right_shift(r - nv, 31)
        dst = dump + jnp.bitwise_and(valid, real - dump)
        pltpu.make_async_copy(ybuf.at[pl.ds(r * ROW_TILE, ROW_TILE)], y_ref.at[dst], sem).start()

    def block_wait(buf, sem):
        pltpu.make_async_copy(buf, buf, sem).wait()

    def switch_weights(b):
        prev = blk_e_ref[jnp.maximum(b - 1, 0)]
        e = blk_e_ref[b]

        @pl.when((b == 0) | (e != prev))
        def _():
            for c in weight_copies(e):
                c.wait()
            wgu_bf[...] = wgu_st[...].astype(BF16)
            wd_bf[...] = wd_st[...].astype(BF16)

            @pl.when(nxt_e_ref[e] >= 0)
            def _():
                for c in weight_copies(nxt_e_ref[e]):
                    c.start()

    def block(b, x_in, y_out, g_next, x_next, gsem_next, s_prev, y_prev, ssem_prev, prev_odd, wait_y_free):
        gj0 = blk_j0_ref[g_next + 1]
        sj0 = blk_j0_ref[s_prev + 1]
        snv = blk_nv_ref[s_prev + 1]
        def issue(k, after=None):
            zero = 0
            if after is not None:
                bits = pltpu.bitcast(jnp.abs(after[0:SUBLANES, 0:LANES]), jnp.int32)
                zero = jnp.minimum(bits[0, 0], 0)
            half = DMA_BATCHES // 2
            q = k % half
            for r in range(q * bm // half, (q + 1) * bm // half):
                if k < half:
                    gather_start(gj0 + zero, x_next, gsem_next, r)
                else:
                    scatter_start(sj0 + zero, snv, y_prev, ssem_prev, r, prev_odd)

        xb[...] = _rows_from_tiles(x_in, bm).astype(BF16)
        e = blk_e_ref[b]
        cw = D_FF // 4
        prev = None
        for c in range(4):
            issue(2 * c, prev)
            gs, ls = slice(c * cw, (c + 1) * cw), slice(D_FF + c * cw, D_FF + (c + 1) * cw)
            g = jnp.dot(xb[...], wgu_bf[:, gs], preferred_element_type=F32) + bgu_ref[e][:, gs]
            issue(2 * c + 1, g)
            lin = jnp.dot(xb[...], wgu_bf[:, ls], preferred_element_type=F32) + bgu_ref[e][:, ls]
            prev = lin
            g = jnp.minimum(g, SWIGLU_LIMIT)
            lin = jnp.clip(lin, -SWIGLU_LIMIT, SWIGLU_LIMIT)
            act_s[:, gs] = (g * _sigmoid(SWIGLU_ALPHA * g) * (lin + 1.0)).astype(BF16)
        hw = D_MODEL // 2
        wait_y_free()
        for h in range(2):
            issue(8 + h, prev)
            y = jnp.dot(act_s[...], wd_bf[:, h * hw:(h + 1) * hw], preferred_element_type=F32)
            y = y + bd_ref[e][:, h * hw:(h + 1) * hw]
            prev = y
            for c in range(hw // LANES):
                y_out[pl.ds(h * (hw // LANES) + c, bm, stride=ROW_TILE), :] = y[:, c * LANES:(c + 1) * LANES]

    b0 = 2 * s
    b1 = b0 + 1

    @pl.when(b0 < used)
    def _():
        @pl.when(s == 0)
        def _():
            for c in weight_copies(blk_e_ref[0]):
                c.start()
            y1[...] = jnp.zeros(y1.shape, F32)

            def first(r, c):
                scatter_start(0, 0, y1, ssem.at[0], r, 0)
                gather_start(blk_j0_ref[1], x0, gsem.at[0], r)
                return c

            lax.fori_loop(0, bm, first, 0)
            block_wait(y1, ssem.at[0])

        def y0_free():
            @pl.when(s > 0)
            def _():
                block_wait(y0, ssem.at[0])

        def y1_free():
            block_wait(y1, ssem.at[1])

        switch_weights(b0)
        block_wait(x0, gsem.at[0])
        block(b0, x0, y0, b1, x1, gsem.at[1], b0 - 1, y1, ssem.at[1], 1, y0_free)

        switch_weights(b1)
        block_wait(x1, gsem.at[1])
        block(b1, x1, y1, b0 + 2, x0, gsem.at[0], b0, y0, ssem.at[0], 0, y1_free)

        @pl.when(b0 + 2 >= used)
        def _():
            def last(r, c):
                scatter_start(blk_j0_ref[b1 + 1], blk_nv_ref[b1 + 1], y1, ssem.at[1], r, 1)
                return c

            lax.fori_loop(0, bm, last, 0)
            block_wait(y0, ssem.at[0])
            block_wait(y1, ssem.at[1])
            block_wait(x0, gsem.at[0])


def _expert(order, blk_e, blk_j0, blk_nv, nxt_e, used, h2, w, bm):
    n_tok = h2.shape[0]
    nblk = blk_e.shape[0]
    f32buf = lambda shp: pltpu.VMEM(shp, F32)
    return pl.pallas_call(
        functools.partial(_expert_kernel, bm=bm, n_tok=n_tok),
        grid_spec=pltpu.PrefetchScalarGridSpec(
            num_scalar_prefetch=6,
            grid=(nblk // 2,),
            in_specs=[
                pl.BlockSpec(memory_space=pl.ANY),
                pl.BlockSpec(memory_space=pl.ANY),
                pl.BlockSpec((N_EXPERTS, 1, 2 * D_FF), lambda s, *_: (0, 0, 0)),
                pl.BlockSpec(memory_space=pl.ANY),
                pl.BlockSpec((N_EXPERTS, 1, D_MODEL), lambda s, *_: (0, 0, 0)),
            ],
            out_specs=pl.BlockSpec(memory_space=pl.ANY),
            scratch_shapes=[f32buf((bm * ROW_TILE, LANES)), f32buf((bm * ROW_TILE, LANES)),
                            f32buf((bm * ROW_TILE, LANES)), f32buf((bm * ROW_TILE, LANES)),
                            pltpu.VMEM((bm, D_MODEL), BF16), pltpu.VMEM((bm, D_FF), BF16),
                            f32buf((D_MODEL, 2 * D_FF)), f32buf((D_FF, D_MODEL)),
                            pltpu.VMEM((D_MODEL, 2 * D_FF), BF16), pltpu.VMEM((D_FF, D_MODEL), BF16),
                            pltpu.SemaphoreType.DMA((2,)), pltpu.SemaphoreType.DMA((2,)),
                            pltpu.SemaphoreType.DMA((2,))],
        ),
        out_shape=jax.ShapeDtypeStruct((TOP_K * n_tok + 2 * bm, ROW_TILE, LANES), F32),
        compiler_params=_cparams(("arbitrary",)),
        name="moe_expert",
    )(order, blk_e, blk_j0, blk_nv, nxt_e, used, h2, w["w_gu"], w["b_gu"], w["w_down"], w["b_down"])


def _combine_kernel(y0_ref, y1_ref, y2_ref, y3_ref, x1_ref, gates_ref, gt2_ref, l2g_ref, l2b_ref, o_ref, *, alpha):
    gates = gates_ref[...]
    tm = gates.shape[0]
    f = gates[:, 0:1] * _rows_from_tiles(y0_ref, tm)
    for k, y_ref in enumerate((y1_ref, y2_ref, y3_ref), start=1):
        f = f + gates[:, k:k + 1] * _rows_from_tiles(y_ref, tm)
    o_ref[...] = _layernorm(alpha * x1_ref[...] + gt2_ref[0] * f) * l2g_ref[...] + l2b_ref[...]


def _combine(y_rows, x1, gates, gt2, w, tm, tokens_per_batch, alpha, n_tok, tok0):
    N = x1.shape[0]
    per_b = tokens_per_batch // tm
    nt = N // tm
    assert n_tok % tm == 0 and tok0 % tm == 0
    slot = lambda k: pl.BlockSpec((tm * ROW_TILE, LANES), lambda i: ((k * n_tok + tok0) // tm + i, 0))
    return pl.pallas_call(
        functools.partial(_combine_kernel, alpha=alpha),
        grid=(nt,),
        in_specs=[slot(0), slot(1), slot(2), slot(3),
                  pl.BlockSpec((tm, D_MODEL), lambda i: (i, 0)),
                  pl.BlockSpec((tm, LANES), lambda i: (i, 0)),
                  pl.BlockSpec((1, 1, D_MODEL), lambda i: (i // per_b, 0, 0)),
                  pl.BlockSpec((1, D_MODEL), lambda i: (0, 0)),
                  pl.BlockSpec((1, D_MODEL), lambda i: (0, 0))],
        out_specs=pl.BlockSpec((tm, D_MODEL), lambda i: (i, 0)),
        out_shape=jax.ShapeDtypeStruct((N, D_MODEL), F32),
        compiler_params=_cparams(("arbitrary",)),
        name="moe_combine",
    )(y_rows, y_rows, y_rows, y_rows, x1, gates, gt2, w["ln2_g"], w["ln2_b"])


def _moe_experts(h2, idx, counts, w, bm):
    N = h2.shape[0]
    n_rows = N * TOP_K
    nblk = (n_rows + N_EXPERTS * (bm - 1)) // bm
    nblk += nblk % 2
    experts = jnp.arange(N_EXPERTS, dtype=jnp.int32)
    id_bits = (n_rows - 1).bit_length()
    tok_bits = (N - 1).bit_length()
    assert tok_bits + (TOP_K * N - 1).bit_length() <= 32
    keys = jnp.left_shift(idx.reshape(-1), id_bits) + jnp.arange(n_rows, dtype=jnp.int32)
    keys = jnp.concatenate([keys, jnp.full(((1 << id_bits) - n_rows,), jnp.iinfo(jnp.int32).max, jnp.int32)])
    flat = jnp.bitwise_and(jnp.sort(keys, stable=False)[:n_rows], (1 << id_bits) - 1).astype(jnp.uint32)
    tok = jnp.right_shift(flat, 2)
    row = jnp.bitwise_and(flat, TOP_K - 1) * N + tok
    order = lax.bitcast_convert_type(jnp.left_shift(row, tok_bits) | tok, jnp.int32)
    order = jnp.concatenate([order, jnp.zeros((bm,), jnp.int32)])
    nb_e = (counts + bm - 1) // bm
    blk_end = jnp.cumsum(nb_e)
    first_blk = blk_end - nb_e
    start_sorted = jnp.cumsum(counts) - counts
    used = blk_end[-1].astype(jnp.int32)
    b = jnp.arange(-1, nblk + 1, dtype=jnp.int32)
    bc = jnp.clip(b, 0, used - 1)
    e = jnp.minimum(jnp.sum(blk_end[None, :] <= bc[:, None], axis=1), N_EXPERTS - 1).astype(jnp.int32)
    pick = lambda table: jnp.sum(jnp.where(e[:, None] == experts, table, 0), axis=1)
    local = bc - pick(first_blk)
    blk_j0 = (pick(start_sorted) + local * bm).astype(jnp.int32)
    blk_nv = jnp.where((b >= 0) & (b < used), jnp.minimum(bm, pick(counts) - local * bm), 0).astype(jnp.int32)
    blk_e = e[1:nblk + 1]
    later = (experts[None, :] > experts[:, None]) & (counts[None, :] > 0)
    nxt = jnp.min(jnp.where(later, experts[None, :], N_EXPERTS), axis=1)
    nxt_e = jnp.where(nxt < N_EXPERTS, nxt, -1).astype(jnp.int32)
    y_rows = _expert(order, blk_e, blk_j0, blk_nv, nxt_e, used.reshape(1), h2, w, bm)
    return y_rows.reshape(-1, LANES)


def _rope_tables(pos):
    half = QK_ROPE // 2
    inv = ROPE_THETA ** (-jnp.arange(half, dtype=F32) / half)
    ang = pos.astype(F32)[:, None] * inv[None, :]
    cos, sin = jnp.cos(ang), jnp.sin(ang)
    cos32 = jnp.concatenate([cos, cos], axis=1)
    sin32 = jnp.concatenate([-sin, sin], axis=1)
    return jnp.tile(cos32, (1, LANES // QK_ROPE)), jnp.tile(sin32, (1, LANES // QK_ROPE))


def _swap_halves(w32):
    shp = w32.shape
    w = w32.reshape(shp[:-1] + (shp[-1] // QK_ROPE, 2, QK_ROPE // 2))
    return w[..., ::-1, :].reshape(shp)


def _prep_weights(l, w_in, conv_w, g_qa, w_qb, g_kva, w_kvb, g_out_conv, g_out_attn, w_out,
                  ln1_g, ln1_b, router_w, router_b, w_gu, b_gu, w_down, b_down, ln2_g, ln2_b):
    w = {}
    wi = w_in[l]
    k_r = wi[:, _O_KR:_O_KR + QK_ROPE]
    rep = LANES // QK_ROPE
    w["w_in"] = jnp.concatenate([wi[:, :_O_KR], jnp.tile(k_r, (1, rep)), jnp.tile(_swap_halves(k_r), (1, rep))],
                                axis=1).astype(BF16)
    w["conv_w"] = conv_w[l]
    w["g_qa"] = g_qa[l].reshape(1, Q_LORA)
    w["g_kva"] = g_kva[l].reshape(1, KV_LORA)
    w["g_oc"] = g_out_conv[l].reshape(1, CONV_WIDTH)
    w["g_oa"] = g_out_attn[l].reshape(1, ATTN_WIDTH)
    wq = w_qb[l].reshape(Q_LORA, N_HEADS, QK_NOPE + QK_ROPE)
    w["wq_nope"] = wq[:, :, :QK_NOPE].reshape(Q_LORA, N_HEADS * QK_NOPE).astype(BF16)
    wq_rope = wq[:, :, QK_NOPE:].reshape(Q_LORA, N_HEADS * QK_ROPE)
    w["wq_rope"] = wq_rope.astype(BF16)
    w["wq_rope_sw"] = _swap_halves(wq_rope).astype(BF16)
    w_uk = jnp.transpose(w_kvb[l][:, :, :QK_NOPE], (1, 2, 0))
    w_uv = jnp.transpose(w_kvb[l][:, :, QK_NOPE:], (1, 0, 2))
    zk = jnp.zeros((QK_NOPE, KV_LORA), F32)
    zv = jnp.zeros((KV_LORA, V_HEAD), F32)
    w["w_uk_pair"] = jnp.stack([
        jnp.concatenate([jnp.concatenate([w_uk[2 * p], zk], axis=1),
                         jnp.concatenate([zk, w_uk[2 * p + 1]], axis=1)], axis=0)
        for p in range(N_HEADS // 2)]).astype(BF16)
    w["w_uv_pair"] = jnp.stack([
        jnp.concatenate([jnp.concatenate([w_uv[2 * p], zv], axis=1),
                         jnp.concatenate([zv, w_uv[2 * p + 1]], axis=1)], axis=0)
        for p in range(N_HEADS // 2)]).astype(BF16)
    w["w_out"] = w_out[l].astype(BF16)
    w["ln1_g"] = ln1_g[l].reshape(1, D_MODEL)
    w["ln1_b"] = ln1_b[l].reshape(1, D_MODEL)
    w["ln2_g"] = ln2_g[l].reshape(1, D_MODEL)
    w["ln2_b"] = ln2_b[l].reshape(1, D_MODEL)
    w["router_w"] = jnp.pad(router_w[l], ((0, 0), (0, LANES - N_EXPERTS))).astype(BF16)
    w["router_b"] = jnp.concatenate([router_b[l], jnp.full((LANES - N_EXPERTS,), NEG, F32)]).reshape(1, LANES)
    w["w_gu"] = w_gu[l]
    w["b_gu"] = b_gu[l].reshape(N_EXPERTS, 1, 2 * D_FF)
    w["w_down"] = w_down[l]
    w["b_down"] = b_down[l].reshape(N_EXPERTS, 1, D_MODEL)
    return w


def _mixer(x, mod, conv_prev, past, pos0, h2_all, tok0, w, alpha, *, tm_in, tm_out, tq=128, tk=512):
    B, T, _ = x.shape
    sh1, sc1, gt1, sh2, sc2, gt2 = [mod[:, None, i * D_MODEL:(i + 1) * D_MODEL] for i in range(N_MOD)]
    cos_t, sin_t = _rope_tables(pos0 + jnp.arange(T, dtype=jnp.int32))
    a_n, qlat, qrope, kcat, ckv, krope, conv_new, *vt = _inproj(x, sc1, sh1, conv_prev, cos_t, sin_t, w, tm_in,
                                                                 with_vt=past is None)
    if past is None:
        b_n = _attn_prompt(qlat, qrope, kcat, vt[0], w, tq, tk)
    else:
        b_n = _attn_sample(qlat, qrope, kcat, past[0], past[1], w, tk)
    x1, h2_all, meta, gates, cnt = _outproj(x, a_n, b_n, gt1, sc2, sh2, h2_all, tok0, w, tm_out, alpha)
    N = B * T
    route = dict(x1=x1.reshape(N, D_MODEL), idx=meta.reshape(N, LANES)[:, :TOP_K], gates=gates.reshape(N, LANES),
                 counts=cnt[0, :N_EXPERTS].astype(jnp.int32), gt2=gt2, shape=(B, T))
    return h2_all, route, ckv, krope, conv_new


def kernel(x_prompt, x_sample, c_prompt, c_sample, cache_ckv, cache_krope, state_conv, w_ada, b_ada, w_in, conv_w, g_qa, w_qb, g_kva, w_kvb, g_out_conv, g_out_attn, w_out, ln1_g, ln1_b, router_w, router_b, w_gu, b_gu, w_down, b_down, ln2_g, ln2_b):
    depth = w_ada.shape[0]
    Bp, Tp, _ = x_prompt.shape
    Bs, Ts, _ = x_sample.shape
    past_len = cache_ckv.shape[2]
    assert Ts == CHUNK and past_len % 512 == 0 and Tp % 1024 == 0
    alpha = (2.0 * depth) ** 0.25
    xp, xs = x_prompt, x_sample
    outs = [[] for _ in range(6)]
    c_all = jnp.concatenate([c_prompt, c_sample, jnp.zeros((16 - Bp - Bs, D_MODEL), F32)], axis=0)
    for l in range(depth):
        w = _prep_weights(l, w_in, conv_w, g_qa, w_qb, g_kva, w_kvb, g_out_conv, g_out_attn, w_out,
                          ln1_g, ln1_b, router_w, router_b, w_gu, b_gu, w_down, b_down, ln2_g, ln2_b)
        mod = _ada(c_all, w_ada[l], b_ada[l])
        n_p, n_s = Bp * Tp, Bs * Ts
        n_tok = n_p + n_s
        h2_all = jnp.zeros((n_tok * ROW_TILE, LANES), F32)
        h2_all, rp, ckv_p, kr_p, cv_p = _mixer(xp, mod[:Bp], jnp.zeros((Bp, CONV_K - 1, CONV_WIDTH), F32), None, 0,
                                               h2_all, 0, w, alpha, tm_in=1024, tm_out=512)
        h2_all, rs, ckv_s, kr_s, cv_s = _mixer(xs, mod[Bp:Bp + Bs], state_conv[l], (cache_ckv[l], cache_krope[l]),
                                               past_len, h2_all, n_p, w, alpha, tm_in=Ts, tm_out=Ts)
        y_rows = _moe_experts(h2_all.reshape(n_tok, ROW_TILE, LANES), jnp.concatenate([rp["idx"], rs["idx"]]),
                              rp["counts"] + rs["counts"], w, bm=256)
        xp = _combine(y_rows, rp["x1"], rp["gates"], rp["gt2"], w, 512, Tp, alpha, n_tok, 0).reshape(rp["shape"] + (D_MODEL,))
        xs = _combine(y_rows, rs["x1"], rs["gates"], rs["gt2"], w, Ts, Ts, alpha, n_tok, n_p).reshape(rs["shape"] + (D_MODEL,))
        for o, v in zip(outs, (ckv_p, kr_p, cv_p, ckv_s, kr_s, cv_s)):
            o.append(v)
    return (xp, xs) + tuple(jnp.stack(o) for o in outs)
```
